```python
import jax, jax.numpy as jnp
from jax import lax
import numpy as np

D_MODEL = 4096
BATCH = 4
SEQ = 4096
DEPTH = 1

HEAD_DIM = 128
ATTN_WIDTH = D_MODEL // 2
N_ATTN_HEADS = ATTN_WIDTH // HEAD_DIM
DILATED_CONFIGS = ((128, 1), (512, 4), (2048, 16))
N_DIL = len(DILATED_CONFIGS)
BLOCK = 128
POOL_WIDTH = D_MODEL - ATTN_WIDTH
POOL_WINDOWS = (2, 4, 8, 16)
N_POOL_GROUPS = len(POOL_WINDOWS)
POOL_GROUP = POOL_WIDTH // N_POOL_GROUPS
ATTN_IN = N_DIL * 3 * ATTN_WIDTH
MIX_IN = ATTN_IN + POOL_WIDTH
MIX_OUT = ATTN_WIDTH + POOL_WIDTH
D_FF = ((8 * D_MODEL // 3 + 255) // 256) * 256
MEM_LEN = 256
CROSS_HEADS = 4
CROSS_DIM = 128
ROPE_THETA = 10000.0
EPS = 1e-6
NEG_INF = -1e30

kernel_name = "hymba_pool_dilated_macaron_layer"


def rmsnorm(x, g):
    xf = x.astype(jnp.float32)
    y = xf * lax.rsqrt(jnp.mean(xf * xf, axis=-1, keepdims=True) + EPS)
    return (y * g.astype(jnp.float32)).astype(x.dtype)


def swiglu(u, w_in, w_out):
    a, b = jnp.split(u @ w_in, 2, axis=-1)
    return (jax.nn.silu(a) * b) @ w_out


def rope_tables(positions):
    inv = 1.0 / (ROPE_THETA ** (jnp.arange(0, HEAD_DIM, 2, dtype=jnp.float32) / HEAD_DIM))
    ang = positions.astype(jnp.float32)[..., None] * inv
    ang = jnp.concatenate([ang, ang], axis=-1)[:, :, None, :]
    return jnp.cos(ang), jnp.sin(ang)


def apply_rope(t, cos, sin):
    tf = t.astype(jnp.float32)
    t1, t2 = jnp.split(tf, 2, axis=-1)
    rot = jnp.concatenate([-t2, t1], axis=-1)
    return (tf * cos + rot * sin).astype(t.dtype)


def dilated_window_attention(q, k, v, dil, sub_window):
    B, S, H, Dh = q.shape
    L = S // dil
    N = B * dil

    def to_sub(t):
        return t.reshape(B, L, dil, H, Dh).transpose(0, 2, 1, 3, 4).reshape(N, L, H, Dh)

    qs, ks, vs = to_sub(q), to_sub(k), to_sub(v)
    nb = -(-L // BLOCK)
    Lp = nb * BLOCK
    qs = jnp.pad(qs, ((0, 0), (0, Lp - L), (0, 0), (0, 0)))
    ks = jnp.pad(ks, ((0, 0), (BLOCK, Lp - L), (0, 0), (0, 0)))
    vs = jnp.pad(vs, ((0, 0), (BLOCK, Lp - L), (0, 0), (0, 0)))

    qb = qs.reshape(N, nb, BLOCK, H, Dh)

    def band(t):
        prev = t[:, :Lp].reshape(N, nb, BLOCK, H, Dh)
        cur = t[:, BLOCK:].reshape(N, nb, BLOCK, H, Dh)
        return jnp.concatenate([prev, cur], axis=2)

    kb, vb = band(ks), band(vs)
    scores = jnp.einsum('nbqhd,nbkhd->nbhqk', qb, kb,
                        preferred_element_type=jnp.float32) * (HEAD_DIM ** -0.5)
    qi = jnp.arange(BLOCK)[:, None]
    kj = jnp.arange(2 * BLOCK)[None, :]
    dist = qi + BLOCK - kj
    key_idx = jnp.arange(nb)[:, None, None] * BLOCK - BLOCK + kj[None]
    mask = (dist >= 0)[None] & (dist <= sub_window)[None] & (key_idx >= 0)
    scores = jnp.where(mask[None, :, None], scores, NEG_INF)
    lse = jax.nn.logsumexp(scores, axis=-1)
    p = jnp.exp(scores - lse[..., None])
    o = jnp.einsum('nbhqk,nbkhd->nbqhd', p.astype(v.dtype), vb)

    o = o.reshape(N, Lp, H, Dh)[:, :L].reshape(B, dil, L, H, Dh)
    o = o.transpose(0, 2, 1, 3, 4).reshape(B, S, H, Dh)
    lse = lse.transpose(0, 1, 3, 2).reshape(N, Lp, H)[:, :L].reshape(B, dil, L, H)
    lse = lse.transpose(0, 2, 1, 3).reshape(B, S, H)
    return o, lse


def pooling_mixer(zp, w_pool, pool_scale):
    B, S, _ = zp.shape
    zg = zp.reshape(B, S, N_POOL_GROUPS, POOL_GROUP).astype(jnp.float32)
    csum = jnp.cumsum(zg, axis=1)
    t = jnp.arange(S)
    groups = []
    for g, w in enumerate(POOL_WINDOWS):
        c = csum[:, :, g]
        lag = jnp.pad(c, ((0, 0), (w, 0), (0, 0)))[:, :S]
        cnt = jnp.minimum(t + 1, w).astype(jnp.float32)[None, :, None]
        groups.append((c - lag) / cnt - zg[:, :, g])
    y = jnp.stack(groups, axis=2).astype(zp.dtype)
    y = jnp.einsum('bsgc,gcd->bsgd', y, w_pool).reshape(B, S, POOL_WIDTH)
    return y * pool_scale


def hybrid_mixer(u, w_mix_in, w_pool, pool_scale, w_mix_out, cos, sin):
    B, S, _ = u.shape
    z = u @ w_mix_in
    za = z[..., :ATTN_IN].reshape(B, S, N_DIL, 3, N_ATTN_HEADS, HEAD_DIM)
    zp = z[..., ATTN_IN:]
    outs, lses = [], []
    for i, (win, dil) in enumerate(DILATED_CONFIGS):
        q = apply_rope(za[:, :, i, 0], cos, sin)
        k = apply_rope(za[:, :, i, 1], cos, sin)
        v = za[:, :, i, 2]
        o, lse = dilated_window_attention(q, k, v, dil, win // dil)
        outs.append(o)
        lses.append(lse)
    wts = jax.nn.softmax(jnp.stack(lses, axis=0), axis=0)
    o_attn = jnp.einsum('gbsh,gbshd->bshd', wts, jnp.stack(outs, axis=0).astype(jnp.float32))
    o_attn = o_attn.astype(u.dtype).reshape(B, S, ATTN_WIDTH)
    o_pool = pooling_mixer(zp, w_pool, pool_scale)
    return jnp.concatenate([o_attn, o_pool], axis=-1) @ w_mix_out


def memory_cross_attention(u, m, w_q, w_kv, w_o):
    B, S, _ = u.shape
    M = m.shape[1]
    q = (u @ w_q).reshape(B, S, CROSS_HEADS, CROSS_DIM)
    kv = (m @ w_kv).reshape(B, M, 2, CROSS_HEADS, CROSS_DIM)
    k, v = kv[:, :, 0], kv[:, :, 1]
    s = jnp.einsum('bshd,bmhd->bhsm', q, k, preferred_element_type=jnp.float32) * (CROSS_DIM ** -0.5)
    p = jax.nn.softmax(s, axis=-1)
    o = jnp.einsum('bhsm,bmhd->bshd', p.astype(v.dtype), v).reshape(B, S, CROSS_HEADS * CROSS_DIM)
    return o @ w_o


def setup_inputs(seed: int = 0) -> dict:
    key = jax.random.key(seed)
    ks = jax.random.split(key, 24)
    f32 = jnp.float32

    def nrm(k, shape, fan_in):
        return jax.random.normal(k, shape, f32) * (fan_in ** -0.5)

    def gain(k, shape):
        return jnp.ones(shape, f32) + 0.02 * jax.random.normal(k, shape, f32)

    x = jax.random.normal(ks[0], (BATCH, SEQ, D_MODEL), f32)
    mem = jax.random.normal(ks[1], (BATCH, MEM_LEN, D_MODEL), f32)
    offset = jax.random.randint(ks[2], (BATCH, 1), 0, 1024, dtype=jnp.int32)
    positions = jnp.arange(SEQ, dtype=jnp.int32)[None, :] + offset
    return {
        "x": x,
        "mem": mem,
        "positions": positions,
        "g_ffn1": gain(ks[3], (DEPTH, D_MODEL)),
        "w_ffn1_in": nrm(ks[4], (DEPTH, D_MODEL, 2 * D_FF), D_MODEL),
        "w_ffn1_out": nrm(ks[5], (DEPTH, D_FF, D_MODEL), D_FF),
        "g_mix": gain(ks[6], (DEPTH, D_MODEL)),
        "w_mix_in": nrm(ks[7], (DEPTH, D_MODEL, MIX_IN), D_MODEL),
        "w_pool": nrm(ks[8], (DEPTH, N_POOL_GROUPS, POOL_GROUP, POOL_GROUP), POOL_GROUP),
        "pool_scale": gain(ks[9], (DEPTH, POOL_WIDTH)),
        "w_mix_out": nrm(ks[10], (DEPTH, MIX_OUT, D_MODEL), MIX_OUT),
        "g_cross": gain(ks[11], (DEPTH, D_MODEL)),
        "g_mem": gain(ks[12], (DEPTH, D_MODEL)),
        "w_cross_q": nrm(ks[13], (DEPTH, D_MODEL, CROSS_HEADS * CROSS_DIM), D_MODEL),
        "w_cross_kv": nrm(ks[14], (DEPTH, D_MODEL, 2 * CROSS_HEADS * CROSS_DIM), D_MODEL),
        "w_cross_o": nrm(ks[15], (DEPTH, CROSS_HEADS * CROSS_DIM, D_MODEL), CROSS_HEADS * CROSS_DIM),
        "g_ffn2": gain(ks[16], (DEPTH, D_MODEL)),
        "w_ffn2_in": nrm(ks[17], (DEPTH, D_MODEL, 2 * D_FF), D_MODEL),
        "w_ffn2_out": nrm(ks[18], (DEPTH, D_FF, D_MODEL), D_FF),
        "g_final": gain(ks[19], (D_MODEL,)),
    }


def reference(x, mem, positions, g_ffn1, w_ffn1_in, w_ffn1_out, g_mix, w_mix_in, w_pool,
              pool_scale, w_mix_out, g_cross, g_mem, w_cross_q, w_cross_kv, w_cross_o,
              g_ffn2, w_ffn2_in, w_ffn2_out, g_final):
    cos, sin = rope_tables(positions)
    h = x
    for l in range(DEPTH):
        h = h + 0.5 * swiglu(rmsnorm(h, g_ffn1[l]), w_ffn1_in[l], w_ffn1_out[l])
        h = h + hybrid_mixer(rmsnorm(h, g_mix[l]), w_mix_in[l], w_pool[l], pool_scale[l],
                             w_mix_out[l], cos, sin)
        h = h + memory_cross_attention(rmsnorm(h, g_cross[l]), rmsnorm(mem, g_mem[l]),
                                       w_cross_q[l], w_cross_kv[l], w_cross_o[l])
        h = h + 0.5 * swiglu(rmsnorm(h, g_ffn2[l]), w_ffn2_in[l], w_ffn2_out[l])
    return rmsnorm(h, g_final)
```

```python
import functools
import math

import jax
import jax.numpy as jnp
from jax import lax
from jax.experimental import pallas as pl
from jax.experimental.pallas import tpu as pltpu

F32 = jnp.float32
BF16 = jnp.bfloat16

D_MODEL = 4096
HEAD_DIM = 128
ATTN_WIDTH = D_MODEL // 2
N_ATTN_HEADS = ATTN_WIDTH // HEAD_DIM
DILATED_CONFIGS = ((128, 1), (512, 4), (2048, 16))
N_DIL = len(DILATED_CONFIGS)
BLOCK = 128
POOL_WIDTH = D_MODEL - ATTN_WIDTH
POOL_WINDOWS = (2, 4, 8, 16)
POOL_GROUP = POOL_WIDTH // len(POOL_WINDOWS)
ATTN_IN = N_DIL * 3 * ATTN_WIDTH
MIX_IN = ATTN_IN + POOL_WIDTH
D_FF = ((8 * D_MODEL // 3 + 255) // 256) * 256
CROSS_HEADS = 4
CROSS_DIM = 128
CROSS_WIDTH = CROSS_HEADS * CROSS_DIM
ROPE_THETA = 10000.0
EPS = 1e-6
NEG_INF = -1e30
LOG2E = math.log2(math.e)

V7X_LANES = 128
V7X_VMEM_BYTES = 64 * 1024 * 1024
V7X_VMEM_REQUEST_CAP = V7X_VMEM_BYTES - 6 * 1024 * 1024

POOL_HALO = max(POOL_WINDOWS)
MIX_COL_BLOCK = ATTN_WIDTH
N_MIX_COL_BLOCKS = MIX_IN // MIX_COL_BLOCK
POOL_COL_BLOCK = ATTN_IN // MIX_COL_BLOCK


def _params(n_grid, vmem_bytes):
    return pltpu.CompilerParams(
        dimension_semantics=("arbitrary",) * n_grid,
        vmem_limit_bytes=int(min(vmem_bytes, V7X_VMEM_REQUEST_CAP)),
    )


def _nbytes(shape, dtype):
    return math.prod(shape) * jnp.dtype(dtype).itemsize


def _resident(block_shape, index_map):
    return pl.BlockSpec(block_shape, index_map, pipeline_mode=pl.Buffered(1))


def _dot(a, b):
    return jnp.dot(a, b, preferred_element_type=F32)


def _dot_nt(a, b):
    return lax.dot_general(a, b, (((1,), (1,)), ((), ())), preferred_element_type=F32)


def _rmsnorm_kernel(x_ref, g_ref, o_ref):
    x = x_ref[...]
    ms = jnp.mean(x * x, axis=-1, keepdims=True)
    o_ref[...] = (x * lax.rsqrt(ms + EPS) * g_ref[...]).astype(o_ref.dtype)


def _rmsnorm(x, g, out_dtype, tm=256):
    m, d = x.shape
    vmem = 2 * tm * d * (4 + jnp.dtype(out_dtype).itemsize) + 4 * tm * d * 4
    return pl.pallas_call(
        _rmsnorm_kernel,
        grid=(m // tm,),
        in_specs=[pl.BlockSpec((tm, d), lambda i: (i, 0)),
                  pl.BlockSpec((1, d), lambda i: (0, 0))],
        out_specs=pl.BlockSpec((tm, d), lambda i: (i, 0)),
        out_shape=jax.ShapeDtypeStruct((m, d), out_dtype),
        compiler_params=_params(1, vmem),
        name="rmsnorm",
    )(x, g.reshape(1, d))


def _rope_kernel(pos_ref, inv_ref, sign_ref, cos_ref, sin_ref):
    ang = pos_ref[...].astype(F32) * inv_ref[...]
    cos_ref[...] = jnp.cos(ang)
    sin_ref[...] = jnp.sin(ang) * sign_ref[...]


def _rope_tables(positions, tm=1024):
    m = positions.size
    half = HEAD_DIM // 2
    inv = 1.0 / (ROPE_THETA ** (jnp.arange(0, HEAD_DIM, 2, dtype=F32) / HEAD_DIM))
    inv = jnp.concatenate([inv, inv]).reshape(1, HEAD_DIM)
    sign = jnp.concatenate([-jnp.ones((half,), F32), jnp.ones((half,), F32)]).reshape(1, HEAD_DIM)
    row = pl.BlockSpec((1, HEAD_DIM), lambda i: (0, 0))
    tab = pl.BlockSpec((tm, HEAD_DIM), lambda i: (i, 0))
    return pl.pallas_call(
        _rope_kernel,
        grid=(m // tm,),
        in_specs=[pl.BlockSpec((tm, 1), lambda i: (i, 0)), row, row],
        out_specs=[tab, tab],
        out_shape=[jax.ShapeDtypeStruct((m, HEAD_DIM), F32)] * 2,
        compiler_params=_params(1, 16 * tm * HEAD_DIM * 4),
        name="rope_tables",
    )(positions.reshape(m, 1), inv, sign)


def _ffn_in_kernel(u_ref, wa_ref, wb_ref, o_ref):
    u = u_ref[...]
    a = _dot(u, wa_ref[...].astype(BF16))
    b = _dot(u, wb_ref[...].astype(BF16))
    o_ref[...] = (a * jax.nn.sigmoid(a) * b).astype(o_ref.dtype)


def _ffn_in(u, w_in, tm=1024, tf=256):
    m, d = u.shape
    nf = D_FF // tf
    vmem = (_nbytes((tm, d), BF16) + 2 * 2 * _nbytes((d, tf), F32)
            + 2 * _nbytes((d, tf), BF16) + 2 * _nbytes((tm, tf), BF16)
            + 4 * _nbytes((tm, tf), F32))
    return pl.pallas_call(
        _ffn_in_kernel,
        grid=(m // tm, nf),
        in_specs=[_resident((tm, d), lambda i, j: (i, 0)),
                  pl.BlockSpec((d, tf), lambda i, j: (0, j)),
                  pl.BlockSpec((d, tf), lambda i, j: (0, j + nf))],
        out_specs=pl.BlockSpec((tm, tf), lambda i, j: (i, j)),
        out_shape=jax.ShapeDtypeStruct((m, D_FF), BF16),
        compiler_params=_params(2, vmem),
        name="ffn_in",
    )(u, w_in, w_in)


def _proj_res_kernel(lhs_ref, w_ref, res_ref, o_ref, *, scale):
    acc = _dot(lhs_ref[...], w_ref[...].astype(BF16))
    o_ref[...] = res_ref[...] + scale * acc


def _proj_res(lhs, w, res, scale, tm, tn):
    m, k = lhs.shape
    n = w.shape[1]
    vmem = (_nbytes((tm, k), BF16) + 2 * _nbytes((k, tn), F32) + _nbytes((k, tn), BF16)
            + 6 * _nbytes((tm, tn), F32))
    return pl.pallas_call(
        functools.partial(_proj_res_kernel, scale=scale),
        grid=(m // tm, n // tn),
        in_specs=[_resident((tm, k), lambda i, j: (i, 0)),
                  pl.BlockSpec((k, tn), lambda i, j: (0, j)),
                  pl.BlockSpec((tm, tn), lambda i, j: (i, j))],
        out_specs=pl.BlockSpec((tm, tn), lambda i, j: (i, j)),
        out_shape=jax.ShapeDtypeStruct((m, n), F32),
        compiler_params=_params(2, vmem),
        name="proj_residual",
    )(lhs, w, res)


def _proj_kernel(lhs_ref, w_ref, o_ref):
    o_ref[...] = _dot(lhs_ref[...], w_ref[...].astype(BF16)).astype(o_ref.dtype)


def _proj(lhs, w, tm, tn):
    m, k = lhs.shape
    n = w.shape[1]
    vmem = (_nbytes((tm, k), BF16) + 2 * _nbytes((k, tn), F32) + _nbytes((k, tn), BF16)
            + 4 * _nbytes((tm, tn), F32))
    return pl.pallas_call(
        _proj_kernel,
        grid=(m // tm, n // tn),
        in_specs=[_resident((tm, k), lambda i, j: (i, 0)),
                  pl.BlockSpec((k, tn), lambda i, j: (0, j))],
        out_specs=pl.BlockSpec((tm, tn), lambda i, j: (i, j)),
        out_shape=jax.ShapeDtypeStruct((m, n), BF16),
        compiler_params=_params(2, vmem),
        name="proj",
    )(lhs, w)


def _mix_in_kernel(u_ref, w_ref, cos_ref, sin_ref, o_ref, *, tn):
    acc = _dot(u_ref[...], w_ref[...].astype(BF16))
    col_block = (pl.program_id(1) * tn) // MIX_COL_BLOCK
    is_rope = jnp.logical_and(col_block < POOL_COL_BLOCK, col_block % 3 != 2)

    @pl.when(is_rope)
    def _():
        cos = cos_ref[...]
        sin = sin_ref[...]
        for h in range(tn // HEAD_DIM):
            sl = slice(h * HEAD_DIM, (h + 1) * HEAD_DIM)
            t = acc[:, sl]
            rot = pltpu.roll(t, HEAD_DIM // 2, axis=1)
            o_ref[:, sl] = (t * cos + rot * sin).astype(o_ref.dtype)

    @pl.when(jnp.logical_not(is_rope))
    def _():
        o_ref[...] = acc.astype(o_ref.dtype)


def _mix_in(u, w, cos, sin, tm=1024, tn=512):
    m, k = u.shape
    n = w.shape[1]
    vmem = (_nbytes((tm, k), BF16) + 2 * _nbytes((k, tn), F32) + _nbytes((k, tn), BF16)
            + 4 * _nbytes((tm, tn), F32) + 4 * _nbytes((tm, HEAD_DIM), F32))
    return pl.pallas_call(
        functools.partial(_mix_in_kernel, tn=tn),
        grid=(m // tm, n // tn),
        in_specs=[_resident((tm, k), lambda i, j: (i, 0)),
                  pl.BlockSpec((k, tn), lambda i, j: (0, j)),
                  pl.BlockSpec((tm, HEAD_DIM), lambda i, j: (i, 0)),
                  pl.BlockSpec((tm, HEAD_DIM), lambda i, j: (i, 0))],
        out_specs=pl.BlockSpec((tm, tn), lambda i, j: (i, j)),
        out_shape=jax.ShapeDtypeStruct((m, n), BF16),
        compiler_params=_params(2, vmem),
        name="mix_in",
    )(u, w, cos, sin)


def _dilated_attn_kernel(q_ref, kp_ref, kc_ref, vp_ref, vc_ref, o_ref, lse_ref):
    qi = lax.broadcasted_iota(jnp.int32, (BLOCK, BLOCK), 0)
    kj = lax.broadcasted_iota(jnp.int32, (BLOCK, BLOCK), 1)
    mask_prev = jnp.logical_and(kj >= qi, pl.program_id(2) > 0)
    mask_cur = kj <= qi
    scale = HEAD_DIM ** -0.5
    lane = lax.broadcasted_iota(jnp.int32, (BLOCK, V7X_LANES), 1)
    lse_tile = jnp.zeros((BLOCK, V7X_LANES), F32)
    for h in range(N_ATTN_HEADS):
        sl = slice(h * HEAD_DIM, (h + 1) * HEAD_DIM)
        q = q_ref[:, sl]
        sp = jnp.where(mask_prev, _dot_nt(q, kp_ref[:, sl]), NEG_INF)
        sc = jnp.where(mask_cur, _dot_nt(q, kc_ref[:, sl]), NEG_INF)
        m = jnp.maximum(jnp.max(sp, axis=1, keepdims=True), jnp.max(sc, axis=1, keepdims=True))
        pp = jnp.exp2((sp - m) * (scale * LOG2E))
        pc = jnp.exp2((sc - m) * (scale * LOG2E))
        l = jnp.sum(pp, axis=1, keepdims=True) + jnp.sum(pc, axis=1, keepdims=True)
        o = _dot(pp.astype(BF16), vp_ref[:, sl]) + _dot(pc.astype(BF16), vc_ref[:, sl])
        o_ref[:, sl] = (o / l).astype(o_ref.dtype)
        lse_tile = jnp.where(lane == h, m * scale + jnp.log(l), lse_tile)
    lse_ref[...] = lse_tile


def _dilated_attention(z, group, dil):
    b, s, _ = z.shape
    sub_len = s // dil
    zv = z.reshape(b, sub_len, dil * MIX_IN)
    q_col, k_col, v_col = (3 * group + c for c in range(3))

    def cur(col):
        return pl.BlockSpec((None, BLOCK, MIX_COL_BLOCK),
                            lambda bi, r, t: (bi, t, r * N_MIX_COL_BLOCKS + col))

    def prev(col):
        return pl.BlockSpec((None, BLOCK, MIX_COL_BLOCK),
                            lambda bi, r, t: (bi, jnp.maximum(t - 1, 0), r * N_MIX_COL_BLOCKS + col))

    blk = _nbytes((BLOCK, MIX_COL_BLOCK), BF16)
    o, lse = pl.pallas_call(
        _dilated_attn_kernel,
        grid=(b, dil, sub_len // BLOCK),
        in_specs=[cur(q_col), prev(k_col), cur(k_col), prev(v_col), cur(v_col)],
        out_specs=[pl.BlockSpec((None, BLOCK, ATTN_WIDTH), lambda bi, r, t: (bi, t, r)),
                   pl.BlockSpec((None, BLOCK, V7X_LANES), lambda bi, r, t: (bi, t, r))],
        out_shape=[jax.ShapeDtypeStruct((b, sub_len, dil * ATTN_WIDTH), BF16),
                   jax.ShapeDtypeStruct((b, sub_len, dil * V7X_LANES), F32)],
        compiler_params=_params(3, 16 * blk + (8 << 20)),
        name=f"dilated_attn_{dil}",
    )(zv, zv, zv, zv, zv)
    return o.reshape(b * s, ATTN_WIDTH), lse.reshape(b * s, V7X_LANES)


def _combine_kernel(o0_ref, o1_ref, o2_ref, l0_ref, l1_ref, l2_ref, out_ref):
    l0, l1, l2 = l0_ref[...], l1_ref[...], l2_ref[...]
    m = jnp.maximum(jnp.maximum(l0, l1), l2)
    e0, e1, e2 = jnp.exp(l0 - m), jnp.exp(l1 - m), jnp.exp(l2 - m)
    den = e0 + e1 + e2
    w0, w1, w2 = e0 / den, e1 / den, e2 / den
    for h in range(N_ATTN_HEADS):
        sl = slice(h * HEAD_DIM, (h + 1) * HEAD_DIM)
        acc = (w0[:, h:h + 1] * o0_ref[:, sl].astype(F32)
               + w1[:, h:h + 1] * o1_ref[:, sl].astype(F32)
               + w2[:, h:h + 1] * o2_ref[:, sl].astype(F32))
        out_ref[:, sl] = acc.astype(out_ref.dtype)


def _combine(outs, lses, tq=512):
    m = outs[0].shape[0]
    o_spec = pl.BlockSpec((tq, ATTN_WIDTH), lambda i: (i, 0))
    l_spec = pl.BlockSpec((tq, V7X_LANES), lambda i: (i, 0))
    vmem = 8 * _nbytes((tq, ATTN_WIDTH), BF16) + 6 * _nbytes((tq, V7X_LANES), F32) + (8 << 20)
    return pl.pallas_call(
        _combine_kernel,
        grid=(m // tq,),
        in_specs=[o_spec] * 3 + [l_spec] * 3,
        out_specs=o_spec,
        out_shape=jax.ShapeDtypeStruct((m, ATTN_WIDTH), BF16),
        compiler_params=_params(1, vmem),
        name="attn_combine",
    )(*outs, *lses)


def _pool_kernel(zm_ref, zh_ref, wp_ref, scale_ref, o_ref, *, tp):
    t = pl.program_id(1)
    pos = t * tp + lax.broadcasted_iota(jnp.int32, (tp, 1), 0)
    for g, win in enumerate(POOL_WINDOWS):
        sl = slice(g * POOL_GROUP, (g + 1) * POOL_GROUP)
        main = zm_ref[:, sl].astype(F32)
        halo = jnp.where(t > 0, zh_ref[:, sl].astype(F32), 0.0)
        s = jnp.concatenate([halo, main], axis=0)
        shift = 1
        while shift < win:
            s = s + pltpu.roll(s, shift, axis=0)
            shift *= 2
        cnt = jnp.minimum(pos + 1, win).astype(F32)
        y = s[POOL_HALO:] / cnt - main
        yo = _dot(y.astype(BF16), wp_ref[g].astype(BF16))
        o_ref[:, sl] = (yo * scale_ref[:, sl]).astype(o_ref.dtype)


def _pool_mixer(z, w_pool, pool_scale, tp=512):
    b, s, _ = z.shape
    halo_blocks = tp // POOL_HALO
    vmem = (4 * _nbytes((tp, POOL_WIDTH), BF16) + 2 * _nbytes(w_pool.shape, F32)
            + 12 * _nbytes((tp, POOL_GROUP), F32) + (8 << 20))
    out = pl.pallas_call(
        functools.partial(_pool_kernel, tp=tp),
        grid=(b, s // tp),
        in_specs=[pl.BlockSpec((None, tp, POOL_WIDTH), lambda bi, t: (bi, t, POOL_COL_BLOCK)),
                  pl.BlockSpec((None, POOL_HALO, POOL_WIDTH),
                               lambda bi, t: (bi, jnp.maximum(t * halo_blocks - 1, 0), POOL_COL_BLOCK)),
                  pl.BlockSpec(w_pool.shape, lambda bi, t: (0, 0, 0)),
                  pl.BlockSpec((1, POOL_WIDTH), lambda bi, t: (0, 0))],
        out_specs=pl.BlockSpec((None, tp, POOL_WIDTH), lambda bi, t: (bi, t, 0)),
        out_shape=jax.ShapeDtypeStruct((b, s, POOL_WIDTH), BF16),
        compiler_params=_params(2, vmem),
        name="pool_mixer",
    )(z, z, w_pool, pool_scale.reshape(1, POOL_WIDTH))
    return out.reshape(b * s, POOL_WIDTH)


def _mix_out_kernel(a_ref, p_ref, wa_ref, wp_ref, res_ref, o_ref):
    acc = _dot(a_ref[...], wa_ref[...].astype(BF16)) + _dot(p_ref[...], wp_ref[...].astype(BF16))
    o_ref[...] = res_ref[...] + acc


def _mix_out(o_attn, o_pool, w, res, tm=1024, tn=512):
    m = o_attn.shape[0]
    n = w.shape[1]
    ka, kp = o_attn.shape[1], o_pool.shape[1]
    assert ka == kp
    vmem = (_nbytes((tm, ka + kp), BF16) + 2 * _nbytes((ka + kp, tn), F32)
            + _nbytes((ka + kp, tn), BF16) + 6 * _nbytes((tm, tn), F32))
    return pl.pallas_call(
        _mix_out_kernel,
        grid=(m // tm, n // tn),
        in_specs=[_resident((tm, ka), lambda i, j: (i, 0)),
                  _resident((tm, kp), lambda i, j: (i, 0)),
                  pl.BlockSpec((ka, tn), lambda i, j: (0, j)),
                  pl.BlockSpec((kp, tn), lambda i, j: (1, j)),
                  pl.BlockSpec((tm, tn), lambda i, j: (i, j))],
        out_specs=pl.BlockSpec((tm, tn), lambda i, j: (i, j)),
        out_shape=jax.ShapeDtypeStruct((m, n), F32),
        compiler_params=_params(2, vmem),
        name="mix_out",
    )(o_attn, o_pool, w, w, res)


def _cross_kernel(q_ref, kv_ref, wo_ref, res_ref, o_ref):
    scale = CROSS_DIM ** -0.5
    heads = []
    for h in range(CROSS_HEADS):
        sl = slice(h * CROSS_DIM, (h + 1) * CROSS_DIM)
        k = kv_ref[:, sl]
        v = kv_ref[:, CROSS_WIDTH + h * CROSS_DIM:CROSS_WIDTH + (h + 1) * CROSS_DIM]
        s = _dot_nt(q_ref[:, sl], k)
        m = jnp.max(s, axis=1, keepdims=True)
        p = jnp.exp2((s - m) * (scale * LOG2E))
        l = jnp.sum(p, axis=1, keepdims=True)
        heads.append((_dot(p.astype(BF16), v) / l).astype(BF16))
    o = jnp.concatenate(heads, axis=1)
    o_ref[...] = res_ref[...] + _dot(o, wo_ref[...].astype(BF16))


def _cross_attention(q, kv, w_o, res, batch, ts=256):
    m, d = res.shape
    mem_len = kv.shape[0] // batch
    steps_per_batch = (m // batch) // ts
    vmem = (4 * _nbytes((ts, d), F32) + 2 * _nbytes(w_o.shape, F32) + _nbytes(w_o.shape, BF16)
            + 2 * _nbytes((ts, d), F32) + (8 << 20))
    return pl.pallas_call(
        _cross_kernel,
        grid=(m // ts,),
        in_specs=[pl.BlockSpec((ts, CROSS_WIDTH), lambda i: (i, 0)),
                  pl.BlockSpec((mem_len, 2 * CROSS_WIDTH), lambda i: (i // steps_per_batch, 0)),
                  pl.BlockSpec(w_o.shape, lambda i: (0, 0)),
                  pl.BlockSpec((ts, d), lambda i: (i, 0))],
        out_specs=pl.BlockSpec((ts, d), lambda i: (i, 0)),
        out_shape=jax.ShapeDtypeStruct((m, d), F32),
        compiler_params=_params(1, vmem),
        name="cross_attention",
    )(q, kv, w_o, res)


def _swiglu_block(h, g, w_in, w_out):
    u = _rmsnorm(h, g, BF16)
    gate = _ffn_in(u, w_in)
    return _proj_res(gate, w_out, h, 0.5, tm=1024, tn=256)


def kernel(x, mem, positions, g_ffn1, w_ffn1_in, w_ffn1_out, g_mix, w_mix_in, w_pool, pool_scale, w_mix_out, g_cross, g_mem, w_cross_q, w_cross_kv, w_cross_o, g_ffn2, w_ffn2_in, w_ffn2_out, g_final):
    batch, seq, d = x.shape
    m = batch * seq
    depth = g_ffn1.shape[0]
    cos, sin = _rope_tables(positions)
    h = x.reshape(m, d)
    mem2 = mem.reshape(batch * mem.shape[1], d)
    for l in range(depth):
        h = _swiglu_block(h, g_ffn1[l], w_ffn1_in[l], w_ffn1_out[l])

        u = _rmsnorm(h, g_mix[l], BF16)
        z = _mix_in(u, w_mix_in[l], cos, sin)
        z3 = z.reshape(batch, seq, MIX_IN)
        outs, lses = [], []
        for group, (_, dil) in enumerate(DILATED_CONFIGS):
            o, lse = _dilated_attention(z3, group, dil)
            outs.append(o)
            lses.append(lse)
        o_attn = _combine(outs, lses)
        o_pool = _pool_mixer(z3, w_pool[l], pool_scale[l])
        h = _mix_out(o_attn, o_pool, w_mix_out[l], h)

        u = _rmsnorm(h, g_cross[l], BF16)
        um = _rmsnorm(mem2, g_mem[l], BF16)
        q = _proj(u, w_cross_q[l], tm=1024, tn=CROSS_WIDTH)
        kv = _proj(um, w_cross_kv[l], tm=um.shape[0], tn=CROSS_WIDTH)
        h = _cross_attention(q, kv, w_cross_o[l], h, batch)

        h = _swiglu_block(h, g_ffn2[l], w_ffn2_in[l], w_ffn2_out[l])
    return _rmsnorm(h, g_final, F32).reshape(batch, seq, d)
```

```python
import functools
import math

import jax
import jax.numpy as jnp
from jax import lax
from jax.experimental import pallas as pl
from jax.experimental.pallas import tpu as pltpu

F32 = jnp.float32
BF16 = jnp.bfloat16

D_MODEL = 4096
HEAD_DIM = 128
ATTN_WIDTH = D_MODEL // 2
N_ATTN_HEADS = ATTN_WIDTH // HEAD_DIM
DILATED_CONFIGS = ((128, 1), (512, 4), (2048, 16))
N_DIL = len(DILATED_CONFIGS)
BLOCK = 128
POOL_WIDTH = D_MODEL - ATTN_WIDTH
POOL_WINDOWS = (2, 4, 8, 16)
POOL_GROUP = POOL_WIDTH // len(POOL_WINDOWS)
QKV_WIDTH = 3 * ATTN_WIDTH
ATTN_IN = N_DIL * QKV_WIDTH
D_FF = ((8 * D_MODEL // 3 + 255) // 256) * 256
CROSS_HEADS = 4
CROSS_DIM = 128
CROSS_WIDTH = CROSS_HEADS * CROSS_DIM
ROPE_THETA = 10000.0
EPS = 1e-6
NEG_INF = -1e30
LOG2E = math.log2(math.e)

V7X_LANES = 128
V7X_VMEM_BYTES = 64 * 1024 * 1024
V7X_VMEM_REQUEST_CAP = V7X_VMEM_BYTES - 6 * 1024 * 1024

POOL_HALO = max(POOL_WINDOWS)
MAX_DIL = max(d for _, d in DILATED_CONFIGS)
ATTN_TILE = BLOCK * MAX_DIL
ATTN_HEADS_PER_STEP = 4
LSE_WIDTH = (N_ATTN_HEADS // ATTN_HEADS_PER_STEP) * V7X_LANES


def _params(n_grid, vmem_bytes):
    return pltpu.CompilerParams(
        dimension_semantics=("arbitrary",) * n_grid,
        vmem_limit_bytes=int(min(vmem_bytes, V7X_VMEM_REQUEST_CAP)),
    )


def _nbytes(shape, dtype):
    return math.prod(shape) * jnp.dtype(dtype).itemsize


def _resident(block_shape, index_map):
    return pl.BlockSpec(block_shape, index_map, pipeline_mode=pl.Buffered(1))


def _dot(a, b):
    return jnp.dot(a, b, preferred_element_type=F32)


def _dot_nt(a, b):
    return lax.dot_general(a, b, (((1,), (1,)), ((), ())), preferred_element_type=F32)


def _rmsnorm_kernel(x_ref, g_ref, o_ref):
    x = x_ref[...]
    ms = jnp.mean(x * x, axis=-1, keepdims=True)
    o_ref[...] = (x * lax.rsqrt(ms + EPS) * g_ref[...]).astype(o_ref.dtype)


def _rmsnorm(x, g, out_dtype, tm=256):
    m, d = x.shape
    vmem = 2 * tm * d * (4 + jnp.dtype(out_dtype).itemsize) + 4 * tm * d * 4
    return pl.pallas_call(
        _rmsnorm_kernel,
        grid=(m // tm,),
        in_specs=[pl.BlockSpec((tm, d), lambda i: (i, 0)),
                  pl.BlockSpec((1, d), lambda i: (0, 0))],
        out_specs=pl.BlockSpec((tm, d), lambda i: (i, 0)),
        out_shape=jax.ShapeDtypeStruct((m, d), out_dtype),
        compiler_params=_params(1, vmem),
        name="rmsnorm",
    )(x, g.reshape(1, d))


def _rope_kernel(pos_ref, inv_ref, sign_ref, cos_ref, sin_ref):
    ang = pos_ref[...].astype(F32) * inv_ref[...]
    cos_ref[...] = jnp.cos(ang)
    sin_ref[...] = jnp.sin(ang) * sign_ref[...]


def _rope_tables(positions, tm=1024):
    m = positions.size
    half = HEAD_DIM // 2
    inv = 1.0 / (ROPE_THETA ** (jnp.arange(0, HEAD_DIM, 2, dtype=F32) / HEAD_DIM))
    inv = jnp.concatenate([inv, inv]).reshape(1, HEAD_DIM)
    sign = jnp.concatenate([-jnp.ones((half,), F32), jnp.ones((half,), F32)]).reshape(1, HEAD_DIM)
    row = pl.BlockSpec((1, HEAD_DIM), lambda i: (0, 0))
    tab = pl.BlockSpec((tm, HEAD_DIM), lambda i: (i, 0))
    return pl.pallas_call(
        _rope_kernel,
        grid=(m // tm,),
        in_specs=[pl.BlockSpec((tm, 1), lambda i: (i, 0)), row, row],
        out_specs=[tab, tab],
        out_shape=[jax.ShapeDtypeStruct((m, HEAD_DIM), F32)] * 2,
        compiler_params=_params(1, 16 * tm * HEAD_DIM * 4),
        name="rope_tables",
    )(positions.reshape(m, 1), inv, sign)


def _ffn_in_kernel(u_ref, wa_ref, wb_ref, o_ref):
    u = u_ref[...]
    a = _dot(u, wa_ref[...].astype(BF16))
    b = _dot(u, wb_ref[...].astype(BF16))
    o_ref[...] = (a * jax.nn.sigmoid(a) * b).astype(o_ref.dtype)


def _ffn_in(u, w_in, tm=1024, tf=256):
    m, d = u.shape
    nf = D_FF // tf
    vmem = (_nbytes((tm, d), BF16) + 2 * 2 * _nbytes((d, tf), F32)
            + 2 * _nbytes((d, tf), BF16) + 2 * _nbytes((tm, tf), BF16)
            + 4 * _nbytes((tm, tf), F32))
    return pl.pallas_call(
        _ffn_in_kernel,
        grid=(m // tm, nf),
        in_specs=[_resident((tm, d), lambda i, j: (i, 0)),
                  pl.BlockSpec((d, tf), lambda i, j: (0, j)),
                  pl.BlockSpec((d, tf), lambda i, j: (0, j + nf))],
        out_specs=pl.BlockSpec((tm, tf), lambda i, j: (i, j)),
        out_shape=jax.ShapeDtypeStruct((m, D_FF), BF16),
        compiler_params=_params(2, vmem),
        name="ffn_in",
    )(u, w_in, w_in)


def _proj_res_kernel(lhs_ref, w_ref, res_ref, o_ref, *, scale):
    acc = _dot(lhs_ref[...], w_ref[...].astype(BF16))
    o_ref[...] = res_ref[...] + scale * acc


def _proj_res(lhs, w, res, scale, tm, tn):
    m, k = lhs.shape
    n = w.shape[1]
    vmem = (_nbytes((tm, k), BF16) + 2 * _nbytes((k, tn), F32) + _nbytes((k, tn), BF16)
            + 6 * _nbytes((tm, tn), F32))
    return pl.pallas_call(
        functools.partial(_proj_res_kernel, scale=scale),
        grid=(m // tm, n // tn),
        in_specs=[_resident((tm, k), lambda i, j: (i, 0)),
                  pl.BlockSpec((k, tn), lambda i, j: (0, j)),
                  pl.BlockSpec((tm, tn), lambda i, j: (i, j))],
        out_specs=pl.BlockSpec((tm, tn), lambda i, j: (i, j)),
        out_shape=jax.ShapeDtypeStruct((m, n), F32),
        compiler_params=_params(2, vmem),
        name="proj_residual",
    )(lhs, w, res)


def _proj_kernel(lhs_ref, w_ref, o_ref):
    o_ref[...] = _dot(lhs_ref[...], w_ref[...].astype(BF16)).astype(o_ref.dtype)


def _proj(lhs, w, tm, tn, out_dtype=BF16, col0=0, n=None):
    m, k = lhs.shape
    n = w.shape[1] if n is None else n
    col_blk0 = col0 // tn
    vmem = (_nbytes((tm, k), BF16) + 2 * _nbytes((k, tn), F32) + _nbytes((k, tn), BF16)
            + 4 * _nbytes((tm, tn), F32))
    return pl.pallas_call(
        _proj_kernel,
        grid=(m // tm, n // tn),
        in_specs=[_resident((tm, k), lambda i, j: (i, 0)),
                  pl.BlockSpec((k, tn), lambda i, j: (0, j + col_blk0))],
        out_specs=pl.BlockSpec((tm, tn), lambda i, j: (i, j)),
        out_shape=jax.ShapeDtypeStruct((m, n), out_dtype),
        compiler_params=_params(2, vmem),
        name="proj",
    )(lhs, w)


def _mix_qkv_kernel(u_ref, w_ref, cos_ref, sin_ref, o_ref, scr_ref, *, tn, dil):
    tm = u_ref.shape[0]
    acc = _dot(u_ref[...], w_ref[...].astype(BF16))
    is_rope = (pl.program_id(1) * tn) // ATTN_WIDTH != 2
    slabs = [slice(h * HEAD_DIM, (h + 1) * HEAD_DIM) for h in range(tn // HEAD_DIM)]

    def emit(vals):
        if dil == 1:
            for sl, v in zip(slabs, vals):
                o_ref[0, :, sl] = v.astype(o_ref.dtype)
            return
        for h, v in enumerate(vals):
            scr_ref[h] = v
        for h, sl in enumerate(slabs):
            for r in range(dil):
                o_ref[r, :, sl] = scr_ref[h, pl.ds(r, tm // dil, stride=dil), :].astype(o_ref.dtype)

    @pl.when(is_rope)
    def _():
        cos = cos_ref[...]
        sin = sin_ref[...]
        emit([acc[:, sl] * cos + pltpu.roll(acc[:, sl], HEAD_DIM // 2, axis=1) * sin for sl in slabs])

    @pl.when(jnp.logical_not(is_rope))
    def _():
        emit([acc[:, sl] for sl in slabs])


def _mix_qkv(u, w, cos, sin, group, dil, batch, tm=1024, tn=512):
    m, k = u.shape
    seq = m // batch
    tiles_per_seq = seq // tm
    col_blk0 = group * QKV_WIDTH // tn
    vmem = (_nbytes((tm, k), BF16) + 2 * _nbytes((k, tn), F32) + _nbytes((k, tn), BF16)
            + 6 * _nbytes((tm, tn), F32) + 4 * _nbytes((tm, HEAD_DIM), F32))
    return pl.pallas_call(
        functools.partial(_mix_qkv_kernel, tn=tn, dil=dil),
        grid=(m // tm, QKV_WIDTH // tn),
        in_specs=[_resident((tm, k), lambda i, j: (i, 0)),
                  pl.BlockSpec((k, tn), lambda i, j: (0, j + col_blk0)),
                  pl.BlockSpec((tm, HEAD_DIM), lambda i, j: (i, 0)),
                  pl.BlockSpec((tm, HEAD_DIM), lambda i, j: (i, 0))],
        out_specs=pl.BlockSpec((None, dil, tm // dil, tn),
                               lambda i, j: (i // tiles_per_seq, 0, i % tiles_per_seq, j)),
        out_shape=jax.ShapeDtypeStruct((batch, dil, seq // dil, QKV_WIDTH), BF16),
        scratch_shapes=[pltpu.VMEM((tn // HEAD_DIM, tm, HEAD_DIM), F32)],
        compiler_params=_params(2, vmem),
        name=f"mix_qkv_{dil}",
    )(u, w, cos, sin)


def _dilated_attn_kernel(q_ref, kp_ref, kc_ref, vp_ref, vc_ref, o_ref, lse_ref,
                         o_scr, lse_scr, *, dil):
    nq = MAX_DIL // dil
    nblk = dil * nq
    qi = lax.broadcasted_iota(jnp.int32, (nblk, BLOCK, BLOCK), 1)
    kj = lax.broadcasted_iota(jnp.int32, (nblk, BLOCK, BLOCK), 2)
    blk = lax.broadcasted_iota(jnp.int32, (nblk, BLOCK, BLOCK), 0)
    has_prev = jnp.logical_or(pl.program_id(1) > 0, blk % nq != 0)
    mask_prev = jnp.logical_and(kj >= qi, has_prev)
    mask_cur = kj <= qi
    scale = HEAD_DIM ** -0.5
    lane = lax.broadcasted_iota(jnp.int32, (nblk, BLOCK, V7X_LANES), 2)
    lse_tile = jnp.zeros((nblk, BLOCK, V7X_LANES), F32)

    def blocks(ref, sl):
        return ref[:, :, sl].reshape(nblk, BLOCK, HEAD_DIM)

    def prev_blocks(first_ref, cur, sl):
        first = first_ref[:, :, sl]
        if nq == 1:
            return first
        cur4 = cur.reshape(dil, nq, BLOCK, HEAD_DIM)
        return jnp.concatenate([first[:, None], cur4[:, :-1]], axis=1).reshape(nblk, BLOCK, HEAD_DIM)

    def token_rows(g):
        r, i = divmod(g, nq)
        return pl.ds(i * BLOCK * dil + r, BLOCK, stride=dil)

    for h in range(ATTN_HEADS_PER_STEP):
        sl = slice(h * HEAD_DIM, (h + 1) * HEAD_DIM)
        q = blocks(q_ref, sl)
        kc = blocks(kc_ref, sl)
        vc = blocks(vc_ref, sl)
        kp = prev_blocks(kp_ref, kc, sl)
        vp = prev_blocks(vp_ref, vc, sl)
        sp = jnp.einsum("gqd,gkd->gqk", q, kp, preferred_element_type=F32)
        sc = jnp.einsum("gqd,gkd->gqk", q, kc, preferred_element_type=F32)
        sp = jnp.where(mask_prev, sp, NEG_INF)
        sc = jnp.where(mask_cur, sc, NEG_INF)
        m = jnp.maximum(jnp.max(sp, axis=2, keepdims=True), jnp.max(sc, axis=2, keepdims=True))
        pp = jnp.exp2((sp - m) * (scale * LOG2E))
        pc = jnp.exp2((sc - m) * (scale * LOG2E))
        l = jnp.sum(pp, axis=2, keepdims=True) + jnp.sum(pc, axis=2, keepdims=True)
        o = (jnp.einsum("gqk,gkd->gqd", pp.astype(BF16), vp, preferred_element_type=F32)
             + jnp.einsum("gqk,gkd->gqd", pc.astype(BF16), vc, preferred_element_type=F32)) / l
        for g in range(nblk):
            o_scr[h, token_rows(g), :] = o[g]
        lse_tile = jnp.where(lane == h, m * scale + jnp.log(l), lse_tile)
    for g in range(nblk):
        lse_scr[token_rows(g), :] = lse_tile[g]
    for h in range(ATTN_HEADS_PER_STEP):
        o_ref[:, h * HEAD_DIM:(h + 1) * HEAD_DIM] = o_scr[h].astype(o_ref.dtype)
    lse_ref[...] = lse_scr[...]


def _dilated_attention(z, dil):
    b, _, sub_len, _ = z.shape
    seq = sub_len * dil
    rows = ATTN_TILE // dil
    tiles_per_seq = seq // ATTN_TILE
    width = ATTN_HEADS_PER_STEP * HEAD_DIM
    col_blocks = ATTN_WIDTH // width

    def cur(part):
        return pl.BlockSpec((None, dil, rows, width),
                            lambda bi, t, hg: (bi, 0, t, part * col_blocks + hg))

    def prev(part):
        return pl.BlockSpec((None, dil, BLOCK, width),
                            lambda bi, t, hg: (bi, 0, jnp.maximum(t * (rows // BLOCK) - 1, 0),
                                               part * col_blocks + hg))

    vmem = (6 * _nbytes((dil, rows, width), BF16) + 4 * _nbytes((dil, BLOCK, width), BF16)
            + 2 * _nbytes((ATTN_TILE, width), BF16) + 3 * _nbytes((ATTN_TILE, V7X_LANES), F32)
            + _nbytes((ATTN_TILE, width), F32) + 12 * _nbytes((ATTN_TILE, BLOCK), F32))
    o, lse = pl.pallas_call(
        functools.partial(_dilated_attn_kernel, dil=dil),
        grid=(b, tiles_per_seq, col_blocks),
        in_specs=[cur(0), prev(1), cur(1), prev(2), cur(2)],
        out_specs=[pl.BlockSpec((ATTN_TILE, width), lambda bi, t, hg: (bi * tiles_per_seq + t, hg)),
                   pl.BlockSpec((ATTN_TILE, V7X_LANES), lambda bi, t, hg: (bi * tiles_per_seq + t, hg))],
        out_shape=[jax.ShapeDtypeStruct((b * seq, ATTN_WIDTH), BF16),
                   jax.ShapeDtypeStruct((b * seq, LSE_WIDTH), F32)],
        scratch_shapes=[pltpu.VMEM((ATTN_HEADS_PER_STEP, ATTN_TILE, HEAD_DIM), F32),
                        pltpu.VMEM((ATTN_TILE, V7X_LANES), F32)],
        compiler_params=_params(3, vmem),
        name=f"dilated_attn_{dil}",
    )(z, z, z, z, z)
    return o, lse


def _combine_kernel(o0_ref, o1_ref, o2_ref, l0_ref, l1_ref, l2_ref, out_ref):
    l0, l1, l2 = l0_ref[...], l1_ref[...], l2_ref[...]
    m = jnp.maximum(jnp.maximum(l0, l1), l2)
    e0, e1, e2 = jnp.exp(l0 - m), jnp.exp(l1 - m), jnp.exp(l2 - m)
    den = e0 + e1 + e2
    w0, w1, w2 = e0 / den, e1 / den, e2 / den
    for h in range(N_ATTN_HEADS):
        sl = slice(h * HEAD_DIM, (h + 1) * HEAD_DIM)
        c = (h // ATTN_HEADS_PER_STEP) * V7X_LANES + h % ATTN_HEADS_PER_STEP
        acc = (w0[:, c:c + 1] * o0_ref[:, sl].astype(F32)
               + w1[:, c:c + 1] * o1_ref[:, sl].astype(F32)
               + w2[:, c:c + 1] * o2_ref[:, sl].astype(F32))
        out_ref[:, sl] = acc.astype(out_ref.dtype)


def _combine(outs, lses, tq=512):
    m = outs[0].shape[0]
    o_spec = pl.BlockSpec((tq, ATTN_WIDTH), lambda i: (i, 0))
    l_spec = pl.BlockSpec((tq, LSE_WIDTH), lambda i: (i, 0))
    vmem = 8 * _nbytes((tq, ATTN_WIDTH), BF16) + 16 * _nbytes((tq, LSE_WIDTH), F32) + (8 << 20)
    return pl.pallas_call(
        _combine_kernel,
        grid=(m // tq,),
        in_specs=[o_spec] * 3 + [l_spec] * 3,
        out_specs=o_spec,
        out_shape=jax.ShapeDtypeStruct((m, ATTN_WIDTH), BF16),
        compiler_params=_params(1, vmem),
        name="attn_combine",
    )(*outs, *lses)


def _pool_kernel(zm_ref, zh_ref, wp_ref, scale_ref, o_ref, *, tp):
    t = pl.program_id(1)
    pos = t * tp + lax.broadcasted_iota(jnp.int32, (tp, 1), 0)
    for g, win in enumerate(POOL_WINDOWS):
        sl = slice(g * POOL_GROUP, (g + 1) * POOL_GROUP)
        main = zm_ref[:, sl]
        halo = jnp.where(t > 0, zh_ref[:, sl], 0.0)
        s = jnp.concatenate([halo, main], axis=0)
        shift = 1
        while shift < win:
            s = s + pltpu.roll(s, shift, axis=0)
            shift *= 2
        cnt = jnp.minimum(pos + 1, win).astype(F32)
        y = s[POOL_HALO:] / cnt - main
        yo = _dot(y.astype(BF16), wp_ref[g].astype(BF16))
        o_ref[:, sl] = (yo * scale_ref[:, sl]).astype(o_ref.dtype)


def _pool_mixer(zp, w_pool, pool_scale, batch, tp=512):
    m = zp.shape[0]
    seq = m // batch
    tiles_per_seq = seq // tp
    halo_blocks = tp // POOL_HALO
    vmem = (6 * _nbytes((tp, POOL_WIDTH), F32) + 2 * _nbytes(w_pool.shape, F32)
            + 12 * _nbytes((tp, POOL_GROUP), F32) + (8 << 20))
    return pl.pallas_call(
        functools.partial(_pool_kernel, tp=tp),
        grid=(batch, tiles_per_seq),
        in_specs=[pl.BlockSpec((tp, POOL_WIDTH), lambda bi, t: (bi * tiles_per_seq + t, 0)),
                  pl.BlockSpec((POOL_HALO, POOL_WIDTH),
                               lambda bi, t: (jnp.maximum((bi * tiles_per_seq + t) * halo_blocks - 1, 0), 0)),
                  pl.BlockSpec(w_pool.shape, lambda bi, t: (0, 0, 0)),
                  pl.BlockSpec((1, POOL_WIDTH), lambda bi, t: (0, 0))],
        out_specs=pl.BlockSpec((tp, POOL_WIDTH), lambda bi, t: (bi * tiles_per_seq + t, 0)),
        out_shape=jax.ShapeDtypeStruct((m, POOL_WIDTH), BF16),
        compiler_params=_params(2, vmem),
        name="pool_mixer",
    )(zp, zp, w_pool, pool_scale.reshape(1, POOL_WIDTH))


def _mix_out_kernel(a_ref, p_ref, wa_ref, wp_ref, res_ref, o_ref):
    acc = _dot(a_ref[...], wa_ref[...].astype(BF16)) + _dot(p_ref[...], wp_ref[...].astype(BF16))
    o_ref[...] = res_ref[...] + acc


def _mix_out(o_attn, o_pool, w, res, tm=1024, tn=512):
    m = o_attn.shape[0]
    n = w.shape[1]
    ka, kp = o_attn.shape[1], o_pool.shape[1]
    assert ka == kp
    vmem = (_nbytes((tm, ka + kp), BF16) + 2 * _nbytes((ka + kp, tn), F32)
            + _nbytes((ka + kp, tn), BF16) + 6 * _nbytes((tm, tn), F32))
    return pl.pallas_call(
        _mix_out_kernel,
        grid=(m // tm, n // tn),
        in_specs=[_resident((tm, ka), lambda i, j: (i, 0)),
                  _resident((tm, kp), lambda i, j: (i, 0)),
                  pl.BlockSpec((ka, tn), lambda i, j: (0, j)),
                  pl.BlockSpec((kp, tn), lambda i, j: (1, j)),
                  pl.BlockSpec((tm, tn), lambda i, j: (i, j))],
        out_specs=pl.BlockSpec((tm, tn), lambda i, j: (i, j)),
        out_shape=jax.ShapeDtypeStruct((m, n), F32),
        compiler_params=_params(2, vmem),
        name="mix_out",
    )(o_attn, o_pool, w, w, res)


def _cross_kernel(q_ref, kv_ref, wo_ref, res_ref, o_ref):
    scale = CROSS_DIM ** -0.5
    heads = []
    for h in range(CROSS_HEADS):
        sl = slice(h * CROSS_DIM, (h + 1) * CROSS_DIM)
        k = kv_ref[:, sl]
        v = kv_ref[:, CROSS_WIDTH + h * CROSS_DIM:CROSS_WIDTH + (h + 1) * CROSS_DIM]
        s = _dot_nt(q_ref[:, sl], k)
        m = jnp.max(s, axis=1, keepdims=True)
        p = jnp.exp2((s - m) * (scale * LOG2E))
        l = jnp.sum(p, axis=1, keepdims=True)
        heads.append((_dot(p.astype(BF16), v) / l).astype(BF16))
    o = jnp.concatenate(heads, axis=1)
    o_ref[...] = res_ref[...] + _dot(o, wo_ref[...].astype(BF16))


def _cross_attention(q, kv, w_o, res, batch, ts=256):
    m, d = res.shape
    mem_len = kv.shape[0] // batch
    steps_per_batch = (m // batch) // ts
    vmem = (4 * _nbytes((ts, d), F32) + 2 * _nbytes(w_o.shape, F32) + _nbytes(w_o.shape, BF16)
            + 2 * _nbytes((ts, d), F32) + (8 << 20))
    return pl.pallas_call(
        _cross_kernel,
        grid=(m // ts,),
        in_specs=[pl.BlockSpec((ts, CROSS_WIDTH), lambda i: (i, 0)),
                  pl.BlockSpec((mem_len, 2 * CROSS_WIDTH), lambda i: (i // steps_per_batch, 0)),
                  pl.BlockSpec(w_o.shape, lambda i: (0, 0)),
                  pl.BlockSpec((ts, d), lambda i: (i, 0))],
        out_specs=pl.BlockSpec((ts, d), lambda i: (i, 0)),
        out_shape=jax.ShapeDtypeStruct((m, d), F32),
        compiler_params=_params(1, vmem),
        name="cross_attention",
    )(q, kv, w_o, res)


def _swiglu_block(h, g, w_in, w_out):
    u = _rmsnorm(h, g, BF16)
    gate = _ffn_in(u, w_in)
    return _proj_res(gate, w_out, h, 0.5, tm=1024, tn=256)


def kernel(x, mem, positions, g_ffn1, w_ffn1_in, w_ffn1_out, g_mix, w_mix_in, w_pool, pool_scale, w_mix_out, g_cross, g_mem, w_cross_q, w_cross_kv, w_cross_o, g_ffn2, w_ffn2_in, w_ffn2_out, g_final):
    batch, seq, d = x.shape
    m = batch * seq
    depth = g_ffn1.shape[0]
    cos, sin = _rope_tables(positions)
    h = x.reshape(m, d)
    mem2 = mem.reshape(batch * mem.shape[1], d)
    for l in range(depth):
        h = _swiglu_block(h, g_ffn1[l], w_ffn1_in[l], w_ffn1_out[l])

        u = _rmsnorm(h, g_mix[l], BF16)
        outs, lses = [], []
        for group, (_, dil) in enumerate(DILATED_CONFIGS):
            z = _mix_qkv(u, w_mix_in[l], cos, sin, group, dil, batch)
            o, lse = _dilated_attention(z, dil)
            outs.append(o)
            lses.append(lse)
        o_attn = _combine(outs, lses)
        zp = _proj(u, w_mix_in[l], tm=1024, tn=512, out_dtype=F32, col0=ATTN_IN, n=POOL_WIDTH)
        o_pool = _pool_mixer(zp, w_pool[l], pool_scale[l], batch)
        h = _mix_out(o_attn, o_pool, w_mix_out[l], h)

        u = _rmsnorm(h, g_cross[l], BF16)
        um = _rmsnorm(mem2, g_mem[l], BF16)
        q = _proj(u, w_cross_q[l], tm=1024, tn=CROSS_WIDTH)
        kv = _proj(um, w_cross_kv[l], tm=um.shape[0], tn=CROSS_WIDTH)
        h = _cross_attention(q, kv, w_cross_o[l], h, batch)

        h = _swiglu_block(h, g_ffn2[l], w_ffn2_in[l], w_ffn2_out[l])
    return _rmsnorm(h, g_final, F32).reshape(batch, seq, d)
```

```python
import functools
import math

import jax
import jax.numpy as jnp
from jax import lax
from jax.experimental import pallas as pl
from jax.experimental.pallas import tpu as pltpu

F32 = jnp.float32
BF16 = jnp.bfloat16

D_MODEL = 4096
HEAD_DIM = 128
ATTN_WIDTH = D_MODEL // 2
N_ATTN_HEADS = ATTN_WIDTH // HEAD_DIM
DILATED_CONFIGS = ((128, 1), (512, 4), (2048, 16))
N_DIL = len(DILATED_CONFIGS)
BLOCK = 128
POOL_WIDTH = D_MODEL - ATTN_WIDTH
POOL_WINDOWS = (2, 4, 8, 16)
POOL_GROUP = POOL_WIDTH // len(POOL_WINDOWS)
QKV_WIDTH = 3 * ATTN_WIDTH
ATTN_IN = N_DIL * QKV_WIDTH
D_FF = ((8 * D_MODEL // 3 + 255) // 256) * 256
CROSS_HEADS = 4
CROSS_DIM = 128
CROSS_WIDTH = CROSS_HEADS * CROSS_DIM
ROPE_THETA = 10000.0
EPS = 1e-6
NEG_INF = -1e30
LOG2E = math.log2(math.e)

V7X_LANES = 128
V7X_VMEM_BYTES = 64 * 1024 * 1024
V7X_VMEM_REQUEST_CAP = V7X_VMEM_BYTES - 6 * 1024 * 1024

POOL_HALO = max(POOL_WINDOWS)
MAX_DIL = max(d for _, d in DILATED_CONFIGS)
ATTN_TILE = BLOCK * MAX_DIL
ATTN_HEADS_PER_STEP = 4
LSE_WIDTH = (N_ATTN_HEADS // ATTN_HEADS_PER_STEP) * V7X_LANES


def _params(n_grid, vmem_bytes):
    return pltpu.CompilerParams(
        dimension_semantics=("arbitrary",) * n_grid,
        vmem_limit_bytes=int(min(vmem_bytes, V7X_VMEM_REQUEST_CAP)),
    )


def _nbytes(shape, dtype):
    return math.prod(shape) * jnp.dtype(dtype).itemsize


def _resident(block_shape, index_map):
    return pl.BlockSpec(block_shape, index_map, pipeline_mode=pl.Buffered(1))


def _dot(a, b):
    return jnp.dot(a, b, preferred_element_type=F32)


def _dot_nt(a, b):
    return lax.dot_general(a, b, (((1,), (1,)), ((), ())), preferred_element_type=F32)


def _rmsnorm_kernel(x_ref, g_ref, o_ref):
    x = x_ref[...]
    ms = jnp.mean(x * x, axis=-1, keepdims=True)
    o_ref[...] = (x * lax.rsqrt(ms + EPS) * g_ref[...]).astype(o_ref.dtype)


def _rmsnorm(x, g, out_dtype, tm=256):
    m, d = x.shape
    vmem = 2 * tm * d * (4 + jnp.dtype(out_dtype).itemsize) + 4 * tm * d * 4
    return pl.pallas_call(
        _rmsnorm_kernel,
        grid=(m // tm,),
        in_specs=[pl.BlockSpec((tm, d), lambda i: (i, 0)),
                  pl.BlockSpec((1, d), lambda i: (0, 0))],
        out_specs=pl.BlockSpec((tm, d), lambda i: (i, 0)),
        out_shape=jax.ShapeDtypeStruct((m, d), out_dtype),
        compiler_params=_params(1, vmem),
        name="rmsnorm",
    )(x, g.reshape(1, d))


def _rope_kernel(pos_ref, inv_ref, sign_ref, cos_ref, sin_ref):
    ang = pos_ref[...].astype(F32) * inv_ref[...]
    cos_ref[...] = jnp.cos(ang)
    sin_ref[...] = jnp.sin(ang) * sign_ref[...]


def _rope_tables(positions, tm=1024):
    m = positions.size
    half = HEAD_DIM // 2
    inv = 1.0 / (ROPE_THETA ** (jnp.arange(0, HEAD_DIM, 2, dtype=F32) / HEAD_DIM))
    inv = jnp.concatenate([inv, inv]).reshape(1, HEAD_DIM)
    sign = jnp.concatenate([-jnp.ones((half,), F32), jnp.ones((half,), F32)]).reshape(1, HEAD_DIM)
    row = pl.BlockSpec((1, HEAD_DIM), lambda i: (0, 0))
    tab = pl.BlockSpec((tm, HEAD_DIM), lambda i: (i, 0))
    return pl.pallas_call(
        _rope_kernel,
        grid=(m // tm,),
        in_specs=[pl.BlockSpec((tm, 1), lambda i: (i, 0)), row, row],
        out_specs=[tab, tab],
        out_shape=[jax.ShapeDtypeStruct((m, HEAD_DIM), F32)] * 2,
        compiler_params=_params(1, 16 * tm * HEAD_DIM * 4),
        name="rope_tables",
    )(positions.reshape(m, 1), inv, sign)


def _ffn_in_kernel(u_ref, wa_ref, wb_ref, o_ref):
    u = u_ref[...]
    half = o_ref.shape[1] // 2
    for c in range(2):
        sl = slice(c * half, (c + 1) * half)
        w = jnp.concatenate([wa_ref[:, sl].astype(BF16), wb_ref[:, sl].astype(BF16)], axis=1)
        ab = _dot(u, w)
        a, b = ab[:, :half], ab[:, half:]
        o_ref[:, sl] = (a * jax.nn.sigmoid(a) * b).astype(o_ref.dtype)


def _ffn_in(u, w_in, tm=2048, tf=256):
    m, d = u.shape
    nf = D_FF // tf
    vmem = (_nbytes((tm, d), BF16) + 2 * 2 * _nbytes((d, tf), F32)
            + 2 * _nbytes((d, tf), BF16) + 2 * _nbytes((tm, tf), BF16)
            + 6 * _nbytes((tm, tf), F32))
    return pl.pallas_call(
        _ffn_in_kernel,
        grid=(m // tm, nf),
        in_specs=[_resident((tm, d), lambda i, j: (i, 0)),
                  pl.BlockSpec((d, tf), lambda i, j: (0, j)),
                  pl.BlockSpec((d, tf), lambda i, j: (0, j + nf))],
        out_specs=pl.BlockSpec((tm, tf), lambda i, j: (i, j)),
        out_shape=jax.ShapeDtypeStruct((m, D_FF), BF16),
        compiler_params=_params(2, vmem),
        name="ffn_in",
    )(u, w_in, w_in)


def _proj_res_kernel(lhs_ref, w_ref, res_ref, o_ref, *, scale):
    acc = _dot(lhs_ref[...], w_ref[...].astype(BF16))
    o_ref[...] = res_ref[...] + scale * acc


def _proj_res(lhs, w, res, scale, tm, tn):
    m, k = lhs.shape
    n = w.shape[1]
    vmem = (_nbytes((tm, k), BF16) + 2 * _nbytes((k, tn), F32) + _nbytes((k, tn), BF16)
            + 6 * _nbytes((tm, tn), F32))
    return pl.pallas_call(
        functools.partial(_proj_res_kernel, scale=scale),
        grid=(m // tm, n // tn),
        in_specs=[_resident((tm, k), lambda i, j: (i, 0)),
                  pl.BlockSpec((k, tn), lambda i, j: (0, j)),
                  pl.BlockSpec((tm, tn), lambda i, j: (i, j))],
        out_specs=pl.BlockSpec((tm, tn), lambda i, j: (i, j)),
        out_shape=jax.ShapeDtypeStruct((m, n), F32),
        compiler_params=_params(2, vmem),
        name="proj_residual",
    )(lhs, w, res)


def _proj_kernel(lhs_ref, w_ref, o_ref):
    o_ref[...] = _dot(lhs_ref[...], w_ref[...].astype(BF16)).astype(o_ref.dtype)


def _proj(lhs, w, tm, tn, out_dtype=BF16, col0=0, n=None):
    m, k = lhs.shape
    n = w.shape[1] if n is None else n
    col_blk0 = col0 // tn
    vmem = (2 * _nbytes((tm, k), BF16) + 2 * _nbytes((k, tn), F32) + _nbytes((k, tn), BF16)
            + 4 * _nbytes((tm, tn), F32))
    lhs_spec = _resident if n // tn > 1 else pl.BlockSpec
    return pl.pallas_call(
        _proj_kernel,
        grid=(m // tm, n // tn),
        in_specs=[lhs_spec((tm, k), lambda i, j: (i, 0)),
                  pl.BlockSpec((k, tn), lambda i, j: (0, j + col_blk0))],
        out_specs=pl.BlockSpec((tm, tn), lambda i, j: (i, j)),
        out_shape=jax.ShapeDtypeStruct((m, n), out_dtype),
        compiler_params=_params(2, vmem),
        name="proj",
    )(lhs, w)


def _mix_qkv_kernel(u_ref, wq_ref, wk_ref, wv_ref, cos_ref, sin_ref, q_ref, k_ref, v_ref, scr_ref,
                    *, tn, dil):
    tm = u_ref.shape[0]
    u = u_ref[...]
    cos = cos_ref[...]
    sin = sin_ref[...]
    heads = tn // HEAD_DIM
    for part, (w_ref, o_ref) in enumerate(((wq_ref, q_ref), (wk_ref, k_ref), (wv_ref, v_ref))):
        acc = _dot(u, w_ref[...].astype(BF16))
        for h in range(heads):
            sl = slice(h * HEAD_DIM, (h + 1) * HEAD_DIM)
            val = acc[:, sl]
            if o_ref is not v_ref:
                val = val * cos + pltpu.roll(val, HEAD_DIM // 2, axis=1) * sin
            if dil == 1:
                o_ref[0, :, sl] = val.astype(o_ref.dtype)
                continue
            slab = part * heads + h
            scr_ref[slab] = val
            for r in range(dil):
                o_ref[r, :, sl] = scr_ref[slab, pl.ds(r, tm // dil, stride=dil), :].astype(o_ref.dtype)


def _mix_qkv(u, w, cos, sin, group, dil, batch, tm=1024, tn=256):
    m, k = u.shape
    seq = m // batch
    tiles_per_seq = seq // tm
    part_blocks = ATTN_WIDTH // tn

    def w_spec(part):
        col_blk0 = (group * 3 + part) * part_blocks
        return pl.BlockSpec((k, tn), lambda i, j: (0, j + col_blk0))

    out_spec = pl.BlockSpec((None, dil, tm // dil, tn),
                            lambda i, j: (i // tiles_per_seq, 0, i % tiles_per_seq, j))
    out_shape = jax.ShapeDtypeStruct((batch, dil, seq // dil, ATTN_WIDTH), BF16)
    vmem = (_nbytes((tm, k), BF16) + 3 * (2 * _nbytes((k, tn), F32) + _nbytes((k, tn), BF16))
            + 3 * 6 * _nbytes((tm, tn), F32) + 4 * _nbytes((tm, HEAD_DIM), F32))
    return pl.pallas_call(
        functools.partial(_mix_qkv_kernel, tn=tn, dil=dil),
        grid=(m // tm, part_blocks),
        in_specs=[_resident((tm, k), lambda i, j: (i, 0)),
                  w_spec(0), w_spec(1), w_spec(2),
                  pl.BlockSpec((tm, HEAD_DIM), lambda i, j: (i, 0)),
                  pl.BlockSpec((tm, HEAD_DIM), lambda i, j: (i, 0))],
        out_specs=[out_spec] * 3,
        out_shape=[out_shape] * 3,
        scratch_shapes=[pltpu.VMEM((3 * tn // HEAD_DIM, tm, HEAD_DIM), F32)],
        compiler_params=_params(2, vmem),
        name=f"mix_qkv_{dil}",
    )(u, w, w, w, cos, sin)


def _dilated_attn_kernel(q_ref, kp_ref, kc_ref, vp_ref, vc_ref, o_ref, lse_ref,
                         o_scr, lse_scr, *, dil):
    nq = MAX_DIL // dil
    nblk = dil * nq
    qi = lax.broadcasted_iota(jnp.int32, (nblk, BLOCK, BLOCK), 1)
    kj = lax.broadcasted_iota(jnp.int32, (nblk, BLOCK, BLOCK), 2)
    blk = lax.broadcasted_iota(jnp.int32, (nblk, BLOCK, BLOCK), 0)
    has_prev = jnp.logical_or(pl.program_id(1) > 0, blk % nq != 0)
    mask_prev = jnp.logical_and(kj >= qi, has_prev)
    mask_cur = kj <= qi
    scale = HEAD_DIM ** -0.5
    lane = lax.broadcasted_iota(jnp.int32, (nblk, BLOCK, V7X_LANES), 2)
    lse_tile = jnp.zeros((nblk, BLOCK, V7X_LANES), F32)

    def blocks(ref, sl):
        return ref[:, :, sl].reshape(nblk, BLOCK, HEAD_DIM)

    def prev_blocks(first_ref, cur, sl):
        first = first_ref[:, :, sl]
        if nq == 1:
            return first
        cur4 = cur.reshape(dil, nq, BLOCK, HEAD_DIM)
        return jnp.concatenate([first[:, None], cur4[:, :-1]], axis=1).reshape(nblk, BLOCK, HEAD_DIM)

    def token_rows(g):
        r, i = divmod(g, nq)
        return pl.ds(i * BLOCK * dil + r, BLOCK, stride=dil)

    for h in range(ATTN_HEADS_PER_STEP):
        sl = slice(h * HEAD_DIM, (h + 1) * HEAD_DIM)
        q = blocks(q_ref, sl)
        kc = blocks(kc_ref, sl)
        vc = blocks(vc_ref, sl)
        kp = prev_blocks(kp_ref, kc, sl)
        vp = prev_blocks(vp_ref, vc, sl)
        sp = jnp.einsum("gqd,gkd->gqk", q, kp, preferred_element_type=F32)
        sc = jnp.einsum("gqd,gkd->gqk", q, kc, preferred_element_type=F32)
        sp = jnp.where(mask_prev, sp, NEG_INF)
        sc = jnp.where(mask_cur, sc, NEG_INF)
        m = jnp.maximum(jnp.max(sp, axis=2, keepdims=True), jnp.max(sc, axis=2, keepdims=True))
        pp = jnp.exp2((sp - m) * (scale * LOG2E))
        pc = jnp.exp2((sc - m) * (scale * LOG2E))
        l = jnp.sum(pp, axis=2, keepdims=True) + jnp.sum(pc, axis=2, keepdims=True)
        o = (jnp.einsum("gqk,gkd->gqd", pp.astype(BF16), vp, preferred_element_type=F32)
             + jnp.einsum("gqk,gkd->gqd", pc.astype(BF16), vc, preferred_element_type=F32)) / l
        for g in range(nblk):
            o_scr[h, token_rows(g), :] = o[g]
        lse_tile = jnp.where(lane == h, m * scale + jnp.log(l), lse_tile)
    for g in range(nblk):
        lse_scr[token_rows(g), :] = lse_tile[g]
    for h in range(ATTN_HEADS_PER_STEP):
        o_ref[:, h * HEAD_DIM:(h + 1) * HEAD_DIM] = o_scr[h].astype(o_ref.dtype)
    lse_ref[...] = lse_scr[...]


def _dilated_attention(q, k, v, dil):
    b, _, sub_len, _ = q.shape
    seq = sub_len * dil
    rows = ATTN_TILE // dil
    tiles_per_seq = seq // ATTN_TILE
    width = ATTN_HEADS_PER_STEP * HEAD_DIM
    col_blocks = ATTN_WIDTH // width

    cur = pl.BlockSpec((None, dil, rows, width), lambda bi, t, hg: (bi, 0, t, hg))
    prev = pl.BlockSpec((None, dil, BLOCK, width),
                        lambda bi, t, hg: (bi, 0, jnp.maximum(t * (rows // BLOCK) - 1, 0), hg))

    vmem = (6 * _nbytes((dil, rows, width), BF16) + 4 * _nbytes((dil, BLOCK, width), BF16)
            + 2 * _nbytes((ATTN_TILE, width), BF16) + 3 * _nbytes((ATTN_TILE, V7X_LANES), F32)
            + _nbytes((ATTN_TILE, width), F32) + 12 * _nbytes((ATTN_TILE, BLOCK), F32))
    o, lse = pl.pallas_call(
        functools.partial(_dilated_attn_kernel, dil=dil),
        grid=(b, tiles_per_seq, col_blocks),
        in_specs=[cur, prev, cur, prev, cur],
        out_specs=[pl.BlockSpec((ATTN_TILE, width), lambda bi, t, hg: (bi * tiles_per_seq + t, hg)),
                   pl.BlockSpec((ATTN_TILE, V7X_LANES), lambda bi, t, hg: (bi * tiles_per_seq + t, hg))],
        out_shape=[jax.ShapeDtypeStruct((b * seq, ATTN_WIDTH), BF16),
                   jax.ShapeDtypeStruct((b * seq, LSE_WIDTH), F32)],
        scratch_shapes=[pltpu.VMEM((ATTN_HEADS_PER_STEP, ATTN_TILE, HEAD_DIM), F32),
                        pltpu.VMEM((ATTN_TILE, V7X_LANES), F32)],
        compiler_params=_params(3, vmem),
        name=f"dilated_attn_{dil}",
    )(q, k, k, v, v)
    return o, lse


def _combine_kernel(o0_ref, o1_ref, o2_ref, l0_ref, l1_ref, l2_ref, out_ref):
    l0, l1, l2 = l0_ref[...], l1_ref[...], l2_ref[...]
    m = jnp.maximum(jnp.maximum(l0, l1), l2)
    e0, e1, e2 = jnp.exp(l0 - m), jnp.exp(l1 - m), jnp.exp(l2 - m)
    den = e0 + e1 + e2
    w0, w1, w2 = e0 / den, e1 / den, e2 / den
    for h in range(N_ATTN_HEADS):
        sl = slice(h * HEAD_DIM, (h + 1) * HEAD_DIM)
        c = (h // ATTN_HEADS_PER_STEP) * V7X_LANES + h % ATTN_HEADS_PER_STEP
        acc = (w0[:, c:c + 1] * o0_ref[:, sl].astype(F32)
               + w1[:, c:c + 1] * o1_ref[:, sl].astype(F32)
               + w2[:, c:c + 1] * o2_ref[:, sl].astype(F32))
        out_ref[:, sl] = acc.astype(out_ref.dtype)


def _combine(outs, lses, tq=512):
    m = outs[0].shape[0]
    o_spec = pl.BlockSpec((tq, ATTN_WIDTH), lambda i: (i, 0))
    l_spec = pl.BlockSpec((tq, LSE_WIDTH), lambda i: (i, 0))
    vmem = 8 * _nbytes((tq, ATTN_WIDTH), BF16) + 16 * _nbytes((tq, LSE_WIDTH), F32) + (8 << 20)
    return pl.pallas_call(
        _combine_kernel,
        grid=(m // tq,),
        in_specs=[o_spec] * 3 + [l_spec] * 3,
        out_specs=o_spec,
        out_shape=jax.ShapeDtypeStruct((m, ATTN_WIDTH), BF16),
        compiler_params=_params(1, vmem),
        name="attn_combine",
    )(*outs, *lses)


def _pool_kernel(zm_ref, zh_ref, wp_ref, scale_ref, o_ref, *, tp):
    t = pl.program_id(1)
    pos = t * tp + lax.broadcasted_iota(jnp.int32, (tp, 1), 0)
    for g, win in enumerate(POOL_WINDOWS):
        sl = slice(g * POOL_GROUP, (g + 1) * POOL_GROUP)
        main = zm_ref[:, sl]
        halo = jnp.where(t > 0, zh_ref[:, sl], 0.0)
        s = jnp.concatenate([halo, main], axis=0)
        shift = 1
        while shift < win:
            s = s + pltpu.roll(s, shift, axis=0)
            shift *= 2
        cnt = jnp.minimum(pos + 1, win).astype(F32)
        y = s[POOL_HALO:] / cnt - main
        yo = _dot(y.astype(BF16), wp_ref[g].astype(BF16))
        o_ref[:, sl] = (yo * scale_ref[:, sl]).astype(o_ref.dtype)


def _pool_mixer(zp, w_pool, pool_scale, batch, tp=512):
    m = zp.shape[0]
    seq = m // batch
    tiles_per_seq = seq // tp
    halo_blocks = tp // POOL_HALO
    vmem = (6 * _nbytes((tp, POOL_WIDTH), F32) + 2 * _nbytes(w_pool.shape, F32)
            + 12 * _nbytes((tp, POOL_GROUP), F32) + (8 << 20))
    return pl.pallas_call(
        functools.partial(_pool_kernel, tp=tp),
        grid=(batch, tiles_per_seq),
        in_specs=[pl.BlockSpec((tp, POOL_WIDTH), lambda bi, t: (bi * tiles_per_seq + t, 0)),
                  pl.BlockSpec((POOL_HALO, POOL_WIDTH),
                               lambda bi, t: (jnp.maximum((bi * tiles_per_seq + t) * halo_blocks - 1, 0), 0)),
                  pl.BlockSpec(w_pool.shape, lambda bi, t: (0, 0, 0)),
                  pl.BlockSpec((1, POOL_WIDTH), lambda bi, t: (0, 0))],
        out_specs=pl.BlockSpec((tp, POOL_WIDTH), lambda bi, t: (bi * tiles_per_seq + t, 0)),
        out_shape=jax.ShapeDtypeStruct((m, POOL_WIDTH), BF16),
        compiler_params=_params(2, vmem),
        name="pool_mixer",
    )(zp, zp, w_pool, pool_scale.reshape(1, POOL_WIDTH))


def _mix_out_kernel(a_ref, p_ref, wa_ref, wp_ref, res_ref, o_ref):
    acc = _dot(a_ref[...], wa_ref[...].astype(BF16)) + _dot(p_ref[...], wp_ref[...].astype(BF16))
    o_ref[...] = res_ref[...] + acc


def _mix_out(o_attn, o_pool, w, res, tm=1024, tn=512):
    m = o_attn.shape[0]
    n = w.shape[1]
    ka, kp = o_attn.shape[1], o_pool.shape[1]
    assert ka == kp
    vmem = (2 * _nbytes((tm, ka + kp), BF16) + 2 * _nbytes((ka + kp, tn), F32)
            + _nbytes((ka + kp, tn), BF16) + 6 * _nbytes((tm, tn), F32))
    return pl.pallas_call(
        _mix_out_kernel,
        grid=(m // tm, n // tn),
        in_specs=[pl.BlockSpec((tm, ka), lambda i, j: (i, 0)),
                  pl.BlockSpec((tm, kp), lambda i, j: (i, 0)),
                  pl.BlockSpec((ka, tn), lambda i, j: (0, j)),
                  pl.BlockSpec((kp, tn), lambda i, j: (1, j)),
                  pl.BlockSpec((tm, tn), lambda i, j: (i, j))],
        out_specs=pl.BlockSpec((tm, tn), lambda i, j: (i, j)),
        out_shape=jax.ShapeDtypeStruct((m, n), F32),
        compiler_params=_params(2, vmem),
        name="mix_out",
    )(o_attn, o_pool, w, w, res)


def _cross_kernel(q_ref, kv_ref, wo_ref, res_ref, o_ref):
    scale = CROSS_DIM ** -0.5
    heads = []
    for h in range(CROSS_HEADS):
        sl = slice(h * CROSS_DIM, (h + 1) * CROSS_DIM)
        k = kv_ref[:, sl]
        v = kv_ref[:, CROSS_WIDTH + h * CROSS_DIM:CROSS_WIDTH + (h + 1) * CROSS_DIM]
        s = _dot_nt(q_ref[:, sl], k)
        m = jnp.max(s, axis=1, keepdims=True)
        p = jnp.exp2((s - m) * (scale * LOG2E))
        l = jnp.sum(p, axis=1, keepdims=True)
        heads.append((_dot(p.astype(BF16), v) / l).astype(BF16))
    o = jnp.concatenate(heads, axis=1)
    o_ref[...] = res_ref[...] + _dot(o, wo_ref[...].astype(BF16))


def _cross_attention(q, kv, w_o, res, batch, ts=256):
    m, d = res.shape
    mem_len = kv.shape[0] // batch
    steps_per_batch = (m // batch) // ts
    vmem = (4 * _nbytes((ts, d), F32) + 2 * _nbytes(w_o.shape, F32) + _nbytes(w_o.shape, BF16)
            + 2 * _nbytes((ts, d), F32) + (8 << 20))
    return pl.pallas_call(
        _cross_kernel,
        grid=(m // ts,),
        in_specs=[pl.BlockSpec((ts, CROSS_WIDTH), lambda i: (i, 0)),
                  pl.BlockSpec((mem_len, 2 * CROSS_WIDTH), lambda i: (i // steps_per_batch, 0)),
                  pl.BlockSpec(w_o.shape, lambda i: (0, 0)),
                  pl.BlockSpec((ts, d), lambda i: (i, 0))],
        out_specs=pl.BlockSpec((ts, d), lambda i: (i, 0)),
        out_shape=jax.ShapeDtypeStruct((m, d), F32),
        compiler_params=_params(1, vmem),
        name="cross_attention",
    )(q, kv, w_o, res)


def _swiglu_block(h, g, w_in, w_out):
    u = _rmsnorm(h, g, BF16)
    gate = _ffn_in(u, w_in)
    return _proj_res(gate, w_out, h, 0.5, tm=1024, tn=256)


def kernel(x, mem, positions, g_ffn1, w_ffn1_in, w_ffn1_out, g_mix, w_mix_in, w_pool, pool_scale, w_mix_out, g_cross, g_mem, w_cross_q, w_cross_kv, w_cross_o, g_ffn2, w_ffn2_in, w_ffn2_out, g_final):
    batch, seq, d = x.shape
    m = batch * seq
    depth = g_ffn1.shape[0]
    cos, sin = _rope_tables(positions)
    h = x.reshape(m, d)
    mem2 = mem.reshape(batch * mem.shape[1], d)
    for l in range(depth):
        h = _swiglu_block(h, g_ffn1[l], w_ffn1_in[l], w_ffn1_out[l])

        u = _rmsnorm(h, g_mix[l], BF16)
        outs, lses = [], []
        for group, (_, dil) in enumerate(DILATED_CONFIGS):
            q, k, v = _mix_qkv(u, w_mix_in[l], cos, sin, group, dil, batch)
            o, lse = _dilated_attention(q, k, v, dil)
            outs.append(o)
            lses.append(lse)
        o_attn = _combine(outs, lses)
        zp = _proj(u, w_mix_in[l], tm=1024, tn=512, out_dtype=F32, col0=ATTN_IN, n=POOL_WIDTH)
        o_pool = _pool_mixer(zp, w_pool[l], pool_scale[l], batch)
        h = _mix_out(o_attn, o_pool, w_mix_out[l], h)

        u = _rmsnorm(h, g_cross[l], BF16)
        um = _rmsnorm(mem2, g_mem[l], BF16)
        q = _proj(u, w_cross_q[l], tm=1024, tn=CROSS_WIDTH)
        kv = _proj(um, w_cross_kv[l], tm=um.shape[0], tn=CROSS_WIDTH)
        h = _cross_attention(q, kv, w_cross_o[l], h, batch)

        h = _swiglu_block(h, g_ffn2[l], w_ffn2_in[l], w_ffn2_out[l])
    return _rmsnorm(h, g_final, F32).reshape(batch, seq, d)
```

```python
import functools
import math

import jax
import jax.numpy as jnp
from jax import lax
from jax.experimental import pallas as pl
from jax.experimental.pallas import tpu as pltpu

F32 = jnp.float32
BF16 = jnp.bfloat16

D_MODEL = 4096
HEAD_DIM = 128
ATTN_WIDTH = D_MODEL // 2
N_ATTN_HEADS = ATTN_WIDTH // HEAD_DIM
DILATED_CONFIGS = ((128, 1), (512, 4), (2048, 16))
N_DIL = len(DILATED_CONFIGS)
BLOCK = 128
POOL_WIDTH = D_MODEL - ATTN_WIDTH
POOL_WINDOWS = (2, 4, 8, 16)
POOL_GROUP = POOL_WIDTH // len(POOL_WINDOWS)
QKV_WIDTH = 3 * ATTN_WIDTH
ATTN_IN = N_DIL * QKV_WIDTH
D_FF = ((8 * D_MODEL // 3 + 255) // 256) * 256
CROSS_HEADS = 4
CROSS_DIM = 128
CROSS_WIDTH = CROSS_HEADS * CROSS_DIM
ROPE_THETA = 10000.0
EPS = 1e-6
NEG_INF = -1e30
LOG2E = math.log2(math.e)

V7X_LANES = 128
V7X_VMEM_BYTES = 64 * 1024 * 1024
V7X_VMEM_REQUEST_CAP = V7X_VMEM_BYTES - 6 * 1024 * 1024

POOL_HALO = max(POOL_WINDOWS)
MAX_DIL = max(d for _, d in DILATED_CONFIGS)
ATTN_TILE = BLOCK * MAX_DIL
ATTN_HEADS_PER_STEP = 4
LSE_WIDTH = (N_ATTN_HEADS // ATTN_HEADS_PER_STEP) * V7X_LANES


def _params(n_grid, vmem_bytes):
    return pltpu.CompilerParams(
        dimension_semantics=("arbitrary",) * n_grid,
        vmem_limit_bytes=int(min(vmem_bytes, V7X_VMEM_REQUEST_CAP)),
    )


def _nbytes(shape, dtype):
    return math.prod(shape) * jnp.dtype(dtype).itemsize


def _resident(block_shape, index_map):
    return pl.BlockSpec(block_shape, index_map, pipeline_mode=pl.Buffered(1))


def _dot(a, b):
    return jnp.dot(a, b, preferred_element_type=F32)


def _dot_nt(a, b):
    return lax.dot_general(a, b, (((1,), (1,)), ((), ())), preferred_element_type=F32)


def _rmsnorm_kernel(x_ref, g_ref, o_ref):
    x = x_ref[...]
    ms = jnp.mean(x * x, axis=-1, keepdims=True)
    o_ref[...] = (x * lax.rsqrt(ms + EPS) * g_ref[...]).astype(o_ref.dtype)


def _rmsnorm(x, g, out_dtype, tm=256):
    m, d = x.shape
    vmem = 2 * tm * d * (4 + jnp.dtype(out_dtype).itemsize) + 4 * tm * d * 4
    return pl.pallas_call(
        _rmsnorm_kernel,
        grid=(m // tm,),
        in_specs=[pl.BlockSpec((tm, d), lambda i: (i, 0)),
                  pl.BlockSpec((1, d), lambda i: (0, 0))],
        out_specs=pl.BlockSpec((tm, d), lambda i: (i, 0)),
        out_shape=jax.ShapeDtypeStruct((m, d), out_dtype),
        compiler_params=_params(1, vmem),
        name="rmsnorm",
    )(x, g.reshape(1, d))


def _rope_kernel(pos_ref, inv_ref, sign_ref, cos_ref, sin_ref):
    ang = pos_ref[...].astype(F32) * inv_ref[...]
    cos_ref[...] = jnp.cos(ang)
    sin_ref[...] = jnp.sin(ang) * sign_ref[...]


def _rope_tables(positions, tm=1024):
    m = positions.size
    half = HEAD_DIM // 2
    inv = 1.0 / (ROPE_THETA ** (jnp.arange(0, HEAD_DIM, 2, dtype=F32) / HEAD_DIM))
    inv = jnp.concatenate([inv, inv]).reshape(1, HEAD_DIM)
    sign = jnp.concatenate([-jnp.ones((half,), F32), jnp.ones((half,), F32)]).reshape(1, HEAD_DIM)
    row = pl.BlockSpec((1, HEAD_DIM), lambda i: (0, 0))
    tab = pl.BlockSpec((tm, HEAD_DIM), lambda i: (i, 0))
    return pl.pallas_call(
        _rope_kernel,
        grid=(m // tm,),
        in_specs=[pl.BlockSpec((tm, 1), lambda i: (i, 0)), row, row],
        out_specs=[tab, tab],
        out_shape=[jax.ShapeDtypeStruct((m, HEAD_DIM), F32)] * 2,
        compiler_params=_params(1, 16 * tm * HEAD_DIM * 4),
        name="rope_tables",
    )(positions.reshape(m, 1), inv, sign)


def _ffn_in_kernel(u_ref, wa_ref, wb_ref, o_ref):
    u = u_ref[...]
    half = o_ref.shape[1] // 2
    for c in range(2):
        sl = slice(c * half, (c + 1) * half)
        w = jnp.concatenate([wa_ref[:, sl].astype(BF16), wb_ref[:, sl].astype(BF16)], axis=1)
        ab = _dot(u, w)
        a, b = ab[:, :half], ab[:, half:]
        o_ref[:, sl] = (a * jax.nn.sigmoid(a) * b).astype(o_ref.dtype)


def _ffn_in(u, w_in, tm=2048, tf=256):
    m, d = u.shape
    nf = D_FF // tf
    vmem = (_nbytes((tm, d), BF16) + 2 * 2 * _nbytes((d, tf), F32)
            + 2 * _nbytes((d, tf), BF16) + 2 * _nbytes((tm, tf), BF16)
            + 6 * _nbytes((tm, tf), F32))
    return pl.pallas_call(
        _ffn_in_kernel,
        grid=(m // tm, nf),
        in_specs=[_resident((tm, d), lambda i, j: (i, 0)),
                  pl.BlockSpec((d, tf), lambda i, j: (0, j)),
                  pl.BlockSpec((d, tf), lambda i, j: (0, j + nf))],
        out_specs=pl.BlockSpec((tm, tf), lambda i, j: (i, j)),
        out_shape=jax.ShapeDtypeStruct((m, D_FF), BF16),
        compiler_params=_params(2, vmem),
        name="ffn_in",
    )(u, w_in, w_in)


def _emit_prenorm(out, gain_ref, hg_ref, ssq_ref):
    hg_ref[...] = (out * gain_ref[...]).astype(hg_ref.dtype)

    @pl.when(pl.program_id(1) == 0)
    def _():
        ssq_ref[...] = jnp.zeros_like(ssq_ref)

    ssq_ref[...] += jnp.broadcast_to(jnp.sum(out * out, axis=1, keepdims=True), ssq_ref.shape)


def _row_scale(ssq_ref, d):
    return lax.rsqrt(ssq_ref[:, :1] * (1.0 / d) + EPS)


def _prenorm_specs(m, n, tm, tn):
    specs = [pl.BlockSpec((tm, tn), lambda i, j: (i, j)), pl.BlockSpec((tm, V7X_LANES), lambda i, j: (i, 0))]
    shapes = [jax.ShapeDtypeStruct((m, n), BF16), jax.ShapeDtypeStruct((m, V7X_LANES), F32)]
    return specs, shapes


def _proj_res_kernel(lhs_ref, w_ref, res_ref, *rest, scale):
    acc = _dot(lhs_ref[...], w_ref[...].astype(BF16))
    out = res_ref[...] + scale * acc
    if len(rest) == 1:
        rest[0][...] = out
    else:
        gain_ref, o_ref, hg_ref, ssq_ref = rest
        o_ref[...] = out
        _emit_prenorm(out, gain_ref, hg_ref, ssq_ref)


def _proj_res(lhs, w, res, scale, tm, tn, next_gain=None):
    m, k = lhs.shape
    n = w.shape[1]
    vmem = (_nbytes((tm, k), BF16) + 2 * _nbytes((k, tn), F32) + _nbytes((k, tn), BF16)
            + 8 * _nbytes((tm, tn), F32))
    in_specs = [_resident((tm, k), lambda i, j: (i, 0)),
                pl.BlockSpec((k, tn), lambda i, j: (0, j)),
                pl.BlockSpec((tm, tn), lambda i, j: (i, j))]
    out_specs = [pl.BlockSpec((tm, tn), lambda i, j: (i, j))]
    out_shape = [jax.ShapeDtypeStruct((m, n), F32)]
    args = [lhs, w, res]
    if next_gain is not None:
        in_specs.append(pl.BlockSpec((1, tn), lambda i, j: (0, j)))
        args.append(next_gain.reshape(1, n))
        specs, shapes = _prenorm_specs(m, n, tm, tn)
        out_specs += specs
        out_shape += shapes
    outs = pl.pallas_call(
        functools.partial(_proj_res_kernel, scale=scale),
        grid=(m // tm, n // tn),
        in_specs=in_specs,
        out_specs=out_specs,
        out_shape=out_shape,
        compiler_params=_params(2, vmem),
        name="proj_residual",
    )(*args)
    return outs[0] if next_gain is None else outs


def _proj_kernel(lhs_ref, w_ref, *rest):
    acc = _dot(lhs_ref[...], w_ref[...].astype(BF16))
    if len(rest) == 2:
        acc = acc * _row_scale(rest[0], lhs_ref.shape[1])
    rest[-1][...] = acc.astype(rest[-1].dtype)


def _proj(lhs, w, tm, tn, out_dtype=BF16, col0=0, n=None, ssq=None):
    m, k = lhs.shape
    n = w.shape[1] if n is None else n
    col_blk0 = col0 // tn
    vmem = (2 * _nbytes((tm, k), BF16) + 2 * _nbytes((k, tn), F32) + _nbytes((k, tn), BF16)
            + 4 * _nbytes((tm, tn), F32))
    in_specs = [pl.BlockSpec((tm, k), lambda i, j: (i, 0)),
                pl.BlockSpec((k, tn), lambda i, j: (0, j + col_blk0))]
    args = [lhs, w]
    if ssq is not None:
        in_specs.append(pl.BlockSpec((tm, V7X_LANES), lambda i, j: (i, 0)))
        args.append(ssq)
    return pl.pallas_call(
        _proj_kernel,
        grid=(m // tm, n // tn),
        in_specs=in_specs,
        out_specs=pl.BlockSpec((tm, tn), lambda i, j: (i, j)),
        out_shape=jax.ShapeDtypeStruct((m, n), out_dtype),
        compiler_params=_params(2, vmem),
        name="proj",
    )(*args)


def _mix_qkv_kernel(u_ref, ssq_ref, wq_ref, wk_ref, wv_ref, cos_ref, sin_ref, q_ref, k_ref, v_ref, scr_ref,
                    *, tn, dil):
    tm, d_in = u_ref.shape
    u = u_ref[...]
    rs = _row_scale(ssq_ref, d_in)
    cos = cos_ref[...] * rs
    sin = sin_ref[...] * rs
    heads = tn // HEAD_DIM
    for part, (w_ref, o_ref) in enumerate(((wq_ref, q_ref), (wk_ref, k_ref), (wv_ref, v_ref))):
        acc = _dot(u, w_ref[...].astype(BF16))
        for h in range(heads):
            sl = slice(h * HEAD_DIM, (h + 1) * HEAD_DIM)
            val = acc[:, sl]
            if o_ref is v_ref:
                val = val * rs
            else:
                val = val * cos + pltpu.roll(val, HEAD_DIM // 2, axis=1) * sin
            if dil == 1:
                o_ref[0, :, sl] = val.astype(o_ref.dtype)
                continue
            slab = part * heads + h
            scr_ref[slab] = val
            for r in range(dil):
                o_ref[r, :, sl] = scr_ref[slab, pl.ds(r, tm // dil, stride=dil), :].astype(o_ref.dtype)


def _mix_qkv(u, ssq, w, cos, sin, group, dil, batch, tm=1024, tn=256):
    m, k = u.shape
    seq = m // batch
    tiles_per_seq = seq // tm
    part_blocks = ATTN_WIDTH // tn

    def w_spec(part):
        col_blk0 = (group * 3 + part) * part_blocks
        return pl.BlockSpec((k, tn), lambda i, j: (0, j + col_blk0))

    out_spec = pl.BlockSpec((None, dil, tm // dil, tn),
                            lambda i, j: (i // tiles_per_seq, 0, i % tiles_per_seq, j))
    out_shape = jax.ShapeDtypeStruct((batch, dil, seq // dil, ATTN_WIDTH), BF16)
    vmem = (_nbytes((tm, k), BF16) + 3 * (2 * _nbytes((k, tn), F32) + _nbytes((k, tn), BF16))
            + 3 * 6 * _nbytes((tm, tn), F32) + 4 * _nbytes((tm, HEAD_DIM), F32))
    return pl.pallas_call(
        functools.partial(_mix_qkv_kernel, tn=tn, dil=dil),
        grid=(m // tm, part_blocks),
        in_specs=[_resident((tm, k), lambda i, j: (i, 0)),
                  pl.BlockSpec((tm, V7X_LANES), lambda i, j: (i, 0)),
                  w_spec(0), w_spec(1), w_spec(2),
                  pl.BlockSpec((tm, HEAD_DIM), lambda i, j: (i, 0)),
                  pl.BlockSpec((tm, HEAD_DIM), lambda i, j: (i, 0))],
        out_specs=[out_spec] * 3,
        out_shape=[out_shape] * 3,
        scratch_shapes=[pltpu.VMEM((3 * tn // HEAD_DIM, tm, HEAD_DIM), F32)],
        compiler_params=_params(2, vmem),
        name=f"mix_qkv_{dil}",
    )(u, ssq, w, w, w, cos, sin)


def _dilated_attn_kernel(q_ref, kp_ref, kc_ref, vp_ref, vc_ref, o_ref, lse_ref,
                         o_scr, lse_scr, *, dil):
    nq = MAX_DIL // dil
    nblk = dil * nq
    qi = lax.broadcasted_iota(jnp.int32, (nblk, BLOCK, BLOCK), 1)
    kj = lax.broadcasted_iota(jnp.int32, (nblk, BLOCK, BLOCK), 2)
    blk = lax.broadcasted_iota(jnp.int32, (nblk, BLOCK, BLOCK), 0)
    has_prev = jnp.logical_or(pl.program_id(1) > 0, blk % nq != 0)
    mask_prev = jnp.logical_and(kj >= qi, has_prev)
    mask_cur = kj <= qi
    scale = HEAD_DIM ** -0.5
    lane = lax.broadcasted_iota(jnp.int32, (nblk, BLOCK, V7X_LANES), 2)
    lse_tile = jnp.zeros((nblk, BLOCK, V7X_LANES), F32)

    def blocks(ref, sl):
        return ref[:, :, sl].reshape(nblk, BLOCK, HEAD_DIM)

    def prev_blocks(first_ref, cur, sl):
        first = first_ref[:, :, sl]
        if nq == 1:
            return first
        cur4 = cur.reshape(dil, nq, BLOCK, HEAD_DIM)
        return jnp.concatenate([first[:, None], cur4[:, :-1]], axis=1).reshape(nblk, BLOCK, HEAD_DIM)

    def token_rows(g):
        r, i = divmod(g, nq)
        return pl.ds(i * BLOCK * dil + r, BLOCK, stride=dil)

    for h in range(ATTN_HEADS_PER_STEP):
        sl = slice(h * HEAD_DIM, (h + 1) * HEAD_DIM)
        q = blocks(q_ref, sl)
        kc = blocks(kc_ref, sl)
        vc = blocks(vc_ref, sl)
        kp = prev_blocks(kp_ref, kc, sl)
        vp = prev_blocks(vp_ref, vc, sl)
        sp = jnp.einsum("gqd,gkd->gqk", q, kp, preferred_element_type=F32)
        sc = jnp.einsum("gqd,gkd->gqk", q, kc, preferred_element_type=F32)
        sp = jnp.where(mask_prev, sp, NEG_INF)
        sc = jnp.where(mask_cur, sc, NEG_INF)
        m = jnp.maximum(jnp.max(sp, axis=2, keepdims=True), jnp.max(sc, axis=2, keepdims=True))
        pp = jnp.exp2((sp - m) * (scale * LOG2E))
        pc = jnp.exp2((sc - m) * (scale * LOG2E))
        l = jnp.sum(pp, axis=2, keepdims=True) + jnp.sum(pc, axis=2, keepdims=True)
        o = (jnp.einsum("gqk,gkd->gqd", pp.astype(BF16), vp, preferred_element_type=F32)
             + jnp.einsum("gqk,gkd->gqd", pc.astype(BF16), vc, preferred_element_type=F32)) / l
        for g in range(nblk):
            o_scr[h, token_rows(g), :] = o[g]
        lse_tile = jnp.where(lane == h, m * scale + jnp.log(l), lse_tile)
    for g in range(nblk):
        lse_scr[token_rows(g), :] = lse_tile[g]
    for h in range(ATTN_HEADS_PER_STEP):
        o_ref[:, h * HEAD_DIM:(h + 1) * HEAD_DIM] = o_scr[h].astype(o_ref.dtype)
    lse_ref[...] = lse_scr[...]


def _dilated_attention(q, k, v, dil):
    b, _, sub_len, _ = q.shape
    seq = sub_len * dil
    rows = ATTN_TILE // dil
    tiles_per_seq = seq // ATTN_TILE
    width = ATTN_HEADS_PER_STEP * HEAD_DIM
    col_blocks = ATTN_WIDTH // width

    cur = pl.BlockSpec((None, dil, rows, width), lambda bi, t, hg: (bi, 0, t, hg))
    prev = pl.BlockSpec((None, dil, BLOCK, width),
                        lambda bi, t, hg: (bi, 0, jnp.maximum(t * (rows // BLOCK) - 1, 0), hg))

    vmem = (6 * _nbytes((dil, rows, width), BF16) + 4 * _nbytes((dil, BLOCK, width), BF16)
            + 2 * _nbytes((ATTN_TILE, width), BF16) + 3 * _nbytes((ATTN_TILE, V7X_LANES), F32)
            + _nbytes((ATTN_TILE, width), F32) + 12 * _nbytes((ATTN_TILE, BLOCK), F32))
    o, lse = pl.pallas_call(
        functools.partial(_dilated_attn_kernel, dil=dil),
        grid=(b, tiles_per_seq, col_blocks),
        in_specs=[cur, prev, cur, prev, cur],
        out_specs=[pl.BlockSpec((ATTN_TILE, width), lambda bi, t, hg: (bi * tiles_per_seq + t, hg)),
                   pl.BlockSpec((ATTN_TILE, V7X_LANES), lambda bi, t, hg: (bi * tiles_per_seq + t, hg))],
        out_shape=[jax.ShapeDtypeStruct((b * seq, ATTN_WIDTH), BF16),
                   jax.ShapeDtypeStruct((b * seq, LSE_WIDTH), F32)],
        scratch_shapes=[pltpu.VMEM((ATTN_HEADS_PER_STEP, ATTN_TILE, HEAD_DIM), F32),
                        pltpu.VMEM((ATTN_TILE, V7X_LANES), F32)],
        compiler_params=_params(3, vmem),
        name=f"dilated_attn_{dil}",
    )(q, k, k, v, v)
    return o, lse


def _combine_kernel(o0_ref, o1_ref, o2_ref, l0_ref, l1_ref, l2_ref, out_ref):
    l0, l1, l2 = l0_ref[...], l1_ref[...], l2_ref[...]
    m = jnp.maximum(jnp.maximum(l0, l1), l2)
    e0, e1, e2 = jnp.exp(l0 - m), jnp.exp(l1 - m), jnp.exp(l2 - m)
    den = e0 + e1 + e2
    w0, w1, w2 = e0 / den, e1 / den, e2 / den
    for h in range(N_ATTN_HEADS):
        sl = slice(h * HEAD_DIM, (h + 1) * HEAD_DIM)
        c = (h // ATTN_HEADS_PER_STEP) * V7X_LANES + h % ATTN_HEADS_PER_STEP
        acc = (w0[:, c:c + 1] * o0_ref[:, sl].astype(F32)
               + w1[:, c:c + 1] * o1_ref[:, sl].astype(F32)
               + w2[:, c:c + 1] * o2_ref[:, sl].astype(F32))
        out_ref[:, sl] = acc.astype(out_ref.dtype)


def _combine(outs, lses, tq=512):
    m = outs[0].shape[0]
    o_spec = pl.BlockSpec((tq, ATTN_WIDTH), lambda i: (i, 0))
    l_spec = pl.BlockSpec((tq, LSE_WIDTH), lambda i: (i, 0))
    vmem = 8 * _nbytes((tq, ATTN_WIDTH), BF16) + 16 * _nbytes((tq, LSE_WIDTH), F32) + (8 << 20)
    return pl.pallas_call(
        _combine_kernel,
        grid=(m // tq,),
        in_specs=[o_spec] * 3 + [l_spec] * 3,
        out_specs=o_spec,
        out_shape=jax.ShapeDtypeStruct((m, ATTN_WIDTH), BF16),
        compiler_params=_params(1, vmem),
        name="attn_combine",
    )(*outs, *lses)


def _pool_kernel(zm_ref, zh_ref, wp_ref, scale_ref, o_ref, *, tp):
    t = pl.program_id(1)
    pos = t * tp + lax.broadcasted_iota(jnp.int32, (tp, 1), 0)
    for g, win in enumerate(POOL_WINDOWS):
        sl = slice(g * POOL_GROUP, (g + 1) * POOL_GROUP)
        main = zm_ref[:, sl]
        halo = jnp.where(t > 0, zh_ref[:, sl], 0.0)
        s = jnp.concatenate([halo, main], axis=0)
        shift = 1
        while shift < win:
            s = s + pltpu.roll(s, shift, axis=0)
            shift *= 2
        cnt = jnp.minimum(pos + 1, win).astype(F32)
        y = s[POOL_HALO:] / cnt - main
        yo = _dot(y.astype(BF16), wp_ref[g].astype(BF16))
        o_ref[:, sl] = (yo * scale_ref[:, sl]).astype(o_ref.dtype)


def _pool_mixer(zp, w_pool, pool_scale, batch, tp=512):
    m = zp.shape[0]
    seq = m // batch
    tiles_per_seq = seq // tp
    halo_blocks = tp // POOL_HALO
    vmem = (6 * _nbytes((tp, POOL_WIDTH), F32) + 2 * _nbytes(w_pool.shape, F32)
            + 12 * _nbytes((tp, POOL_GROUP), F32) + (8 << 20))
    return pl.pallas_call(
        functools.partial(_pool_kernel, tp=tp),
        grid=(batch, tiles_per_seq),
        in_specs=[pl.BlockSpec((tp, POOL_WIDTH), lambda bi, t: (bi * tiles_per_seq + t, 0)),
                  pl.BlockSpec((POOL_HALO, POOL_WIDTH),
                               lambda bi, t: (jnp.maximum((bi * tiles_per_seq + t) * halo_blocks - 1, 0), 0)),
                  pl.BlockSpec(w_pool.shape, lambda bi, t: (0, 0, 0)),
                  pl.BlockSpec((1, POOL_WIDTH), lambda bi, t: (0, 0))],
        out_specs=pl.BlockSpec((tp, POOL_WIDTH), lambda bi, t: (bi * tiles_per_seq + t, 0)),
        out_shape=jax.ShapeDtypeStruct((m, POOL_WIDTH), BF16),
        compiler_params=_params(2, vmem),
        name="pool_mixer",
    )(zp, zp, w_pool, pool_scale.reshape(1, POOL_WIDTH))


def _mix_out_kernel(a_ref, p_ref, wa_ref, wp_ref, res_ref, gain_ref, o_ref, hg_ref, ssq_ref):
    acc = _dot(a_ref[...], wa_ref[...].astype(BF16)) + _dot(p_ref[...], wp_ref[...].astype(BF16))
    out = res_ref[...] + acc
    o_ref[...] = out
    _emit_prenorm(out, gain_ref, hg_ref, ssq_ref)


def _mix_out(o_attn, o_pool, w, res, next_gain, tm=1024, tn=512):
    m = o_attn.shape[0]
    n = w.shape[1]
    ka, kp = o_attn.shape[1], o_pool.shape[1]
    assert ka == kp
    vmem = (2 * _nbytes((tm, ka + kp), BF16) + 2 * _nbytes((ka + kp, tn), F32)
            + _nbytes((ka + kp, tn), BF16) + 8 * _nbytes((tm, tn), F32))
    norm_specs, norm_shapes = _prenorm_specs(m, n, tm, tn)
    return pl.pallas_call(
        _mix_out_kernel,
        grid=(m // tm, n // tn),
        in_specs=[pl.BlockSpec((tm, ka), lambda i, j: (i, 0)),
                  pl.BlockSpec((tm, kp), lambda i, j: (i, 0)),
                  pl.BlockSpec((ka, tn), lambda i, j: (0, j)),
                  pl.BlockSpec((kp, tn), lambda i, j: (1, j)),
                  pl.BlockSpec((tm, tn), lambda i, j: (i, j)),
                  pl.BlockSpec((1, tn), lambda i, j: (0, j))],
        out_specs=[pl.BlockSpec((tm, tn), lambda i, j: (i, j))] + norm_specs,
        out_shape=[jax.ShapeDtypeStruct((m, n), F32)] + norm_shapes,
        compiler_params=_params(2, vmem),
        name="mix_out",
    )(o_attn, o_pool, w, w, res, next_gain.reshape(1, n))


def _cast_kernel(x_ref, o_ref):
    o_ref[...] = x_ref[...].astype(o_ref.dtype)


def _cast_bf16(w, rows=512):
    k, n = w.shape
    rows = min(rows, k)
    return pl.pallas_call(
        _cast_kernel,
        grid=(k // rows,),
        in_specs=[pl.BlockSpec((rows, n), lambda i: (i, 0))],
        out_specs=pl.BlockSpec((rows, n), lambda i: (i, 0)),
        out_shape=jax.ShapeDtypeStruct((k, n), BF16),
        compiler_params=_params(1, 8 * _nbytes((rows, n), F32)),
        name="cast_bf16",
    )(w)


def _cross_kernel(hg_ref, ssq_ref, kv_ref, wq_ref, wo_ref, res_ref, gain_ref, o_ref, u_ref):
    scale = CROSS_DIM ** -0.5
    q = (_dot(hg_ref[...], wq_ref[...]) * _row_scale(ssq_ref, hg_ref.shape[1])).astype(BF16)
    heads = []
    for h in range(CROSS_HEADS):
        sl = slice(h * CROSS_DIM, (h + 1) * CROSS_DIM)
        k = kv_ref[:, sl]
        v = kv_ref[:, CROSS_WIDTH + h * CROSS_DIM:CROSS_WIDTH + (h + 1) * CROSS_DIM]
        s = _dot_nt(q[:, sl], k)
        m = jnp.max(s, axis=1, keepdims=True)
        p = jnp.exp2((s - m) * (scale * LOG2E))
        l = jnp.sum(p, axis=1, keepdims=True)
        heads.append((_dot(p.astype(BF16), v) / l).astype(BF16))
    out = res_ref[...] + _dot(jnp.concatenate(heads, axis=1), wo_ref[...])
    o_ref[...] = out
    ms = jnp.mean(out * out, axis=-1, keepdims=True)
    u_ref[...] = (out * lax.rsqrt(ms + EPS) * gain_ref[...]).astype(u_ref.dtype)


def _cross_attention(hg, ssq, kv, w_q, w_o, res, next_gain, batch, ts=256):
    m, d = res.shape
    mem_len = kv.shape[0] // batch
    steps_per_batch = (m // batch) // ts
    vmem = (2 * _nbytes((ts, d), BF16) * 2 + 4 * _nbytes((ts, d), F32) + _nbytes(w_q.shape, BF16)
            + _nbytes(w_o.shape, BF16) + 4 * _nbytes((ts, d), F32) + (4 << 20))
    row_f32 = pl.BlockSpec((ts, d), lambda i: (i, 0))
    return pl.pallas_call(
        _cross_kernel,
        grid=(m // ts,),
        in_specs=[pl.BlockSpec((ts, d), lambda i: (i, 0)),
                  pl.BlockSpec((ts, V7X_LANES), lambda i: (i, 0)),
                  pl.BlockSpec((mem_len, 2 * CROSS_WIDTH), lambda i: (i // steps_per_batch, 0)),
                  _resident(w_q.shape, lambda i: (0, 0)),
                  _resident(w_o.shape, lambda i: (0, 0)),
                  row_f32,
                  pl.BlockSpec((1, d), lambda i: (0, 0))],
        out_specs=[row_f32, pl.BlockSpec((ts, d), lambda i: (i, 0))],
        out_shape=[jax.ShapeDtypeStruct((m, d), F32), jax.ShapeDtypeStruct((m, d), BF16)],
        compiler_params=_params(1, vmem),
        name="cross_attention",
    )(hg, ssq, kv, w_q, w_o, res, next_gain.reshape(1, d))


def _swiglu_block(h, u, w_in, w_out, next_gain=None):
    gate = _ffn_in(u, w_in)
    return _proj_res(gate, w_out, h, 0.5, tm=1024, tn=256, next_gain=next_gain)


def kernel(x, mem, positions, g_ffn1, w_ffn1_in, w_ffn1_out, g_mix, w_mix_in, w_pool, pool_scale, w_mix_out, g_cross, g_mem, w_cross_q, w_cross_kv, w_cross_o, g_ffn2, w_ffn2_in, w_ffn2_out, g_final):
    batch, seq, d = x.shape
    m = batch * seq
    depth = g_ffn1.shape[0]
    cos, sin = _rope_tables(positions)
    h = x.reshape(m, d)
    mem2 = mem.reshape(batch * mem.shape[1], d)
    for l in range(depth):
        u = _rmsnorm(h, g_ffn1[l], BF16)
        h, hg, ssq = _swiglu_block(h, u, w_ffn1_in[l], w_ffn1_out[l], next_gain=g_mix[l])

        outs, lses = [], []
        for group, (_, dil) in enumerate(DILATED_CONFIGS):
            q, k, v = _mix_qkv(hg, ssq, w_mix_in[l], cos, sin, group, dil, batch)
            o, lse = _dilated_attention(q, k, v, dil)
            outs.append(o)
            lses.append(lse)
        o_attn = _combine(outs, lses)
        zp = _proj(hg, w_mix_in[l], tm=1024, tn=512, out_dtype=F32, col0=ATTN_IN, n=POOL_WIDTH, ssq=ssq)
        o_pool = _pool_mixer(zp, w_pool[l], pool_scale[l], batch)
        h, hg, ssq = _mix_out(o_attn, o_pool, w_mix_out[l], h, g_cross[l])

        um = _rmsnorm(mem2, g_mem[l], BF16)
        kv = _proj(um, w_cross_kv[l], tm=um.shape[0], tn=CROSS_WIDTH)
        h, u = _cross_attention(hg, ssq, kv, _cast_bf16(w_cross_q[l]), _cast_bf16(w_cross_o[l]),
                                h, g_ffn2[l], batch)

        h = _swiglu_block(h, u, w_ffn2_in[l], w_ffn2_out[l])
    return _rmsnorm(h, g_final, F32).reshape(batch, seq, d)
```

```python
import functools
import math

import jax
import jax.numpy as jnp
from jax import lax
from jax.experimental import pallas as pl
from jax.experimental.pallas import tpu as pltpu

F32 = jnp.float32
BF16 = jnp.bfloat16

D_MODEL = 4096
HEAD_DIM = 128
ATTN_WIDTH = D_MODEL // 2
N_ATTN_HEADS = ATTN_WIDTH // HEAD_DIM
DILATED_CONFIGS = ((128, 1), (512, 4), (2048, 16))
N_DIL = len(DILATED_CONFIGS)
BLOCK = 128
POOL_WIDTH = D_MODEL - ATTN_WIDTH
POOL_WINDOWS = (2, 4, 8, 16)
POOL_GROUP = POOL_WIDTH // len(POOL_WINDOWS)
QKV_WIDTH = 3 * ATTN_WIDTH
ATTN_IN = N_DIL * QKV_WIDTH
D_FF = ((8 * D_MODEL // 3 + 255) // 256) * 256
CROSS_HEADS = 4
CROSS_DIM = 128
CROSS_WIDTH = CROSS_HEADS * CROSS_DIM
ROPE_THETA = 10000.0
EPS = 1e-6
NEG_INF = -1e30
LOG2E = math.log2(math.e)

V7X_LANES = 128
V7X_BF16_SUBLANES = 16
V7X_VMEM_BYTES = 64 * 1024 * 1024
V7X_VMEM_REQUEST_CAP = V7X_VMEM_BYTES - 6 * 1024 * 1024

POOL_HALO = max(POOL_WINDOWS)
MAX_DIL = max(d for _, d in DILATED_CONFIGS)
ATTN_TILE = BLOCK * MAX_DIL
ATTN_HEADS_PER_STEP = 4
LSE_WIDTH = (N_ATTN_HEADS // ATTN_HEADS_PER_STEP) * V7X_LANES


def _params(n_grid, vmem_bytes):
    return pltpu.CompilerParams(
        dimension_semantics=("arbitrary",) * n_grid,
        vmem_limit_bytes=int(min(vmem_bytes, V7X_VMEM_REQUEST_CAP)),
    )


def _nbytes(shape, dtype):
    return math.prod(shape) * jnp.dtype(dtype).itemsize


def _weight_vmem(block_shape, dtype):
    cast_copy = 0 if dtype == BF16 else _nbytes(block_shape, BF16)
    return 2 * _nbytes(block_shape, dtype) + cast_copy


def _resident(block_shape, index_map):
    return pl.BlockSpec(block_shape, index_map, pipeline_mode=pl.Buffered(1))


def _dot(a, b):
    return jnp.dot(a, b, preferred_element_type=F32)


def _dot_nt(a, b):
    return lax.dot_general(a, b, (((1,), (1,)), ((), ())), preferred_element_type=F32)


def _rmsnorm_kernel(x_ref, g_ref, o_ref):
    x = x_ref[...]
    ms = jnp.mean(x * x, axis=-1, keepdims=True)
    o_ref[...] = (x * lax.rsqrt(ms + EPS) * g_ref[...]).astype(o_ref.dtype)


def _rmsnorm(x, g, out_dtype, tm=256):
    m, d = x.shape
    vmem = 2 * tm * d * (4 + jnp.dtype(out_dtype).itemsize) + 4 * tm * d * 4
    return pl.pallas_call(
        _rmsnorm_kernel,
        grid=(m // tm,),
        in_specs=[pl.BlockSpec((tm, d), lambda i: (i, 0)),
                  pl.BlockSpec((1, d), lambda i: (0, 0))],
        out_specs=pl.BlockSpec((tm, d), lambda i: (i, 0)),
        out_shape=jax.ShapeDtypeStruct((m, d), out_dtype),
        compiler_params=_params(1, vmem),
        name="rmsnorm",
    )(x, g.reshape(1, d))


def _rope_kernel(pos_ref, inv_ref, sign_ref, cos_ref, sin_ref):
    ang = pos_ref[...].astype(F32) * inv_ref[...]
    cos_ref[...] = jnp.cos(ang)
    sin_ref[...] = jnp.sin(ang) * sign_ref[...]


def _rope_tables(positions, tm=1024):
    m = positions.size
    half = HEAD_DIM // 2
    inv = 1.0 / (ROPE_THETA ** (jnp.arange(0, HEAD_DIM, 2, dtype=F32) / HEAD_DIM))
    inv = jnp.concatenate([inv, inv]).reshape(1, HEAD_DIM)
    sign = jnp.concatenate([-jnp.ones((half,), F32), jnp.ones((half,), F32)]).reshape(1, HEAD_DIM)
    row = pl.BlockSpec((1, HEAD_DIM), lambda i: (0, 0))
    tab = pl.BlockSpec((tm, HEAD_DIM), lambda i: (i, 0))
    return pl.pallas_call(
        _rope_kernel,
        grid=(m // tm,),
        in_specs=[pl.BlockSpec((tm, 1), lambda i: (i, 0)), row, row],
        out_specs=[tab, tab],
        out_shape=[jax.ShapeDtypeStruct((m, HEAD_DIM), F32)] * 2,
        compiler_params=_params(1, 16 * tm * HEAD_DIM * 4),
        name="rope_tables",
    )(positions.reshape(m, 1), inv, sign)


def _ffn_in_kernel(u_ref, wa_ref, wb_ref, *rest):
    n_jobs = (len(rest) - 1) // 2
    srcs, o_ref, dsts = rest[:n_jobs], rest[n_jobs], rest[n_jobs + 1:]
    half = o_ref.shape[1] // 2
    for c in range(2):
        sl = slice(c * half, (c + 1) * half)
        w = jnp.concatenate([wa_ref[:, sl].astype(BF16), wb_ref[:, sl].astype(BF16)], axis=1)
        ab = _dot(u_ref[...], w)
        a, b = ab[:, :half], ab[:, half:]
        o_ref[:, sl] = (a * jax.nn.sigmoid(a) * b).astype(o_ref.dtype)
    for src, dst in zip(srcs, dsts):
        dst[...] = src[...].astype(dst.dtype)


def _ffn_in(u, w_in, cast_jobs=(), tm=2048, tf=256):
    m, d = u.shape
    nf = D_FF // tf
    n_steps = (m // tm) * nf
    vmem = (_nbytes((tm, d), BF16) + 2 * 2 * _nbytes((d, tf), F32)
            + 2 * _nbytes((d, tf), BF16) + 2 * _nbytes((tm, tf), BF16)
            + 6 * _nbytes((tm, tf), F32))
    job_specs, job_shapes = [], []
    for w in cast_jobs:
        rows = next(r for r in range(V7X_BF16_SUBLANES, w.shape[0] + 1, V7X_BF16_SUBLANES)
                    if w.shape[0] % r == 0 and w.shape[0] // r <= n_steps)
        n_blocks = w.shape[0] // rows
        job_specs.append(pl.BlockSpec(
            (rows, w.shape[1]), lambda i, j, n_blocks=n_blocks: (jnp.minimum(i * nf + j, n_blocks - 1), 0)))
        job_shapes.append(jax.ShapeDtypeStruct(w.shape, BF16))
        vmem += 2 * _nbytes((rows, w.shape[1]), F32) + 2 * _nbytes((rows, w.shape[1]), BF16)
    outs = pl.pallas_call(
        _ffn_in_kernel,
        grid=(m // tm, nf),
        in_specs=[_resident((tm, d), lambda i, j: (i, 0)),
                  pl.BlockSpec((d, tf), lambda i, j: (0, j)),
                  pl.BlockSpec((d, tf), lambda i, j: (0, j + nf))] + job_specs,
        out_specs=[pl.BlockSpec((tm, tf), lambda i, j: (i, j))] + job_specs,
        out_shape=[jax.ShapeDtypeStruct((m, D_FF), BF16)] + job_shapes,
        compiler_params=_params(2, vmem),
        name="ffn_in",
    )(u, w_in, w_in, *cast_jobs)
    return outs[0], outs[1:]


def _emit_prenorm(out, gain_ref, hg_ref, ssq_ref):
    hg_ref[...] = (out * gain_ref[...]).astype(hg_ref.dtype)

    @pl.when(pl.program_id(1) == 0)
    def _():
        ssq_ref[...] = jnp.zeros_like(ssq_ref)

    ssq_ref[...] += jnp.broadcast_to(jnp.sum(out * out, axis=1, keepdims=True), ssq_ref.shape)


def _row_scale(ssq_ref, d):
    return lax.rsqrt(ssq_ref[:, :1] * (1.0 / d) + EPS)


def _prenorm_specs(m, n, tm, tn):
    specs = [pl.BlockSpec((tm, tn), lambda i, j: (i, j)), pl.BlockSpec((tm, V7X_LANES), lambda i, j: (i, 0))]
    shapes = [jax.ShapeDtypeStruct((m, n), BF16), jax.ShapeDtypeStruct((m, V7X_LANES), F32)]
    return specs, shapes


def _proj_res_kernel(lhs_ref, w_ref, res_ref, *rest, scale):
    acc = _dot(lhs_ref[...], w_ref[...].astype(BF16))
    out = res_ref[...] + scale * acc
    if len(rest) == 1:
        rest[0][...] = out
    else:
        gain_ref, o_ref, hg_ref, ssq_ref = rest
        o_ref[...] = out
        _emit_prenorm(out, gain_ref, hg_ref, ssq_ref)


def _proj_res(lhs, w, res, scale, tm, tn, next_gain=None):
    m, k = lhs.shape
    n = w.shape[1]
    vmem = _nbytes((tm, k), BF16) + _weight_vmem((k, tn), w.dtype) + 8 * _nbytes((tm, tn), F32)
    in_specs = [_resident((tm, k), lambda i, j: (i, 0)),
                pl.BlockSpec((k, tn), lambda i, j: (0, j)),
                pl.BlockSpec((tm, tn), lambda i, j: (i, j))]
    out_specs = [pl.BlockSpec((tm, tn), lambda i, j: (i, j))]
    out_shape = [jax.ShapeDtypeStruct((m, n), F32)]
    args = [lhs, w, res]
    if next_gain is not None:
        in_specs.append(pl.BlockSpec((1, tn), lambda i, j: (0, j)))
        args.append(next_gain.reshape(1, n))
        specs, shapes = _prenorm_specs(m, n, tm, tn)
        out_specs += specs
        out_shape += shapes
    outs = pl.pallas_call(
        functools.partial(_proj_res_kernel, scale=scale),
        grid=(m // tm, n // tn),
        in_specs=in_specs,
        out_specs=out_specs,
        out_shape=out_shape,
        compiler_params=_params(2, vmem),
        name="proj_residual",
    )(*args)
    return outs[0] if next_gain is None else outs


def _proj_kernel(lhs_ref, w_ref, *rest):
    acc = _dot(lhs_ref[...], w_ref[...].astype(BF16))
    if len(rest) == 2:
        acc = acc * _row_scale(rest[0], lhs_ref.shape[1])
    rest[-1][...] = acc.astype(rest[-1].dtype)


def _proj(lhs, w, tm, tn, out_dtype=BF16, col0=0, n=None, ssq=None):
    m, k = lhs.shape
    n = w.shape[1] if n is None else n
    col_blk0 = col0 // tn
    vmem = 2 * _nbytes((tm, k), BF16) + _weight_vmem((k, tn), w.dtype) + 4 * _nbytes((tm, tn), F32)
    in_specs = [pl.BlockSpec((tm, k), lambda i, j: (i, 0)),
                pl.BlockSpec((k, tn), lambda i, j: (0, j + col_blk0))]
    args = [lhs, w]
    if ssq is not None:
        in_specs.append(pl.BlockSpec((tm, V7X_LANES), lambda i, j: (i, 0)))
        args.append(ssq)
    return pl.pallas_call(
        _proj_kernel,
        grid=(m // tm, n // tn),
        in_specs=in_specs,
        out_specs=pl.BlockSpec((tm, tn), lambda i, j: (i, j)),
        out_shape=jax.ShapeDtypeStruct((m, n), out_dtype),
        compiler_params=_params(2, vmem),
        name="proj",
    )(*args)


def _mix_qkv_kernel(u_ref, ssq_ref, wq_ref, wk_ref, wv_ref, cos_ref, sin_ref, q_ref, k_ref, v_ref,
                    scr_ref, tab_ref, *, tn, dil):
    tm, d_in = u_ref.shape
    rows = tm // dil

    def regroup(slab):
        if dil == 1:
            return [scr_ref[slab]]
        return [scr_ref[slab, pl.ds(r, rows, stride=dil), :] for r in range(dil)]

    @pl.when(pl.program_id(1) == 0)
    def _():
        rs = _row_scale(ssq_ref, d_in)
        tables = (cos_ref[...] * rs, sin_ref[...] * rs, jnp.broadcast_to(rs, (tm, HEAD_DIM)))
        for t, table in enumerate(tables):
            scr_ref[0] = table
            for r, piece in enumerate(regroup(0)):
                tab_ref[t, r] = piece

    heads = tn // HEAD_DIM
    for part, (w_ref, o_ref) in enumerate(((wq_ref, q_ref), (wk_ref, k_ref), (wv_ref, v_ref))):
        acc = _dot(u_ref[...], w_ref[...].astype(BF16))
        for h in range(heads):
            scr_ref[part * heads + h] = acc[:, h * HEAD_DIM:(h + 1) * HEAD_DIM]
        for h in range(heads):
            for r, piece in enumerate(regroup(part * heads + h)):
                if o_ref is v_ref:
                    piece = piece * tab_ref[2, r]
                else:
                    piece = piece * tab_ref[0, r] + pltpu.roll(piece, HEAD_DIM // 2, axis=1) * tab_ref[1, r]
                o_ref[r, :, h * HEAD_DIM:(h + 1) * HEAD_DIM] = piece.astype(o_ref.dtype)


def _mix_qkv(u, ssq, w, cos, sin, group, dil, batch, tm=1024, tn=256):
    m, k = u.shape
    seq = m // batch
    tiles_per_seq = seq // tm
    part_blocks = ATTN_WIDTH // tn

    def w_spec(part):
        col_blk0 = (group * 3 + part) * part_blocks
        return pl.BlockSpec((k, tn), lambda i, j: (0, j + col_blk0))

    out_spec = pl.BlockSpec((None, dil, tm // dil, tn),
                            lambda i, j: (i // tiles_per_seq, 0, i % tiles_per_seq, j))
    out_shape = jax.ShapeDtypeStruct((batch, dil, seq // dil, ATTN_WIDTH), BF16)
    vmem = (2 * _nbytes((tm, k), BF16) + 3 * _weight_vmem((k, tn), w.dtype)
            + 3 * 6 * _nbytes((tm, tn), F32) + 10 * _nbytes((tm, HEAD_DIM), F32))
    return pl.pallas_call(
        functools.partial(_mix_qkv_kernel, tn=tn, dil=dil),
        grid=(m // tm, part_blocks),
        in_specs=[pl.BlockSpec((tm, k), lambda i, j: (i, 0)),
                  pl.BlockSpec((tm, V7X_LANES), lambda i, j: (i, 0)),
                  w_spec(0), w_spec(1), w_spec(2),
                  pl.BlockSpec((tm, HEAD_DIM), lambda i, j: (i, 0)),
                  pl.BlockSpec((tm, HEAD_DIM), lambda i, j: (i, 0))],
        out_specs=[out_spec] * 3,
        out_shape=[out_shape] * 3,
        scratch_shapes=[pltpu.VMEM((3 * tn // HEAD_DIM, tm, HEAD_DIM), F32),
                        pltpu.VMEM((3, dil, tm // dil, HEAD_DIM), F32)],
        compiler_params=_params(2, vmem),
        name=f"mix_qkv_{dil}",
    )(u, ssq, w, w, w, cos, sin)


def _dilated_attn_kernel(q_ref, kp_ref, kc_ref, vp_ref, vc_ref, o_ref, lse_ref,
                         o_scr, lse_scr, *, dil):
    nq = MAX_DIL // dil
    nblk = dil * nq
    qi = lax.broadcasted_iota(jnp.int32, (nblk, BLOCK, BLOCK), 1)
    kj = lax.broadcasted_iota(jnp.int32, (nblk, BLOCK, BLOCK), 2)
    blk = lax.broadcasted_iota(jnp.int32, (nblk, BLOCK, BLOCK), 0)
    has_prev = jnp.logical_or(pl.program_id(1) > 0, blk % nq != 0)
    mask_prev = jnp.logical_and(kj >= qi, has_prev)
    mask_cur = kj <= qi
    scale = HEAD_DIM ** -0.5
    lane = lax.broadcasted_iota(jnp.int32, (nblk, BLOCK, V7X_LANES), 2)
    lse_tile = jnp.zeros((nblk, BLOCK, V7X_LANES), F32)

    def blocks(ref, sl):
        return ref[:, :, sl].reshape(nblk, BLOCK, HEAD_DIM)

    def prev_blocks(first_ref, cur, sl):
        first = first_ref[:, :, sl]
        if nq == 1:
            return first
        cur4 = cur.reshape(dil, nq, BLOCK, HEAD_DIM)
        return jnp.concatenate([first[:, None], cur4[:, :-1]], axis=1).reshape(nblk, BLOCK, HEAD_DIM)

    def token_rows(g):
        r, i = divmod(g, nq)
        return pl.ds(i * BLOCK * dil + r, BLOCK, stride=dil)

    for h in range(ATTN_HEADS_PER_STEP):
        sl = slice(h * HEAD_DIM, (h + 1) * HEAD_DIM)
        q = blocks(q_ref, sl)
        kc = blocks(kc_ref, sl)
        vc = blocks(vc_ref, sl)
        kp = prev_blocks(kp_ref, kc, sl)
        vp = prev_blocks(vp_ref, vc, sl)
        sp = jnp.einsum("gqd,gkd->gqk", q, kp, preferred_element_type=F32)
        sc = jnp.einsum("gqd,gkd->gqk", q, kc, preferred_element_type=F32)
        sp = jnp.where(mask_prev, sp, NEG_INF)
        sc = jnp.where(mask_cur, sc, NEG_INF)
        m = jnp.maximum(jnp.max(sp, axis=2, keepdims=True), jnp.max(sc, axis=2, keepdims=True))
        pp = jnp.exp2((sp - m) * (scale * LOG2E))
        pc = jnp.exp2((sc - m) * (scale * LOG2E))
        l = jnp.sum(pp, axis=2, keepdims=True) + jnp.sum(pc, axis=2, keepdims=True)
        o = (jnp.einsum("gqk,gkd->gqd", pp.astype(BF16), vp, preferred_element_type=F32)
             + jnp.einsum("gqk,gkd->gqd", pc.astype(BF16), vc, preferred_element_type=F32)) / l
        for g in range(nblk):
            o_scr[h, token_rows(g), :] = o[g]
        lse_tile = jnp.where(lane == h, m * scale + jnp.log(l), lse_tile)
    for g in range(nblk):
        lse_scr[token_rows(g), :] = lse_tile[g]
    for h in range(ATTN_HEADS_PER_STEP):
        o_ref[:, h * HEAD_DIM:(h + 1) * HEAD_DIM] = o_scr[h].astype(o_ref.dtype)
    lse_ref[...] = lse_scr[...]


def _dilated_attention(q, k, v, dil):
    b, _, sub_len, _ = q.shape
    seq = sub_len * dil
    rows = ATTN_TILE // dil
    tiles_per_seq = seq // ATTN_TILE
    width = ATTN_HEADS_PER_STEP * HEAD_DIM
    col_blocks = ATTN_WIDTH // width

    cur = pl.BlockSpec((None, dil, rows, width), lambda bi, t, hg: (bi, 0, t, hg))
    prev = pl.BlockSpec((None, dil, BLOCK, width),
                        lambda bi, t, hg: (bi, 0, jnp.maximum(t * (rows // BLOCK) - 1, 0), hg))

    vmem = (6 * _nbytes((dil, rows, width), BF16) + 4 * _nbytes((dil, BLOCK, width), BF16)
            + 2 * _nbytes((ATTN_TILE, width), BF16) + 3 * _nbytes((ATTN_TILE, V7X_LANES), F32)
            + _nbytes((ATTN_TILE, width), F32) + 12 * _nbytes((ATTN_TILE, BLOCK), F32))
    o, lse = pl.pallas_call(
        functools.partial(_dilated_attn_kernel, dil=dil),
        grid=(b, tiles_per_seq, col_blocks),
        in_specs=[cur, prev, cur, prev, cur],
        out_specs=[pl.BlockSpec((ATTN_TILE, width), lambda bi, t, hg: (bi * tiles_per_seq + t, hg)),
                   pl.BlockSpec((ATTN_TILE, V7X_LANES), lambda bi, t, hg: (bi * tiles_per_seq + t, hg))],
        out_shape=[jax.ShapeDtypeStruct((b * seq, ATTN_WIDTH), BF16),
                   jax.ShapeDtypeStruct((b * seq, LSE_WIDTH), F32)],
        scratch_shapes=[pltpu.VMEM((ATTN_HEADS_PER_STEP, ATTN_TILE, HEAD_DIM), F32),
                        pltpu.VMEM((ATTN_TILE, V7X_LANES), F32)],
        compiler_params=_params(3, vmem),
        name=f"dilated_attn_{dil}",
    )(q, k, k, v, v)
    return o, lse


def _combine_kernel(o0_ref, o1_ref, o2_ref, l0_ref, l1_ref, l2_ref, out_ref):
    l0, l1, l2 = l0_ref[...], l1_ref[...], l2_ref[...]
    m = jnp.maximum(jnp.maximum(l0, l1), l2)
    e0, e1, e2 = jnp.exp(l0 - m), jnp.exp(l1 - m), jnp.exp(l2 - m)
    den = e0 + e1 + e2
    w0, w1, w2 = e0 / den, e1 / den, e2 / den
    for h in range(N_ATTN_HEADS):
        sl = slice(h * HEAD_DIM, (h + 1) * HEAD_DIM)
        c = (h // ATTN_HEADS_PER_STEP) * V7X_LANES + h % ATTN_HEADS_PER_STEP
        acc = (w0[:, c:c + 1] * o0_ref[:, sl].astype(F32)
               + w1[:, c:c + 1] * o1_ref[:, sl].astype(F32)
               + w2[:, c:c + 1] * o2_ref[:, sl].astype(F32))
        out_ref[:, sl] = acc.astype(out_ref.dtype)


def _combine(outs, lses, tq=512):
    m = outs[0].shape[0]
    o_spec = pl.BlockSpec((tq, ATTN_WIDTH), lambda i: (i, 0))
    l_spec = pl.BlockSpec((tq, LSE_WIDTH), lambda i: (i, 0))
    vmem = 8 * _nbytes((tq, ATTN_WIDTH), BF16) + 16 * _nbytes((tq, LSE_WIDTH), F32) + (8 << 20)
    return pl.pallas_call(
        _combine_kernel,
        grid=(m // tq,),
        in_specs=[o_spec] * 3 + [l_spec] * 3,
        out_specs=o_spec,
        out_shape=jax.ShapeDtypeStruct((m, ATTN_WIDTH), BF16),
        compiler_params=_params(1, vmem),
        name="attn_combine",
    )(*outs, *lses)


def _pool_kernel(zm_ref, zh_ref, wp_ref, scale_ref, o_ref, *, tp):
    t = pl.program_id(1)
    pos = t * tp + lax.broadcasted_iota(jnp.int32, (tp, 1), 0)
    for g, win in enumerate(POOL_WINDOWS):
        sl = slice(g * POOL_GROUP, (g + 1) * POOL_GROUP)
        main = zm_ref[:, sl]
        halo = jnp.where(t > 0, zh_ref[:, sl], 0.0)
        s = jnp.concatenate([halo, main], axis=0)
        shift = 1
        while shift < win:
            s = s + pltpu.roll(s, shift, axis=0)
            shift *= 2
        cnt = jnp.minimum(pos + 1, win).astype(F32)
        y = s[POOL_HALO:] / cnt - main
        yo = _dot(y.astype(BF16), wp_ref[g].astype(BF16))
        o_ref[:, sl] = (yo * scale_ref[:, sl]).astype(o_ref.dtype)


def _pool_mixer(zp, w_pool, pool_scale, batch, tp=512):
    m = zp.shape[0]
    seq = m // batch
    tiles_per_seq = seq // tp
    halo_blocks = tp // POOL_HALO
    vmem = (6 * _nbytes((tp, POOL_WIDTH), F32) + 2 * _nbytes(w_pool.shape, F32)
            + 12 * _nbytes((tp, POOL_GROUP), F32) + (8 << 20))
    return pl.pallas_call(
        functools.partial(_pool_kernel, tp=tp),
        grid=(batch, tiles_per_seq),
        in_specs=[pl.BlockSpec((tp, POOL_WIDTH), lambda bi, t: (bi * tiles_per_seq + t, 0)),
                  pl.BlockSpec((POOL_HALO, POOL_WIDTH),
                               lambda bi, t: (jnp.maximum((bi * tiles_per_seq + t) * halo_blocks - 1, 0), 0)),
                  pl.BlockSpec(w_pool.shape, lambda bi, t: (0, 0, 0)),
                  pl.BlockSpec((1, POOL_WIDTH), lambda bi, t: (0, 0))],
        out_specs=pl.BlockSpec((tp, POOL_WIDTH), lambda bi, t: (bi * tiles_per_seq + t, 0)),
        out_shape=jax.ShapeDtypeStruct((m, POOL_WIDTH), BF16),
        compiler_params=_params(2, vmem),
        name="pool_mixer",
    )(zp, zp, w_pool, pool_scale.reshape(1, POOL_WIDTH))


def _mix_out_kernel(a_ref, p_ref, wa_ref, wp_ref, res_ref, gain_ref, o_ref, hg_ref, ssq_ref):
    acc = _dot(a_ref[...], wa_ref[...].astype(BF16)) + _dot(p_ref[...], wp_ref[...].astype(BF16))
    out = res_ref[...] + acc
    o_ref[...] = out
    _emit_prenorm(out, gain_ref, hg_ref, ssq_ref)


def _mix_out(o_attn, o_pool, w, res, next_gain, tm=1024, tn=512):
    m = o_attn.shape[0]
    n = w.shape[1]
    ka, kp = o_attn.shape[1], o_pool.shape[1]
    assert ka == kp
    vmem = (2 * _nbytes((tm, ka + kp), BF16) + _weight_vmem((ka + kp, tn), w.dtype)
            + 8 * _nbytes((tm, tn), F32))
    norm_specs, norm_shapes = _prenorm_specs(m, n, tm, tn)
    return pl.pallas_call(
        _mix_out_kernel,
        grid=(m // tm, n // tn),
        in_specs=[pl.BlockSpec((tm, ka), lambda i, j: (i, 0)),
                  pl.BlockSpec((tm, kp), lambda i, j: (i, 0)),
                  pl.BlockSpec((ka, tn), lambda i, j: (0, j)),
                  pl.BlockSpec((kp, tn), lambda i, j: (1, j)),
                  pl.BlockSpec((tm, tn), lambda i, j: (i, j)),
                  pl.BlockSpec((1, tn), lambda i, j: (0, j))],
        out_specs=[pl.BlockSpec((tm, tn), lambda i, j: (i, j))] + norm_specs,
        out_shape=[jax.ShapeDtypeStruct((m, n), F32)] + norm_shapes,
        compiler_params=_params(2, vmem),
        name="mix_out",
    )(o_attn, o_pool, w, w, res, next_gain.reshape(1, n))


def _cast_kernel(x_ref, o_ref):
    o_ref[...] = x_ref[...].astype(o_ref.dtype)


def _cast_bf16(w, rows=512):
    k, n = w.shape
    rows = min(rows, k)
    return pl.pallas_call(
        _cast_kernel,
        grid=(k // rows,),
        in_specs=[pl.BlockSpec((rows, n), lambda i: (i, 0))],
        out_specs=pl.BlockSpec((rows, n), lambda i: (i, 0)),
        out_shape=jax.ShapeDtypeStruct((k, n), BF16),
        compiler_params=_params(1, 8 * _nbytes((rows, n), F32)),
        name="cast_bf16",
    )(w)


def _cross_kernel(hg_ref, ssq_ref, kv_ref, wq_ref, wo_ref, res_ref, gain_ref, o_ref, u_ref):
    scale = CROSS_DIM ** -0.5
    q = (_dot(hg_ref[...], wq_ref[...]) * _row_scale(ssq_ref, hg_ref.shape[1])).astype(BF16)
    heads = []
    for h in range(CROSS_HEADS):
        sl = slice(h * CROSS_DIM, (h + 1) * CROSS_DIM)
        k = kv_ref[:, sl]
        v = kv_ref[:, CROSS_WIDTH + h * CROSS_DIM:CROSS_WIDTH + (h + 1) * CROSS_DIM]
        s = _dot_nt(q[:, sl], k)
        m = jnp.max(s, axis=1, keepdims=True)
        p = jnp.exp2((s - m) * (scale * LOG2E))
        l = jnp.sum(p, axis=1, keepdims=True)
        heads.append((_dot(p.astype(BF16), v) / l).astype(BF16))
    out = res_ref[...] + _dot(jnp.concatenate(heads, axis=1), wo_ref[...])
    o_ref[...] = out
    ms = jnp.mean(out * out, axis=-1, keepdims=True)
    u_ref[...] = (out * lax.rsqrt(ms + EPS) * gain_ref[...]).astype(u_ref.dtype)


def _cross_attention(hg, ssq, kv, w_q, w_o, res, next_gain, batch, ts=256):
    m, d = res.shape
    mem_len = kv.shape[0] // batch
    steps_per_batch = (m // batch) // ts
    vmem = (2 * _nbytes((ts, d), BF16) * 2 + 4 * _nbytes((ts, d), F32) + _nbytes(w_q.shape, BF16)
            + _nbytes(w_o.shape, BF16) + 4 * _nbytes((ts, d), F32) + (4 << 20))
    row_f32 = pl.BlockSpec((ts, d), lambda i: (i, 0))
    return pl.pallas_call(
        _cross_kernel,
        grid=(m // ts,),
        in_specs=[pl.BlockSpec((ts, d), lambda i: (i, 0)),
                  pl.BlockSpec((ts, V7X_LANES), lambda i: (i, 0)),
                  pl.BlockSpec((mem_len, 2 * CROSS_WIDTH), lambda i: (i // steps_per_batch, 0)),
                  _resident(w_q.shape, lambda i: (0, 0)),
                  _resident(w_o.shape, lambda i: (0, 0)),
                  row_f32,
                  pl.BlockSpec((1, d), lambda i: (0, 0))],
        out_specs=[row_f32, pl.BlockSpec((ts, d), lambda i: (i, 0))],
        out_shape=[jax.ShapeDtypeStruct((m, d), F32), jax.ShapeDtypeStruct((m, d), BF16)],
        compiler_params=_params(1, vmem),
        name="cross_attention",
    )(hg, ssq, kv, w_q, w_o, res, next_gain.reshape(1, d))


def _swiglu_block(h, u, w_in, w_out, next_gain=None, later_weights=()):
    gate, (w_out_bf16, *later_bf16) = _ffn_in(u, w_in, cast_jobs=(w_out,) + tuple(later_weights))
    return _proj_res(gate, w_out_bf16, h, 0.5, tm=1024, tn=256, next_gain=next_gain), later_bf16


def kernel(x, mem, positions, g_ffn1, w_ffn1_in, w_ffn1_out, g_mix, w_mix_in, w_pool, pool_scale, w_mix_out, g_cross, g_mem, w_cross_q, w_cross_kv, w_cross_o, g_ffn2, w_ffn2_in, w_ffn2_out, g_final):
    batch, seq, d = x.shape
    m = batch * seq
    depth = g_ffn1.shape[0]
    cos, sin = _rope_tables(positions)
    h = x.reshape(m, d)
    mem2 = mem.reshape(batch * mem.shape[1], d)
    for l in range(depth):
        u = _rmsnorm(h, g_ffn1[l], BF16)
        (h, hg, ssq), (w_mix_in_bf16, w_mix_out_bf16) = _swiglu_block(
            h, u, w_ffn1_in[l], w_ffn1_out[l], next_gain=g_mix[l],
            later_weights=(w_mix_in[l], w_mix_out[l]))

        outs, lses = [], []
        for group, (_, dil) in enumerate(DILATED_CONFIGS):
            q, k, v = _mix_qkv(hg, ssq, w_mix_in_bf16, cos, sin, group, dil, batch)
            o, lse = _dilated_attention(q, k, v, dil)
            outs.append(o)
            lses.append(lse)
        o_attn = _combine(outs, lses)
        zp = _proj(hg, w_mix_in_bf16, tm=1024, tn=512, out_dtype=F32, col0=ATTN_IN, n=POOL_WIDTH, ssq=ssq)
        o_pool = _pool_mixer(zp, w_pool[l], pool_scale[l], batch)
        h, hg, ssq = _mix_out(o_attn, o_pool, w_mix_out_bf16, h, g_cross[l])

        um = _rmsnorm(mem2, g_mem[l], BF16)
        kv = _proj(um, w_cross_kv[l], tm=um.shape[0], tn=CROSS_WIDTH)
        h, u = _cross_attention(hg, ssq, kv, _cast_bf16(w_cross_q[l]), _cast_bf16(w_cross_o[l]),
                                h, g_ffn2[l], batch)

        h, _ = _swiglu_block(h, u, w_ffn2_in[l], w_ffn2_out[l])
    return _rmsnorm(h, g_final, F32).reshape(batch, seq, d)
```

```python
import functools
import math

import jax
import jax.numpy as jnp
from jax import lax
from jax.experimental import pallas as pl
from jax.experimental.pallas import tpu as pltpu

F32 = jnp.float32
BF16 = jnp.bfloat16

D_MODEL = 4096
HEAD_DIM = 128
ATTN_WIDTH = D_MODEL // 2
N_ATTN_HEADS = ATTN_WIDTH // HEAD_DIM
DILATED_CONFIGS = ((128, 1), (512, 4), (2048, 16))
N_DIL = len(DILATED_CONFIGS)
BLOCK = 128
POOL_WIDTH = D_MODEL - ATTN_WIDTH
POOL_WINDOWS = (2, 4, 8, 16)
POOL_GROUP = POOL_WIDTH // len(POOL_WINDOWS)
QKV_WIDTH = 3 * ATTN_WIDTH
ATTN_IN = N_DIL * QKV_WIDTH
D_FF = ((8 * D_MODEL // 3 + 255) // 256) * 256
CROSS_HEADS = 4
CROSS_DIM = 128
CROSS_WIDTH = CROSS_HEADS * CROSS_DIM
ROPE_THETA = 10000.0
EPS = 1e-6
NEG_INF = -1e30
LOG2E = math.log2(math.e)

V7X_LANES = 128
V7X_BF16_SUBLANES = 16
V7X_VMEM_BYTES = 64 * 1024 * 1024
V7X_VMEM_REQUEST_CAP = V7X_VMEM_BYTES - 6 * 1024 * 1024

POOL_HALO = max(POOL_WINDOWS)
MAX_DIL = max(d for _, d in DILATED_CONFIGS)
ATTN_TILE = BLOCK * MAX_DIL
ATTN_HEADS_PER_STEP = 4
LSE_WIDTH = (N_ATTN_HEADS // ATTN_HEADS_PER_STEP) * V7X_LANES


def _params(n_grid, vmem_bytes):
    return pltpu.CompilerParams(
        dimension_semantics=("arbitrary",) * n_grid,
        vmem_limit_bytes=int(min(vmem_bytes, V7X_VMEM_REQUEST_CAP)),
    )


def _nbytes(shape, dtype):
    return math.prod(shape) * jnp.dtype(dtype).itemsize


def _weight_vmem(block_shape, dtype):
    cast_copy = 0 if dtype == BF16 else _nbytes(block_shape, BF16)
    return 2 * _nbytes(block_shape, dtype) + cast_copy


def _resident(block_shape, index_map):
    return pl.BlockSpec(block_shape, index_map, pipeline_mode=pl.Buffered(1))


def _dot(a, b):
    return jnp.dot(a, b, preferred_element_type=F32)


def _dot_nt(a, b):
    return lax.dot_general(a, b, (((1,), (1,)), ((), ())), preferred_element_type=F32)


def _rmsnorm_kernel(x_ref, g_ref, o_ref):
    x = x_ref[...]
    ms = jnp.mean(x * x, axis=-1, keepdims=True)
    o_ref[...] = (x * lax.rsqrt(ms + EPS) * g_ref[...]).astype(o_ref.dtype)


def _rmsnorm(x, g, out_dtype, tm=256):
    m, d = x.shape
    vmem = 2 * tm * d * (4 + jnp.dtype(out_dtype).itemsize) + 4 * tm * d * 4
    return pl.pallas_call(
        _rmsnorm_kernel,
        grid=(m // tm,),
        in_specs=[pl.BlockSpec((tm, d), lambda i: (i, 0)),
                  pl.BlockSpec((1, d), lambda i: (0, 0))],
        out_specs=pl.BlockSpec((tm, d), lambda i: (i, 0)),
        out_shape=jax.ShapeDtypeStruct((m, d), out_dtype),
        compiler_params=_params(1, vmem),
        name="rmsnorm",
    )(x, g.reshape(1, d))


def _rope_kernel(pos_ref, inv_ref, sign_ref, cos_ref, sin_ref):
    ang = pos_ref[...].astype(F32) * inv_ref[...]
    cos_ref[...] = jnp.cos(ang)
    sin_ref[...] = jnp.sin(ang) * sign_ref[...]


def _rope_tables(positions, tm=1024):
    m = positions.size
    half = HEAD_DIM // 2
    inv = 1.0 / (ROPE_THETA ** (jnp.arange(0, HEAD_DIM, 2, dtype=F32) / HEAD_DIM))
    inv = jnp.concatenate([inv, inv]).reshape(1, HEAD_DIM)
    sign = jnp.concatenate([-jnp.ones((half,), F32), jnp.ones((half,), F32)]).reshape(1, HEAD_DIM)
    row = pl.BlockSpec((1, HEAD_DIM), lambda i: (0, 0))
    tab = pl.BlockSpec((tm, HEAD_DIM), lambda i: (i, 0))
    return pl.pallas_call(
        _rope_kernel,
        grid=(m // tm,),
        in_specs=[pl.BlockSpec((tm, 1), lambda i: (i, 0)), row, row],
        out_specs=[tab, tab],
        out_shape=[jax.ShapeDtypeStruct((m, HEAD_DIM), F32)] * 2,
        compiler_params=_params(1, 16 * tm * HEAD_DIM * 4),
        name="rope_tables",
    )(positions.reshape(m, 1), inv, sign)


def _ffn_in_kernel(u_ref, wa_ref, wb_ref, *rest):
    n_jobs = (len(rest) - 1) // 2
    srcs, o_ref, dsts = rest[:n_jobs], rest[n_jobs], rest[n_jobs + 1:]
    half = o_ref.shape[1] // 2
    for c in range(2):
        sl = slice(c * half, (c + 1) * half)
        w = jnp.concatenate([wa_ref[:, sl].astype(BF16), wb_ref[:, sl].astype(BF16)], axis=1)
        ab = _dot(u_ref[...], w)
        a, b = ab[:, :half], ab[:, half:]
        o_ref[:, sl] = (a * jax.nn.sigmoid(a) * b).astype(o_ref.dtype)
    for src, dst in zip(srcs, dsts):
        dst[...] = src[...].astype(dst.dtype)


def _ffn_in(u, w_in, cast_jobs=(), tm=2048, tf=256):
    m, d = u.shape
    nf = D_FF // tf
    n_steps = (m // tm) * nf
    vmem = (_nbytes((tm, d), BF16) + 2 * 2 * _nbytes((d, tf), F32)
            + 2 * _nbytes((d, tf), BF16) + 2 * _nbytes((tm, tf), BF16)
            + 6 * _nbytes((tm, tf), F32))
    job_specs, job_shapes = [], []
    for w in cast_jobs:
        rows = next(r for r in range(V7X_BF16_SUBLANES, w.shape[0] + 1, V7X_BF16_SUBLANES)
                    if w.shape[0] % r == 0 and w.shape[0] // r <= n_steps)
        n_blocks = w.shape[0] // rows
        job_specs.append(pl.BlockSpec(
            (rows, w.shape[1]), lambda i, j, n_blocks=n_blocks: (jnp.minimum(i * nf + j, n_blocks - 1), 0)))
        job_shapes.append(jax.ShapeDtypeStruct(w.shape, BF16))
        vmem += 2 * _nbytes((rows, w.shape[1]), F32) + 2 * _nbytes((rows, w.shape[1]), BF16)
    outs = pl.pallas_call(
        _ffn_in_kernel,
        grid=(m // tm, nf),
        in_specs=[_resident((tm, d), lambda i, j: (i, 0)),
                  pl.BlockSpec((d, tf), lambda i, j: (0, j)),
                  pl.BlockSpec((d, tf), lambda i, j: (0, j + nf))] + job_specs,
        out_specs=[pl.BlockSpec((tm, tf), lambda i, j: (i, j))] + job_specs,
        out_shape=[jax.ShapeDtypeStruct((m, D_FF), BF16)] + job_shapes,
        compiler_params=_params(2, vmem),
        name="ffn_in",
    )(u, w_in, w_in, *cast_jobs)
    return outs[0], outs[1:]


def _emit_prenorm(out, gain_ref, hg_ref, ssq_ref):
    hg_ref[...] = (out * gain_ref[...]).astype(hg_ref.dtype)

    @pl.when(pl.program_id(1) == 0)
    def _():
        ssq_ref[...] = jnp.zeros_like(ssq_ref)

    ssq_ref[...] += jnp.broadcast_to(jnp.sum(out * out, axis=1, keepdims=True), ssq_ref.shape)


def _row_scale(ssq_ref, d):
    return lax.rsqrt(ssq_ref[:, :1] * (1.0 / d) + EPS)


def _prenorm_specs(m, n, tm, tn):
    specs = [pl.BlockSpec((tm, tn), lambda i, j: (i, j)), pl.BlockSpec((tm, V7X_LANES), lambda i, j: (i, 0))]
    shapes = [jax.ShapeDtypeStruct((m, n), BF16), jax.ShapeDtypeStruct((m, V7X_LANES), F32)]
    return specs, shapes


def _proj_res_kernel(lhs_ref, w_ref, res_ref, *rest, scale):
    acc = _dot(lhs_ref[...], w_ref[...].astype(BF16))
    out = res_ref[...] + scale * acc
    if len(rest) == 1:
        rest[0][...] = out
    else:
        gain_ref, o_ref, hg_ref, ssq_ref = rest
        o_ref[...] = out
        _emit_prenorm(out, gain_ref, hg_ref, ssq_ref)


def _proj_res(lhs, w, res, scale, tm, tn, next_gain=None):
    m, k = lhs.shape
    n = w.shape[1]
    vmem = 2 * _nbytes((tm, k), BF16) + _weight_vmem((k, tn), w.dtype) + 8 * _nbytes((tm, tn), F32)
    in_specs = [pl.BlockSpec((tm, k), lambda i, j: (i, 0)),
                pl.BlockSpec((k, tn), lambda i, j: (0, j)),
                pl.BlockSpec((tm, tn), lambda i, j: (i, j))]
    out_specs = [pl.BlockSpec((tm, tn), lambda i, j: (i, j))]
    out_shape = [jax.ShapeDtypeStruct((m, n), F32)]
    args = [lhs, w, res]
    if next_gain is not None:
        in_specs.append(pl.BlockSpec((1, tn), lambda i, j: (0, j)))
        args.append(next_gain.reshape(1, n))
        specs, shapes = _prenorm_specs(m, n, tm, tn)
        out_specs += specs
        out_shape += shapes
    outs = pl.pallas_call(
        functools.partial(_proj_res_kernel, scale=scale),
        grid=(m // tm, n // tn),
        in_specs=in_specs,
        out_specs=out_specs,
        out_shape=out_shape,
        compiler_params=_params(2, vmem),
        name="proj_residual",
    )(*args)
    return outs[0] if next_gain is None else outs


def _proj_kernel(lhs_ref, w_ref, *rest):
    acc = _dot(lhs_ref[...], w_ref[...].astype(BF16))
    if len(rest) == 2:
        acc = acc * _row_scale(rest[0], lhs_ref.shape[1])
    rest[-1][...] = acc.astype(rest[-1].dtype)


def _proj(lhs, w, tm, tn, out_dtype=BF16, col0=0, n=None, ssq=None):
    m, k = lhs.shape
    n = w.shape[1] if n is None else n
    col_blk0 = col0 // tn
    vmem = 2 * _nbytes((tm, k), BF16) + _weight_vmem((k, tn), w.dtype) + 6 * _nbytes((tm, tn), F32)
    in_specs = [pl.BlockSpec((tm, k), lambda i, j: (i, 0)),
                pl.BlockSpec((k, tn), lambda i, j: (0, j + col_blk0))]
    args = [lhs, w]
    if ssq is not None:
        in_specs.append(pl.BlockSpec((tm, V7X_LANES), lambda i, j: (i, 0)))
        args.append(ssq)
    return pl.pallas_call(
        _proj_kernel,
        grid=(m // tm, n // tn),
        in_specs=in_specs,
        out_specs=pl.BlockSpec((tm, tn), lambda i, j: (i, j)),
        out_shape=jax.ShapeDtypeStruct((m, n), out_dtype),
        compiler_params=_params(2, vmem),
        name="proj",
    )(*args)


def _mix_qkv_kernel(u_ref, ssq_ref, wq_ref, wk_ref, wv_ref, cos_ref, sin_ref, q_ref, k_ref, v_ref,
                    scr_ref, tab_ref, *, tn, dil):
    tm, d_in = u_ref.shape
    rows = tm // dil

    def regroup(slab):
        if dil == 1:
            return [scr_ref[slab]]
        return [scr_ref[slab, pl.ds(r, rows, stride=dil), :] for r in range(dil)]

    @pl.when(pl.program_id(1) == 0)
    def _():
        rs = _row_scale(ssq_ref, d_in)
        tables = (cos_ref[...] * rs, sin_ref[...] * rs, jnp.broadcast_to(rs, (tm, HEAD_DIM)))
        for t, table in enumerate(tables):
            scr_ref[0] = table
            for r, piece in enumerate(regroup(0)):
                tab_ref[t, r] = piece

    heads = tn // HEAD_DIM
    for part, (w_ref, o_ref) in enumerate(((wq_ref, q_ref), (wk_ref, k_ref), (wv_ref, v_ref))):
        acc = _dot(u_ref[...], w_ref[...].astype(BF16))
        for h in range(heads):
            scr_ref[part * heads + h] = acc[:, h * HEAD_DIM:(h + 1) * HEAD_DIM]
        for h in range(heads):
            for r, piece in enumerate(regroup(part * heads + h)):
                if o_ref is v_ref:
                    piece = piece * tab_ref[2, r]
                else:
                    piece = piece * tab_ref[0, r] + pltpu.roll(piece, HEAD_DIM // 2, axis=1) * tab_ref[1, r]
                o_ref[r, :, h * HEAD_DIM:(h + 1) * HEAD_DIM] = piece.astype(o_ref.dtype)


def _mix_qkv(u, ssq, w, cos, sin, group, dil, batch, tm=1024, tn=256):
    m, k = u.shape
    seq = m // batch
    tiles_per_seq = seq // tm
    part_blocks = ATTN_WIDTH // tn

    def w_spec(part):
        col_blk0 = (group * 3 + part) * part_blocks
        return pl.BlockSpec((k, tn), lambda i, j: (0, j + col_blk0))

    out_spec = pl.BlockSpec((None, dil, tm // dil, tn),
                            lambda i, j: (i // tiles_per_seq, 0, i % tiles_per_seq, j))
    out_shape = jax.ShapeDtypeStruct((batch, dil, seq // dil, ATTN_WIDTH), BF16)
    vmem = (2 * _nbytes((tm, k), BF16) + 3 * _weight_vmem((k, tn), w.dtype)
            + 3 * 6 * _nbytes((tm, tn), F32) + 10 * _nbytes((tm, HEAD_DIM), F32))
    return pl.pallas_call(
        functools.partial(_mix_qkv_kernel, tn=tn, dil=dil),
        grid=(m // tm, part_blocks),
        in_specs=[pl.BlockSpec((tm, k), lambda i, j: (i, 0)),
                  pl.BlockSpec((tm, V7X_LANES), lambda i, j: (i, 0)),
                  w_spec(0), w_spec(1), w_spec(2),
                  pl.BlockSpec((tm, HEAD_DIM), lambda i, j: (i, 0)),
                  pl.BlockSpec((tm, HEAD_DIM), lambda i, j: (i, 0))],
        out_specs=[out_spec] * 3,
        out_shape=[out_shape] * 3,
        scratch_shapes=[pltpu.VMEM((3 * tn // HEAD_DIM, tm, HEAD_DIM), F32),
                        pltpu.VMEM((3, dil, tm // dil, HEAD_DIM), F32)],
        compiler_params=_params(2, vmem),
        name=f"mix_qkv_{dil}",
    )(u, ssq, w, w, w, cos, sin)


def _dilated_attn_kernel(q_ref, kp_ref, kc_ref, vp_ref, vc_ref, o_ref, lse_ref,
                         o_scr, lse_scr, *, dil):
    nq = MAX_DIL // dil
    nblk = dil * nq
    qi = lax.broadcasted_iota(jnp.int32, (nblk, BLOCK, BLOCK), 1)
    kj = lax.broadcasted_iota(jnp.int32, (nblk, BLOCK, BLOCK), 2)
    blk = lax.broadcasted_iota(jnp.int32, (nblk, BLOCK, BLOCK), 0)
    has_prev = jnp.logical_or(pl.program_id(1) > 0, blk % nq != 0)
    mask_prev = jnp.logical_and(kj >= qi, has_prev)
    mask_cur = kj <= qi
    scale = HEAD_DIM ** -0.5
    lane = lax.broadcasted_iota(jnp.int32, (nblk, BLOCK, V7X_LANES), 2)
    lse_tile = jnp.zeros((nblk, BLOCK, V7X_LANES), F32)

    def blocks(ref, sl):
        return ref[:, :, sl].reshape(nblk, BLOCK, HEAD_DIM)

    def prev_blocks(first_ref, cur, sl):
        first = first_ref[:, :, sl]
        if nq == 1:
            return first
        cur4 = cur.reshape(dil, nq, BLOCK, HEAD_DIM)
        return jnp.concatenate([first[:, None], cur4[:, :-1]], axis=1).reshape(nblk, BLOCK, HEAD_DIM)

    def token_rows(g):
        r, i = divmod(g, nq)
        return pl.ds(i * BLOCK * dil + r, BLOCK, stride=dil)

    for h in range(ATTN_HEADS_PER_STEP):
        sl = slice(h * HEAD_DIM, (h + 1) * HEAD_DIM)
        q = blocks(q_ref, sl)
        kc = blocks(kc_ref, sl)
        vc = blocks(vc_ref, sl)
        kp = prev_blocks(kp_ref, kc, sl)
        vp = prev_blocks(vp_ref, vc, sl)
        sp = jnp.einsum("gqd,gkd->gqk", q, kp, preferred_element_type=F32)
        sc = jnp.einsum("gqd,gkd->gqk", q, kc, preferred_element_type=F32)
        sp = jnp.where(mask_prev, sp, NEG_INF)
        sc = jnp.where(mask_cur, sc, NEG_INF)
        m = jnp.maximum(jnp.max(sp, axis=2, keepdims=True), jnp.max(sc, axis=2, keepdims=True))
        pp = jnp.exp2((sp - m) * (scale * LOG2E))
        pc = jnp.exp2((sc - m) * (scale * LOG2E))
        l = jnp.sum(pp, axis=2, keepdims=True) + jnp.sum(pc, axis=2, keepdims=True)
        o = (jnp.einsum("gqk,gkd->gqd", pp.astype(BF16), vp, preferred_element_type=F32)
             + jnp.einsum("gqk,gkd->gqd", pc.astype(BF16), vc, preferred_element_type=F32)) / l
        for g in range(nblk):
            o_scr[h, token_rows(g), :] = o[g]
        lse_tile = jnp.where(lane == h, m * scale + jnp.log(l), lse_tile)
    for g in range(nblk):
        lse_scr[token_rows(g), :] = lse_tile[g]
    for h in range(ATTN_HEADS_PER_STEP):
        o_ref[:, h * HEAD_DIM:(h + 1) * HEAD_DIM] = o_scr[h].astype(o_ref.dtype)
    lse_ref[...] = lse_scr[...]


def _dilated_attention(q, k, v, dil):
    b, _, sub_len, _ = q.shape
    seq = sub_len * dil
    rows = ATTN_TILE // dil
    tiles_per_seq = seq // ATTN_TILE
    width = ATTN_HEADS_PER_STEP * HEAD_DIM
    col_blocks = ATTN_WIDTH // width

    cur = pl.BlockSpec((None, dil, rows, width), lambda bi, t, hg: (bi, 0, t, hg))
    prev = pl.BlockSpec((None, dil, BLOCK, width),
                        lambda bi, t, hg: (bi, 0, jnp.maximum(t * (rows // BLOCK) - 1, 0), hg))

    vmem = (6 * _nbytes((dil, rows, width), BF16) + 4 * _nbytes((dil, BLOCK, width), BF16)
            + 2 * _nbytes((ATTN_TILE, width), BF16) + 3 * _nbytes((ATTN_TILE, V7X_LANES), F32)
            + _nbytes((ATTN_TILE, width), F32) + 12 * _nbytes((ATTN_TILE, BLOCK), F32))
    o, lse = pl.pallas_call(
        functools.partial(_dilated_attn_kernel, dil=dil),
        grid=(b, tiles_per_seq, col_blocks),
        in_specs=[cur, prev, cur, prev, cur],
        out_specs=[pl.BlockSpec((ATTN_TILE, width), lambda bi, t, hg: (bi * tiles_per_seq + t, hg)),
                   pl.BlockSpec((ATTN_TILE, V7X_LANES), lambda bi, t, hg: (bi * tiles_per_seq + t, hg))],
        out_shape=[jax.ShapeDtypeStruct((b * seq, ATTN_WIDTH), BF16),
                   jax.ShapeDtypeStruct((b * seq, LSE_WIDTH), F32)],
        scratch_shapes=[pltpu.VMEM((ATTN_HEADS_PER_STEP, ATTN_TILE, HEAD_DIM), F32),
                        pltpu.VMEM((ATTN_TILE, V7X_LANES), F32)],
        compiler_params=_params(3, vmem),
        name=f"dilated_attn_{dil}",
    )(q, k, k, v, v)
    return o, lse


def _combine_kernel(o0_ref, o1_ref, o2_ref, l0_ref, l1_ref, l2_ref, out_ref):
    l0, l1, l2 = l0_ref[...], l1_ref[...], l2_ref[...]
    m = jnp.maximum(jnp.maximum(l0, l1), l2)
    e0, e1, e2 = jnp.exp(l0 - m), jnp.exp(l1 - m), jnp.exp(l2 - m)
    den = e0 + e1 + e2
    w0, w1, w2 = e0 / den, e1 / den, e2 / den
    for h in range(N_ATTN_HEADS):
        sl = slice(h * HEAD_DIM, (h + 1) * HEAD_DIM)
        c = (h // ATTN_HEADS_PER_STEP) * V7X_LANES + h % ATTN_HEADS_PER_STEP
        acc = (w0[:, c:c + 1] * o0_ref[:, sl].astype(F32)
               + w1[:, c:c + 1] * o1_ref[:, sl].astype(F32)
               + w2[:, c:c + 1] * o2_ref[:, sl].astype(F32))
        out_ref[:, sl] = acc.astype(out_ref.dtype)


def _combine(outs, lses, tq=512):
    m = outs[0].shape[0]
    o_spec = pl.BlockSpec((tq, ATTN_WIDTH), lambda i: (i, 0))
    l_spec = pl.BlockSpec((tq, LSE_WIDTH), lambda i: (i, 0))
    vmem = 8 * _nbytes((tq, ATTN_WIDTH), BF16) + 16 * _nbytes((tq, LSE_WIDTH), F32) + (8 << 20)
    return pl.pallas_call(
        _combine_kernel,
        grid=(m // tq,),
        in_specs=[o_spec] * 3 + [l_spec] * 3,
        out_specs=o_spec,
        out_shape=jax.ShapeDtypeStruct((m, ATTN_WIDTH), BF16),
        compiler_params=_params(1, vmem),
        name="attn_combine",
    )(*outs, *lses)


def _pool_kernel(zm_ref, zh_ref, wp_ref, scale_ref, o_ref, *, tp):
    t = pl.program_id(1)
    pos = t * tp + lax.broadcasted_iota(jnp.int32, (tp, 1), 0)
    for g, win in enumerate(POOL_WINDOWS):
        sl = slice(g * POOL_GROUP, (g + 1) * POOL_GROUP)
        main = zm_ref[:, sl]
        halo = jnp.where(t > 0, zh_ref[:, sl], 0.0)
        s = jnp.concatenate([halo, main], axis=0)
        shift = 1
        while shift < win:
            s = s + pltpu.roll(s, shift, axis=0)
            shift *= 2
        cnt = jnp.minimum(pos + 1, win).astype(F32)
        y = s[POOL_HALO:] / cnt - main
        yo = _dot(y.astype(BF16), wp_ref[g].astype(BF16))
        o_ref[:, sl] = (yo * scale_ref[:, sl]).astype(o_ref.dtype)


def _pool_mixer(zp, w_pool, pool_scale, batch, tp=512):
    m = zp.shape[0]
    seq = m // batch
    tiles_per_seq = seq // tp
    halo_blocks = tp // POOL_HALO
    vmem = (6 * _nbytes((tp, POOL_WIDTH), F32) + 2 * _nbytes(w_pool.shape, F32)
            + 12 * _nbytes((tp, POOL_GROUP), F32) + (8 << 20))
    return pl.pallas_call(
        functools.partial(_pool_kernel, tp=tp),
        grid=(batch, tiles_per_seq),
        in_specs=[pl.BlockSpec((tp, POOL_WIDTH), lambda bi, t: (bi * tiles_per_seq + t, 0)),
                  pl.BlockSpec((POOL_HALO, POOL_WIDTH),
                               lambda bi, t: (jnp.maximum((bi * tiles_per_seq + t) * halo_blocks - 1, 0), 0)),
                  pl.BlockSpec(w_pool.shape, lambda bi, t: (0, 0, 0)),
                  pl.BlockSpec((1, POOL_WIDTH), lambda bi, t: (0, 0))],
        out_specs=pl.BlockSpec((tp, POOL_WIDTH), lambda bi, t: (bi * tiles_per_seq + t, 0)),
        out_shape=jax.ShapeDtypeStruct((m, POOL_WIDTH), BF16),
        compiler_params=_params(2, vmem),
        name="pool_mixer",
    )(zp, zp, w_pool, pool_scale.reshape(1, POOL_WIDTH))


def _mix_out_kernel(a_ref, p_ref, wa_ref, wp_ref, res_ref, gain_ref, o_ref, hg_ref, ssq_ref):
    acc = _dot(a_ref[...], wa_ref[...].astype(BF16)) + _dot(p_ref[...], wp_ref[...].astype(BF16))
    out = res_ref[...] + acc
    o_ref[...] = out
    _emit_prenorm(out, gain_ref, hg_ref, ssq_ref)


def _mix_out(o_attn, o_pool, w, res, next_gain, tm=1024, tn=512):
    m = o_attn.shape[0]
    n = w.shape[1]
    ka, kp = o_attn.shape[1], o_pool.shape[1]
    assert ka == kp
    vmem = (2 * _nbytes((tm, ka + kp), BF16) + _weight_vmem((ka + kp, tn), w.dtype)
            + 8 * _nbytes((tm, tn), F32))
    norm_specs, norm_shapes = _prenorm_specs(m, n, tm, tn)
    return pl.pallas_call(
        _mix_out_kernel,
        grid=(m // tm, n // tn),
        in_specs=[pl.BlockSpec((tm, ka), lambda i, j: (i, 0)),
                  pl.BlockSpec((tm, kp), lambda i, j: (i, 0)),
                  pl.BlockSpec((ka, tn), lambda i, j: (0, j)),
                  pl.BlockSpec((kp, tn), lambda i, j: (1, j)),
                  pl.BlockSpec((tm, tn), lambda i, j: (i, j)),
                  pl.BlockSpec((1, tn), lambda i, j: (0, j))],
        out_specs=[pl.BlockSpec((tm, tn), lambda i, j: (i, j))] + norm_specs,
        out_shape=[jax.ShapeDtypeStruct((m, n), F32)] + norm_shapes,
        compiler_params=_params(2, vmem),
        name="mix_out",
    )(o_attn, o_pool, w, w, res, next_gain.reshape(1, n))


def _cast_kernel(x_ref, o_ref):
    o_ref[...] = x_ref[...].astype(o_ref.dtype)


def _cast_bf16(w, rows=512):
    k, n = w.shape
    rows = min(rows, k)
    return pl.pallas_call(
        _cast_kernel,
        grid=(k // rows,),
        in_specs=[pl.BlockSpec((rows, n), lambda i: (i, 0))],
        out_specs=pl.BlockSpec((rows, n), lambda i: (i, 0)),
        out_shape=jax.ShapeDtypeStruct((k, n), BF16),
        compiler_params=_params(1, 8 * _nbytes((rows, n), F32)),
        name="cast_bf16",
    )(w)


def _cross_kernel(hg_ref, ssq_ref, kv_ref, wq_ref, wo_ref, res_ref, gain_ref, o_ref, u_ref):
    scale = CROSS_DIM ** -0.5
    q = (_dot(hg_ref[...], wq_ref[...]) * _row_scale(ssq_ref, hg_ref.shape[1])).astype(BF16)
    heads = []
    for h in range(CROSS_HEADS):
        sl = slice(h * CROSS_DIM, (h + 1) * CROSS_DIM)
        k = kv_ref[:, sl]
        v = kv_ref[:, CROSS_WIDTH + h * CROSS_DIM:CROSS_WIDTH + (h + 1) * CROSS_DIM]
        s = _dot_nt(q[:, sl], k)
        m = jnp.max(s, axis=1, keepdims=True)
        p = jnp.exp2((s - m) * (scale * LOG2E))
        l = jnp.sum(p, axis=1, keepdims=True)
        heads.append((_dot(p.astype(BF16), v) / l).astype(BF16))
    out = res_ref[...] + _dot(jnp.concatenate(heads, axis=1), wo_ref[...])
    o_ref[...] = out
    ms = jnp.mean(out * out, axis=-1, keepdims=True)
    u_ref[...] = (out * lax.rsqrt(ms + EPS) * gain_ref[...]).astype(u_ref.dtype)


def _cross_attention(hg, ssq, kv, w_q, w_o, res, next_gain, batch, ts=256):
    m, d = res.shape
    mem_len = kv.shape[0] // batch
    steps_per_batch = (m // batch) // ts
    vmem = (2 * _nbytes((ts, d), BF16) * 2 + 4 * _nbytes((ts, d), F32) + _nbytes(w_q.shape, BF16)
            + _nbytes(w_o.shape, BF16) + 4 * _nbytes((ts, d), F32) + (4 << 20))
    row_f32 = pl.BlockSpec((ts, d), lambda i: (i, 0))
    return pl.pallas_call(
        _cross_kernel,
        grid=(m // ts,),
        in_specs=[pl.BlockSpec((ts, d), lambda i: (i, 0)),
                  pl.BlockSpec((ts, V7X_LANES), lambda i: (i, 0)),
                  pl.BlockSpec((mem_len, 2 * CROSS_WIDTH), lambda i: (i // steps_per_batch, 0)),
                  _resident(w_q.shape, lambda i: (0, 0)),
                  _resident(w_o.shape, lambda i: (0, 0)),
                  row_f32,
                  pl.BlockSpec((1, d), lambda i: (0, 0))],
        out_specs=[row_f32, pl.BlockSpec((ts, d), lambda i: (i, 0))],
        out_shape=[jax.ShapeDtypeStruct((m, d), F32), jax.ShapeDtypeStruct((m, d), BF16)],
        compiler_params=_params(1, vmem),
        name="cross_attention",
    )(hg, ssq, kv, w_q, w_o, res, next_gain.reshape(1, d))


def _swiglu_block(h, u, w_in, w_out, next_gain=None, later_weights=()):
    gate, (w_out_bf16, *later_bf16) = _ffn_in(u, w_in, cast_jobs=(w_out,) + tuple(later_weights))
    return _proj_res(gate, w_out_bf16, h, 0.5, tm=512, tn=512, next_gain=next_gain), later_bf16


def kernel(x, mem, positions, g_ffn1, w_ffn1_in, w_ffn1_out, g_mix, w_mix_in, w_pool, pool_scale, w_mix_out, g_cross, g_mem, w_cross_q, w_cross_kv, w_cross_o, g_ffn2, w_ffn2_in, w_ffn2_out, g_final):
    batch, seq, d = x.shape
    m = batch * seq
    depth = g_ffn1.shape[0]
    cos, sin = _rope_tables(positions)
    h = x.reshape(m, d)
    mem2 = mem.reshape(batch * mem.shape[1], d)
    for l in range(depth):
        u = _rmsnorm(h, g_ffn1[l], BF16)
        (h, hg, ssq), (w_mix_in_bf16, w_mix_out_bf16) = _swiglu_block(
            h, u, w_ffn1_in[l], w_ffn1_out[l], next_gain=g_mix[l],
            later_weights=(w_mix_in[l], w_mix_out[l]))

        outs, lses = [], []
        for group, (_, dil) in enumerate(DILATED_CONFIGS):
            q, k, v = _mix_qkv(hg, ssq, w_mix_in_bf16, cos, sin, group, dil, batch)
            o, lse = _dilated_attention(q, k, v, dil)
            outs.append(o)
            lses.append(lse)
        o_attn = _combine(outs, lses)
        zp = _proj(hg, w_mix_in_bf16, tm=1024, tn=1024, out_dtype=F32, col0=ATTN_IN, n=POOL_WIDTH, ssq=ssq)
        o_pool = _pool_mixer(zp, w_pool[l], pool_scale[l], batch)
        h, hg, ssq = _mix_out(o_attn, o_pool, w_mix_out_bf16, h, g_cross[l])

        um = _rmsnorm(mem2, g_mem[l], BF16)
        kv = _proj(um, w_cross_kv[l], tm=um.shape[0], tn=CROSS_WIDTH)
        h, u = _cross_attention(hg, ssq, kv, _cast_bf16(w_cross_q[l]), _cast_bf16(w_cross_o[l]),
                                h, g_ffn2[l], batch)

        h, _ = _swiglu_block(h, u, w_ffn2_in[l], w_ffn2_out[l])
    return _rmsnorm(h, g_final, F32).reshape(batch, seq, d)
```

```python
import functools
import math

import jax
import jax.numpy as jnp
from jax import lax
from jax.experimental import pallas as pl
from jax.experimental.pallas import tpu as pltpu

F32 = jnp.float32
BF16 = jnp.bfloat16

D_MODEL = 4096
HEAD_DIM = 128
ATTN_WIDTH = D_MODEL // 2
N_ATTN_HEADS = ATTN_WIDTH // HEAD_DIM
DILATED_CONFIGS = ((128, 1), (512, 4), (2048, 16))
N_DIL = len(DILATED_CONFIGS)
BLOCK = 128
POOL_WIDTH = D_MODEL - ATTN_WIDTH
POOL_WINDOWS = (2, 4, 8, 16)
POOL_GROUP = POOL_WIDTH // len(POOL_WINDOWS)
QKV_WIDTH = 3 * ATTN_WIDTH
ATTN_IN = N_DIL * QKV_WIDTH
D_FF = ((8 * D_MODEL // 3 + 255) // 256) * 256
CROSS_HEADS = 4
CROSS_DIM = 128
CROSS_WIDTH = CROSS_HEADS * CROSS_DIM
ROPE_THETA = 10000.0
EPS = 1e-6
NEG_INF = -1e30
LOG2E = math.log2(math.e)

V7X_LANES = 128
V7X_BF16_SUBLANES = 16
V7X_VMEM_BYTES = 64 * 1024 * 1024
V7X_VMEM_REQUEST_CAP = V7X_VMEM_BYTES - 6 * 1024 * 1024

POOL_HALO = max(POOL_WINDOWS)
MAX_DIL = max(d for _, d in DILATED_CONFIGS)
ATTN_TILE = BLOCK * MAX_DIL
ATTN_HEADS_PER_STEP = 4
LSE_WIDTH = (N_ATTN_HEADS // ATTN_HEADS_PER_STEP) * V7X_LANES


def _params(n_grid, vmem_bytes):
    return pltpu.CompilerParams(
        dimension_semantics=("arbitrary",) * n_grid,
        vmem_limit_bytes=int(min(vmem_bytes, V7X_VMEM_REQUEST_CAP)),
    )


def _nbytes(shape, dtype):
    return math.prod(shape) * jnp.dtype(dtype).itemsize


def _weight_vmem(block_shape, dtype):
    cast_copy = 0 if dtype == BF16 else _nbytes(block_shape, BF16)
    return 2 * _nbytes(block_shape, dtype) + cast_copy


def _resident(block_shape, index_map):
    return pl.BlockSpec(block_shape, index_map, pipeline_mode=pl.Buffered(1))


def _dot(a, b):
    return jnp.dot(a, b, preferred_element_type=F32)


def _dot_nt(a, b):
    return lax.dot_general(a, b, (((1,), (1,)), ((), ())), preferred_element_type=F32)


def _rmsnorm_kernel(x_ref, g_ref, o_ref):
    x = x_ref[...]
    ms = jnp.mean(x * x, axis=-1, keepdims=True)
    o_ref[...] = (x * lax.rsqrt(ms + EPS) * g_ref[...]).astype(o_ref.dtype)


def _rmsnorm(x, g, out_dtype, tm=256):
    m, d = x.shape
    vmem = 2 * tm * d * (4 + jnp.dtype(out_dtype).itemsize) + 4 * tm * d * 4
    return pl.pallas_call(
        _rmsnorm_kernel,
        grid=(m // tm,),
        in_specs=[pl.BlockSpec((tm, d), lambda i: (i, 0)),
                  pl.BlockSpec((1, d), lambda i: (0, 0))],
        out_specs=pl.BlockSpec((tm, d), lambda i: (i, 0)),
        out_shape=jax.ShapeDtypeStruct((m, d), out_dtype),
        compiler_params=_params(1, vmem),
        name="rmsnorm",
    )(x, g.reshape(1, d))


def _rope_kernel(pos_ref, inv_ref, sign_ref, cos_ref, sin_ref):
    ang = pos_ref[...].astype(F32) * inv_ref[...]
    cos_ref[...] = jnp.cos(ang)
    sin_ref[...] = jnp.sin(ang) * sign_ref[...]


def _rope_tables(positions, tm=1024):
    m = positions.size
    half = HEAD_DIM // 2
    inv = 1.0 / (ROPE_THETA ** (jnp.arange(0, HEAD_DIM, 2, dtype=F32) / HEAD_DIM))
    inv = jnp.concatenate([inv, inv]).reshape(1, HEAD_DIM)
    sign = jnp.concatenate([-jnp.ones((half,), F32), jnp.ones((half,), F32)]).reshape(1, HEAD_DIM)
    row = pl.BlockSpec((1, HEAD_DIM), lambda i: (0, 0))
    tab = pl.BlockSpec((tm, HEAD_DIM), lambda i: (i, 0))
    return pl.pallas_call(
        _rope_kernel,
        grid=(m // tm,),
        in_specs=[pl.BlockSpec((tm, 1), lambda i: (i, 0)), row, row],
        out_specs=[tab, tab],
        out_shape=[jax.ShapeDtypeStruct((m, HEAD_DIM), F32)] * 2,
        compiler_params=_params(1, 16 * tm * HEAD_DIM * 4),
        name="rope_tables",
    )(positions.reshape(m, 1), inv, sign)


def _ffn_in_kernel(u_ref, wa_ref, wb_ref, *rest):
    n_jobs = (len(rest) - 1) // 2
    srcs, o_ref, dsts = rest[:n_jobs], rest[n_jobs], rest[n_jobs + 1:]
    half = o_ref.shape[1] // 2
    for c in range(2):
        sl = slice(c * half, (c + 1) * half)
        w = jnp.concatenate([wa_ref[:, sl].astype(BF16), wb_ref[:, sl].astype(BF16)], axis=1)
        ab = _dot(u_ref[...], w)
        a, b = ab[:, :half], ab[:, half:]
        o_ref[:, sl] = (a * jax.nn.sigmoid(a) * b).astype(o_ref.dtype)
    for src, dst in zip(srcs, dsts):
        dst[...] = src[...].astype(dst.dtype)


def _ffn_in(u, w_in, cast_jobs=(), tm=2048, tf=256):
    m, d = u.shape
    nf = D_FF // tf
    n_steps = (m // tm) * nf
    lhs_spec, lhs_buffers = (pl.BlockSpec, 2) if w_in.dtype == BF16 else (_resident, 1)
    vmem = (lhs_buffers * _nbytes((tm, d), BF16) + 2 * _weight_vmem((d, tf), w_in.dtype)
            + 2 * _nbytes((tm, tf), BF16) + 6 * _nbytes((tm, tf), F32))
    job_specs, job_shapes = [], []
    for w in cast_jobs:
        rows = next(r for r in range(V7X_BF16_SUBLANES, w.shape[0] + 1, V7X_BF16_SUBLANES)
                    if w.shape[0] % r == 0 and w.shape[0] // r <= n_steps)
        n_blocks = w.shape[0] // rows
        job_specs.append(pl.BlockSpec(
            (rows, w.shape[1]), lambda i, j, n_blocks=n_blocks: (jnp.minimum(i * nf + j, n_blocks - 1), 0)))
        job_shapes.append(jax.ShapeDtypeStruct(w.shape, BF16))
        vmem += 2 * _nbytes((rows, w.shape[1]), F32) + 2 * _nbytes((rows, w.shape[1]), BF16)
    outs = pl.pallas_call(
        _ffn_in_kernel,
        grid=(m // tm, nf),
        in_specs=[lhs_spec((tm, d), lambda i, j: (i, 0)),
                  pl.BlockSpec((d, tf), lambda i, j: (0, j)),
                  pl.BlockSpec((d, tf), lambda i, j: (0, j + nf))] + job_specs,
        out_specs=[pl.BlockSpec((tm, tf), lambda i, j: (i, j))] + job_specs,
        out_shape=[jax.ShapeDtypeStruct((m, D_FF), BF16)] + job_shapes,
        compiler_params=_params(2, vmem),
        name="ffn_in",
    )(u, w_in, w_in, *cast_jobs)
    return outs[0], outs[1:]


def _emit_prenorm(out, gain_ref, hg_ref, ssq_ref):
    hg_ref[...] = (out * gain_ref[...]).astype(hg_ref.dtype)

    @pl.when(pl.program_id(1) == 0)
    def _():
        ssq_ref[...] = jnp.zeros_like(ssq_ref)

    ssq_ref[...] += jnp.broadcast_to(jnp.sum(out * out, axis=1, keepdims=True), ssq_ref.shape)


def _row_scale(ssq_ref, d):
    return lax.rsqrt(ssq_ref[:, :1] * (1.0 / d) + EPS)


def _prenorm_specs(m, n, tm, tn):
    specs = [pl.BlockSpec((tm, tn), lambda i, j: (i, j)), pl.BlockSpec((tm, V7X_LANES), lambda i, j: (i, 0))]
    shapes = [jax.ShapeDtypeStruct((m, n), BF16), jax.ShapeDtypeStruct((m, V7X_LANES), F32)]
    return specs, shapes


def _proj_res_kernel(lhs_ref, w_ref, res_ref, *rest, scale):
    acc = _dot(lhs_ref[...], w_ref[...].astype(BF16))
    out = res_ref[...] + scale * acc
    if len(rest) == 1:
        rest[0][...] = out
    else:
        gain_ref, o_ref, hg_ref, ssq_ref = rest
        o_ref[...] = out
        _emit_prenorm(out, gain_ref, hg_ref, ssq_ref)


def _proj_res(lhs, w, res, scale, tm, tn, next_gain=None):
    m, k = lhs.shape
    n = w.shape[1]
    vmem = 2 * _nbytes((tm, k), BF16) + _weight_vmem((k, tn), w.dtype) + 8 * _nbytes((tm, tn), F32)
    in_specs = [pl.BlockSpec((tm, k), lambda i, j: (i, 0)),
                pl.BlockSpec((k, tn), lambda i, j: (0, j)),
                pl.BlockSpec((tm, tn), lambda i, j: (i, j))]
    out_specs = [pl.BlockSpec((tm, tn), lambda i, j: (i, j))]
    out_shape = [jax.ShapeDtypeStruct((m, n), F32)]
    args = [lhs, w, res]
    if next_gain is not None:
        in_specs.append(pl.BlockSpec((1, tn), lambda i, j: (0, j)))
        args.append(next_gain.reshape(1, n))
        specs, shapes = _prenorm_specs(m, n, tm, tn)
        out_specs += specs
        out_shape += shapes
    outs = pl.pallas_call(
        functools.partial(_proj_res_kernel, scale=scale),
        grid=(m // tm, n // tn),
        in_specs=in_specs,
        out_specs=out_specs,
        out_shape=out_shape,
        compiler_params=_params(2, vmem),
        name="proj_residual",
    )(*args)
    return outs[0] if next_gain is None else outs


def _proj_kernel(lhs_ref, w_ref, *rest):
    acc = _dot(lhs_ref[...], w_ref[...].astype(BF16))
    if len(rest) == 2:
        acc = acc * _row_scale(rest[0], lhs_ref.shape[1])
    rest[-1][...] = acc.astype(rest[-1].dtype)


def _proj(lhs, w, tm, tn, out_dtype=BF16, col0=0, n=None, ssq=None):
    m, k = lhs.shape
    n = w.shape[1] if n is None else n
    col_blk0 = col0 // tn
    vmem = 2 * _nbytes((tm, k), BF16) + _weight_vmem((k, tn), w.dtype) + 6 * _nbytes((tm, tn), F32)
    in_specs = [pl.BlockSpec((tm, k), lambda i, j: (i, 0)),
                pl.BlockSpec((k, tn), lambda i, j: (0, j + col_blk0))]
    args = [lhs, w]
    if ssq is not None:
        in_specs.append(pl.BlockSpec((tm, V7X_LANES), lambda i, j: (i, 0)))
        args.append(ssq)
    return pl.pallas_call(
        _proj_kernel,
        grid=(m // tm, n // tn),
        in_specs=in_specs,
        out_specs=pl.BlockSpec((tm, tn), lambda i, j: (i, j)),
        out_shape=jax.ShapeDtypeStruct((m, n), out_dtype),
        compiler_params=_params(2, vmem),
        name="proj",
    )(*args)


def _mix_qkv_kernel(u_ref, ssq_ref, wq_ref, wk_ref, wv_ref, cos_ref, sin_ref, q_ref, k_ref, v_ref,
                    scr_ref, tab_ref, *, tn, dil):
    tm, d_in = u_ref.shape
    rows = tm // dil

    def regroup(slab):
        if dil == 1:
            return [scr_ref[slab]]
        return [scr_ref[slab, pl.ds(r, rows, stride=dil), :] for r in range(dil)]

    @pl.when(pl.program_id(1) == 0)
    def _():
        rs = _row_scale(ssq_ref, d_in)
        tables = (cos_ref[...] * rs, sin_ref[...] * rs, jnp.broadcast_to(rs, (tm, HEAD_DIM)))
        for t, table in enumerate(tables):
            scr_ref[0] = table
            for r, piece in enumerate(regroup(0)):
                tab_ref[t, r] = piece

    heads = tn // HEAD_DIM
    for part, (w_ref, o_ref) in enumerate(((wq_ref, q_ref), (wk_ref, k_ref), (wv_ref, v_ref))):
        acc = _dot(u_ref[...], w_ref[...].astype(BF16))
        for h in range(heads):
            scr_ref[part * heads + h] = acc[:, h * HEAD_DIM:(h + 1) * HEAD_DIM]
        for h in range(heads):
            for r, piece in enumerate(regroup(part * heads + h)):
                if o_ref is v_ref:
                    piece = piece * tab_ref[2, r]
                else:
                    piece = piece * tab_ref[0, r] + pltpu.roll(piece, HEAD_DIM // 2, axis=1) * tab_ref[1, r]
                o_ref[r, :, h * HEAD_DIM:(h + 1) * HEAD_DIM] = piece.astype(o_ref.dtype)


def _mix_qkv(u, ssq, w, cos, sin, group, dil, batch, tm=1024, tn=256):
    m, k = u.shape
    seq = m // batch
    tiles_per_seq = seq // tm
    part_blocks = ATTN_WIDTH // tn

    def w_spec(part):
        col_blk0 = (group * 3 + part) * part_blocks
        return pl.BlockSpec((k, tn), lambda i, j: (0, j + col_blk0))

    out_spec = pl.BlockSpec((None, dil, tm // dil, tn),
                            lambda i, j: (i // tiles_per_seq, 0, i % tiles_per_seq, j))
    out_shape = jax.ShapeDtypeStruct((batch, dil, seq // dil, ATTN_WIDTH), BF16)
    vmem = (2 * _nbytes((tm, k), BF16) + 3 * _weight_vmem((k, tn), w.dtype)
            + 3 * 6 * _nbytes((tm, tn), F32) + 10 * _nbytes((tm, HEAD_DIM), F32))
    return pl.pallas_call(
        functools.partial(_mix_qkv_kernel, tn=tn, dil=dil),
        grid=(m // tm, part_blocks),
        in_specs=[pl.BlockSpec((tm, k), lambda i, j: (i, 0)),
                  pl.BlockSpec((tm, V7X_LANES), lambda i, j: (i, 0)),
                  w_spec(0), w_spec(1), w_spec(2),
                  pl.BlockSpec((tm, HEAD_DIM), lambda i, j: (i, 0)),
                  pl.BlockSpec((tm, HEAD_DIM), lambda i, j: (i, 0))],
        out_specs=[out_spec] * 3,
        out_shape=[out_shape] * 3,
        scratch_shapes=[pltpu.VMEM((3 * tn // HEAD_DIM, tm, HEAD_DIM), F32),
                        pltpu.VMEM((3, dil, tm // dil, HEAD_DIM), F32)],
        compiler_params=_params(2, vmem),
        name=f"mix_qkv_{dil}",
    )(u, ssq, w, w, w, cos, sin)


def _dilated_attn_kernel(q_ref, kp_ref, kc_ref, vp_ref, vc_ref, o_ref, lse_ref,
                         o_scr, lse_scr, *, dil):
    nq = MAX_DIL // dil
    nblk = dil * nq
    qi = lax.broadcasted_iota(jnp.int32, (nblk, BLOCK, BLOCK), 1)
    kj = lax.broadcasted_iota(jnp.int32, (nblk, BLOCK, BLOCK), 2)
    blk = lax.broadcasted_iota(jnp.int32, (nblk, BLOCK, BLOCK), 0)
    has_prev = jnp.logical_or(pl.program_id(1) > 0, blk % nq != 0)
    mask_prev = jnp.logical_and(kj >= qi, has_prev)
    mask_cur = kj <= qi
    scale = HEAD_DIM ** -0.5
    lane = lax.broadcasted_iota(jnp.int32, (nblk, BLOCK, V7X_LANES), 2)
    lse_tile = jnp.zeros((nblk, BLOCK, V7X_LANES), F32)

    def blocks(ref, sl):
        return ref[:, :, sl].reshape(nblk, BLOCK, HEAD_DIM)

    def prev_blocks(first_ref, cur, sl):
        first = first_ref[:, :, sl]
        if nq == 1:
            return first
        cur4 = cur.reshape(dil, nq, BLOCK, HEAD_DIM)
        return jnp.concatenate([first[:, None], cur4[:, :-1]], axis=1).reshape(nblk, BLOCK, HEAD_DIM)

    def token_rows(g):
        r, i = divmod(g, nq)
        return pl.ds(i * BLOCK * dil + r, BLOCK, stride=dil)

    for h in range(ATTN_HEADS_PER_STEP):
        sl = slice(h * HEAD_DIM, (h + 1) * HEAD_DIM)
        q = blocks(q_ref, sl)
        kc = blocks(kc_ref, sl)
        vc = blocks(vc_ref, sl)
        kp = prev_blocks(kp_ref, kc, sl)
        vp = prev_blocks(vp_ref, vc, sl)
        sp = jnp.einsum("gqd,gkd->gqk", q, kp, preferred_element_type=F32)
        sc = jnp.einsum("gqd,gkd->gqk", q, kc, preferred_element_type=F32)
        sp = jnp.where(mask_prev, sp, NEG_INF)
        sc = jnp.where(mask_cur, sc, NEG_INF)
        m = jnp.maximum(jnp.max(sp, axis=2, keepdims=True), jnp.max(sc, axis=2, keepdims=True))
        pp = jnp.exp2((sp - m) * (scale * LOG2E))
        pc = jnp.exp2((sc - m) * (scale * LOG2E))
        l = jnp.sum(pp, axis=2, keepdims=True) + jnp.sum(pc, axis=2, keepdims=True)
        o = (jnp.einsum("gqk,gkd->gqd", pp.astype(BF16), vp, preferred_element_type=F32)
             + jnp.einsum("gqk,gkd->gqd", pc.astype(BF16), vc, preferred_element_type=F32)) / l
        for g in range(nblk):
            o_scr[h, token_rows(g), :] = o[g]
        lse_tile = jnp.where(lane == h, m * scale + jnp.log(l), lse_tile)
    for g in range(nblk):
        lse_scr[token_rows(g), :] = lse_tile[g]
    for h in range(ATTN_HEADS_PER_STEP):
        o_ref[:, h * HEAD_DIM:(h + 1) * HEAD_DIM] = o_scr[h].astype(o_ref.dtype)
    lse_ref[...] = lse_scr[...]


def _dilated_attention(q, k, v, dil):
    b, _, sub_len, _ = q.shape
    seq = sub_len * dil
    rows = ATTN_TILE // dil
    tiles_per_seq = seq // ATTN_TILE
    width = ATTN_HEADS_PER_STEP * HEAD_DIM
    col_blocks = ATTN_WIDTH // width

    cur = pl.BlockSpec((None, dil, rows, width), lambda bi, t, hg: (bi, 0, t, hg))
    prev = pl.BlockSpec((None, dil, BLOCK, width),
                        lambda bi, t, hg: (bi, 0, jnp.maximum(t * (rows // BLOCK) - 1, 0), hg))

    vmem = (6 * _nbytes((dil, rows, width), BF16) + 4 * _nbytes((dil, BLOCK, width), BF16)
            + 2 * _nbytes((ATTN_TILE, width), BF16) + 3 * _nbytes((ATTN_TILE, V7X_LANES), F32)
            + _nbytes((ATTN_TILE, width), F32) + 12 * _nbytes((ATTN_TILE, BLOCK), F32))
    o, lse = pl.pallas_call(
        functools.partial(_dilated_attn_kernel, dil=dil),
        grid=(b, tiles_per_seq, col_blocks),
        in_specs=[cur, prev, cur, prev, cur],
        out_specs=[pl.BlockSpec((ATTN_TILE, width), lambda bi, t, hg: (bi * tiles_per_seq + t, hg)),
                   pl.BlockSpec((ATTN_TILE, V7X_LANES), lambda bi, t, hg: (bi * tiles_per_seq + t, hg))],
        out_shape=[jax.ShapeDtypeStruct((b * seq, ATTN_WIDTH), BF16),
                   jax.ShapeDtypeStruct((b * seq, LSE_WIDTH), F32)],
        scratch_shapes=[pltpu.VMEM((ATTN_HEADS_PER_STEP, ATTN_TILE, HEAD_DIM), F32),
                        pltpu.VMEM((ATTN_TILE, V7X_LANES), F32)],
        compiler_params=_params(3, vmem),
        name=f"dilated_attn_{dil}",
    )(q, k, k, v, v)
    return o, lse


def _combine_kernel(o0_ref, o1_ref, o2_ref, l0_ref, l1_ref, l2_ref, out_ref):
    l0, l1, l2 = l0_ref[...], l1_ref[...], l2_ref[...]
    m = jnp.maximum(jnp.maximum(l0, l1), l2)
    e0, e1, e2 = jnp.exp(l0 - m), jnp.exp(l1 - m), jnp.exp(l2 - m)
    den = e0 + e1 + e2
    w0, w1, w2 = e0 / den, e1 / den, e2 / den
    for h in range(N_ATTN_HEADS):
        sl = slice(h * HEAD_DIM, (h + 1) * HEAD_DIM)
        c = (h // ATTN_HEADS_PER_STEP) * V7X_LANES + h % ATTN_HEADS_PER_STEP
        acc = (w0[:, c:c + 1] * o0_ref[:, sl].astype(F32)
               + w1[:, c:c + 1] * o1_ref[:, sl].astype(F32)
               + w2[:, c:c + 1] * o2_ref[:, sl].astype(F32))
        out_ref[:, sl] = acc.astype(out_ref.dtype)


def _combine(outs, lses, tq=512):
    m = outs[0].shape[0]
    o_spec = pl.BlockSpec((tq, ATTN_WIDTH), lambda i: (i, 0))
    l_spec = pl.BlockSpec((tq, LSE_WIDTH), lambda i: (i, 0))
    vmem = 8 * _nbytes((tq, ATTN_WIDTH), BF16) + 16 * _nbytes((tq, LSE_WIDTH), F32) + (8 << 20)
    return pl.pallas_call(
        _combine_kernel,
        grid=(m // tq,),
        in_specs=[o_spec] * 3 + [l_spec] * 3,
        out_specs=o_spec,
        out_shape=jax.ShapeDtypeStruct((m, ATTN_WIDTH), BF16),
        compiler_params=_params(1, vmem),
        name="attn_combine",
    )(*outs, *lses)


def _pool_kernel(zm_ref, zh_ref, wp_ref, scale_ref, o_ref, *, tp):
    t = pl.program_id(1)
    pos = t * tp + lax.broadcasted_iota(jnp.int32, (tp, 1), 0)
    for g, win in enumerate(POOL_WINDOWS):
        sl = slice(g * POOL_GROUP, (g + 1) * POOL_GROUP)
        main = zm_ref[:, sl]
        halo = jnp.where(t > 0, zh_ref[:, sl], 0.0)
        s = jnp.concatenate([halo, main], axis=0)
        shift = 1
        while shift < win:
            s = s + pltpu.roll(s, shift, axis=0)
            shift *= 2
        cnt = jnp.minimum(pos + 1, win).astype(F32)
        y = s[POOL_HALO:] / cnt - main
        yo = _dot(y.astype(BF16), wp_ref[g].astype(BF16))
        o_ref[:, sl] = (yo * scale_ref[:, sl]).astype(o_ref.dtype)


def _pool_mixer(zp, w_pool, pool_scale, batch, tp=512):
    m = zp.shape[0]
    seq = m // batch
    tiles_per_seq = seq // tp
    halo_blocks = tp // POOL_HALO
    vmem = (6 * _nbytes((tp, POOL_WIDTH), F32) + 2 * _nbytes(w_pool.shape, F32)
            + 12 * _nbytes((tp, POOL_GROUP), F32) + (8 << 20))
    return pl.pallas_call(
        functools.partial(_pool_kernel, tp=tp),
        grid=(batch, tiles_per_seq),
        in_specs=[pl.BlockSpec((tp, POOL_WIDTH), lambda bi, t: (bi * tiles_per_seq + t, 0)),
                  pl.BlockSpec((POOL_HALO, POOL_WIDTH),
                               lambda bi, t: (jnp.maximum((bi * tiles_per_seq + t) * halo_blocks - 1, 0), 0)),
                  pl.BlockSpec(w_pool.shape, lambda bi, t: (0, 0, 0)),
                  pl.BlockSpec((1, POOL_WIDTH), lambda bi, t: (0, 0))],
        out_specs=pl.BlockSpec((tp, POOL_WIDTH), lambda bi, t: (bi * tiles_per_seq + t, 0)),
        out_shape=jax.ShapeDtypeStruct((m, POOL_WIDTH), BF16),
        compiler_params=_params(2, vmem),
        name="pool_mixer",
    )(zp, zp, w_pool, pool_scale.reshape(1, POOL_WIDTH))


def _mix_out_kernel(a_ref, p_ref, wa_ref, wp_ref, res_ref, gain_ref, o_ref, hg_ref, ssq_ref):
    acc = _dot(a_ref[...], wa_ref[...].astype(BF16)) + _dot(p_ref[...], wp_ref[...].astype(BF16))
    out = res_ref[...] + acc
    o_ref[...] = out
    _emit_prenorm(out, gain_ref, hg_ref, ssq_ref)


def _mix_out(o_attn, o_pool, w, res, next_gain, tm=1024, tn=512):
    m = o_attn.shape[0]
    n = w.shape[1]
    ka, kp = o_attn.shape[1], o_pool.shape[1]
    assert ka == kp
    vmem = (2 * _nbytes((tm, ka + kp), BF16) + _weight_vmem((ka + kp, tn), w.dtype)
            + 8 * _nbytes((tm, tn), F32))
    norm_specs, norm_shapes = _prenorm_specs(m, n, tm, tn)
    return pl.pallas_call(
        _mix_out_kernel,
        grid=(m // tm, n // tn),
        in_specs=[pl.BlockSpec((tm, ka), lambda i, j: (i, 0)),
                  pl.BlockSpec((tm, kp), lambda i, j: (i, 0)),
                  pl.BlockSpec((ka, tn), lambda i, j: (0, j)),
                  pl.BlockSpec((kp, tn), lambda i, j: (1, j)),
                  pl.BlockSpec((tm, tn), lambda i, j: (i, j)),
                  pl.BlockSpec((1, tn), lambda i, j: (0, j))],
        out_specs=[pl.BlockSpec((tm, tn), lambda i, j: (i, j))] + norm_specs,
        out_shape=[jax.ShapeDtypeStruct((m, n), F32)] + norm_shapes,
        compiler_params=_params(2, vmem),
        name="mix_out",
    )(o_attn, o_pool, w, w, res, next_gain.reshape(1, n))


def _cast_kernel(x_ref, o_ref):
    o_ref[...] = x_ref[...].astype(o_ref.dtype)


def _cast_bf16(w, rows=512):
    k, n = w.shape
    rows = min(rows, k)
    return pl.pallas_call(
        _cast_kernel,
        grid=(k // rows,),
        in_specs=[pl.BlockSpec((rows, n), lambda i: (i, 0))],
        out_specs=pl.BlockSpec((rows, n), lambda i: (i, 0)),
        out_shape=jax.ShapeDtypeStruct((k, n), BF16),
        compiler_params=_params(1, 8 * _nbytes((rows, n), F32)),
        name="cast_bf16",
    )(w)


def _cross_kernel(hg_ref, ssq_ref, kv_ref, wq_ref, wo_ref, res_ref, gain_ref, o_ref, u_ref):
    scale = CROSS_DIM ** -0.5
    q = (_dot(hg_ref[...], wq_ref[...]) * _row_scale(ssq_ref, hg_ref.shape[1])).astype(BF16)
    heads = []
    for h in range(CROSS_HEADS):
        sl = slice(h * CROSS_DIM, (h + 1) * CROSS_DIM)
        k = kv_ref[:, sl]
        v = kv_ref[:, CROSS_WIDTH + h * CROSS_DIM:CROSS_WIDTH + (h + 1) * CROSS_DIM]
        s = _dot_nt(q[:, sl], k)
        m = jnp.max(s, axis=1, keepdims=True)
        p = jnp.exp2((s - m) * (scale * LOG2E))
        l = jnp.sum(p, axis=1, keepdims=True)
        heads.append((_dot(p.astype(BF16), v) / l).astype(BF16))
    out = res_ref[...] + _dot(jnp.concatenate(heads, axis=1), wo_ref[...])
    o_ref[...] = out
    ms = jnp.mean(out * out, axis=-1, keepdims=True)
    u_ref[...] = (out * lax.rsqrt(ms + EPS) * gain_ref[...]).astype(u_ref.dtype)


def _cross_attention(hg, ssq, kv, w_q, w_o, res, next_gain, batch, ts=256):
    m, d = res.shape
    mem_len = kv.shape[0] // batch
    steps_per_batch = (m // batch) // ts
    vmem = (2 * _nbytes((ts, d), BF16) * 2 + 4 * _nbytes((ts, d), F32) + _nbytes(w_q.shape, BF16)
            + _nbytes(w_o.shape, BF16) + 4 * _nbytes((ts, d), F32) + (4 << 20))
    row_f32 = pl.BlockSpec((ts, d), lambda i: (i, 0))
    return pl.pallas_call(
        _cross_kernel,
        grid=(m // ts,),
        in_specs=[pl.BlockSpec((ts, d), lambda i: (i, 0)),
                  pl.BlockSpec((ts, V7X_LANES), lambda i: (i, 0)),
                  pl.BlockSpec((mem_len, 2 * CROSS_WIDTH), lambda i: (i // steps_per_batch, 0)),
                  _resident(w_q.shape, lambda i: (0, 0)),
                  _resident(w_o.shape, lambda i: (0, 0)),
                  row_f32,
                  pl.BlockSpec((1, d), lambda i: (0, 0))],
        out_specs=[row_f32, pl.BlockSpec((ts, d), lambda i: (i, 0))],
        out_shape=[jax.ShapeDtypeStruct((m, d), F32), jax.ShapeDtypeStruct((m, d), BF16)],
        compiler_params=_params(1, vmem),
        name="cross_attention",
    )(hg, ssq, kv, w_q, w_o, res, next_gain.reshape(1, d))


def _swiglu_block(h, u, w_in, w_out, next_gain=None, later_weights=()):
    gate, (w_out_bf16, *later_bf16) = _ffn_in(u, w_in, cast_jobs=(w_out,) + tuple(later_weights))
    return _proj_res(gate, w_out_bf16, h, 0.5, tm=512, tn=512, next_gain=next_gain), later_bf16


def kernel(x, mem, positions, g_ffn1, w_ffn1_in, w_ffn1_out, g_mix, w_mix_in, w_pool, pool_scale, w_mix_out, g_cross, g_mem, w_cross_q, w_cross_kv, w_cross_o, g_ffn2, w_ffn2_in, w_ffn2_out, g_final):
    batch, seq, d = x.shape
    m = batch * seq
    depth = g_ffn1.shape[0]
    cos, sin = _rope_tables(positions)
    h = x.reshape(m, d)
    mem2 = mem.reshape(batch * mem.shape[1], d)
    for l in range(depth):
        u = _rmsnorm(h, g_ffn1[l], BF16)
        (h, hg, ssq), (w_mix_in_bf16, w_mix_out_bf16, w_ffn2_in_bf16) = _swiglu_block(
            h, u, w_ffn1_in[l], w_ffn1_out[l], next_gain=g_mix[l],
            later_weights=(w_mix_in[l], w_mix_out[l], w_ffn2_in[l]))

        outs, lses = [], []
        for group, (_, dil) in enumerate(DILATED_CONFIGS):
            q, k, v = _mix_qkv(hg, ssq, w_mix_in_bf16, cos, sin, group, dil, batch)
            o, lse = _dilated_attention(q, k, v, dil)
            outs.append(o)
            lses.append(lse)
        o_attn = _combine(outs, lses)
        zp = _proj(hg, w_mix_in_bf16, tm=1024, tn=1024, out_dtype=F32, col0=ATTN_IN, n=POOL_WIDTH, ssq=ssq)
        o_pool = _pool_mixer(zp, w_pool[l], pool_scale[l], batch)
        h, hg, ssq = _mix_out(o_attn, o_pool, w_mix_out_bf16, h, g_cross[l])

        um = _rmsnorm(mem2, g_mem[l], BF16)
        kv = _proj(um, w_cross_kv[l], tm=um.shape[0], tn=CROSS_WIDTH)
        h, u = _cross_attention(hg, ssq, kv, _cast_bf16(w_cross_q[l]), _cast_bf16(w_cross_o[l]),
                                h, g_ffn2[l], batch)

        h, _ = _swiglu_block(h, u, w_ffn2_in_bf16, w_ffn2_out[l])
    return _rmsnorm(h, g_final, F32).reshape(batch, seq, d)
```

```python
import functools
import math

import jax
import jax.numpy as jnp
from jax import lax
from jax.experimental import pallas as pl
from jax.experimental.pallas import tpu as pltpu

F32 = jnp.float32
BF16 = jnp.bfloat16

D_MODEL = 4096
HEAD_DIM = 128
ATTN_WIDTH = D_MODEL // 2
N_ATTN_HEADS = ATTN_WIDTH // HEAD_DIM
DILATED_CONFIGS = ((128, 1), (512, 4), (2048, 16))
N_DIL = len(DILATED_CONFIGS)
BLOCK = 128
POOL_WIDTH = D_MODEL - ATTN_WIDTH
POOL_WINDOWS = (2, 4, 8, 16)
POOL_GROUP = POOL_WIDTH // len(POOL_WINDOWS)
QKV_WIDTH = 3 * ATTN_WIDTH
ATTN_IN = N_DIL * QKV_WIDTH
D_FF = ((8 * D_MODEL // 3 + 255) // 256) * 256
CROSS_HEADS = 4
CROSS_DIM = 128
CROSS_WIDTH = CROSS_HEADS * CROSS_DIM
ROPE_THETA = 10000.0
EPS = 1e-6
NEG_INF = -1e30
LOG2E = math.log2(math.e)

V7X_LANES = 128
V7X_BF16_SUBLANES = 16
V7X_VMEM_BYTES = 64 * 1024 * 1024
V7X_VMEM_REQUEST_CAP = V7X_VMEM_BYTES - 6 * 1024 * 1024

POOL_HALO = max(POOL_WINDOWS)
MAX_DIL = max(d for _, d in DILATED_CONFIGS)
ATTN_TILE = BLOCK * MAX_DIL
ATTN_HEADS_PER_STEP = 4


def _params(n_grid, vmem_bytes):
    return pltpu.CompilerParams(
        dimension_semantics=("arbitrary",) * n_grid,
        vmem_limit_bytes=int(min(vmem_bytes, V7X_VMEM_REQUEST_CAP)),
    )


def _nbytes(shape, dtype):
    return math.prod(shape) * jnp.dtype(dtype).itemsize


def _weight_vmem(block_shape, dtype):
    cast_copy = 0 if dtype == BF16 else _nbytes(block_shape, BF16)
    return 2 * _nbytes(block_shape, dtype) + cast_copy


def _resident(block_shape, index_map):
    return pl.BlockSpec(block_shape, index_map, pipeline_mode=pl.Buffered(1))


def _dot(a, b):
    return jnp.dot(a, b, preferred_element_type=F32)


def _dot_nt(a, b):
    return lax.dot_general(a, b, (((1,), (1,)), ((), ())), preferred_element_type=F32)


def _rmsnorm_kernel(x_ref, g_ref, o_ref):
    x = x_ref[...]
    ms = jnp.mean(x * x, axis=-1, keepdims=True)
    o_ref[...] = (x * lax.rsqrt(ms + EPS) * g_ref[...]).astype(o_ref.dtype)


def _rmsnorm(x, g, out_dtype, tm=256):
    m, d = x.shape
    vmem = 2 * tm * d * (4 + jnp.dtype(out_dtype).itemsize) + 4 * tm * d * 4
    return pl.pallas_call(
        _rmsnorm_kernel,
        grid=(m // tm,),
        in_specs=[pl.BlockSpec((tm, d), lambda i: (i, 0)),
                  pl.BlockSpec((1, d), lambda i: (0, 0))],
        out_specs=pl.BlockSpec((tm, d), lambda i: (i, 0)),
        out_shape=jax.ShapeDtypeStruct((m, d), out_dtype),
        compiler_params=_params(1, vmem),
        name="rmsnorm",
    )(x, g.reshape(1, d))


def _rope_kernel(pos_ref, inv_ref, sign_ref, cos_ref, sin_ref):
    ang = pos_ref[...].astype(F32) * inv_ref[...]
    cos_ref[...] = jnp.cos(ang)
    sin_ref[...] = jnp.sin(ang) * sign_ref[...]


def _rope_tables(positions, tm=1024):
    m = positions.size
    half = HEAD_DIM // 2
    inv = 1.0 / (ROPE_THETA ** (jnp.arange(0, HEAD_DIM, 2, dtype=F32) / HEAD_DIM))
    inv = jnp.concatenate([inv, inv]).reshape(1, HEAD_DIM)
    sign = jnp.concatenate([-jnp.ones((half,), F32), jnp.ones((half,), F32)]).reshape(1, HEAD_DIM)
    row = pl.BlockSpec((1, HEAD_DIM), lambda i: (0, 0))
    tab = pl.BlockSpec((tm, HEAD_DIM), lambda i: (i, 0))
    return pl.pallas_call(
        _rope_kernel,
        grid=(m // tm,),
        in_specs=[pl.BlockSpec((tm, 1), lambda i: (i, 0)), row, row],
        out_specs=[tab, tab],
        out_shape=[jax.ShapeDtypeStruct((m, HEAD_DIM), F32)] * 2,
        compiler_params=_params(1, 16 * tm * HEAD_DIM * 4),
        name="rope_tables",
    )(positions.reshape(m, 1), inv, sign)


def _ffn_in_kernel(u_ref, wa_ref, wb_ref, *rest):
    n_jobs = (len(rest) - 1) // 2
    srcs, o_ref, dsts = rest[:n_jobs], rest[n_jobs], rest[n_jobs + 1:]
    half = o_ref.shape[1] // 2
    for c in range(2):
        sl = slice(c * half, (c + 1) * half)
        w = jnp.concatenate([wa_ref[:, sl].astype(BF16), wb_ref[:, sl].astype(BF16)], axis=1)
        ab = _dot(u_ref[...], w)
        a, b = ab[:, :half], ab[:, half:]
        o_ref[:, sl] = (a * jax.nn.sigmoid(a) * b).astype(o_ref.dtype)
    for src, dst in zip(srcs, dsts):
        dst[...] = src[...].astype(dst.dtype)


def _ffn_in(u, w_in, cast_jobs=(), tm=2048, tf=256):
    m, d = u.shape
    nf = D_FF // tf
    n_steps = (m // tm) * nf
    lhs_spec, lhs_buffers = (pl.BlockSpec, 2) if w_in.dtype == BF16 else (_resident, 1)
    vmem = (lhs_buffers * _nbytes((tm, d), BF16) + 2 * _weight_vmem((d, tf), w_in.dtype)
            + 2 * _nbytes((tm, tf), BF16) + 6 * _nbytes((tm, tf), F32))
    job_specs, job_shapes = [], []
    for w in cast_jobs:
        rows = next(r for r in range(V7X_BF16_SUBLANES, w.shape[0] + 1, V7X_BF16_SUBLANES)
                    if w.shape[0] % r == 0 and w.shape[0] // r <= n_steps)
        n_blocks = w.shape[0] // rows
        job_specs.append(pl.BlockSpec(
            (rows, w.shape[1]), lambda i, j, n_blocks=n_blocks: (jnp.minimum(i * nf + j, n_blocks - 1), 0)))
        job_shapes.append(jax.ShapeDtypeStruct(w.shape, BF16))
        vmem += 2 * _nbytes((rows, w.shape[1]), F32) + 2 * _nbytes((rows, w.shape[1]), BF16)
    outs = pl.pallas_call(
        _ffn_in_kernel,
        grid=(m // tm, nf),
        in_specs=[lhs_spec((tm, d), lambda i, j: (i, 0)),
                  pl.BlockSpec((d, tf), lambda i, j: (0, j)),
                  pl.BlockSpec((d, tf), lambda i, j: (0, j + nf))] + job_specs,
        out_specs=[pl.BlockSpec((tm, tf), lambda i, j: (i, j))] + job_specs,
        out_shape=[jax.ShapeDtypeStruct((m, D_FF), BF16)] + job_shapes,
        compiler_params=_params(2, vmem),
        name="ffn_in",
    )(u, w_in, w_in, *cast_jobs)
    return outs[0], outs[1:]


def _emit_prenorm(out, gain_ref, hg_ref, ssq_ref):
    hg_ref[...] = (out * gain_ref[...]).astype(hg_ref.dtype)
    ssq_ref[...] += jnp.broadcast_to(jnp.sum(out * out, axis=1, keepdims=True), ssq_ref.shape)


def _init_prenorm(ssq_ref):
    @pl.when(pl.program_id(1) == 0)
    def _():
        ssq_ref[...] = jnp.zeros_like(ssq_ref)


def _row_scale(ssq_ref, d):
    return lax.rsqrt(ssq_ref[:, :1] * (1.0 / d) + EPS)


def _prenorm_specs(m, n, tm, tn):
    specs = [pl.BlockSpec((tm, tn), lambda i, j: (i, j)), pl.BlockSpec((tm, V7X_LANES), lambda i, j: (i, 0))]
    shapes = [jax.ShapeDtypeStruct((m, n), BF16), jax.ShapeDtypeStruct((m, V7X_LANES), F32)]
    return specs, shapes


def _column_halves(ref):
    half = ref.shape[1] // 2
    return [slice(0, half), slice(half, 2 * half)]


def _proj_res_kernel(lhs_ref, w_ref, res_ref, *rest, scale):
    prenorm = len(rest) > 1
    if prenorm:
        gain_ref, o_ref, hg_ref, ssq_ref = rest
        _init_prenorm(ssq_ref)
    else:
        o_ref, = rest
    for sl in _column_halves(o_ref):
        acc = _dot(lhs_ref[...], w_ref[:, sl].astype(BF16))
        out = res_ref[:, sl] + scale * acc
        o_ref[:, sl] = out
        if prenorm:
            _emit_prenorm(out, gain_ref.at[:, sl], hg_ref.at[:, sl], ssq_ref)


def _proj_res(lhs, w, res, scale, tm, tn, next_gain=None):
    m, k = lhs.shape
    n = w.shape[1]
    vmem = 2 * _nbytes((tm, k), BF16) + _weight_vmem((k, tn), w.dtype) + 8 * _nbytes((tm, tn), F32)
    in_specs = [pl.BlockSpec((tm, k), lambda i, j: (i, 0)),
                pl.BlockSpec((k, tn), lambda i, j: (0, j)),
                pl.BlockSpec((tm, tn), lambda i, j: (i, j))]
    out_specs = [pl.BlockSpec((tm, tn), lambda i, j: (i, j))]
    out_shape = [jax.ShapeDtypeStruct((m, n), F32)]
    args = [lhs, w, res]
    if next_gain is not None:
        in_specs.append(pl.BlockSpec((1, tn), lambda i, j: (0, j)))
        args.append(next_gain.reshape(1, n))
        specs, shapes = _prenorm_specs(m, n, tm, tn)
        out_specs += specs
        out_shape += shapes
    outs = pl.pallas_call(
        functools.partial(_proj_res_kernel, scale=scale),
        grid=(m // tm, n // tn),
        in_specs=in_specs,
        out_specs=out_specs,
        out_shape=out_shape,
        compiler_params=_params(2, vmem),
        name="proj_residual",
    )(*args)
    return outs[0] if next_gain is None else outs


def _proj_kernel(lhs_ref, w_ref, *rest):
    acc = _dot(lhs_ref[...], w_ref[...].astype(BF16))
    if len(rest) == 2:
        acc = acc * _row_scale(rest[0], lhs_ref.shape[1])
    rest[-1][...] = acc.astype(rest[-1].dtype)


def _proj(lhs, w, tm, tn, out_dtype=BF16, col0=0, n=None, ssq=None):
    m, k = lhs.shape
    n = w.shape[1] if n is None else n
    col_blk0 = col0 // tn
    vmem = 2 * _nbytes((tm, k), BF16) + _weight_vmem((k, tn), w.dtype) + 6 * _nbytes((tm, tn), F32)
    in_specs = [pl.BlockSpec((tm, k), lambda i, j: (i, 0)),
                pl.BlockSpec((k, tn), lambda i, j: (0, j + col_blk0))]
    args = [lhs, w]
    if ssq is not None:
        in_specs.append(pl.BlockSpec((tm, V7X_LANES), lambda i, j: (i, 0)))
        args.append(ssq)
    return pl.pallas_call(
        _proj_kernel,
        grid=(m // tm, n // tn),
        in_specs=in_specs,
        out_specs=pl.BlockSpec((tm, tn), lambda i, j: (i, j)),
        out_shape=jax.ShapeDtypeStruct((m, n), out_dtype),
        compiler_params=_params(2, vmem),
        name="proj",
    )(*args)


def _mix_qkv_kernel(u_ref, ssq_ref, wq_ref, wk_ref, wv_ref, cos_ref, sin_ref, q_ref, k_ref, v_ref,
                    scr_ref, tab_ref, *, tn, dil):
    tm, d_in = u_ref.shape
    rows = tm // dil

    def regroup(slab, val):
        if dil == 1:
            return [val]
        scr_ref[slab] = val
        return [scr_ref[slab, pl.ds(r, rows, stride=dil), :] for r in range(dil)]

    @pl.when(pl.program_id(1) == 0)
    def _():
        rs = _row_scale(ssq_ref, d_in)
        tables = (cos_ref[...] * rs, sin_ref[...] * rs, jnp.broadcast_to(rs, (tm, HEAD_DIM)))
        for t, table in enumerate(tables):
            for r, piece in enumerate(regroup(0, table)):
                tab_ref[t, r] = piece

    heads = tn // HEAD_DIM
    for part, (w_ref, o_ref) in enumerate(((wq_ref, q_ref), (wk_ref, k_ref), (wv_ref, v_ref))):
        acc = _dot(u_ref[...], w_ref[...].astype(BF16))
        for h in range(heads):
            pieces = regroup(part * heads + h, acc[:, h * HEAD_DIM:(h + 1) * HEAD_DIM])
            for r, piece in enumerate(pieces):
                if o_ref is v_ref:
                    piece = piece * tab_ref[2, r]
                else:
                    piece = piece * tab_ref[0, r] + pltpu.roll(piece, HEAD_DIM // 2, axis=1) * tab_ref[1, r]
                o_ref[r, :, h * HEAD_DIM:(h + 1) * HEAD_DIM] = piece.astype(o_ref.dtype)


def _mix_qkv(u, ssq, w, cos, sin, group, dil, batch, tm=1024, tn=256):
    m, k = u.shape
    seq = m // batch
    tiles_per_seq = seq // tm
    part_blocks = ATTN_WIDTH // tn

    def w_spec(part):
        col_blk0 = (group * 3 + part) * part_blocks
        return pl.BlockSpec((k, tn), lambda i, j: (0, j + col_blk0))

    out_spec = pl.BlockSpec((None, dil, tm // dil, tn),
                            lambda i, j: (i // tiles_per_seq, 0, i % tiles_per_seq, j))
    out_shape = jax.ShapeDtypeStruct((batch, dil, seq // dil, ATTN_WIDTH), BF16)
    vmem = (2 * _nbytes((tm, k), BF16) + 3 * _weight_vmem((k, tn), w.dtype)
            + 3 * 6 * _nbytes((tm, tn), F32) + 10 * _nbytes((tm, HEAD_DIM), F32))
    return pl.pallas_call(
        functools.partial(_mix_qkv_kernel, tn=tn, dil=dil),
        grid=(m // tm, part_blocks),
        in_specs=[pl.BlockSpec((tm, k), lambda i, j: (i, 0)),
                  pl.BlockSpec((tm, V7X_LANES), lambda i, j: (i, 0)),
                  w_spec(0), w_spec(1), w_spec(2),
                  pl.BlockSpec((tm, HEAD_DIM), lambda i, j: (i, 0)),
                  pl.BlockSpec((tm, HEAD_DIM), lambda i, j: (i, 0))],
        out_specs=[out_spec] * 3,
        out_shape=[out_shape] * 3,
        scratch_shapes=[pltpu.VMEM((3 * tn // HEAD_DIM, tm, HEAD_DIM), F32),
                        pltpu.VMEM((3, dil, tm // dil, HEAD_DIM), F32)],
        compiler_params=_params(2, vmem),
        name=f"mix_qkv_{dil}",
    )(u, ssq, w, w, w, cos, sin)


def _dilated_attn_kernel(q_ref, kp_ref, kc_ref, vp_ref, vc_ref, o_ref, lse_ref,
                         o_scr, lse_scr, *, dil):
    nq = MAX_DIL // dil
    nblk = dil * nq
    qi = lax.broadcasted_iota(jnp.int32, (nblk, BLOCK, BLOCK), 1)
    kj = lax.broadcasted_iota(jnp.int32, (nblk, BLOCK, BLOCK), 2)
    blk = lax.broadcasted_iota(jnp.int32, (nblk, BLOCK, BLOCK), 0)
    has_prev = jnp.logical_or(pl.program_id(1) > 0, blk % nq != 0)
    mask_prev = jnp.logical_and(kj >= qi, has_prev)
    mask_cur = kj <= qi
    scale = HEAD_DIM ** -0.5
    lane = lax.broadcasted_iota(jnp.int32, (nblk, BLOCK, V7X_LANES), 2)
    first_head = pl.program_id(2) * ATTN_HEADS_PER_STEP
    lse_tile = jnp.zeros((nblk, BLOCK, V7X_LANES), F32)

    def blocks(ref, sl):
        return ref[:, :, sl].reshape(nblk, BLOCK, HEAD_DIM)

    def prev_blocks(first_ref, cur, sl):
        first = first_ref[:, :, sl]
        if nq == 1:
            return first
        cur4 = cur.reshape(dil, nq, BLOCK, HEAD_DIM)
        return jnp.concatenate([first[:, None], cur4[:, :-1]], axis=1).reshape(nblk, BLOCK, HEAD_DIM)

    def token_rows(g):
        r, i = divmod(g, nq)
        return pl.ds(i * BLOCK * dil + r, BLOCK, stride=dil)

    for h in range(ATTN_HEADS_PER_STEP):
        sl = slice(h * HEAD_DIM, (h + 1) * HEAD_DIM)
        q = blocks(q_ref, sl)
        kc = blocks(kc_ref, sl)
        vc = blocks(vc_ref, sl)
        kp = prev_blocks(kp_ref, kc, sl)
        vp = prev_blocks(vp_ref, vc, sl)
        sp = jnp.einsum("gqd,gkd->gqk", q, kp, preferred_element_type=F32)
        sc = jnp.einsum("gqd,gkd->gqk", q, kc, preferred_element_type=F32)
        sp = jnp.where(mask_prev, sp, NEG_INF)
        sc = jnp.where(mask_cur, sc, NEG_INF)
        m = jnp.maximum(jnp.max(sp, axis=2, keepdims=True), jnp.max(sc, axis=2, keepdims=True))
        pp = jnp.exp2((sp - m) * (scale * LOG2E))
        pc = jnp.exp2((sc - m) * (scale * LOG2E))
        l = jnp.sum(pp, axis=2, keepdims=True) + jnp.sum(pc, axis=2, keepdims=True)
        o = (jnp.einsum("gqk,gkd->gqd", pp.astype(BF16), vp, preferred_element_type=F32)
             + jnp.einsum("gqk,gkd->gqd", pc.astype(BF16), vc, preferred_element_type=F32)) / l
        for g in range(nblk):
            o_scr[h, token_rows(g), :] = o[g]
        lse_tile = jnp.where(lane == first_head + h, m * scale + jnp.log(l), lse_tile)
    for g in range(nblk):
        lse_scr[token_rows(g), :] = lse_tile[g]
    for h in range(ATTN_HEADS_PER_STEP):
        o_ref[:, h * HEAD_DIM:(h + 1) * HEAD_DIM] = o_scr[h].astype(o_ref.dtype)
    @pl.when(pl.program_id(2) == 0)
    def _():
        lse_ref[...] = lse_scr[...]

    @pl.when(pl.program_id(2) > 0)
    def _():
        lse_ref[...] += lse_scr[...]


def _dilated_attention(q, k, v, dil):
    b, _, sub_len, _ = q.shape
    seq = sub_len * dil
    rows = ATTN_TILE // dil
    tiles_per_seq = seq // ATTN_TILE
    width = ATTN_HEADS_PER_STEP * HEAD_DIM
    col_blocks = ATTN_WIDTH // width

    cur = pl.BlockSpec((None, dil, rows, width), lambda bi, t, hg: (bi, 0, t, hg))
    prev = pl.BlockSpec((None, dil, BLOCK, width),
                        lambda bi, t, hg: (bi, 0, jnp.maximum(t * (rows // BLOCK) - 1, 0), hg))

    vmem = (6 * _nbytes((dil, rows, width), BF16) + 4 * _nbytes((dil, BLOCK, width), BF16)
            + 2 * _nbytes((ATTN_TILE, width), BF16) + 3 * _nbytes((ATTN_TILE, V7X_LANES), F32)
            + _nbytes((ATTN_TILE, width), F32) + 12 * _nbytes((ATTN_TILE, BLOCK), F32))
    o, lse = pl.pallas_call(
        functools.partial(_dilated_attn_kernel, dil=dil),
        grid=(b, tiles_per_seq, col_blocks),
        in_specs=[cur, prev, cur, prev, cur],
        out_specs=[pl.BlockSpec((ATTN_TILE, width), lambda bi, t, hg: (bi * tiles_per_seq + t, hg)),
                   pl.BlockSpec((ATTN_TILE, V7X_LANES), lambda bi, t, hg: (bi * tiles_per_seq + t, 0))],
        out_shape=[jax.ShapeDtypeStruct((b * seq, ATTN_WIDTH), BF16),
                   jax.ShapeDtypeStruct((b * seq, V7X_LANES), F32)],
        scratch_shapes=[pltpu.VMEM((ATTN_HEADS_PER_STEP, ATTN_TILE, HEAD_DIM), F32),
                        pltpu.VMEM((ATTN_TILE, V7X_LANES), F32)],
        compiler_params=_params(3, vmem),
        name=f"dilated_attn_{dil}",
    )(q, k, k, v, v)
    return o, lse


def _combine_kernel(o0_ref, o1_ref, o2_ref, l0_ref, l1_ref, l2_ref, out_ref):
    l0, l1, l2 = l0_ref[...], l1_ref[...], l2_ref[...]
    m = jnp.maximum(jnp.maximum(l0, l1), l2)
    e0, e1, e2 = jnp.exp(l0 - m), jnp.exp(l1 - m), jnp.exp(l2 - m)
    den = e0 + e1 + e2
    w0, w1, w2 = e0 / den, e1 / den, e2 / den
    for h in range(N_ATTN_HEADS):
        sl = slice(h * HEAD_DIM, (h + 1) * HEAD_DIM)
        acc = (w0[:, h:h + 1] * o0_ref[:, sl].astype(F32)
               + w1[:, h:h + 1] * o1_ref[:, sl].astype(F32)
               + w2[:, h:h + 1] * o2_ref[:, sl].astype(F32))
        out_ref[:, sl] = acc.astype(out_ref.dtype)


def _combine(outs, lses, tq=512):
    m = outs[0].shape[0]
    o_spec = pl.BlockSpec((tq, ATTN_WIDTH), lambda i: (i, 0))
    l_spec = pl.BlockSpec((tq, V7X_LANES), lambda i: (i, 0))
    vmem = 8 * _nbytes((tq, ATTN_WIDTH), BF16) + 16 * _nbytes((tq, V7X_LANES), F32) + (8 << 20)
    return pl.pallas_call(
        _combine_kernel,
        grid=(m // tq,),
        in_specs=[o_spec] * 3 + [l_spec] * 3,
        out_specs=o_spec,
        out_shape=jax.ShapeDtypeStruct((m, ATTN_WIDTH), BF16),
        compiler_params=_params(1, vmem),
        name="attn_combine",
    )(*outs, *lses)


def _pool_kernel(zm_ref, zh_ref, wp_ref, scale_ref, o_ref, *, tp):
    t = pl.program_id(1)
    pos = t * tp + lax.broadcasted_iota(jnp.int32, (tp, 1), 0)
    for g, win in enumerate(POOL_WINDOWS):
        sl = slice(g * POOL_GROUP, (g + 1) * POOL_GROUP)
        main = zm_ref[:, sl]
        halo = jnp.where(t > 0, zh_ref[:, sl], 0.0)
        s = jnp.concatenate([halo, main], axis=0)
        shift = 1
        while shift < win:
            s = s + pltpu.roll(s, shift, axis=0)
            shift *= 2
        cnt = jnp.minimum(pos + 1, win).astype(F32)
        y = s[POOL_HALO:] / cnt - main
        yo = _dot(y.astype(BF16), wp_ref[g].astype(BF16))
        o_ref[:, sl] = (yo * scale_ref[:, sl]).astype(o_ref.dtype)


def _pool_mixer(zp, w_pool, pool_scale, batch, tp=512):
    m = zp.shape[0]
    seq = m // batch
    tiles_per_seq = seq // tp
    halo_blocks = tp // POOL_HALO
    vmem = (6 * _nbytes((tp, POOL_WIDTH), F32) + 2 * _nbytes(w_pool.shape, F32)
            + 12 * _nbytes((tp, POOL_GROUP), F32) + (8 << 20))
    return pl.pallas_call(
        functools.partial(_pool_kernel, tp=tp),
        grid=(batch, tiles_per_seq),
        in_specs=[pl.BlockSpec((tp, POOL_WIDTH), lambda bi, t: (bi * tiles_per_seq + t, 0)),
                  pl.BlockSpec((POOL_HALO, POOL_WIDTH),
                               lambda bi, t: (jnp.maximum((bi * tiles_per_seq + t) * halo_blocks - 1, 0), 0)),
                  pl.BlockSpec(w_pool.shape, lambda bi, t: (0, 0, 0)),
                  pl.BlockSpec((1, POOL_WIDTH), lambda bi, t: (0, 0))],
        out_specs=pl.BlockSpec((tp, POOL_WIDTH), lambda bi, t: (bi * tiles_per_seq + t, 0)),
        out_shape=jax.ShapeDtypeStruct((m, POOL_WIDTH), BF16),
        compiler_params=_params(2, vmem),
        name="pool_mixer",
    )(zp, zp, w_pool, pool_scale.reshape(1, POOL_WIDTH))


def _mix_out_kernel(a_ref, p_ref, wa_ref, wp_ref, res_ref, gain_ref, o_ref, hg_ref, ssq_ref):
    _init_prenorm(ssq_ref)
    for sl in _column_halves(o_ref):
        acc = (_dot(a_ref[...], wa_ref[:, sl].astype(BF16))
               + _dot(p_ref[...], wp_ref[:, sl].astype(BF16)))
        out = res_ref[:, sl] + acc
        o_ref[:, sl] = out
        _emit_prenorm(out, gain_ref.at[:, sl], hg_ref.at[:, sl], ssq_ref)


def _mix_out(o_attn, o_pool, w, res, next_gain, tm=1024, tn=512):
    m = o_attn.shape[0]
    n = w.shape[1]
    ka, kp = o_attn.shape[1], o_pool.shape[1]
    assert ka == kp
    vmem = (2 * _nbytes((tm, ka + kp), BF16) + _weight_vmem((ka + kp, tn), w.dtype)
            + 8 * _nbytes((tm, tn), F32))
    norm_specs, norm_shapes = _prenorm_specs(m, n, tm, tn)
    return pl.pallas_call(
        _mix_out_kernel,
        grid=(m // tm, n // tn),
        in_specs=[pl.BlockSpec((tm, ka), lambda i, j: (i, 0)),
                  pl.BlockSpec((tm, kp), lambda i, j: (i, 0)),
                  pl.BlockSpec((ka, tn), lambda i, j: (0, j)),
                  pl.BlockSpec((kp, tn), lambda i, j: (1, j)),
                  pl.BlockSpec((tm, tn), lambda i, j: (i, j)),
                  pl.BlockSpec((1, tn), lambda i, j: (0, j))],
        out_specs=[pl.BlockSpec((tm, tn), lambda i, j: (i, j))] + norm_specs,
        out_shape=[jax.ShapeDtypeStruct((m, n), F32)] + norm_shapes,
        compiler_params=_params(2, vmem),
        name="mix_out",
    )(o_attn, o_pool, w, w, res, next_gain.reshape(1, n))


def _cast_kernel(x_ref, o_ref):
    o_ref[...] = x_ref[...].astype(o_ref.dtype)


def _cast_bf16(w, rows=512):
    k, n = w.shape
    rows = min(rows, k)
    return pl.pallas_call(
        _cast_kernel,
        grid=(k // rows,),
        in_specs=[pl.BlockSpec((rows, n), lambda i: (i, 0))],
        out_specs=pl.BlockSpec((rows, n), lambda i: (i, 0)),
        out_shape=jax.ShapeDtypeStruct((k, n), BF16),
        compiler_params=_params(1, 8 * _nbytes((rows, n), F32)),
        name="cast_bf16",
    )(w)


def _cross_kernel(hg_ref, ssq_ref, kv_ref, wq_ref, wo_ref, res_ref, gain_ref, o_ref, u_ref):
    scale = CROSS_DIM ** -0.5
    q = (_dot(hg_ref[...], wq_ref[...]) * _row_scale(ssq_ref, hg_ref.shape[1])).astype(BF16)
    heads = []
    for h in range(CROSS_HEADS):
        sl = slice(h * CROSS_DIM, (h + 1) * CROSS_DIM)
        k = kv_ref[:, sl]
        v = kv_ref[:, CROSS_WIDTH + h * CROSS_DIM:CROSS_WIDTH + (h + 1) * CROSS_DIM]
        s = _dot_nt(q[:, sl], k)
        m = jnp.max(s, axis=1, keepdims=True)
        p = jnp.exp2((s - m) * (scale * LOG2E))
        l = jnp.sum(p, axis=1, keepdims=True)
        heads.append((_dot(p.astype(BF16), v) / l).astype(BF16))
    out = res_ref[...] + _dot(jnp.concatenate(heads, axis=1), wo_ref[...])
    o_ref[...] = out
    ms = jnp.mean(out * out, axis=-1, keepdims=True)
    u_ref[...] = (out * lax.rsqrt(ms + EPS) * gain_ref[...]).astype(u_ref.dtype)


def _cross_attention(hg, ssq, kv, w_q, w_o, res, next_gain, batch, ts=256):
    m, d = res.shape
    mem_len = kv.shape[0] // batch
    steps_per_batch = (m // batch) // ts
    vmem = (2 * _nbytes((ts, d), BF16) * 2 + 4 * _nbytes((ts, d), F32) + _nbytes(w_q.shape, BF16)
            + _nbytes(w_o.shape, BF16) + 4 * _nbytes((ts, d), F32) + (4 << 20))
    row_f32 = pl.BlockSpec((ts, d), lambda i: (i, 0))
    return pl.pallas_call(
        _cross_kernel,
        grid=(m // ts,),
        in_specs=[pl.BlockSpec((ts, d), lambda i: (i, 0)),
                  pl.BlockSpec((ts, V7X_LANES), lambda i: (i, 0)),
                  pl.BlockSpec((mem_len, 2 * CROSS_WIDTH), lambda i: (i // steps_per_batch, 0)),
                  _resident(w_q.shape, lambda i: (0, 0)),
                  _resident(w_o.shape, lambda i: (0, 0)),
                  row_f32,
                  pl.BlockSpec((1, d), lambda i: (0, 0))],
        out_specs=[row_f32, pl.BlockSpec((ts, d), lambda i: (i, 0))],
        out_shape=[jax.ShapeDtypeStruct((m, d), F32), jax.ShapeDtypeStruct((m, d), BF16)],
        compiler_params=_params(1, vmem),
        name="cross_attention",
    )(hg, ssq, kv, w_q, w_o, res, next_gain.reshape(1, d))


def _swiglu_block(h, u, w_in, w_out, next_gain=None, later_weights=()):
    gate, (w_out_bf16, *later_bf16) = _ffn_in(u, w_in, cast_jobs=(w_out,) + tuple(later_weights))
    return _proj_res(gate, w_out_bf16, h, 0.5, tm=512, tn=512, next_gain=next_gain), later_bf16


def kernel(x, mem, positions, g_ffn1, w_ffn1_in, w_ffn1_out, g_mix, w_mix_in, w_pool, pool_scale, w_mix_out, g_cross, g_mem, w_cross_q, w_cross_kv, w_cross_o, g_ffn2, w_ffn2_in, w_ffn2_out, g_final):
    batch, seq, d = x.shape
    m = batch * seq
    depth = g_ffn1.shape[0]
    cos, sin = _rope_tables(positions)
    h = x.reshape(m, d)
    mem2 = mem.reshape(batch * mem.shape[1], d)
    for l in range(depth):
        u = _rmsnorm(h, g_ffn1[l], BF16)
        (h, hg, ssq), (w_mix_in_bf16, w_mix_out_bf16, w_ffn2_in_bf16) = _swiglu_block(
            h, u, w_ffn1_in[l], w_ffn1_out[l], next_gain=g_mix[l],
            later_weights=(w_mix_in[l], w_mix_out[l], w_ffn2_in[l]))

        outs, lses = [], []
        for group, (_, dil) in enumerate(DILATED_CONFIGS):
            q, k, v = _mix_qkv(hg, ssq, w_mix_in_bf16, cos, sin, group, dil, batch)
            o, lse = _dilated_attention(q, k, v, dil)
            outs.append(o)
            lses.append(lse)
        o_attn = _combine(outs, lses)
        zp = _proj(hg, w_mix_in_bf16, tm=1024, tn=1024, out_dtype=F32, col0=ATTN_IN, n=POOL_WIDTH, ssq=ssq)
        o_pool = _pool_mixer(zp, w_pool[l], pool_scale[l], batch)
        h, hg, ssq = _mix_out(o_attn, o_pool, w_mix_out_bf16, h, g_cross[l])

        um = _rmsnorm(mem2, g_mem[l], BF16)
        kv = _proj(um, w_cross_kv[l], tm=um.shape[0], tn=CROSS_WIDTH)
        h, u = _cross_attention(hg, ssq, kv, _cast_bf16(w_cross_q[l]), _cast_bf16(w_cross_o[l]),
                                h, g_ffn2[l], batch)

        h, _ = _swiglu_block(h, u, w_ffn2_in_bf16, w_ffn2_out[l])
    return _rmsnorm(h, g_final, F32).reshape(batch, seq, d)
```

```python
import functools
import math

import jax
import jax.numpy as jnp
from jax import lax
from jax.experimental import pallas as pl
from jax.experimental.pallas import tpu as pltpu

F32 = jnp.float32
BF16 = jnp.bfloat16

D_MODEL = 4096
HEAD_DIM = 128
ATTN_WIDTH = D_MODEL // 2
N_ATTN_HEADS = ATTN_WIDTH // HEAD_DIM
DILATED_CONFIGS = ((128, 1), (512, 4), (2048, 16))
N_DIL = len(DILATED_CONFIGS)
BLOCK = 128
POOL_WIDTH = D_MODEL - ATTN_WIDTH
POOL_WINDOWS = (2, 4, 8, 16)
POOL_GROUP = POOL_WIDTH // len(POOL_WINDOWS)
QKV_WIDTH = 3 * ATTN_WIDTH
ATTN_IN = N_DIL * QKV_WIDTH
D_FF = ((8 * D_MODEL // 3 + 255) // 256) * 256
CROSS_HEADS = 4
CROSS_DIM = 128
CROSS_WIDTH = CROSS_HEADS * CROSS_DIM
ROPE_THETA = 10000.0
EPS = 1e-6
NEG_INF = -1e30
LOG2E = math.log2(math.e)

V7X_LANES = 128
V7X_BF16_SUBLANES = 16
V7X_VMEM_BYTES = 64 * 1024 * 1024
V7X_VMEM_REQUEST_CAP = V7X_VMEM_BYTES - 6 * 1024 * 1024
VMEM_ELEMENTWISE_TEMPORARIES = 8 * 1024 * 1024

POOL_HALO = max(POOL_WINDOWS)
MAX_DIL = max(d for _, d in DILATED_CONFIGS)
ATTN_TILE = BLOCK * MAX_DIL
ATTN_HEADS_PER_STEP = 4


def _params(n_grid, vmem_bytes):
    return pltpu.CompilerParams(
        dimension_semantics=("arbitrary",) * n_grid,
        vmem_limit_bytes=int(min(vmem_bytes, V7X_VMEM_REQUEST_CAP)),
    )


def _nbytes(shape, dtype):
    return math.prod(shape) * jnp.dtype(dtype).itemsize


def _weight_vmem(block_shape, dtype):
    cast_copy = 0 if dtype == BF16 else _nbytes(block_shape, BF16)
    return 2 * _nbytes(block_shape, dtype) + cast_copy


def _resident(block_shape, index_map):
    return pl.BlockSpec(block_shape, index_map, pipeline_mode=pl.Buffered(1))


def _dot(a, b):
    return jnp.dot(a, b, preferred_element_type=F32)


def _dot_nt(a, b):
    return lax.dot_general(a, b, (((1,), (1,)), ((), ())), preferred_element_type=F32)


def _rmsnorm_kernel(x_ref, g_ref, o_ref):
    x = x_ref[...]
    ms = jnp.mean(x * x, axis=-1, keepdims=True)
    o_ref[...] = (x * lax.rsqrt(ms + EPS) * g_ref[...]).astype(o_ref.dtype)


def _rmsnorm(x, g, out_dtype, tm=256):
    m, d = x.shape
    vmem = 2 * tm * d * (4 + jnp.dtype(out_dtype).itemsize) + 4 * tm * d * 4
    return pl.pallas_call(
        _rmsnorm_kernel,
        grid=(m // tm,),
        in_specs=[pl.BlockSpec((tm, d), lambda i: (i, 0)),
                  pl.BlockSpec((1, d), lambda i: (0, 0))],
        out_specs=pl.BlockSpec((tm, d), lambda i: (i, 0)),
        out_shape=jax.ShapeDtypeStruct((m, d), out_dtype),
        compiler_params=_params(1, vmem),
        name="rmsnorm",
    )(x, g.reshape(1, d))


def _rope_kernel(pos_ref, inv_ref, sign_ref, cos_ref, sin_ref):
    ang = pos_ref[...].astype(F32) * inv_ref[...]
    cos_ref[...] = jnp.cos(ang)
    sin_ref[...] = jnp.sin(ang) * sign_ref[...]


def _rope_tables(positions, tm=1024):
    m = positions.size
    half = HEAD_DIM // 2
    inv = 1.0 / (ROPE_THETA ** (jnp.arange(0, HEAD_DIM, 2, dtype=F32) / HEAD_DIM))
    inv = jnp.concatenate([inv, inv]).reshape(1, HEAD_DIM)
    sign = jnp.concatenate([-jnp.ones((half,), F32), jnp.ones((half,), F32)]).reshape(1, HEAD_DIM)
    row = pl.BlockSpec((1, HEAD_DIM), lambda i: (0, 0))
    tab = pl.BlockSpec((tm, HEAD_DIM), lambda i: (i, 0))
    return pl.pallas_call(
        _rope_kernel,
        grid=(m // tm,),
        in_specs=[pl.BlockSpec((tm, 1), lambda i: (i, 0)), row, row],
        out_specs=[tab, tab],
        out_shape=[jax.ShapeDtypeStruct((m, HEAD_DIM), F32)] * 2,
        compiler_params=_params(1, 16 * tm * HEAD_DIM * 4),
        name="rope_tables",
    )(positions.reshape(m, 1), inv, sign)


def _ffn_in_kernel(u_ref, wa_ref, wb_ref, *rest):
    n_jobs = (len(rest) - 1) // 2
    srcs, o_ref, dsts = rest[:n_jobs], rest[n_jobs], rest[n_jobs + 1:]
    half = o_ref.shape[1] // 2
    for c in range(2):
        sl = slice(c * half, (c + 1) * half)
        w = jnp.concatenate([wa_ref[:, sl].astype(BF16), wb_ref[:, sl].astype(BF16)], axis=1)
        ab = _dot(u_ref[...], w)
        a, b = ab[:, :half], ab[:, half:]
        o_ref[:, sl] = (a * jax.nn.sigmoid(a) * b).astype(o_ref.dtype)
    for src, dst in zip(srcs, dsts):
        dst[...] = src[...].astype(dst.dtype)


def _ffn_in(u, w_in, cast_jobs=(), tm=2048, tf=256):
    m, d = u.shape
    nf = D_FF // tf
    n_steps = (m // tm) * nf
    lhs_spec, lhs_buffers = (pl.BlockSpec, 2) if w_in.dtype == BF16 else (_resident, 1)
    vmem = (lhs_buffers * _nbytes((tm, d), BF16) + 2 * _weight_vmem((d, tf), w_in.dtype)
            + 2 * _nbytes((tm, tf), BF16) + 6 * _nbytes((tm, tf), F32))
    job_specs, job_shapes = [], []
    for w in cast_jobs:
        rows = next(r for r in range(V7X_BF16_SUBLANES, w.shape[0] + 1, V7X_BF16_SUBLANES)
                    if w.shape[0] % r == 0 and w.shape[0] // r <= n_steps)
        n_blocks = w.shape[0] // rows
        job_specs.append(pl.BlockSpec(
            (rows, w.shape[1]), lambda i, j, n_blocks=n_blocks: (jnp.minimum(i * nf + j, n_blocks - 1), 0)))
        job_shapes.append(jax.ShapeDtypeStruct(w.shape, BF16))
        vmem += 2 * _nbytes((rows, w.shape[1]), F32) + 2 * _nbytes((rows, w.shape[1]), BF16)
    outs = pl.pallas_call(
        _ffn_in_kernel,
        grid=(m // tm, nf),
        in_specs=[lhs_spec((tm, d), lambda i, j: (i, 0)),
                  pl.BlockSpec((d, tf), lambda i, j: (0, j)),
                  pl.BlockSpec((d, tf), lambda i, j: (0, j + nf))] + job_specs,
        out_specs=[pl.BlockSpec((tm, tf), lambda i, j: (i, j))] + job_specs,
        out_shape=[jax.ShapeDtypeStruct((m, D_FF), BF16)] + job_shapes,
        compiler_params=_params(2, vmem),
        name="ffn_in",
    )(u, w_in, w_in, *cast_jobs)
    return outs[0], outs[1:]


def _emit_prenorm(out, gain_ref, hg_ref, ssq_ref):
    hg_ref[...] = (out * gain_ref[...]).astype(hg_ref.dtype)
    ssq_ref[...] += jnp.broadcast_to(jnp.sum(out * out, axis=1, keepdims=True), ssq_ref.shape)


def _init_prenorm(ssq_ref):
    @pl.when(pl.program_id(1) == 0)
    def _():
        ssq_ref[...] = jnp.zeros_like(ssq_ref)


def _row_scale(ssq_ref, d):
    return lax.rsqrt(ssq_ref[:, :1] * (1.0 / d) + EPS)


def _prenorm_specs(m, n, tm, tn):
    specs = [pl.BlockSpec((tm, tn), lambda i, j: (i, j)), pl.BlockSpec((tm, V7X_LANES), lambda i, j: (i, 0))]
    shapes = [jax.ShapeDtypeStruct((m, n), BF16), jax.ShapeDtypeStruct((m, V7X_LANES), F32)]
    return specs, shapes


def _column_halves(ref):
    half = ref.shape[1] // 2
    return [slice(0, half), slice(half, 2 * half)]


def _proj_res_kernel(lhs_ref, w_ref, res_ref, *rest, scale):
    prenorm = len(rest) > 1
    if prenorm:
        gain_ref, o_ref, hg_ref, ssq_ref = rest
        _init_prenorm(ssq_ref)
    else:
        o_ref, = rest
    for sl in _column_halves(o_ref):
        acc = _dot(lhs_ref[...], w_ref[:, sl].astype(BF16))
        out = res_ref[:, sl] + scale * acc
        o_ref[:, sl] = out
        if prenorm:
            _emit_prenorm(out, gain_ref.at[:, sl], hg_ref.at[:, sl], ssq_ref)


def _proj_res(lhs, w, res, scale, tm, tn, next_gain=None):
    m, k = lhs.shape
    n = w.shape[1]
    vmem = 2 * _nbytes((tm, k), BF16) + _weight_vmem((k, tn), w.dtype) + 8 * _nbytes((tm, tn), F32)
    in_specs = [pl.BlockSpec((tm, k), lambda i, j: (i, 0)),
                pl.BlockSpec((k, tn), lambda i, j: (0, j)),
                pl.BlockSpec((tm, tn), lambda i, j: (i, j))]
    out_specs = [pl.BlockSpec((tm, tn), lambda i, j: (i, j))]
    out_shape = [jax.ShapeDtypeStruct((m, n), F32)]
    args = [lhs, w, res]
    if next_gain is not None:
        in_specs.append(pl.BlockSpec((1, tn), lambda i, j: (0, j)))
        args.append(next_gain.reshape(1, n))
        specs, shapes = _prenorm_specs(m, n, tm, tn)
        out_specs += specs
        out_shape += shapes
    outs = pl.pallas_call(
        functools.partial(_proj_res_kernel, scale=scale),
        grid=(m // tm, n // tn),
        in_specs=in_specs,
        out_specs=out_specs,
        out_shape=out_shape,
        compiler_params=_params(2, vmem),
        name="proj_residual",
    )(*args)
    return outs[0] if next_gain is None else outs


def _proj_kernel(lhs_ref, w_ref, *rest):
    acc = _dot(lhs_ref[...], w_ref[...].astype(BF16))
    if len(rest) == 2:
        acc = acc * _row_scale(rest[0], lhs_ref.shape[1])
    rest[-1][...] = acc.astype(rest[-1].dtype)


def _proj(lhs, w, tm, tn, out_dtype=BF16, col0=0, n=None, ssq=None):
    m, k = lhs.shape
    n = w.shape[1] if n is None else n
    col_blk0 = col0 // tn
    vmem = 2 * _nbytes((tm, k), BF16) + _weight_vmem((k, tn), w.dtype) + 6 * _nbytes((tm, tn), F32)
    in_specs = [pl.BlockSpec((tm, k), lambda i, j: (i, 0)),
                pl.BlockSpec((k, tn), lambda i, j: (0, j + col_blk0))]
    args = [lhs, w]
    if ssq is not None:
        in_specs.append(pl.BlockSpec((tm, V7X_LANES), lambda i, j: (i, 0)))
        args.append(ssq)
    return pl.pallas_call(
        _proj_kernel,
        grid=(m // tm, n // tn),
        in_specs=in_specs,
        out_specs=pl.BlockSpec((tm, tn), lambda i, j: (i, j)),
        out_shape=jax.ShapeDtypeStruct((m, n), out_dtype),
        compiler_params=_params(2, vmem),
        name="proj",
    )(*args)


def _mix_qkv_kernel(u_ref, ssq_ref, wq_ref, wk_ref, wv_ref, cos_ref, sin_ref, q_ref, k_ref, v_ref,
                    scr_ref, tab_ref, *, tn, dil):
    tm, d_in = u_ref.shape
    rows = tm // dil

    def regroup(slab, val):
        if dil == 1:
            return [val]
        scr_ref[slab] = val
        return [scr_ref[slab, pl.ds(r, rows, stride=dil), :] for r in range(dil)]

    @pl.when(pl.program_id(1) == 0)
    def _():
        rs = _row_scale(ssq_ref, d_in)
        tables = (cos_ref[...] * rs, sin_ref[...] * rs, jnp.broadcast_to(rs, (tm, HEAD_DIM)))
        for t, table in enumerate(tables):
            for r, piece in enumerate(regroup(0, table)):
                tab_ref[t, r] = piece

    heads = tn // HEAD_DIM
    for part, (w_ref, o_ref) in enumerate(((wq_ref, q_ref), (wk_ref, k_ref), (wv_ref, v_ref))):
        acc = _dot(u_ref[...], w_ref[...].astype(BF16))
        for h in range(heads):
            pieces = regroup(part * heads + h, acc[:, h * HEAD_DIM:(h + 1) * HEAD_DIM])
            for r, piece in enumerate(pieces):
                if o_ref is v_ref:
                    piece = piece * tab_ref[2, r]
                else:
                    piece = piece * tab_ref[0, r] + pltpu.roll(piece, HEAD_DIM // 2, axis=1) * tab_ref[1, r]
                o_ref[r, :, h * HEAD_DIM:(h + 1) * HEAD_DIM] = piece.astype(o_ref.dtype)


def _mix_qkv(u, ssq, w, cos, sin, group, dil, batch, tm=1024, tn=256):
    m, k = u.shape
    seq = m // batch
    tiles_per_seq = seq // tm
    part_blocks = ATTN_WIDTH // tn

    def w_spec(part):
        col_blk0 = (group * 3 + part) * part_blocks
        return pl.BlockSpec((k, tn), lambda i, j: (0, j + col_blk0))

    out_spec = pl.BlockSpec((None, dil, tm // dil, tn),
                            lambda i, j: (i // tiles_per_seq, 0, i % tiles_per_seq, j))
    out_shape = jax.ShapeDtypeStruct((batch, dil, seq // dil, ATTN_WIDTH), BF16)
    vmem = (2 * _nbytes((tm, k), BF16) + 3 * _weight_vmem((k, tn), w.dtype)
            + 3 * 6 * _nbytes((tm, tn), F32) + 10 * _nbytes((tm, HEAD_DIM), F32))
    return pl.pallas_call(
        functools.partial(_mix_qkv_kernel, tn=tn, dil=dil),
        grid=(m // tm, part_blocks),
        in_specs=[pl.BlockSpec((tm, k), lambda i, j: (i, 0)),
                  pl.BlockSpec((tm, V7X_LANES), lambda i, j: (i, 0)),
                  w_spec(0), w_spec(1), w_spec(2),
                  pl.BlockSpec((tm, HEAD_DIM), lambda i, j: (i, 0)),
                  pl.BlockSpec((tm, HEAD_DIM), lambda i, j: (i, 0))],
        out_specs=[out_spec] * 3,
        out_shape=[out_shape] * 3,
        scratch_shapes=[pltpu.VMEM((3 * tn // HEAD_DIM, tm, HEAD_DIM), F32),
                        pltpu.VMEM((3, dil, tm // dil, HEAD_DIM), F32)],
        compiler_params=_params(2, vmem),
        name=f"mix_qkv_{dil}",
    )(u, ssq, w, w, w, cos, sin)


def _dilated_attn_kernel(q_ref, kp_ref, kc_ref, vp_ref, vc_ref, o_ref, lse_ref,
                         o_scr, lse_scr, *, dil):
    nq = MAX_DIL // dil
    nblk = dil * nq
    qi = lax.broadcasted_iota(jnp.int32, (nblk, BLOCK, BLOCK), 1)
    kj = lax.broadcasted_iota(jnp.int32, (nblk, BLOCK, BLOCK), 2)
    blk = lax.broadcasted_iota(jnp.int32, (nblk, BLOCK, BLOCK), 0)
    has_prev = jnp.logical_or(pl.program_id(1) > 0, blk % nq != 0)
    mask_prev = jnp.logical_and(kj >= qi, has_prev)
    mask_cur = kj <= qi
    scale = HEAD_DIM ** -0.5
    lane = lax.broadcasted_iota(jnp.int32, (nblk, BLOCK, V7X_LANES), 2)
    first_head = pl.program_id(2) * ATTN_HEADS_PER_STEP
    lse_tile = jnp.zeros((nblk, BLOCK, V7X_LANES), F32)

    def blocks(ref, sl):
        return ref[:, :, sl].reshape(nblk, BLOCK, HEAD_DIM)

    def prev_blocks(first_ref, cur, sl):
        first = first_ref[:, :, sl]
        if nq == 1:
            return first
        cur4 = cur.reshape(dil, nq, BLOCK, HEAD_DIM)
        return jnp.concatenate([first[:, None], cur4[:, :-1]], axis=1).reshape(nblk, BLOCK, HEAD_DIM)

    def token_rows(g):
        r, i = divmod(g, nq)
        return pl.ds(i * BLOCK * dil + r, BLOCK, stride=dil)

    for h in range(ATTN_HEADS_PER_STEP):
        sl = slice(h * HEAD_DIM, (h + 1) * HEAD_DIM)
        q = blocks(q_ref, sl)
        kc = blocks(kc_ref, sl)
        vc = blocks(vc_ref, sl)
        kp = prev_blocks(kp_ref, kc, sl)
        vp = prev_blocks(vp_ref, vc, sl)
        sp = jnp.einsum("gqd,gkd->gqk", q, kp, preferred_element_type=F32)
        sc = jnp.einsum("gqd,gkd->gqk", q, kc, preferred_element_type=F32)
        sp = jnp.where(mask_prev, sp, NEG_INF)
        sc = jnp.where(mask_cur, sc, NEG_INF)
        m = jnp.maximum(jnp.max(sp, axis=2, keepdims=True), jnp.max(sc, axis=2, keepdims=True))
        pp = jnp.exp2((sp - m) * (scale * LOG2E))
        pc = jnp.exp2((sc - m) * (scale * LOG2E))
        l = jnp.sum(pp, axis=2, keepdims=True) + jnp.sum(pc, axis=2, keepdims=True)
        o = (jnp.einsum("gqk,gkd->gqd", pp.astype(BF16), vp, preferred_element_type=F32)
             + jnp.einsum("gqk,gkd->gqd", pc.astype(BF16), vc, preferred_element_type=F32)) / l
        for g in range(nblk):
            o_scr[h, token_rows(g), :] = o[g]
        lse_tile = jnp.where(lane == first_head + h, m * scale + jnp.log(l), lse_tile)
    for g in range(nblk):
        lse_scr[token_rows(g), :] = lse_tile[g]
    for h in range(ATTN_HEADS_PER_STEP):
        o_ref[:, h * HEAD_DIM:(h + 1) * HEAD_DIM] = o_scr[h].astype(o_ref.dtype)
    @pl.when(pl.program_id(2) == 0)
    def _():
        lse_ref[...] = lse_scr[...]

    @pl.when(pl.program_id(2) > 0)
    def _():
        lse_ref[...] += lse_scr[...]


def _dilated_attention(q, k, v, dil):
    b, _, sub_len, _ = q.shape
    seq = sub_len * dil
    rows = ATTN_TILE // dil
    tiles_per_seq = seq // ATTN_TILE
    width = ATTN_HEADS_PER_STEP * HEAD_DIM
    col_blocks = ATTN_WIDTH // width

    cur = pl.BlockSpec((None, dil, rows, width), lambda bi, t, hg: (bi, 0, t, hg))
    prev = pl.BlockSpec((None, dil, BLOCK, width),
                        lambda bi, t, hg: (bi, 0, jnp.maximum(t * (rows // BLOCK) - 1, 0), hg))

    vmem = (6 * _nbytes((dil, rows, width), BF16) + 4 * _nbytes((dil, BLOCK, width), BF16)
            + 2 * _nbytes((ATTN_TILE, width), BF16) + 3 * _nbytes((ATTN_TILE, V7X_LANES), F32)
            + _nbytes((ATTN_TILE, width), F32) + 12 * _nbytes((ATTN_TILE, BLOCK), F32))
    o, lse = pl.pallas_call(
        functools.partial(_dilated_attn_kernel, dil=dil),
        grid=(b, tiles_per_seq, col_blocks),
        in_specs=[cur, prev, cur, prev, cur],
        out_specs=[pl.BlockSpec((ATTN_TILE, width), lambda bi, t, hg: (bi * tiles_per_seq + t, hg)),
                   pl.BlockSpec((ATTN_TILE, V7X_LANES), lambda bi, t, hg: (bi * tiles_per_seq + t, 0))],
        out_shape=[jax.ShapeDtypeStruct((b * seq, ATTN_WIDTH), BF16),
                   jax.ShapeDtypeStruct((b * seq, V7X_LANES), F32)],
        scratch_shapes=[pltpu.VMEM((ATTN_HEADS_PER_STEP, ATTN_TILE, HEAD_DIM), F32),
                        pltpu.VMEM((ATTN_TILE, V7X_LANES), F32)],
        compiler_params=_params(3, vmem),
        name=f"dilated_attn_{dil}",
    )(q, k, k, v, v)
    return o, lse


def _combine_kernel(o0_ref, o1_ref, o2_ref, l0_ref, l1_ref, l2_ref, out_ref):
    l0, l1, l2 = l0_ref[...], l1_ref[...], l2_ref[...]
    m = jnp.maximum(jnp.maximum(l0, l1), l2)
    e0, e1, e2 = jnp.exp(l0 - m), jnp.exp(l1 - m), jnp.exp(l2 - m)
    den = e0 + e1 + e2
    w0, w1, w2 = e0 / den, e1 / den, e2 / den
    for h in range(N_ATTN_HEADS):
        sl = slice(h * HEAD_DIM, (h + 1) * HEAD_DIM)
        acc = (w0[:, h:h + 1] * o0_ref[:, sl].astype(F32)
               + w1[:, h:h + 1] * o1_ref[:, sl].astype(F32)
               + w2[:, h:h + 1] * o2_ref[:, sl].astype(F32))
        out_ref[:, sl] = acc.astype(out_ref.dtype)


def _combine(outs, lses, tq=512):
    m = outs[0].shape[0]
    o_spec = pl.BlockSpec((tq, ATTN_WIDTH), lambda i: (i, 0))
    l_spec = pl.BlockSpec((tq, V7X_LANES), lambda i: (i, 0))
    vmem = 8 * _nbytes((tq, ATTN_WIDTH), BF16) + 16 * _nbytes((tq, V7X_LANES), F32) + VMEM_ELEMENTWISE_TEMPORARIES
    return pl.pallas_call(
        _combine_kernel,
        grid=(m // tq,),
        in_specs=[o_spec] * 3 + [l_spec] * 3,
        out_specs=o_spec,
        out_shape=jax.ShapeDtypeStruct((m, ATTN_WIDTH), BF16),
        compiler_params=_params(1, vmem),
        name="attn_combine",
    )(*outs, *lses)


def _pool_kernel(zm_ref, zh_ref, wp_ref, scale_ref, o_ref, *, tp):
    t = pl.program_id(1)
    pos = t * tp + lax.broadcasted_iota(jnp.int32, (tp, 1), 0)
    for g, win in enumerate(POOL_WINDOWS):
        sl = slice(g * POOL_GROUP, (g + 1) * POOL_GROUP)
        main = zm_ref[:, sl]
        halo = jnp.where(t > 0, zh_ref[:, sl], 0.0)
        s = jnp.concatenate([halo, main], axis=0)
        shift = 1
        while shift < win:
            s = s + pltpu.roll(s, shift, axis=0)
            shift *= 2
        cnt = jnp.minimum(pos + 1, win).astype(F32)
        y = s[POOL_HALO:] / cnt - main
        yo = _dot(y.astype(BF16), wp_ref[g].astype(BF16))
        o_ref[:, sl] = (yo * scale_ref[:, sl]).astype(o_ref.dtype)


def _pool_mixer(zp, w_pool, pool_scale, batch, tp=512):
    m = zp.shape[0]
    seq = m // batch
    tiles_per_seq = seq // tp
    halo_blocks = tp // POOL_HALO
    vmem = (6 * _nbytes((tp, POOL_WIDTH), F32) + 2 * _nbytes(w_pool.shape, F32)
            + 12 * _nbytes((tp, POOL_GROUP), F32) + VMEM_ELEMENTWISE_TEMPORARIES)
    return pl.pallas_call(
        functools.partial(_pool_kernel, tp=tp),
        grid=(batch, tiles_per_seq),
        in_specs=[pl.BlockSpec((tp, POOL_WIDTH), lambda bi, t: (bi * tiles_per_seq + t, 0)),
                  pl.BlockSpec((POOL_HALO, POOL_WIDTH),
                               lambda bi, t: (jnp.maximum((bi * tiles_per_seq + t) * halo_blocks - 1, 0), 0)),
                  pl.BlockSpec(w_pool.shape, lambda bi, t: (0, 0, 0)),
                  pl.BlockSpec((1, POOL_WIDTH), lambda bi, t: (0, 0))],
        out_specs=pl.BlockSpec((tp, POOL_WIDTH), lambda bi, t: (bi * tiles_per_seq + t, 0)),
        out_shape=jax.ShapeDtypeStruct((m, POOL_WIDTH), BF16),
        compiler_params=_params(2, vmem),
        name="pool_mixer",
    )(zp, zp, w_pool, pool_scale.reshape(1, POOL_WIDTH))


def _mix_out_kernel(a_ref, p_ref, wa_ref, wp_ref, res_ref, gain_ref, o_ref, hg_ref, ssq_ref):
    _init_prenorm(ssq_ref)
    for sl in _column_halves(o_ref):
        acc = (_dot(a_ref[...], wa_ref[:, sl].astype(BF16))
               + _dot(p_ref[...], wp_ref[:, sl].astype(BF16)))
        out = res_ref[:, sl] + acc
        o_ref[:, sl] = out
        _emit_prenorm(out, gain_ref.at[:, sl], hg_ref.at[:, sl], ssq_ref)


def _mix_out(o_attn, o_pool, w, res, next_gain, tm=1024, tn=512):
    m = o_attn.shape[0]
    n = w.shape[1]
    ka, kp = o_attn.shape[1], o_pool.shape[1]
    assert ka == kp
    vmem = (2 * _nbytes((tm, ka + kp), BF16) + _weight_vmem((ka + kp, tn), w.dtype)
            + 8 * _nbytes((tm, tn), F32))
    norm_specs, norm_shapes = _prenorm_specs(m, n, tm, tn)
    return pl.pallas_call(
        _mix_out_kernel,
        grid=(m // tm, n // tn),
        in_specs=[pl.BlockSpec((tm, ka), lambda i, j: (i, 0)),
                  pl.BlockSpec((tm, kp), lambda i, j: (i, 0)),
                  pl.BlockSpec((ka, tn), lambda i, j: (0, j)),
                  pl.BlockSpec((kp, tn), lambda i, j: (1, j)),
                  pl.BlockSpec((tm, tn), lambda i, j: (i, j)),
                  pl.BlockSpec((1, tn), lambda i, j: (0, j))],
        out_specs=[pl.BlockSpec((tm, tn), lambda i, j: (i, j))] + norm_specs,
        out_shape=[jax.ShapeDtypeStruct((m, n), F32)] + norm_shapes,
        compiler_params=_params(2, vmem),
        name="mix_out",
    )(o_attn, o_pool, w, w, res, next_gain.reshape(1, n))


def _cast_kernel(x_ref, o_ref):
    o_ref[...] = x_ref[...].astype(o_ref.dtype)


def _cast_bf16(w, rows=512):
    k, n = w.shape
    rows = min(rows, k)
    return pl.pallas_call(
        _cast_kernel,
        grid=(k // rows,),
        in_specs=[pl.BlockSpec((rows, n), lambda i: (i, 0))],
        out_specs=pl.BlockSpec((rows, n), lambda i: (i, 0)),
        out_shape=jax.ShapeDtypeStruct((k, n), BF16),
        compiler_params=_params(1, 8 * _nbytes((rows, n), F32)),
        name="cast_bf16",
    )(w)


def _cross_kernel(hg_ref, ssq_ref, kv_ref, wq_ref, wo_ref, res_ref, gain_ref, o_ref, u_ref):
    scale = CROSS_DIM ** -0.5
    q = (_dot(hg_ref[...], wq_ref[...]) * _row_scale(ssq_ref, hg_ref.shape[1])).astype(BF16)
    heads = []
    for h in range(CROSS_HEADS):
        sl = slice(h * CROSS_DIM, (h + 1) * CROSS_DIM)
        k = kv_ref[:, sl]
        v = kv_ref[:, CROSS_WIDTH + h * CROSS_DIM:CROSS_WIDTH + (h + 1) * CROSS_DIM]
        s = _dot_nt(q[:, sl], k)
        m = jnp.max(s, axis=1, keepdims=True)
        p = jnp.exp2((s - m) * (scale * LOG2E))
        l = jnp.sum(p, axis=1, keepdims=True)
        heads.append((_dot(p.astype(BF16), v) / l).astype(BF16))
    out = res_ref[...] + _dot(jnp.concatenate(heads, axis=1), wo_ref[...])
    o_ref[...] = out
    ms = jnp.mean(out * out, axis=-1, keepdims=True)
    u_ref[...] = (out * lax.rsqrt(ms + EPS) * gain_ref[...]).astype(u_ref.dtype)


def _cross_attention(hg, ssq, kv, w_q, w_o, res, next_gain, batch, ts=256):
    m, d = res.shape
    mem_len = kv.shape[0] // batch
    steps_per_batch = (m // batch) // ts
    vmem = (2 * _nbytes((ts, d), BF16) * 2 + 4 * _nbytes((ts, d), F32) + _nbytes(w_q.shape, BF16)
            + _nbytes(w_o.shape, BF16) + 4 * _nbytes((ts, d), F32) + VMEM_ELEMENTWISE_TEMPORARIES)
    row_f32 = pl.BlockSpec((ts, d), lambda i: (i, 0))
    return pl.pallas_call(
        _cross_kernel,
        grid=(m // ts,),
        in_specs=[pl.BlockSpec((ts, d), lambda i: (i, 0)),
                  pl.BlockSpec((ts, V7X_LANES), lambda i: (i, 0)),
                  pl.BlockSpec((mem_len, 2 * CROSS_WIDTH), lambda i: (i // steps_per_batch, 0)),
                  _resident(w_q.shape, lambda i: (0, 0)),
                  _resident(w_o.shape, lambda i: (0, 0)),
                  row_f32,
                  pl.BlockSpec((1, d), lambda i: (0, 0))],
        out_specs=[row_f32, pl.BlockSpec((ts, d), lambda i: (i, 0))],
        out_shape=[jax.ShapeDtypeStruct((m, d), F32), jax.ShapeDtypeStruct((m, d), BF16)],
        compiler_params=_params(1, vmem),
        name="cross_attention",
    )(hg, ssq, kv, w_q, w_o, res, next_gain.reshape(1, d))


def _swiglu_block(h, u, w_in, w_out, next_gain=None, later_weights=()):
    gate, (w_out_bf16, *later_bf16) = _ffn_in(u, w_in, cast_jobs=(w_out,) + tuple(later_weights))
    return _proj_res(gate, w_out_bf16, h, 0.5, tm=512, tn=512, next_gain=next_gain), later_bf16


def kernel(x, mem, positions, g_ffn1, w_ffn1_in, w_ffn1_out, g_mix, w_mix_in, w_pool, pool_scale, w_mix_out, g_cross, g_mem, w_cross_q, w_cross_kv, w_cross_o, g_ffn2, w_ffn2_in, w_ffn2_out, g_final):
    batch, seq, d = x.shape
    m = batch * seq
    depth = g_ffn1.shape[0]
    cos, sin = _rope_tables(positions)
    h = x.reshape(m, d)
    mem2 = mem.reshape(batch * mem.shape[1], d)
    for l in range(depth):
        u = _rmsnorm(h, g_ffn1[l], BF16)
        (h, hg, ssq), (w_mix_in_bf16, w_mix_out_bf16, w_ffn2_in_bf16) = _swiglu_block(
            h, u, w_ffn1_in[l], w_ffn1_out[l], next_gain=g_mix[l],
            later_weights=(w_mix_in[l], w_mix_out[l], w_ffn2_in[l]))

        outs, lses = [], []
        for group, (_, dil) in enumerate(DILATED_CONFIGS):
            q, k, v = _mix_qkv(hg, ssq, w_mix_in_bf16, cos, sin, group, dil, batch)
            o, lse = _dilated_attention(q, k, v, dil)
            outs.append(o)
            lses.append(lse)
        o_attn = _combine(outs, lses)
        zp = _proj(hg, w_mix_in_bf16, tm=1024, tn=1024, out_dtype=F32, col0=ATTN_IN, n=POOL_WIDTH, ssq=ssq)
        o_pool = _pool_mixer(zp, w_pool[l], pool_scale[l], batch)
        h, hg, ssq = _mix_out(o_attn, o_pool, w_mix_out_bf16, h, g_cross[l])

        um = _rmsnorm(mem2, g_mem[l], BF16)
        kv = _proj(um, w_cross_kv[l], tm=um.shape[0], tn=CROSS_WIDTH)
        h, u = _cross_attention(hg, ssq, kv, _cast_bf16(w_cross_q[l]), _cast_bf16(w_cross_o[l]),
                                h, g_ffn2[l], batch)

        h, _ = _swiglu_block(h, u, w_ffn2_in_bf16, w_ffn2_out[l])
    return _rmsnorm(h, g_final, F32).reshape(batch, seq, d)
```

```python
import functools
import math

import jax
import jax.numpy as jnp
from jax import lax
from jax.experimental import pallas as pl
from jax.experimental.pallas import tpu as pltpu

F32 = jnp.float32
BF16 = jnp.bfloat16

D_MODEL = 4096
HEAD_DIM = 128
ATTN_WIDTH = D_MODEL // 2
N_ATTN_HEADS = ATTN_WIDTH // HEAD_DIM
DILATED_CONFIGS = ((128, 1), (512, 4), (2048, 16))
N_DIL = len(DILATED_CONFIGS)
BLOCK = 128
POOL_WIDTH = D_MODEL - ATTN_WIDTH
POOL_WINDOWS = (2, 4, 8, 16)
POOL_GROUP = POOL_WIDTH // len(POOL_WINDOWS)
QKV_WIDTH = 3 * ATTN_WIDTH
ATTN_IN = N_DIL * QKV_WIDTH
D_FF = ((8 * D_MODEL // 3 + 255) // 256) * 256
CROSS_HEADS = 4
CROSS_DIM = 128
CROSS_WIDTH = CROSS_HEADS * CROSS_DIM
ROPE_THETA = 10000.0
EPS = 1e-6
NEG_INF = -1e30
LOG2E = math.log2(math.e)

V7X_LANES = 128
V7X_BF16_SUBLANES = 16
V7X_VMEM_BYTES = 64 * 1024 * 1024
V7X_VMEM_REQUEST_CAP = V7X_VMEM_BYTES - 6 * 1024 * 1024
VMEM_ELEMENTWISE_TEMPORARIES = 8 * 1024 * 1024

POOL_HALO = max(POOL_WINDOWS)
MAX_DIL = max(d for _, d in DILATED_CONFIGS)
ATTN_TILE = BLOCK * MAX_DIL
ATTN_HEADS_PER_STEP = 4


def _params(n_grid, vmem_bytes):
    return pltpu.CompilerParams(
        dimension_semantics=("arbitrary",) * n_grid,
        vmem_limit_bytes=int(min(vmem_bytes, V7X_VMEM_REQUEST_CAP)),
    )


def _nbytes(shape, dtype):
    return math.prod(shape) * jnp.dtype(dtype).itemsize


def _weight_vmem(block_shape, dtype):
    cast_copy = 0 if dtype == BF16 else _nbytes(block_shape, BF16)
    return 2 * _nbytes(block_shape, dtype) + cast_copy


def _resident(block_shape, index_map):
    return pl.BlockSpec(block_shape, index_map, pipeline_mode=pl.Buffered(1))


def _dot(a, b):
    return jnp.dot(a, b, preferred_element_type=F32)


def _dot_nt(a, b):
    return lax.dot_general(a, b, (((1,), (1,)), ((), ())), preferred_element_type=F32)


def _rmsnorm_kernel(x_ref, g_ref, o_ref):
    x = x_ref[...]
    ms = jnp.mean(x * x, axis=-1, keepdims=True)
    o_ref[...] = (x * lax.rsqrt(ms + EPS) * g_ref[...]).astype(o_ref.dtype)


def _rmsnorm(x, g, out_dtype, tm=256):
    m, d = x.shape
    vmem = 2 * tm * d * (4 + jnp.dtype(out_dtype).itemsize) + 4 * tm * d * 4
    return pl.pallas_call(
        _rmsnorm_kernel,
        grid=(m // tm,),
        in_specs=[pl.BlockSpec((tm, d), lambda i: (i, 0)),
                  pl.BlockSpec((1, d), lambda i: (0, 0))],
        out_specs=pl.BlockSpec((tm, d), lambda i: (i, 0)),
        out_shape=jax.ShapeDtypeStruct((m, d), out_dtype),
        compiler_params=_params(1, vmem),
        name="rmsnorm",
    )(x, g.reshape(1, d))


def _rope_kernel(pos_ref, inv_ref, sign_ref, cos_ref, sin_ref):
    ang = pos_ref[...].astype(F32) * inv_ref[...]
    cos_ref[...] = jnp.cos(ang)
    sin_ref[...] = jnp.sin(ang) * sign_ref[...]


def _rope_tables(positions, tm=1024):
    m = positions.size
    half = HEAD_DIM // 2
    inv = 1.0 / (ROPE_THETA ** (jnp.arange(0, HEAD_DIM, 2, dtype=F32) / HEAD_DIM))
    inv = jnp.concatenate([inv, inv]).reshape(1, HEAD_DIM)
    sign = jnp.concatenate([-jnp.ones((half,), F32), jnp.ones((half,), F32)]).reshape(1, HEAD_DIM)
    row = pl.BlockSpec((1, HEAD_DIM), lambda i: (0, 0))
    tab = pl.BlockSpec((tm, HEAD_DIM), lambda i: (i, 0))
    return pl.pallas_call(
        _rope_kernel,
        grid=(m // tm,),
        in_specs=[pl.BlockSpec((tm, 1), lambda i: (i, 0)), row, row],
        out_specs=[tab, tab],
        out_shape=[jax.ShapeDtypeStruct((m, HEAD_DIM), F32)] * 2,
        compiler_params=_params(1, 16 * tm * HEAD_DIM * 4),
        name="rope_tables",
    )(positions.reshape(m, 1), inv, sign)


def _ffn_in_kernel(u_ref, wa_ref, wb_ref, *rest):
    n_jobs = (len(rest) - 1) // 2
    srcs, o_ref, dsts = rest[:n_jobs], rest[n_jobs], rest[n_jobs + 1:]
    half = o_ref.shape[1] // 2
    for c in range(2):
        sl = slice(c * half, (c + 1) * half)
        w = jnp.concatenate([wa_ref[:, sl].astype(BF16), wb_ref[:, sl].astype(BF16)], axis=1)
        ab = _dot(u_ref[...], w)
        a, b = ab[:, :half], ab[:, half:]
        o_ref[:, sl] = (a * jax.nn.sigmoid(a) * b).astype(o_ref.dtype)
    for src, dst in zip(srcs, dsts):
        dst[...] = src[...].astype(dst.dtype)


def _ffn_in(u, w_in, cast_jobs=(), tm=2048, tf=256):
    m, d = u.shape
    nf = D_FF // tf
    n_steps = (m // tm) * nf
    lhs_spec, lhs_buffers = (pl.BlockSpec, 2) if w_in.dtype == BF16 else (_resident, 1)
    vmem = (lhs_buffers * _nbytes((tm, d), BF16) + 2 * _weight_vmem((d, tf), w_in.dtype)
            + 2 * _nbytes((tm, tf), BF16) + 6 * _nbytes((tm, tf), F32))
    job_specs, job_shapes = [], []
    for w in cast_jobs:
        rows = next(r for r in range(V7X_BF16_SUBLANES, w.shape[0] + 1, V7X_BF16_SUBLANES)
                    if w.shape[0] % r == 0 and w.shape[0] // r <= n_steps)
        n_blocks = w.shape[0] // rows
        job_specs.append(pl.BlockSpec(
            (rows, w.shape[1]), lambda i, j, n_blocks=n_blocks: (jnp.minimum(i * nf + j, n_blocks - 1), 0)))
        job_shapes.append(jax.ShapeDtypeStruct(w.shape, BF16))
        vmem += 2 * _nbytes((rows, w.shape[1]), F32) + 2 * _nbytes((rows, w.shape[1]), BF16)
    outs = pl.pallas_call(
        _ffn_in_kernel,
        grid=(m // tm, nf),
        in_specs=[lhs_spec((tm, d), lambda i, j: (i, 0)),
                  pl.BlockSpec((d, tf), lambda i, j: (0, j)),
                  pl.BlockSpec((d, tf), lambda i, j: (0, j + nf))] + job_specs,
        out_specs=[pl.BlockSpec((tm, tf), lambda i, j: (i, j))] + job_specs,
        out_shape=[jax.ShapeDtypeStruct((m, D_FF), BF16)] + job_shapes,
        compiler_params=_params(2, vmem),
        name="ffn_in",
    )(u, w_in, w_in, *cast_jobs)
    return outs[0], outs[1:]


def _emit_prenorm(out, gain_ref, hg_ref, ssq_ref):
    hg_ref[...] = (out * gain_ref[...]).astype(hg_ref.dtype)
    ssq_ref[...] += jnp.broadcast_to(jnp.sum(out * out, axis=1, keepdims=True), ssq_ref.shape)


def _init_prenorm(ssq_ref):
    @pl.when(pl.program_id(1) == 0)
    def _():
        ssq_ref[...] = jnp.zeros_like(ssq_ref)


def _row_scale(ssq_ref, d):
    return lax.rsqrt(ssq_ref[:, :1] * (1.0 / d) + EPS)


def _prenorm_specs(m, n, tm, tn):
    specs = [pl.BlockSpec((tm, tn), lambda i, j: (i, j)), pl.BlockSpec((tm, V7X_LANES), lambda i, j: (i, 0))]
    shapes = [jax.ShapeDtypeStruct((m, n), BF16), jax.ShapeDtypeStruct((m, V7X_LANES), F32)]
    return specs, shapes


def _column_halves(ref):
    half = ref.shape[1] // 2
    return [slice(0, half), slice(half, 2 * half)]


def _proj_res_kernel(lhs_ref, w_ref, res_ref, *rest, scale):
    prenorm = len(rest) > 1
    if prenorm:
        gain_ref, o_ref, hg_ref, ssq_ref = rest
        _init_prenorm(ssq_ref)
    else:
        o_ref, = rest
    for sl in _column_halves(o_ref):
        acc = _dot(lhs_ref[...], w_ref[:, sl].astype(BF16))
        out = res_ref[:, sl] + scale * acc
        o_ref[:, sl] = out
        if prenorm:
            _emit_prenorm(out, gain_ref.at[:, sl], hg_ref.at[:, sl], ssq_ref)


def _proj_res(lhs, w, res, scale, tm, tn, next_gain=None):
    m, k = lhs.shape
    n = w.shape[1]
    vmem = 2 * _nbytes((tm, k), BF16) + _weight_vmem((k, tn), w.dtype) + 8 * _nbytes((tm, tn), F32)
    in_specs = [pl.BlockSpec((tm, k), lambda i, j: (i, 0)),
                pl.BlockSpec((k, tn), lambda i, j: (0, j)),
                pl.BlockSpec((tm, tn), lambda i, j: (i, j))]
    out_specs = [pl.BlockSpec((tm, tn), lambda i, j: (i, j))]
    out_shape = [jax.ShapeDtypeStruct((m, n), F32)]
    args = [lhs, w, res]
    if next_gain is not None:
        in_specs.append(pl.BlockSpec((1, tn), lambda i, j: (0, j)))
        args.append(next_gain.reshape(1, n))
        specs, shapes = _prenorm_specs(m, n, tm, tn)
        out_specs += specs
        out_shape += shapes
    outs = pl.pallas_call(
        functools.partial(_proj_res_kernel, scale=scale),
        grid=(m // tm, n // tn),
        in_specs=in_specs,
        out_specs=out_specs,
        out_shape=out_shape,
        compiler_params=_params(2, vmem),
        name="proj_residual",
    )(*args)
    return outs[0] if next_gain is None else outs


def _proj_kernel(lhs_ref, w_ref, *rest):
    acc = _dot(lhs_ref[...], w_ref[...].astype(BF16))
    if len(rest) == 2:
        acc = acc * _row_scale(rest[0], lhs_ref.shape[1])
    rest[-1][...] = acc.astype(rest[-1].dtype)


def _proj(lhs, w, tm, tn, out_dtype=BF16, col0=0, n=None, ssq=None):
    m, k = lhs.shape
    n = w.shape[1] if n is None else n
    col_blk0 = col0 // tn
    vmem = 2 * _nbytes((tm, k), BF16) + _weight_vmem((k, tn), w.dtype) + 6 * _nbytes((tm, tn), F32)
    in_specs = [pl.BlockSpec((tm, k), lambda i, j: (i, 0)),
                pl.BlockSpec((k, tn), lambda i, j: (0, j + col_blk0))]
    args = [lhs, w]
    if ssq is not None:
        in_specs.append(pl.BlockSpec((tm, V7X_LANES), lambda i, j: (i, 0)))
        args.append(ssq)
    return pl.pallas_call(
        _proj_kernel,
        grid=(m // tm, n // tn),
        in_specs=in_specs,
        out_specs=pl.BlockSpec((tm, tn), lambda i, j: (i, j)),
        out_shape=jax.ShapeDtypeStruct((m, n), out_dtype),
        compiler_params=_params(2, vmem),
        name="proj",
    )(*args)


def _attn_blocks(ref, sl):
    return ref[:, :, sl].reshape(MAX_DIL, BLOCK, HEAD_DIM)


def _attn_prev_blocks(first_ref, cur, sl, dil):
    nq = MAX_DIL // dil
    first = first_ref[:, :, sl]
    if nq == 1:
        return first
    cur4 = cur.reshape(dil, nq, BLOCK, HEAD_DIM)
    return jnp.concatenate([first[:, None], cur4[:, :-1]], axis=1).reshape(MAX_DIL, BLOCK, HEAD_DIM)


def _attn_token_rows(g, dil):
    r, i = divmod(g, MAX_DIL // dil)
    return pl.ds(i * BLOCK * dil + r, BLOCK, stride=dil)


def _attn_masks(dil, tile_has_prev):
    shape = (MAX_DIL, BLOCK, BLOCK)
    blk = lax.broadcasted_iota(jnp.int32, shape, 0)
    qi = lax.broadcasted_iota(jnp.int32, shape, 1)
    kj = lax.broadcasted_iota(jnp.int32, shape, 2)
    has_prev = jnp.logical_or(tile_has_prev, blk % (MAX_DIL // dil) != 0)
    return jnp.logical_and(kj >= qi, has_prev), kj <= qi


def _attn_probs(q_ref, kp_ref, kc_ref, sl, dil, masks):
    mask_prev, mask_cur = masks
    scale = HEAD_DIM ** -0.5
    q = _attn_blocks(q_ref, sl)
    kc = _attn_blocks(kc_ref, sl)
    kp = _attn_prev_blocks(kp_ref, kc, sl, dil)
    sp = jnp.einsum("gqd,gkd->gqk", q, kp, preferred_element_type=F32)
    sc = jnp.einsum("gqd,gkd->gqk", q, kc, preferred_element_type=F32)
    sp = jnp.where(mask_prev, sp, NEG_INF)
    sc = jnp.where(mask_cur, sc, NEG_INF)
    m = jnp.maximum(jnp.max(sp, axis=2, keepdims=True), jnp.max(sc, axis=2, keepdims=True))
    pp = jnp.exp2((sp - m) * (scale * LOG2E))
    pc = jnp.exp2((sc - m) * (scale * LOG2E))
    l = jnp.sum(pp, axis=2, keepdims=True) + jnp.sum(pc, axis=2, keepdims=True)
    return pp.astype(BF16), pc.astype(BF16), l, m * scale + jnp.log(l)


def _attn_output(pp, pc, l, vp_ref, vc_ref, sl, dil, o_scr, slot):
    vc = _attn_blocks(vc_ref, sl)
    vp = _attn_prev_blocks(vp_ref, vc, sl, dil)
    o = (jnp.einsum("gqk,gkd->gqd", pp, vp, preferred_element_type=F32)
         + jnp.einsum("gqk,gkd->gqd", pc, vc, preferred_element_type=F32)) / l
    for g in range(MAX_DIL):
        o_scr[slot, _attn_token_rows(g, dil), :] = o[g]


def _attn_store_lse(lse_tile, dil, lse_scr, lse_ref, first):
    for g in range(MAX_DIL):
        lse_scr[_attn_token_rows(g, dil), :] = lse_tile[g]

    @pl.when(first)
    def _():
        lse_ref[...] = lse_scr[...]

    @pl.when(jnp.logical_not(first))
    def _():
        lse_ref[...] += lse_scr[...]


def _mix_qkv_kernel(u_ref, ssq_ref, wq_ref, wk_ref, wv_ref, cos_ref, sin_ref, *rest, tn, dil,
                    attn_dil, tiles_per_batch, tiles_per_attn_tile):
    if attn_dil is None:
        q_ref, k_ref, v_ref, scr_ref, tab_ref = rest
    else:
        (aq_ref, akp_ref, akc_ref, avp_ref, avc_ref, q_ref, k_ref, v_ref, ao_ref, alse_ref,
         scr_ref, tab_ref, ao_scr, alse_scr) = rest
        tile_has_prev = (pl.program_id(0) % tiles_per_batch) >= tiles_per_attn_tile
        head = (pl.program_id(0) % tiles_per_attn_tile) * (ATTN_WIDTH // tn) + pl.program_id(1)
        whole = slice(0, HEAD_DIM)
    tm, d_in = u_ref.shape
    rows = tm // dil

    def regroup(slab, val):
        if dil == 1:
            return [val]
        scr_ref[slab] = val
        return [scr_ref[slab, pl.ds(r, rows, stride=dil), :] for r in range(dil)]

    @pl.when(pl.program_id(1) == 0)
    def _():
        rs = _row_scale(ssq_ref, d_in)
        tables = (cos_ref[...] * rs, sin_ref[...] * rs, jnp.broadcast_to(rs, (tm, HEAD_DIM)))
        for t, table in enumerate(tables):
            for r, piece in enumerate(regroup(0, table)):
                tab_ref[t, r] = piece

    heads = tn // HEAD_DIM
    for part, (w_ref, o_ref) in enumerate(((wq_ref, q_ref), (wk_ref, k_ref), (wv_ref, v_ref))):
        acc = _dot(u_ref[...], w_ref[...].astype(BF16))
        if part == 0 and attn_dil is not None:
            pp, pc, l, lse = _attn_probs(aq_ref, akp_ref, akc_ref, whole, attn_dil,
                                         _attn_masks(attn_dil, tile_has_prev))
        for h in range(heads):
            pieces = regroup(part * heads + h, acc[:, h * HEAD_DIM:(h + 1) * HEAD_DIM])
            for r, piece in enumerate(pieces):
                if o_ref is v_ref:
                    piece = piece * tab_ref[2, r]
                else:
                    piece = piece * tab_ref[0, r] + pltpu.roll(piece, HEAD_DIM // 2, axis=1) * tab_ref[1, r]
                o_ref[r, :, h * HEAD_DIM:(h + 1) * HEAD_DIM] = piece.astype(o_ref.dtype)

    if attn_dil is not None:
        _attn_output(pp, pc, l, avp_ref, avc_ref, whole, attn_dil, ao_scr, 0)
        ao_ref[...] = ao_scr[0].astype(ao_ref.dtype)
        lane = lax.broadcasted_iota(jnp.int32, (MAX_DIL, BLOCK, V7X_LANES), 2)
        _attn_store_lse(jnp.where(lane == head, lse, 0.0), attn_dil, alse_scr, alse_ref, head == 0)


def _attn_unit_specs(dil, unit):
    rows = ATTN_TILE // dil

    def cur_map(*idx):
        b, t, h = unit(*idx)
        return b, 0, t, h

    def prev_map(*idx):
        b, t, h = unit(*idx)
        return b, 0, jnp.maximum(t * (rows // BLOCK) - 1, 0), h

    cur = pl.BlockSpec((None, dil, rows, HEAD_DIM), cur_map)
    prev = pl.BlockSpec((None, dil, BLOCK, HEAD_DIM), prev_map)
    return [cur, prev, cur, prev, cur]


def _mix_qkv(u, ssq, w, cos, sin, group, dil, batch, attn_qkv=None, attn_dil=None, tm=1024, tn=256):
    m, k = u.shape
    seq = m // batch
    tiles_per_seq = seq // tm
    part_blocks = ATTN_WIDTH // tn

    def w_spec(part):
        col_blk0 = (group * 3 + part) * part_blocks
        return pl.BlockSpec((k, tn), lambda i, j: (0, j + col_blk0))

    out_spec = pl.BlockSpec((None, dil, tm // dil, tn),
                            lambda i, j: (i // tiles_per_seq, 0, i % tiles_per_seq, j))
    out_shape = jax.ShapeDtypeStruct((batch, dil, seq // dil, ATTN_WIDTH), BF16)
    vmem = (2 * _nbytes((tm, k), BF16) + 3 * _weight_vmem((k, tn), w.dtype)
            + 3 * 6 * _nbytes((tm, tn), F32) + 10 * _nbytes((tm, HEAD_DIM), F32))
    in_specs = [pl.BlockSpec((tm, k), lambda i, j: (i, 0)),
                pl.BlockSpec((tm, V7X_LANES), lambda i, j: (i, 0)),
                w_spec(0), w_spec(1), w_spec(2),
                pl.BlockSpec((tm, HEAD_DIM), lambda i, j: (i, 0)),
                pl.BlockSpec((tm, HEAD_DIM), lambda i, j: (i, 0))]
    out_specs, out_shapes = [out_spec] * 3, [out_shape] * 3
    scratch = [pltpu.VMEM((3 * tn // HEAD_DIM, tm, HEAD_DIM), F32),
               pltpu.VMEM((3, dil, tm // dil, HEAD_DIM), F32)]
    args = [u, ssq, w, w, w, cos, sin]
    tiles_per_attn_tile = ATTN_TILE // tm
    if attn_qkv is not None:
        assert tiles_per_attn_tile * part_blocks == N_ATTN_HEADS and seq % ATTN_TILE == 0
        attn_tiles_per_seq = seq // ATTN_TILE

        def unit(i, j):
            in_seq = i % tiles_per_seq
            return (i // tiles_per_seq, in_seq // tiles_per_attn_tile,
                    (in_seq % tiles_per_attn_tile) * part_blocks + j)

        def row_tile(i, j):
            b, t, _ = unit(i, j)
            return b * attn_tiles_per_seq + t

        aq, ak, av = attn_qkv
        in_specs += _attn_unit_specs(attn_dil, unit)
        args += [aq, ak, ak, av, av]
        out_specs += [pl.BlockSpec((ATTN_TILE, HEAD_DIM), lambda i, j: (row_tile(i, j), unit(i, j)[2])),
                      pl.BlockSpec((ATTN_TILE, V7X_LANES), lambda i, j: (row_tile(i, j), 0))]
        out_shapes += [jax.ShapeDtypeStruct((m, ATTN_WIDTH), BF16),
                       jax.ShapeDtypeStruct((m, V7X_LANES), F32)]
        scratch += [pltpu.VMEM((1, ATTN_TILE, HEAD_DIM), F32), pltpu.VMEM((ATTN_TILE, V7X_LANES), F32)]
        vmem += (8 * _nbytes((attn_dil, ATTN_TILE // attn_dil, HEAD_DIM), BF16)
                 + 4 * _nbytes((ATTN_TILE, HEAD_DIM), BF16) + 4 * _nbytes((ATTN_TILE, V7X_LANES), F32)
                 + 8 * _nbytes((ATTN_TILE, BLOCK), F32))
    return pl.pallas_call(
        functools.partial(_mix_qkv_kernel, tn=tn, dil=dil, attn_dil=attn_dil,
                          tiles_per_batch=tiles_per_seq, tiles_per_attn_tile=tiles_per_attn_tile),
        grid=(m // tm, part_blocks),
        in_specs=in_specs,
        out_specs=out_specs,
        out_shape=out_shapes,
        scratch_shapes=scratch,
        compiler_params=_params(2, vmem),
        name=f"mix_qkv_{dil}",
    )(*args)


def _dilated_attn_kernel(q_ref, kp_ref, kc_ref, vp_ref, vc_ref, o_ref, lse_ref,
                         o_scr, lse_scr, *, dil):
    masks = _attn_masks(dil, pl.program_id(1) > 0)
    lane = lax.broadcasted_iota(jnp.int32, (MAX_DIL, BLOCK, V7X_LANES), 2)
    first_head = pl.program_id(2) * ATTN_HEADS_PER_STEP
    lse_tile = jnp.zeros((MAX_DIL, BLOCK, V7X_LANES), F32)
    for h in range(ATTN_HEADS_PER_STEP):
        sl = slice(h * HEAD_DIM, (h + 1) * HEAD_DIM)
        pp, pc, l, lse = _attn_probs(q_ref, kp_ref, kc_ref, sl, dil, masks)
        _attn_output(pp, pc, l, vp_ref, vc_ref, sl, dil, o_scr, h)
        lse_tile = jnp.where(lane == first_head + h, lse, lse_tile)
    for h in range(ATTN_HEADS_PER_STEP):
        o_ref[:, h * HEAD_DIM:(h + 1) * HEAD_DIM] = o_scr[h].astype(o_ref.dtype)
    _attn_store_lse(lse_tile, dil, lse_scr, lse_ref, pl.program_id(2) == 0)


def _dilated_attention(q, k, v, dil):
    b, _, sub_len, _ = q.shape
    seq = sub_len * dil
    rows = ATTN_TILE // dil
    tiles_per_seq = seq // ATTN_TILE
    width = ATTN_HEADS_PER_STEP * HEAD_DIM
    col_blocks = ATTN_WIDTH // width

    cur = pl.BlockSpec((None, dil, rows, width), lambda bi, t, hg: (bi, 0, t, hg))
    prev = pl.BlockSpec((None, dil, BLOCK, width),
                        lambda bi, t, hg: (bi, 0, jnp.maximum(t * (rows // BLOCK) - 1, 0), hg))

    vmem = (6 * _nbytes((dil, rows, width), BF16) + 4 * _nbytes((dil, BLOCK, width), BF16)
            + 2 * _nbytes((ATTN_TILE, width), BF16) + 3 * _nbytes((ATTN_TILE, V7X_LANES), F32)
            + _nbytes((ATTN_TILE, width), F32) + 12 * _nbytes((ATTN_TILE, BLOCK), F32))
    o, lse = pl.pallas_call(
        functools.partial(_dilated_attn_kernel, dil=dil),
        grid=(b, tiles_per_seq, col_blocks),
        in_specs=[cur, prev, cur, prev, cur],
        out_specs=[pl.BlockSpec((ATTN_TILE, width), lambda bi, t, hg: (bi * tiles_per_seq + t, hg)),
                   pl.BlockSpec((ATTN_TILE, V7X_LANES), lambda bi, t, hg: (bi * tiles_per_seq + t, 0))],
        out_shape=[jax.ShapeDtypeStruct((b * seq, ATTN_WIDTH), BF16),
                   jax.ShapeDtypeStruct((b * seq, V7X_LANES), F32)],
        scratch_shapes=[pltpu.VMEM((ATTN_HEADS_PER_STEP, ATTN_TILE, HEAD_DIM), F32),
                        pltpu.VMEM((ATTN_TILE, V7X_LANES), F32)],
        compiler_params=_params(3, vmem),
        name=f"dilated_attn_{dil}",
    )(q, k, k, v, v)
    return o, lse


def _combine_kernel(o0_ref, o1_ref, o2_ref, l0_ref, l1_ref, l2_ref, out_ref):
    l0, l1, l2 = l0_ref[...], l1_ref[...], l2_ref[...]
    m = jnp.maximum(jnp.maximum(l0, l1), l2)
    e0, e1, e2 = jnp.exp(l0 - m), jnp.exp(l1 - m), jnp.exp(l2 - m)
    den = e0 + e1 + e2
    w0, w1, w2 = e0 / den, e1 / den, e2 / den
    for h in range(N_ATTN_HEADS):
        sl = slice(h * HEAD_DIM, (h + 1) * HEAD_DIM)
        acc = (w0[:, h:h + 1] * o0_ref[:, sl].astype(F32)
               + w1[:, h:h + 1] * o1_ref[:, sl].astype(F32)
               + w2[:, h:h + 1] * o2_ref[:, sl].astype(F32))
        out_ref[:, sl] = acc.astype(out_ref.dtype)


def _combine(outs, lses, tq=512):
    m = outs[0].shape[0]
    o_spec = pl.BlockSpec((tq, ATTN_WIDTH), lambda i: (i, 0))
    l_spec = pl.BlockSpec((tq, V7X_LANES), lambda i: (i, 0))
    vmem = 8 * _nbytes((tq, ATTN_WIDTH), BF16) + 16 * _nbytes((tq, V7X_LANES), F32) + VMEM_ELEMENTWISE_TEMPORARIES
    return pl.pallas_call(
        _combine_kernel,
        grid=(m // tq,),
        in_specs=[o_spec] * 3 + [l_spec] * 3,
        out_specs=o_spec,
        out_shape=jax.ShapeDtypeStruct((m, ATTN_WIDTH), BF16),
        compiler_params=_params(1, vmem),
        name="attn_combine",
    )(*outs, *lses)


def _pool_kernel(zm_ref, zh_ref, wp_ref, scale_ref, o_ref, *, tp):
    t = pl.program_id(1)
    pos = t * tp + lax.broadcasted_iota(jnp.int32, (tp, 1), 0)
    for g, win in enumerate(POOL_WINDOWS):
        sl = slice(g * POOL_GROUP, (g + 1) * POOL_GROUP)
        main = zm_ref[:, sl]
        halo = jnp.where(t > 0, zh_ref[:, sl], 0.0)
        s = jnp.concatenate([halo, main], axis=0)
        shift = 1
        while shift < win:
            s = s + pltpu.roll(s, shift, axis=0)
            shift *= 2
        cnt = jnp.minimum(pos + 1, win).astype(F32)
        y = s[POOL_HALO:] / cnt - main
        yo = _dot(y.astype(BF16), wp_ref[g].astype(BF16))
        o_ref[:, sl] = (yo * scale_ref[:, sl]).astype(o_ref.dtype)


def _pool_mixer(zp, w_pool, pool_scale, batch, tp=512):
    m = zp.shape[0]
    seq = m // batch
    tiles_per_seq = seq // tp
    halo_blocks = tp // POOL_HALO
    vmem = (6 * _nbytes((tp, POOL_WIDTH), F32) + 2 * _nbytes(w_pool.shape, F32)
            + 12 * _nbytes((tp, POOL_GROUP), F32) + VMEM_ELEMENTWISE_TEMPORARIES)
    return pl.pallas_call(
        functools.partial(_pool_kernel, tp=tp),
        grid=(batch, tiles_per_seq),
        in_specs=[pl.BlockSpec((tp, POOL_WIDTH), lambda bi, t: (bi * tiles_per_seq + t, 0)),
                  pl.BlockSpec((POOL_HALO, POOL_WIDTH),
                               lambda bi, t: (jnp.maximum((bi * tiles_per_seq + t) * halo_blocks - 1, 0), 0)),
                  pl.BlockSpec(w_pool.shape, lambda bi, t: (0, 0, 0)),
                  pl.BlockSpec((1, POOL_WIDTH), lambda bi, t: (0, 0))],
        out_specs=pl.BlockSpec((tp, POOL_WIDTH), lambda bi, t: (bi * tiles_per_seq + t, 0)),
        out_shape=jax.ShapeDtypeStruct((m, POOL_WIDTH), BF16),
        compiler_params=_params(2, vmem),
        name="pool_mixer",
    )(zp, zp, w_pool, pool_scale.reshape(1, POOL_WIDTH))


def _mix_out_kernel(a_ref, p_ref, wa_ref, wp_ref, res_ref, gain_ref, o_ref, hg_ref, ssq_ref):
    _init_prenorm(ssq_ref)
    for sl in _column_halves(o_ref):
        acc = (_dot(a_ref[...], wa_ref[:, sl].astype(BF16))
               + _dot(p_ref[...], wp_ref[:, sl].astype(BF16)))
        out = res_ref[:, sl] + acc
        o_ref[:, sl] = out
        _emit_prenorm(out, gain_ref.at[:, sl], hg_ref.at[:, sl], ssq_ref)


def _mix_out(o_attn, o_pool, w, res, next_gain, tm=1024, tn=512):
    m = o_attn.shape[0]
    n = w.shape[1]
    ka, kp = o_attn.shape[1], o_pool.shape[1]
    assert ka == kp
    vmem = (2 * _nbytes((tm, ka + kp), BF16) + _weight_vmem((ka + kp, tn), w.dtype)
            + 8 * _nbytes((tm, tn), F32))
    norm_specs, norm_shapes = _prenorm_specs(m, n, tm, tn)
    return pl.pallas_call(
        _mix_out_kernel,
        grid=(m // tm, n // tn),
        in_specs=[pl.BlockSpec((tm, ka), lambda i, j: (i, 0)),
                  pl.BlockSpec((tm, kp), lambda i, j: (i, 0)),
                  pl.BlockSpec((ka, tn), lambda i, j: (0, j)),
                  pl.BlockSpec((kp, tn), lambda i, j: (1, j)),
                  pl.BlockSpec((tm, tn), lambda i, j: (i, j)),
                  pl.BlockSpec((1, tn), lambda i, j: (0, j))],
        out_specs=[pl.BlockSpec((tm, tn), lambda i, j: (i, j))] + norm_specs,
        out_shape=[jax.ShapeDtypeStruct((m, n), F32)] + norm_shapes,
        compiler_params=_params(2, vmem),
        name="mix_out",
    )(o_attn, o_pool, w, w, res, next_gain.reshape(1, n))


def _cast_kernel(x_ref, o_ref):
    o_ref[...] = x_ref[...].astype(o_ref.dtype)


def _cast_bf16(w, rows=512):
    k, n = w.shape
    rows = min(rows, k)
    return pl.pallas_call(
        _cast_kernel,
        grid=(k // rows,),
        in_specs=[pl.BlockSpec((rows, n), lambda i: (i, 0))],
        out_specs=pl.BlockSpec((rows, n), lambda i: (i, 0)),
        out_shape=jax.ShapeDtypeStruct((k, n), BF16),
        compiler_params=_params(1, 8 * _nbytes((rows, n), F32)),
        name="cast_bf16",
    )(w)


def _cross_kernel(hg_ref, ssq_ref, kv_ref, wq_ref, wo_ref, res_ref, gain_ref, o_ref, u_ref):
    scale = CROSS_DIM ** -0.5
    q = (_dot(hg_ref[...], wq_ref[...]) * _row_scale(ssq_ref, hg_ref.shape[1])).astype(BF16)
    heads = []
    for h in range(CROSS_HEADS):
        sl = slice(h * CROSS_DIM, (h + 1) * CROSS_DIM)
        k = kv_ref[:, sl]
        v = kv_ref[:, CROSS_WIDTH + h * CROSS_DIM:CROSS_WIDTH + (h + 1) * CROSS_DIM]
        s = _dot_nt(q[:, sl], k)
        m = jnp.max(s, axis=1, keepdims=True)
        p = jnp.exp2((s - m) * (scale * LOG2E))
        l = jnp.sum(p, axis=1, keepdims=True)
        heads.append((_dot(p.astype(BF16), v) / l).astype(BF16))
    out = res_ref[...] + _dot(jnp.concatenate(heads, axis=1), wo_ref[...])
    o_ref[...] = out
    ms = jnp.mean(out * out, axis=-1, keepdims=True)
    u_ref[...] = (out * lax.rsqrt(ms + EPS) * gain_ref[...]).astype(u_ref.dtype)


def _cross_attention(hg, ssq, kv, w_q, w_o, res, next_gain, batch, ts=256):
    m, d = res.shape
    mem_len = kv.shape[0] // batch
    steps_per_batch = (m // batch) // ts
    vmem = (2 * _nbytes((ts, d), BF16) * 2 + 4 * _nbytes((ts, d), F32) + _nbytes(w_q.shape, BF16)
            + _nbytes(w_o.shape, BF16) + 4 * _nbytes((ts, d), F32) + VMEM_ELEMENTWISE_TEMPORARIES)
    row_f32 = pl.BlockSpec((ts, d), lambda i: (i, 0))
    return pl.pallas_call(
        _cross_kernel,
        grid=(m // ts,),
        in_specs=[pl.BlockSpec((ts, d), lambda i: (i, 0)),
                  pl.BlockSpec((ts, V7X_LANES), lambda i: (i, 0)),
                  pl.BlockSpec((mem_len, 2 * CROSS_WIDTH), lambda i: (i // steps_per_batch, 0)),
                  _resident(w_q.shape, lambda i: (0, 0)),
                  _resident(w_o.shape, lambda i: (0, 0)),
                  row_f32,
                  pl.BlockSpec((1, d), lambda i: (0, 0))],
        out_specs=[row_f32, pl.BlockSpec((ts, d), lambda i: (i, 0))],
        out_shape=[jax.ShapeDtypeStruct((m, d), F32), jax.ShapeDtypeStruct((m, d), BF16)],
        compiler_params=_params(1, vmem),
        name="cross_attention",
    )(hg, ssq, kv, w_q, w_o, res, next_gain.reshape(1, d))


def _swiglu_block(h, u, w_in, w_out, next_gain=None, later_weights=()):
    gate, (w_out_bf16, *later_bf16) = _ffn_in(u, w_in, cast_jobs=(w_out,) + tuple(later_weights))
    return _proj_res(gate, w_out_bf16, h, 0.5, tm=512, tn=512, next_gain=next_gain), later_bf16


def kernel(x, mem, positions, g_ffn1, w_ffn1_in, w_ffn1_out, g_mix, w_mix_in, w_pool, pool_scale, w_mix_out, g_cross, g_mem, w_cross_q, w_cross_kv, w_cross_o, g_ffn2, w_ffn2_in, w_ffn2_out, g_final):
    batch, seq, d = x.shape
    m = batch * seq
    depth = g_ffn1.shape[0]
    cos, sin = _rope_tables(positions)
    h = x.reshape(m, d)
    mem2 = mem.reshape(batch * mem.shape[1], d)
    for l in range(depth):
        u = _rmsnorm(h, g_ffn1[l], BF16)
        (h, hg, ssq), (w_mix_in_bf16, w_mix_out_bf16, w_ffn2_in_bf16) = _swiglu_block(
            h, u, w_ffn1_in[l], w_ffn1_out[l], next_gain=g_mix[l],
            later_weights=(w_mix_in[l], w_mix_out[l], w_ffn2_in[l]))

        outs, lses = [], []
        qkv, qkv_dil = None, None
        for group, (_, dil) in enumerate(DILATED_CONFIGS):
            res = _mix_qkv(hg, ssq, w_mix_in_bf16, cos, sin, group, dil, batch,
                           attn_qkv=qkv, attn_dil=qkv_dil)
            if qkv is not None:
                outs.append(res[3])
                lses.append(res[4])
            qkv, qkv_dil = res[:3], dil
        o, lse = _dilated_attention(*qkv, qkv_dil)
        outs.append(o)
        lses.append(lse)
        o_attn = _combine(outs, lses)
        zp = _proj(hg, w_mix_in_bf16, tm=1024, tn=1024, out_dtype=F32, col0=ATTN_IN, n=POOL_WIDTH, ssq=ssq)
        o_pool = _pool_mixer(zp, w_pool[l], pool_scale[l], batch)
        h, hg, ssq = _mix_out(o_attn, o_pool, w_mix_out_bf16, h, g_cross[l])

        um = _rmsnorm(mem2, g_mem[l], BF16)
        kv = _proj(um, w_cross_kv[l], tm=um.shape[0], tn=CROSS_WIDTH)
        h, u = _cross_attention(hg, ssq, kv, _cast_bf16(w_cross_q[l]), _cast_bf16(w_cross_o[l]),
                                h, g_ffn2[l], batch)

        h, _ = _swiglu_block(h, u, w_ffn2_in_bf16, w_ffn2_out[l])
    return _rmsnorm(h, g_final, F32).reshape(batch, seq, d)
```

```python
import functools
import math

import jax
import jax.numpy as jnp
from jax import lax
from jax.experimental import pallas as pl
from jax.experimental.pallas import tpu as pltpu

F32 = jnp.float32
BF16 = jnp.bfloat16

D_MODEL = 4096
HEAD_DIM = 128
ATTN_WIDTH = D_MODEL // 2
N_ATTN_HEADS = ATTN_WIDTH // HEAD_DIM
DILATED_CONFIGS = ((128, 1), (512, 4), (2048, 16))
N_DIL = len(DILATED_CONFIGS)
BLOCK = 128
POOL_WIDTH = D_MODEL - ATTN_WIDTH
POOL_WINDOWS = (2, 4, 8, 16)
POOL_GROUP = POOL_WIDTH // len(POOL_WINDOWS)
QKV_WIDTH = 3 * ATTN_WIDTH
ATTN_IN = N_DIL * QKV_WIDTH
D_FF = ((8 * D_MODEL // 3 + 255) // 256) * 256
CROSS_HEADS = 4
CROSS_DIM = 128
CROSS_WIDTH = CROSS_HEADS * CROSS_DIM
ROPE_THETA = 10000.0
EPS = 1e-6
NEG_INF = -1e30
LOG2E = math.log2(math.e)

V7X_LANES = 128
V7X_BF16_SUBLANES = 16
V7X_VMEM_BYTES = 64 * 1024 * 1024
V7X_VMEM_REQUEST_CAP = V7X_VMEM_BYTES - 6 * 1024 * 1024
VMEM_ELEMENTWISE_TEMPORARIES = 8 * 1024 * 1024

POOL_HALO = max(POOL_WINDOWS)
MAX_DIL = max(d for _, d in DILATED_CONFIGS)
ATTN_TILE = BLOCK * MAX_DIL
ATTN_HEADS_PER_STEP = 4


def _params(n_grid, vmem_bytes):
    return pltpu.CompilerParams(
        dimension_semantics=("arbitrary",) * n_grid,
        vmem_limit_bytes=int(min(vmem_bytes, V7X_VMEM_REQUEST_CAP)),
    )


def _nbytes(shape, dtype):
    return math.prod(shape) * jnp.dtype(dtype).itemsize


def _weight_vmem(block_shape, dtype):
    cast_copy = 0 if dtype == BF16 else _nbytes(block_shape, BF16)
    return 2 * _nbytes(block_shape, dtype) + cast_copy


def _resident(block_shape, index_map):
    return pl.BlockSpec(block_shape, index_map, pipeline_mode=pl.Buffered(1))


def _dot(a, b):
    return jnp.dot(a, b, preferred_element_type=F32)


def _dot_nt(a, b):
    return lax.dot_general(a, b, (((1,), (1,)), ((), ())), preferred_element_type=F32)


def _rmsnorm_kernel(x_ref, g_ref, o_ref):
    x = x_ref[...]
    ms = jnp.mean(x * x, axis=-1, keepdims=True)
    o_ref[...] = (x * lax.rsqrt(ms + EPS) * g_ref[...]).astype(o_ref.dtype)


def _rmsnorm(x, g, out_dtype, tm=256):
    m, d = x.shape
    vmem = 2 * tm * d * (4 + jnp.dtype(out_dtype).itemsize) + 4 * tm * d * 4
    return pl.pallas_call(
        _rmsnorm_kernel,
        grid=(m // tm,),
        in_specs=[pl.BlockSpec((tm, d), lambda i: (i, 0)),
                  pl.BlockSpec((1, d), lambda i: (0, 0))],
        out_specs=pl.BlockSpec((tm, d), lambda i: (i, 0)),
        out_shape=jax.ShapeDtypeStruct((m, d), out_dtype),
        compiler_params=_params(1, vmem),
        name="rmsnorm",
    )(x, g.reshape(1, d))


def _rope_kernel(pos_ref, inv_ref, sign_ref, cos_ref, sin_ref):
    ang = pos_ref[...].astype(F32) * inv_ref[...]
    cos_ref[...] = jnp.cos(ang)
    sin_ref[...] = jnp.sin(ang) * sign_ref[...]


def _rope_tables(positions, tm=1024):
    m = positions.size
    half = HEAD_DIM // 2
    inv = 1.0 / (ROPE_THETA ** (jnp.arange(0, HEAD_DIM, 2, dtype=F32) / HEAD_DIM))
    inv = jnp.concatenate([inv, inv]).reshape(1, HEAD_DIM)
    sign = jnp.concatenate([-jnp.ones((half,), F32), jnp.ones((half,), F32)]).reshape(1, HEAD_DIM)
    row = pl.BlockSpec((1, HEAD_DIM), lambda i: (0, 0))
    tab = pl.BlockSpec((tm, HEAD_DIM), lambda i: (i, 0))
    return pl.pallas_call(
        _rope_kernel,
        grid=(m // tm,),
        in_specs=[pl.BlockSpec((tm, 1), lambda i: (i, 0)), row, row],
        out_specs=[tab, tab],
        out_shape=[jax.ShapeDtypeStruct((m, HEAD_DIM), F32)] * 2,
        compiler_params=_params(1, 16 * tm * HEAD_DIM * 4),
        name="rope_tables",
    )(positions.reshape(m, 1), inv, sign)


def _ffn_in_kernel(u_ref, wa_ref, wb_ref, *rest):
    n_jobs = (len(rest) - 1) // 2
    srcs, o_ref, dsts = rest[:n_jobs], rest[n_jobs], rest[n_jobs + 1:]
    half = o_ref.shape[1] // 2
    for c in range(2):
        sl = slice(c * half, (c + 1) * half)
        w = jnp.concatenate([wa_ref[:, sl].astype(BF16), wb_ref[:, sl].astype(BF16)], axis=1)
        ab = _dot(u_ref[...], w)
        a, b = ab[:, :half], ab[:, half:]
        o_ref[:, sl] = (a * jax.nn.sigmoid(a) * b).astype(o_ref.dtype)
    for src, dst in zip(srcs, dsts):
        dst[...] = src[...].astype(dst.dtype)


def _ffn_in(u, w_in, cast_jobs=(), tm=2048, tf=256):
    m, d = u.shape
    nf = D_FF // tf
    n_steps = (m // tm) * nf
    lhs_spec, lhs_buffers = (pl.BlockSpec, 2) if w_in.dtype == BF16 else (_resident, 1)
    vmem = (lhs_buffers * _nbytes((tm, d), BF16) + 2 * _weight_vmem((d, tf), w_in.dtype)
            + 2 * _nbytes((tm, tf), BF16) + 6 * _nbytes((tm, tf), F32))
    job_specs, job_shapes = [], []
    for w in cast_jobs:
        rows = next(r for r in range(V7X_BF16_SUBLANES, w.shape[0] + 1, V7X_BF16_SUBLANES)
                    if w.shape[0] % r == 0 and w.shape[0] // r <= n_steps)
        n_blocks = w.shape[0] // rows
        job_specs.append(pl.BlockSpec(
            (rows, w.shape[1]), lambda i, j, n_blocks=n_blocks: (jnp.minimum(i * nf + j, n_blocks - 1), 0)))
        job_shapes.append(jax.ShapeDtypeStruct(w.shape, BF16))
        vmem += 2 * _nbytes((rows, w.shape[1]), F32) + 2 * _nbytes((rows, w.shape[1]), BF16)
    outs = pl.pallas_call(
        _ffn_in_kernel,
        grid=(m // tm, nf),
        in_specs=[lhs_spec((tm, d), lambda i, j: (i, 0)),
                  pl.BlockSpec((d, tf), lambda i, j: (0, j)),
                  pl.BlockSpec((d, tf), lambda i, j: (0, j + nf))] + job_specs,
        out_specs=[pl.BlockSpec((tm, tf), lambda i, j: (i, j))] + job_specs,
        out_shape=[jax.ShapeDtypeStruct((m, D_FF), BF16)] + job_shapes,
        compiler_params=_params(2, vmem),
        name="ffn_in",
    )(u, w_in, w_in, *cast_jobs)
    return outs[0], outs[1:]


def _emit_prenorm(out, gain_ref, hg_ref, ssq_ref):
    hg_ref[...] = (out * gain_ref[...]).astype(hg_ref.dtype)
    ssq_ref[...] += jnp.broadcast_to(jnp.sum(out * out, axis=1, keepdims=True), ssq_ref.shape)


def _init_prenorm(ssq_ref):
    @pl.when(pl.program_id(1) == 0)
    def _():
        ssq_ref[...] = jnp.zeros_like(ssq_ref)


def _row_scale(ssq_ref, d):
    return lax.rsqrt(ssq_ref[:, :1] * (1.0 / d) + EPS)


def _prenorm_specs(m, n, tm, tn):
    specs = [pl.BlockSpec((tm, tn), lambda i, j: (i, j)), pl.BlockSpec((tm, V7X_LANES), lambda i, j: (i, 0))]
    shapes = [jax.ShapeDtypeStruct((m, n), BF16), jax.ShapeDtypeStruct((m, V7X_LANES), F32)]
    return specs, shapes


def _column_halves(ref):
    half = ref.shape[1] // 2
    return [slice(0, half), slice(half, 2 * half)]


def _proj_res_kernel(lhs_ref, w_ref, res_ref, *rest, scale):
    prenorm = len(rest) > 1
    if prenorm:
        gain_ref, o_ref, hg_ref, ssq_ref = rest
        _init_prenorm(ssq_ref)
    else:
        o_ref, = rest
    for sl in _column_halves(o_ref):
        acc = _dot(lhs_ref[...], w_ref[:, sl].astype(BF16))
        out = res_ref[:, sl] + scale * acc
        o_ref[:, sl] = out
        if prenorm:
            _emit_prenorm(out, gain_ref.at[:, sl], hg_ref.at[:, sl], ssq_ref)


def _proj_res(lhs, w, res, scale, tm, tn, next_gain=None):
    m, k = lhs.shape
    n = w.shape[1]
    vmem = 2 * _nbytes((tm, k), BF16) + _weight_vmem((k, tn), w.dtype) + 8 * _nbytes((tm, tn), F32)
    in_specs = [pl.BlockSpec((tm, k), lambda i, j: (i, 0)),
                pl.BlockSpec((k, tn), lambda i, j: (0, j)),
                pl.BlockSpec((tm, tn), lambda i, j: (i, j))]
    out_specs = [pl.BlockSpec((tm, tn), lambda i, j: (i, j))]
    out_shape = [jax.ShapeDtypeStruct((m, n), F32)]
    args = [lhs, w, res]
    if next_gain is not None:
        in_specs.append(pl.BlockSpec((1, tn), lambda i, j: (0, j)))
        args.append(next_gain.reshape(1, n))
        specs, shapes = _prenorm_specs(m, n, tm, tn)
        out_specs += specs
        out_shape += shapes
    outs = pl.pallas_call(
        functools.partial(_proj_res_kernel, scale=scale),
        grid=(m // tm, n // tn),
        in_specs=in_specs,
        out_specs=out_specs,
        out_shape=out_shape,
        compiler_params=_params(2, vmem),
        name="proj_residual",
    )(*args)
    return outs[0] if next_gain is None else outs


def _proj_kernel(lhs_ref, w_ref, *rest):
    acc = _dot(lhs_ref[...], w_ref[...].astype(BF16))
    if len(rest) == 2:
        acc = acc * _row_scale(rest[0], lhs_ref.shape[1])
    rest[-1][...] = acc.astype(rest[-1].dtype)


def _proj(lhs, w, tm, tn, out_dtype=BF16, col0=0, n=None, ssq=None):
    m, k = lhs.shape
    n = w.shape[1] if n is None else n
    col_blk0 = col0 // tn
    vmem = 2 * _nbytes((tm, k), BF16) + _weight_vmem((k, tn), w.dtype) + 6 * _nbytes((tm, tn), F32)
    in_specs = [pl.BlockSpec((tm, k), lambda i, j: (i, 0)),
                pl.BlockSpec((k, tn), lambda i, j: (0, j + col_blk0))]
    args = [lhs, w]
    if ssq is not None:
        in_specs.append(pl.BlockSpec((tm, V7X_LANES), lambda i, j: (i, 0)))
        args.append(ssq)
    return pl.pallas_call(
        _proj_kernel,
        grid=(m // tm, n // tn),
        in_specs=in_specs,
        out_specs=pl.BlockSpec((tm, tn), lambda i, j: (i, j)),
        out_shape=jax.ShapeDtypeStruct((m, n), out_dtype),
        compiler_params=_params(2, vmem),
        name="proj",
    )(*args)


def _attn_blocks(ref, sl):
    return ref[:, :, sl].reshape(MAX_DIL, BLOCK, HEAD_DIM)


def _attn_prev_blocks(first_ref, cur, sl, dil):
    nq = MAX_DIL // dil
    first = first_ref[:, :, sl]
    if nq == 1:
        return first
    cur4 = cur.reshape(dil, nq, BLOCK, HEAD_DIM)
    return jnp.concatenate([first[:, None], cur4[:, :-1]], axis=1).reshape(MAX_DIL, BLOCK, HEAD_DIM)


def _attn_token_rows(g, dil):
    r, i = divmod(g, MAX_DIL // dil)
    return pl.ds(i * BLOCK * dil + r, BLOCK, stride=dil)


def _attn_masks(dil, tile_has_prev):
    shape = (MAX_DIL, BLOCK, BLOCK)
    blk = lax.broadcasted_iota(jnp.int32, shape, 0)
    qi = lax.broadcasted_iota(jnp.int32, shape, 1)
    kj = lax.broadcasted_iota(jnp.int32, shape, 2)
    has_prev = jnp.logical_or(tile_has_prev, blk % (MAX_DIL // dil) != 0)
    return jnp.logical_and(kj >= qi, has_prev), kj <= qi


def _attn_probs(q_ref, kp_ref, kc_ref, sl, dil, masks):
    mask_prev, mask_cur = masks
    scale = HEAD_DIM ** -0.5
    q = _attn_blocks(q_ref, sl)
    kc = _attn_blocks(kc_ref, sl)
    kp = _attn_prev_blocks(kp_ref, kc, sl, dil)
    sp = jnp.einsum("gqd,gkd->gqk", q, kp, preferred_element_type=F32)
    sc = jnp.einsum("gqd,gkd->gqk", q, kc, preferred_element_type=F32)
    sp = jnp.where(mask_prev, sp, NEG_INF)
    sc = jnp.where(mask_cur, sc, NEG_INF)
    m = jnp.maximum(jnp.max(sp, axis=2, keepdims=True), jnp.max(sc, axis=2, keepdims=True))
    pp = jnp.exp2((sp - m) * (scale * LOG2E))
    pc = jnp.exp2((sc - m) * (scale * LOG2E))
    l = jnp.sum(pp, axis=2, keepdims=True) + jnp.sum(pc, axis=2, keepdims=True)
    return pp.astype(BF16), pc.astype(BF16), l, m * scale + jnp.log(l)


def _attn_output(pp, pc, l, vp_ref, vc_ref, sl, dil, o_scr, slot):
    vc = _attn_blocks(vc_ref, sl)
    vp = _attn_prev_blocks(vp_ref, vc, sl, dil)
    o = (jnp.einsum("gqk,gkd->gqd", pp, vp, preferred_element_type=F32)
         + jnp.einsum("gqk,gkd->gqd", pc, vc, preferred_element_type=F32)) / l
    for g in range(MAX_DIL):
        o_scr[slot, _attn_token_rows(g, dil), :] = o[g]


def _attn_store_lse(lse_tile, dil, lse_scr, lse_ref, first):
    for g in range(MAX_DIL):
        lse_scr[_attn_token_rows(g, dil), :] = lse_tile[g]

    @pl.when(first)
    def _():
        lse_ref[...] = lse_scr[...]

    @pl.when(jnp.logical_not(first))
    def _():
        lse_ref[...] += lse_scr[...]


def _mix_qkv_kernel(u_ref, ssq_ref, wq_ref, wk_ref, wv_ref, cos_ref, sin_ref, *rest, tn, dil,
                    attn_dil, tiles_per_batch, tiles_per_attn_tile, attn_heads):
    if attn_dil is None:
        q_ref, k_ref, v_ref, scr_ref, tab_ref = rest
    else:
        (aq_ref, akp_ref, akc_ref, avp_ref, avc_ref, q_ref, k_ref, v_ref, ao_ref, alse_ref,
         scr_ref, tab_ref, ao_scr, alse_scr) = rest
        tile_has_prev, first_head = _attn_host_unit(tiles_per_batch, tiles_per_attn_tile, attn_heads)
    tm, d_in = u_ref.shape
    rows = tm // dil

    def regroup(slab, val):
        if dil == 1:
            return [val]
        scr_ref[slab] = val
        return [scr_ref[slab, pl.ds(r, rows, stride=dil), :] for r in range(dil)]

    @pl.when(pl.program_id(1) == 0)
    def _():
        rs = _row_scale(ssq_ref, d_in)
        tables = (cos_ref[...] * rs, sin_ref[...] * rs, jnp.broadcast_to(rs, (tm, HEAD_DIM)))
        for t, table in enumerate(tables):
            for r, piece in enumerate(regroup(0, table)):
                tab_ref[t, r] = piece

    heads = tn // HEAD_DIM
    for part, (w_ref, o_ref) in enumerate(((wq_ref, q_ref), (wk_ref, k_ref), (wv_ref, v_ref))):
        acc = _dot(u_ref[...], w_ref[...].astype(BF16))
        if part == 0 and attn_dil is not None:
            probs = _attn_host_probs(aq_ref, akp_ref, akc_ref, attn_dil, tile_has_prev, attn_heads)
        for h in range(heads):
            pieces = regroup(part * heads + h, acc[:, h * HEAD_DIM:(h + 1) * HEAD_DIM])
            for r, piece in enumerate(pieces):
                if o_ref is v_ref:
                    piece = piece * tab_ref[2, r]
                else:
                    piece = piece * tab_ref[0, r] + pltpu.roll(piece, HEAD_DIM // 2, axis=1) * tab_ref[1, r]
                o_ref[r, :, h * HEAD_DIM:(h + 1) * HEAD_DIM] = piece.astype(o_ref.dtype)

    if attn_dil is not None:
        _attn_host_finish(probs, first_head, avp_ref, avc_ref, attn_dil, ao_ref, alse_ref,
                          ao_scr, alse_scr)


def _attn_unit_specs(dil, unit, heads):
    rows = ATTN_TILE // dil

    def cur_map(*idx):
        b, t, h = unit(*idx)
        return b, 0, t, h

    def prev_map(*idx):
        b, t, h = unit(*idx)
        return b, 0, jnp.maximum(t * (rows // BLOCK) - 1, 0), h

    cur = pl.BlockSpec((None, dil, rows, heads * HEAD_DIM), cur_map)
    prev = pl.BlockSpec((None, dil, BLOCK, heads * HEAD_DIM), prev_map)
    return [cur, prev, cur, prev, cur]


def _mix_qkv(u, ssq, w, cos, sin, group, dil, batch, attn_qkv=None, attn_dil=None, tm=1024, tn=256):
    m, k = u.shape
    seq = m // batch
    tiles_per_seq = seq // tm
    part_blocks = ATTN_WIDTH // tn

    def w_spec(part):
        col_blk0 = (group * 3 + part) * part_blocks
        return pl.BlockSpec((k, tn), lambda i, j: (0, j + col_blk0))

    out_spec = pl.BlockSpec((None, dil, tm // dil, tn),
                            lambda i, j: (i // tiles_per_seq, 0, i % tiles_per_seq, j))
    out_shape = jax.ShapeDtypeStruct((batch, dil, seq // dil, ATTN_WIDTH), BF16)
    vmem = (2 * _nbytes((tm, k), BF16) + 3 * _weight_vmem((k, tn), w.dtype)
            + 3 * 6 * _nbytes((tm, tn), F32) + 10 * _nbytes((tm, HEAD_DIM), F32))
    in_specs = [pl.BlockSpec((tm, k), lambda i, j: (i, 0)),
                pl.BlockSpec((tm, V7X_LANES), lambda i, j: (i, 0)),
                w_spec(0), w_spec(1), w_spec(2),
                pl.BlockSpec((tm, HEAD_DIM), lambda i, j: (i, 0)),
                pl.BlockSpec((tm, HEAD_DIM), lambda i, j: (i, 0))]
    out_specs, out_shapes = [out_spec] * 3, [out_shape] * 3
    scratch = [pltpu.VMEM((3 * tn // HEAD_DIM, tm, HEAD_DIM), F32),
               pltpu.VMEM((3, dil, tm // dil, HEAD_DIM), F32)]
    args = [u, ssq, w, w, w, cos, sin]
    host = _attn_host(attn_qkv, attn_dil, m, seq, tm, part_blocks)
    if attn_qkv is not None:
        in_specs += host["in_specs"]
        args += host["args"]
        out_specs += host["out_specs"]
        out_shapes += host["out_shapes"]
        scratch += host["scratch"]
        vmem += host["vmem"]
    return pl.pallas_call(
        functools.partial(_mix_qkv_kernel, tn=tn, dil=dil, attn_dil=attn_dil, **host["kwargs"]),
        grid=(m // tm, part_blocks),
        in_specs=in_specs,
        out_specs=out_specs,
        out_shape=out_shapes,
        scratch_shapes=scratch,
        compiler_params=_params(2, vmem),
        name=f"mix_qkv_{dil}",
    )(*args)


def _attn_host(attn_qkv, attn_dil, m, seq, tm, col_steps):
    tiles_per_seq = seq // tm
    tiles_per_attn_tile = ATTN_TILE // tm
    heads = N_ATTN_HEADS // (tiles_per_attn_tile * col_steps)
    kwargs = dict(tiles_per_batch=tiles_per_seq, tiles_per_attn_tile=tiles_per_attn_tile, attn_heads=heads)
    if attn_qkv is None:
        return dict(kwargs=kwargs)
    assert heads * tiles_per_attn_tile * col_steps == N_ATTN_HEADS and seq % ATTN_TILE == 0
    attn_tiles_per_seq = seq // ATTN_TILE

    def unit(i, j):
        in_seq = i % tiles_per_seq
        return (i // tiles_per_seq, in_seq // tiles_per_attn_tile,
                (in_seq % tiles_per_attn_tile) * col_steps + j)

    def row_tile(i, j):
        b, t, _ = unit(i, j)
        return b * attn_tiles_per_seq + t

    aq, ak, av = attn_qkv
    return dict(
        kwargs=kwargs,
        in_specs=_attn_unit_specs(attn_dil, unit, heads),
        args=[aq, ak, ak, av, av],
        out_specs=[pl.BlockSpec((ATTN_TILE, heads * HEAD_DIM), lambda i, j: (row_tile(i, j), unit(i, j)[2])),
                   pl.BlockSpec((ATTN_TILE, V7X_LANES), lambda i, j: (row_tile(i, j), 0))],
        out_shapes=[jax.ShapeDtypeStruct((m, ATTN_WIDTH), BF16),
                    jax.ShapeDtypeStruct((m, V7X_LANES), F32)],
        scratch=[pltpu.VMEM((heads, ATTN_TILE, HEAD_DIM), F32), pltpu.VMEM((ATTN_TILE, V7X_LANES), F32)],
        vmem=heads * (8 * _nbytes((attn_dil, ATTN_TILE // attn_dil, HEAD_DIM), BF16)
                      + 4 * _nbytes((ATTN_TILE, HEAD_DIM), BF16) + 8 * _nbytes((ATTN_TILE, BLOCK), F32))
        + 4 * _nbytes((ATTN_TILE, V7X_LANES), F32))


def _attn_host_unit(tiles_per_batch, tiles_per_attn_tile, attn_heads):
    tile_has_prev = (pl.program_id(0) % tiles_per_batch) >= tiles_per_attn_tile
    group = (pl.program_id(0) % tiles_per_attn_tile) * pl.num_programs(1) + pl.program_id(1)
    return tile_has_prev, group * attn_heads


def _attn_host_probs(aq_ref, akp_ref, akc_ref, attn_dil, tile_has_prev, attn_heads):
    masks = _attn_masks(attn_dil, tile_has_prev)
    return [_attn_probs(aq_ref, akp_ref, akc_ref, slice(h * HEAD_DIM, (h + 1) * HEAD_DIM), attn_dil, masks)
            for h in range(attn_heads)]


def _attn_host_finish(probs, first_head, avp_ref, avc_ref, attn_dil, ao_ref, alse_ref, ao_scr, alse_scr):
    lane = lax.broadcasted_iota(jnp.int32, (MAX_DIL, BLOCK, V7X_LANES), 2)
    lse_tile = jnp.zeros((MAX_DIL, BLOCK, V7X_LANES), F32)
    for h, (pp, pc, l, lse) in enumerate(probs):
        sl = slice(h * HEAD_DIM, (h + 1) * HEAD_DIM)
        _attn_output(pp, pc, l, avp_ref, avc_ref, sl, attn_dil, ao_scr, h)
        ao_ref[:, sl] = ao_scr[h].astype(ao_ref.dtype)
        lse_tile = jnp.where(lane == first_head + h, lse, lse_tile)
    _attn_store_lse(lse_tile, attn_dil, alse_scr, alse_ref, first_head == 0)


def _pool_proj_kernel(u_ref, ssq_ref, w_ref, aq_ref, akp_ref, akc_ref, avp_ref, avc_ref,
                      z_ref, ao_ref, alse_ref, ao_scr, alse_scr, *, attn_dil, tiles_per_batch,
                      tiles_per_attn_tile, attn_heads):
    tile_has_prev, first_head = _attn_host_unit(tiles_per_batch, tiles_per_attn_tile, attn_heads)
    rs = _row_scale(ssq_ref, u_ref.shape[1])
    probs = None
    for sl in _column_halves(z_ref):
        z_ref[:, sl] = _dot(u_ref[...], w_ref[:, sl].astype(BF16)) * rs
        if probs is None:
            probs = _attn_host_probs(aq_ref, akp_ref, akc_ref, attn_dil, tile_has_prev, attn_heads)
    _attn_host_finish(probs, first_head, avp_ref, avc_ref, attn_dil, ao_ref, alse_ref, ao_scr, alse_scr)


def _pool_proj_attn(u, ssq, w, col0, attn_qkv, attn_dil, batch, tm=1024, tn=512):
    m, k = u.shape
    seq = m // batch
    col_steps = POOL_WIDTH // tn
    col_blk0 = col0 // tn
    host = _attn_host(attn_qkv, attn_dil, m, seq, tm, col_steps)
    vmem = (2 * _nbytes((tm, k), BF16) + _weight_vmem((k, tn), w.dtype) + 6 * _nbytes((tm, tn), F32)
            + host["vmem"])
    return pl.pallas_call(
        functools.partial(_pool_proj_kernel, attn_dil=attn_dil, **host["kwargs"]),
        grid=(m // tm, col_steps),
        in_specs=[pl.BlockSpec((tm, k), lambda i, j: (i, 0)),
                  pl.BlockSpec((tm, V7X_LANES), lambda i, j: (i, 0)),
                  pl.BlockSpec((k, tn), lambda i, j: (0, j + col_blk0))] + host["in_specs"],
        out_specs=[pl.BlockSpec((tm, tn), lambda i, j: (i, j))] + host["out_specs"],
        out_shape=[jax.ShapeDtypeStruct((m, POOL_WIDTH), F32)] + host["out_shapes"],
        scratch_shapes=host["scratch"],
        compiler_params=_params(2, vmem),
        name="pool_proj_attn",
    )(u, ssq, w, *host["args"])


def _dilated_attn_kernel(q_ref, kp_ref, kc_ref, vp_ref, vc_ref, o_ref, lse_ref,
                         o_scr, lse_scr, *, dil):
    masks = _attn_masks(dil, pl.program_id(1) > 0)
    lane = lax.broadcasted_iota(jnp.int32, (MAX_DIL, BLOCK, V7X_LANES), 2)
    first_head = pl.program_id(2) * ATTN_HEADS_PER_STEP
    lse_tile = jnp.zeros((MAX_DIL, BLOCK, V7X_LANES), F32)
    for h in range(ATTN_HEADS_PER_STEP):
        sl = slice(h * HEAD_DIM, (h + 1) * HEAD_DIM)
        pp, pc, l, lse = _attn_probs(q_ref, kp_ref, kc_ref, sl, dil, masks)
        _attn_output(pp, pc, l, vp_ref, vc_ref, sl, dil, o_scr, h)
        lse_tile = jnp.where(lane == first_head + h, lse, lse_tile)
    for h in range(ATTN_HEADS_PER_STEP):
        o_ref[:, h * HEAD_DIM:(h + 1) * HEAD_DIM] = o_scr[h].astype(o_ref.dtype)
    _attn_store_lse(lse_tile, dil, lse_scr, lse_ref, pl.program_id(2) == 0)


def _dilated_attention(q, k, v, dil):
    b, _, sub_len, _ = q.shape
    seq = sub_len * dil
    rows = ATTN_TILE // dil
    tiles_per_seq = seq // ATTN_TILE
    width = ATTN_HEADS_PER_STEP * HEAD_DIM
    col_blocks = ATTN_WIDTH // width

    cur = pl.BlockSpec((None, dil, rows, width), lambda bi, t, hg: (bi, 0, t, hg))
    prev = pl.BlockSpec((None, dil, BLOCK, width),
                        lambda bi, t, hg: (bi, 0, jnp.maximum(t * (rows // BLOCK) - 1, 0), hg))

    vmem = (6 * _nbytes((dil, rows, width), BF16) + 4 * _nbytes((dil, BLOCK, width), BF16)
            + 2 * _nbytes((ATTN_TILE, width), BF16) + 3 * _nbytes((ATTN_TILE, V7X_LANES), F32)
            + _nbytes((ATTN_TILE, width), F32) + 12 * _nbytes((ATTN_TILE, BLOCK), F32))
    o, lse = pl.pallas_call(
        functools.partial(_dilated_attn_kernel, dil=dil),
        grid=(b, tiles_per_seq, col_blocks),
        in_specs=[cur, prev, cur, prev, cur],
        out_specs=[pl.BlockSpec((ATTN_TILE, width), lambda bi, t, hg: (bi * tiles_per_seq + t, hg)),
                   pl.BlockSpec((ATTN_TILE, V7X_LANES), lambda bi, t, hg: (bi * tiles_per_seq + t, 0))],
        out_shape=[jax.ShapeDtypeStruct((b * seq, ATTN_WIDTH), BF16),
                   jax.ShapeDtypeStruct((b * seq, V7X_LANES), F32)],
        scratch_shapes=[pltpu.VMEM((ATTN_HEADS_PER_STEP, ATTN_TILE, HEAD_DIM), F32),
                        pltpu.VMEM((ATTN_TILE, V7X_LANES), F32)],
        compiler_params=_params(3, vmem),
        name=f"dilated_attn_{dil}",
    )(q, k, k, v, v)
    return o, lse


def _combine_kernel(o0_ref, o1_ref, o2_ref, l0_ref, l1_ref, l2_ref, out_ref):
    l0, l1, l2 = l0_ref[...], l1_ref[...], l2_ref[...]
    m = jnp.maximum(jnp.maximum(l0, l1), l2)
    e0, e1, e2 = jnp.exp(l0 - m), jnp.exp(l1 - m), jnp.exp(l2 - m)
    den = e0 + e1 + e2
    w0, w1, w2 = e0 / den, e1 / den, e2 / den
    for h in range(N_ATTN_HEADS):
        sl = slice(h * HEAD_DIM, (h + 1) * HEAD_DIM)
        acc = (w0[:, h:h + 1] * o0_ref[:, sl].astype(F32)
               + w1[:, h:h + 1] * o1_ref[:, sl].astype(F32)
               + w2[:, h:h + 1] * o2_ref[:, sl].astype(F32))
        out_ref[:, sl] = acc.astype(out_ref.dtype)


def _combine(outs, lses, tq=512):
    m = outs[0].shape[0]
    o_spec = pl.BlockSpec((tq, ATTN_WIDTH), lambda i: (i, 0))
    l_spec = pl.BlockSpec((tq, V7X_LANES), lambda i: (i, 0))
    vmem = 8 * _nbytes((tq, ATTN_WIDTH), BF16) + 16 * _nbytes((tq, V7X_LANES), F32) + VMEM_ELEMENTWISE_TEMPORARIES
    return pl.pallas_call(
        _combine_kernel,
        grid=(m // tq,),
        in_specs=[o_spec] * 3 + [l_spec] * 3,
        out_specs=o_spec,
        out_shape=jax.ShapeDtypeStruct((m, ATTN_WIDTH), BF16),
        compiler_params=_params(1, vmem),
        name="attn_combine",
    )(*outs, *lses)


def _pool_kernel(zm_ref, zh_ref, wp_ref, scale_ref, o_ref, *, tp):
    t = pl.program_id(1)
    pos = t * tp + lax.broadcasted_iota(jnp.int32, (tp, 1), 0)
    for g, win in enumerate(POOL_WINDOWS):
        sl = slice(g * POOL_GROUP, (g + 1) * POOL_GROUP)
        main = zm_ref[:, sl]
        halo = jnp.where(t > 0, zh_ref[:, sl], 0.0)
        s = jnp.concatenate([halo, main], axis=0)
        shift = 1
        while shift < win:
            s = s + pltpu.roll(s, shift, axis=0)
            shift *= 2
        cnt = jnp.minimum(pos + 1, win).astype(F32)
        y = s[POOL_HALO:] / cnt - main
        yo = _dot(y.astype(BF16), wp_ref[g].astype(BF16))
        o_ref[:, sl] = (yo * scale_ref[:, sl]).astype(o_ref.dtype)


def _pool_mixer(zp, w_pool, pool_scale, batch, tp=512):
    m = zp.shape[0]
    seq = m // batch
    tiles_per_seq = seq // tp
    halo_blocks = tp // POOL_HALO
    vmem = (6 * _nbytes((tp, POOL_WIDTH), F32) + 2 * _nbytes(w_pool.shape, F32)
            + 12 * _nbytes((tp, POOL_GROUP), F32) + VMEM_ELEMENTWISE_TEMPORARIES)
    return pl.pallas_call(
        functools.partial(_pool_kernel, tp=tp),
        grid=(batch, tiles_per_seq),
        in_specs=[pl.BlockSpec((tp, POOL_WIDTH), lambda bi, t: (bi * tiles_per_seq + t, 0)),
                  pl.BlockSpec((POOL_HALO, POOL_WIDTH),
                               lambda bi, t: (jnp.maximum((bi * tiles_per_seq + t) * halo_blocks - 1, 0), 0)),
                  pl.BlockSpec(w_pool.shape, lambda bi, t: (0, 0, 0)),
                  pl.BlockSpec((1, POOL_WIDTH), lambda bi, t: (0, 0))],
        out_specs=pl.BlockSpec((tp, POOL_WIDTH), lambda bi, t: (bi * tiles_per_seq + t, 0)),
        out_shape=jax.ShapeDtypeStruct((m, POOL_WIDTH), BF16),
        compiler_params=_params(2, vmem),
        name="pool_mixer",
    )(zp, zp, w_pool, pool_scale.reshape(1, POOL_WIDTH))


def _mix_out_kernel(a_ref, p_ref, wa_ref, wp_ref, res_ref, gain_ref, o_ref, hg_ref, ssq_ref):
    _init_prenorm(ssq_ref)
    for sl in _column_halves(o_ref):
        acc = (_dot(a_ref[...], wa_ref[:, sl].astype(BF16))
               + _dot(p_ref[...], wp_ref[:, sl].astype(BF16)))
        out = res_ref[:, sl] + acc
        o_ref[:, sl] = out
        _emit_prenorm(out, gain_ref.at[:, sl], hg_ref.at[:, sl], ssq_ref)


def _mix_out(o_attn, o_pool, w, res, next_gain, tm=1024, tn=512):
    m = o_attn.shape[0]
    n = w.shape[1]
    ka, kp = o_attn.shape[1], o_pool.shape[1]
    assert ka == kp
    vmem = (2 * _nbytes((tm, ka + kp), BF16) + _weight_vmem((ka + kp, tn), w.dtype)
            + 8 * _nbytes((tm, tn), F32))
    norm_specs, norm_shapes = _prenorm_specs(m, n, tm, tn)
    return pl.pallas_call(
        _mix_out_kernel,
        grid=(m // tm, n // tn),
        in_specs=[pl.BlockSpec((tm, ka), lambda i, j: (i, 0)),
                  pl.BlockSpec((tm, kp), lambda i, j: (i, 0)),
                  pl.BlockSpec((ka, tn), lambda i, j: (0, j)),
                  pl.BlockSpec((kp, tn), lambda i, j: (1, j)),
                  pl.BlockSpec((tm, tn), lambda i, j: (i, j)),
                  pl.BlockSpec((1, tn), lambda i, j: (0, j))],
        out_specs=[pl.BlockSpec((tm, tn), lambda i, j: (i, j))] + norm_specs,
        out_shape=[jax.ShapeDtypeStruct((m, n), F32)] + norm_shapes,
        compiler_params=_params(2, vmem),
        name="mix_out",
    )(o_attn, o_pool, w, w, res, next_gain.reshape(1, n))


def _cast_kernel(x_ref, o_ref):
    o_ref[...] = x_ref[...].astype(o_ref.dtype)


def _cast_bf16(w, rows=512):
    k, n = w.shape
    rows = min(rows, k)
    return pl.pallas_call(
        _cast_kernel,
        grid=(k // rows,),
        in_specs=[pl.BlockSpec((rows, n), lambda i: (i, 0))],
        out_specs=pl.BlockSpec((rows, n), lambda i: (i, 0)),
        out_shape=jax.ShapeDtypeStruct((k, n), BF16),
        compiler_params=_params(1, 8 * _nbytes((rows, n), F32)),
        name="cast_bf16",
    )(w)


def _cross_kernel(hg_ref, ssq_ref, kv_ref, wq_ref, wo_ref, res_ref, gain_ref, o_ref, u_ref):
    scale = CROSS_DIM ** -0.5
    q = (_dot(hg_ref[...], wq_ref[...]) * _row_scale(ssq_ref, hg_ref.shape[1])).astype(BF16)
    heads = []
    for h in range(CROSS_HEADS):
        sl = slice(h * CROSS_DIM, (h + 1) * CROSS_DIM)
        k = kv_ref[:, sl]
        v = kv_ref[:, CROSS_WIDTH + h * CROSS_DIM:CROSS_WIDTH + (h + 1) * CROSS_DIM]
        s = _dot_nt(q[:, sl], k)
        m = jnp.max(s, axis=1, keepdims=True)
        p = jnp.exp2((s - m) * (scale * LOG2E))
        l = jnp.sum(p, axis=1, keepdims=True)
        heads.append((_dot(p.astype(BF16), v) / l).astype(BF16))
    out = res_ref[...] + _dot(jnp.concatenate(heads, axis=1), wo_ref[...])
    o_ref[...] = out
    ms = jnp.mean(out * out, axis=-1, keepdims=True)
    u_ref[...] = (out * lax.rsqrt(ms + EPS) * gain_ref[...]).astype(u_ref.dtype)


def _cross_attention(hg, ssq, kv, w_q, w_o, res, next_gain, batch, ts=256):
    m, d = res.shape
    mem_len = kv.shape[0] // batch
    steps_per_batch = (m // batch) // ts
    vmem = (2 * _nbytes((ts, d), BF16) * 2 + 4 * _nbytes((ts, d), F32) + _nbytes(w_q.shape, BF16)
            + _nbytes(w_o.shape, BF16) + 4 * _nbytes((ts, d), F32) + VMEM_ELEMENTWISE_TEMPORARIES)
    row_f32 = pl.BlockSpec((ts, d), lambda i: (i, 0))
    return pl.pallas_call(
        _cross_kernel,
        grid=(m // ts,),
        in_specs=[pl.BlockSpec((ts, d), lambda i: (i, 0)),
                  pl.BlockSpec((ts, V7X_LANES), lambda i: (i, 0)),
                  pl.BlockSpec((mem_len, 2 * CROSS_WIDTH), lambda i: (i // steps_per_batch, 0)),
                  _resident(w_q.shape, lambda i: (0, 0)),
                  _resident(w_o.shape, lambda i: (0, 0)),
                  row_f32,
                  pl.BlockSpec((1, d), lambda i: (0, 0))],
        out_specs=[row_f32, pl.BlockSpec((ts, d), lambda i: (i, 0))],
        out_shape=[jax.ShapeDtypeStruct((m, d), F32), jax.ShapeDtypeStruct((m, d), BF16)],
        compiler_params=_params(1, vmem),
        name="cross_attention",
    )(hg, ssq, kv, w_q, w_o, res, next_gain.reshape(1, d))


def _swiglu_block(h, u, w_in, w_out, next_gain=None, later_weights=()):
    gate, (w_out_bf16, *later_bf16) = _ffn_in(u, w_in, cast_jobs=(w_out,) + tuple(later_weights))
    return _proj_res(gate, w_out_bf16, h, 0.5, tm=512, tn=512, next_gain=next_gain), later_bf16


def kernel(x, mem, positions, g_ffn1, w_ffn1_in, w_ffn1_out, g_mix, w_mix_in, w_pool, pool_scale, w_mix_out, g_cross, g_mem, w_cross_q, w_cross_kv, w_cross_o, g_ffn2, w_ffn2_in, w_ffn2_out, g_final):
    batch, seq, d = x.shape
    m = batch * seq
    depth = g_ffn1.shape[0]
    cos, sin = _rope_tables(positions)
    h = x.reshape(m, d)
    mem2 = mem.reshape(batch * mem.shape[1], d)
    for l in range(depth):
        u = _rmsnorm(h, g_ffn1[l], BF16)
        (h, hg, ssq), (w_mix_in_bf16, w_mix_out_bf16, w_ffn2_in_bf16) = _swiglu_block(
            h, u, w_ffn1_in[l], w_ffn1_out[l], next_gain=g_mix[l],
            later_weights=(w_mix_in[l], w_mix_out[l], w_ffn2_in[l]))

        outs, lses = [], []
        qkv, qkv_dil = None, None
        for group, (_, dil) in enumerate(DILATED_CONFIGS):
            res = _mix_qkv(hg, ssq, w_mix_in_bf16, cos, sin, group, dil, batch,
                           attn_qkv=qkv, attn_dil=qkv_dil)
            if qkv is not None:
                outs.append(res[3])
                lses.append(res[4])
            qkv, qkv_dil = res[:3], dil
        zp, o, lse = _pool_proj_attn(hg, ssq, w_mix_in_bf16, ATTN_IN, qkv, qkv_dil, batch)
        outs.append(o)
        lses.append(lse)
        o_attn = _combine(outs, lses)
        o_pool = _pool_mixer(zp, w_pool[l], pool_scale[l], batch)
        h, hg, ssq = _mix_out(o_attn, o_pool, w_mix_out_bf16, h, g_cross[l])

        um = _rmsnorm(mem2, g_mem[l], BF16)
        kv = _proj(um, w_cross_kv[l], tm=um.shape[0], tn=CROSS_WIDTH)
        h, u = _cross_attention(hg, ssq, kv, _cast_bf16(w_cross_q[l]), _cast_bf16(w_cross_o[l]),
                                h, g_ffn2[l], batch)

        h, _ = _swiglu_block(h, u, w_ffn2_in_bf16, w_ffn2_out[l])
    return _rmsnorm(h, g_final, F32).reshape(batch, seq, d)
```

```python
import functools
import math

import jax
import jax.numpy as jnp
from jax import lax
from jax.experimental import pallas as pl
from jax.experimental.pallas import tpu as pltpu

F32 = jnp.float32
BF16 = jnp.bfloat16

D_MODEL = 4096
HEAD_DIM = 128
ATTN_WIDTH = D_MODEL // 2
N_ATTN_HEADS = ATTN_WIDTH // HEAD_DIM
DILATED_CONFIGS = ((128, 1), (512, 4), (2048, 16))
N_DIL = len(DILATED_CONFIGS)
BLOCK = 128
POOL_WIDTH = D_MODEL - ATTN_WIDTH
POOL_WINDOWS = (2, 4, 8, 16)
POOL_GROUP = POOL_WIDTH // len(POOL_WINDOWS)
QKV_WIDTH = 3 * ATTN_WIDTH
ATTN_IN = N_DIL * QKV_WIDTH
D_FF = ((8 * D_MODEL // 3 + 255) // 256) * 256
CROSS_HEADS = 4
CROSS_DIM = 128
CROSS_WIDTH = CROSS_HEADS * CROSS_DIM
ROPE_THETA = 10000.0
EPS = 1e-6
NEG_INF = -1e30
LOG2E = math.log2(math.e)

V7X_LANES = 128
V7X_BF16_SUBLANES = 16
V7X_VMEM_BYTES = 64 * 1024 * 1024
V7X_VMEM_REQUEST_CAP = V7X_VMEM_BYTES - 6 * 1024 * 1024
VMEM_ELEMENTWISE_TEMPORARIES = 8 * 1024 * 1024

POOL_HALO = max(POOL_WINDOWS)
MAX_DIL = max(d for _, d in DILATED_CONFIGS)
ATTN_TILE = BLOCK * MAX_DIL


def _params(n_grid, vmem_bytes):
    return pltpu.CompilerParams(
        dimension_semantics=("arbitrary",) * n_grid,
        vmem_limit_bytes=int(min(vmem_bytes, V7X_VMEM_REQUEST_CAP)),
    )


def _nbytes(shape, dtype):
    return math.prod(shape) * jnp.dtype(dtype).itemsize


def _weight_vmem(block_shape, dtype):
    cast_copy = 0 if dtype == BF16 else _nbytes(block_shape, BF16)
    return 2 * _nbytes(block_shape, dtype) + cast_copy


def _resident(block_shape, index_map):
    return pl.BlockSpec(block_shape, index_map, pipeline_mode=pl.Buffered(1))


def _dot(a, b):
    return jnp.dot(a, b, preferred_element_type=F32)


def _dot_nt(a, b):
    return lax.dot_general(a, b, (((1,), (1,)), ((), ())), preferred_element_type=F32)


def _rmsnorm_kernel(x_ref, g_ref, o_ref):
    x = x_ref[...]
    ms = jnp.mean(x * x, axis=-1, keepdims=True)
    o_ref[...] = (x * lax.rsqrt(ms + EPS) * g_ref[...]).astype(o_ref.dtype)


def _rmsnorm(x, g, out_dtype, tm=256):
    m, d = x.shape
    vmem = 2 * tm * d * (4 + jnp.dtype(out_dtype).itemsize) + 4 * tm * d * 4
    return pl.pallas_call(
        _rmsnorm_kernel,
        grid=(m // tm,),
        in_specs=[pl.BlockSpec((tm, d), lambda i: (i, 0)),
                  pl.BlockSpec((1, d), lambda i: (0, 0))],
        out_specs=pl.BlockSpec((tm, d), lambda i: (i, 0)),
        out_shape=jax.ShapeDtypeStruct((m, d), out_dtype),
        compiler_params=_params(1, vmem),
        name="rmsnorm",
    )(x, g.reshape(1, d))


def _rope_kernel(pos_ref, inv_ref, sign_ref, cos_ref, sin_ref):
    ang = pos_ref[...].astype(F32) * inv_ref[...]
    cos_ref[...] = jnp.cos(ang)
    sin_ref[...] = jnp.sin(ang) * sign_ref[...]


def _rope_tables(positions, tm=1024):
    m = positions.size
    half = HEAD_DIM // 2
    inv = 1.0 / (ROPE_THETA ** (jnp.arange(0, HEAD_DIM, 2, dtype=F32) / HEAD_DIM))
    inv = jnp.concatenate([inv, inv]).reshape(1, HEAD_DIM)
    sign = jnp.concatenate([-jnp.ones((half,), F32), jnp.ones((half,), F32)]).reshape(1, HEAD_DIM)
    row = pl.BlockSpec((1, HEAD_DIM), lambda i: (0, 0))
    tab = pl.BlockSpec((tm, HEAD_DIM), lambda i: (i, 0))
    return pl.pallas_call(
        _rope_kernel,
        grid=(m // tm,),
        in_specs=[pl.BlockSpec((tm, 1), lambda i: (i, 0)), row, row],
        out_specs=[tab, tab],
        out_shape=[jax.ShapeDtypeStruct((m, HEAD_DIM), F32)] * 2,
        compiler_params=_params(1, 16 * tm * HEAD_DIM * 4),
        name="rope_tables",
    )(positions.reshape(m, 1), inv, sign)


def _ffn_in_kernel(u_ref, wa_ref, wb_ref, *rest):
    n_jobs = (len(rest) - 1) // 2
    srcs, o_ref, dsts = rest[:n_jobs], rest[n_jobs], rest[n_jobs + 1:]
    half = o_ref.shape[1] // 2
    for c in range(2):
        sl = slice(c * half, (c + 1) * half)
        w = jnp.concatenate([wa_ref[:, sl].astype(BF16), wb_ref[:, sl].astype(BF16)], axis=1)
        ab = _dot(u_ref[...], w)
        a, b = ab[:, :half], ab[:, half:]
        o_ref[:, sl] = (a * jax.nn.sigmoid(a) * b).astype(o_ref.dtype)
    for src, dst in zip(srcs, dsts):
        dst[...] = src[...].astype(dst.dtype)


def _ffn_in(u, w_in, cast_jobs=(), tm=2048, tf=256):
    m, d = u.shape
    nf = D_FF // tf
    n_steps = (m // tm) * nf
    lhs_spec, lhs_buffers = (pl.BlockSpec, 2) if w_in.dtype == BF16 else (_resident, 1)
    vmem = (lhs_buffers * _nbytes((tm, d), BF16) + 2 * _weight_vmem((d, tf), w_in.dtype)
            + 2 * _nbytes((tm, tf), BF16) + 6 * _nbytes((tm, tf), F32))
    job_specs, job_shapes = [], []
    for w in cast_jobs:
        rows = next(r for r in range(V7X_BF16_SUBLANES, w.shape[0] + 1, V7X_BF16_SUBLANES)
                    if w.shape[0] % r == 0 and w.shape[0] // r <= n_steps)
        n_blocks = w.shape[0] // rows
        job_specs.append(pl.BlockSpec(
            (rows, w.shape[1]), lambda i, j, n_blocks=n_blocks: (jnp.minimum(i * nf + j, n_blocks - 1), 0)))
        job_shapes.append(jax.ShapeDtypeStruct(w.shape, BF16))
        vmem += 2 * _nbytes((rows, w.shape[1]), F32) + 2 * _nbytes((rows, w.shape[1]), BF16)
    outs = pl.pallas_call(
        _ffn_in_kernel,
        grid=(m // tm, nf),
        in_specs=[lhs_spec((tm, d), lambda i, j: (i, 0)),
                  pl.BlockSpec((d, tf), lambda i, j: (0, j)),
                  pl.BlockSpec((d, tf), lambda i, j: (0, j + nf))] + job_specs,
        out_specs=[pl.BlockSpec((tm, tf), lambda i, j: (i, j))] + job_specs,
        out_shape=[jax.ShapeDtypeStruct((m, D_FF), BF16)] + job_shapes,
        compiler_params=_params(2, vmem),
        name="ffn_in",
    )(u, w_in, w_in, *cast_jobs)
    return outs[0], outs[1:]


def _emit_prenorm(out, gain_ref, hg_ref, ssq_ref):
    hg_ref[...] = (out * gain_ref[...]).astype(hg_ref.dtype)
    ssq_ref[...] += jnp.broadcast_to(jnp.sum(out * out, axis=1, keepdims=True), ssq_ref.shape)


def _init_prenorm(ssq_ref):
    @pl.when(pl.program_id(1) == 0)
    def _():
        ssq_ref[...] = jnp.zeros_like(ssq_ref)


def _row_scale(ssq_ref, d):
    return lax.rsqrt(ssq_ref[:, :1] * (1.0 / d) + EPS)


def _prenorm_specs(m, n, tm, tn):
    specs = [pl.BlockSpec((tm, tn), lambda i, j: (i, j)), pl.BlockSpec((tm, V7X_LANES), lambda i, j: (i, 0))]
    shapes = [jax.ShapeDtypeStruct((m, n), BF16), jax.ShapeDtypeStruct((m, V7X_LANES), F32)]
    return specs, shapes


def _column_halves(ref):
    half = ref.shape[1] // 2
    return [slice(0, half), slice(half, 2 * half)]


def _proj_res_kernel(lhs_ref, w_ref, res_ref, *rest, scale):
    prenorm = len(rest) > 1
    if prenorm:
        gain_ref, o_ref, hg_ref, ssq_ref = rest
        _init_prenorm(ssq_ref)
    else:
        o_ref, = rest
    for sl in _column_halves(o_ref):
        acc = _dot(lhs_ref[...], w_ref[:, sl].astype(BF16))
        out = res_ref[:, sl] + scale * acc
        o_ref[:, sl] = out
        if prenorm:
            _emit_prenorm(out, gain_ref.at[:, sl], hg_ref.at[:, sl], ssq_ref)


def _proj_res(lhs, w, res, scale, tm, tn, next_gain=None):
    m, k = lhs.shape
    n = w.shape[1]
    vmem = 2 * _nbytes((tm, k), BF16) + _weight_vmem((k, tn), w.dtype) + 8 * _nbytes((tm, tn), F32)
    in_specs = [pl.BlockSpec((tm, k), lambda i, j: (i, 0)),
                pl.BlockSpec((k, tn), lambda i, j: (0, j)),
                pl.BlockSpec((tm, tn), lambda i, j: (i, j))]
    out_specs = [pl.BlockSpec((tm, tn), lambda i, j: (i, j))]
    out_shape = [jax.ShapeDtypeStruct((m, n), F32)]
    args = [lhs, w, res]
    if next_gain is not None:
        in_specs.append(pl.BlockSpec((1, tn), lambda i, j: (0, j)))
        args.append(next_gain.reshape(1, n))
        specs, shapes = _prenorm_specs(m, n, tm, tn)
        out_specs += specs
        out_shape += shapes
    outs = pl.pallas_call(
        functools.partial(_proj_res_kernel, scale=scale),
        grid=(m // tm, n // tn),
        in_specs=in_specs,
        out_specs=out_specs,
        out_shape=out_shape,
        compiler_params=_params(2, vmem),
        name="proj_residual",
    )(*args)
    return outs[0] if next_gain is None else outs


def _proj_kernel(lhs_ref, w_ref, *rest):
    acc = _dot(lhs_ref[...], w_ref[...].astype(BF16))
    if len(rest) == 2:
        acc = acc * _row_scale(rest[0], lhs_ref.shape[1])
    rest[-1][...] = acc.astype(rest[-1].dtype)


def _proj(lhs, w, tm, tn, out_dtype=BF16, col0=0, n=None, ssq=None):
    m, k = lhs.shape
    n = w.shape[1] if n is None else n
    col_blk0 = col0 // tn
    vmem = 2 * _nbytes((tm, k), BF16) + _weight_vmem((k, tn), w.dtype) + 6 * _nbytes((tm, tn), F32)
    in_specs = [pl.BlockSpec((tm, k), lambda i, j: (i, 0)),
                pl.BlockSpec((k, tn), lambda i, j: (0, j + col_blk0))]
    args = [lhs, w]
    if ssq is not None:
        in_specs.append(pl.BlockSpec((tm, V7X_LANES), lambda i, j: (i, 0)))
        args.append(ssq)
    return pl.pallas_call(
        _proj_kernel,
        grid=(m // tm, n // tn),
        in_specs=in_specs,
        out_specs=pl.BlockSpec((tm, tn), lambda i, j: (i, j)),
        out_shape=jax.ShapeDtypeStruct((m, n), out_dtype),
        compiler_params=_params(2, vmem),
        name="proj",
    )(*args)


def _attn_blocks(ref, sl):
    return ref[:, :, sl].reshape(MAX_DIL, BLOCK, HEAD_DIM)


def _attn_prev_blocks(first_ref, cur, sl, dil):
    nq = MAX_DIL // dil
    first = first_ref[:, :, sl]
    if nq == 1:
        return first
    cur4 = cur.reshape(dil, nq, BLOCK, HEAD_DIM)
    return jnp.concatenate([first[:, None], cur4[:, :-1]], axis=1).reshape(MAX_DIL, BLOCK, HEAD_DIM)


def _attn_token_rows(g, dil):
    r, i = divmod(g, MAX_DIL // dil)
    return pl.ds(i * BLOCK * dil + r, BLOCK, stride=dil)


def _attn_masks(dil, tile_has_prev):
    shape = (MAX_DIL, BLOCK, BLOCK)
    blk = lax.broadcasted_iota(jnp.int32, shape, 0)
    qi = lax.broadcasted_iota(jnp.int32, shape, 1)
    kj = lax.broadcasted_iota(jnp.int32, shape, 2)
    has_prev = jnp.logical_or(tile_has_prev, blk % (MAX_DIL // dil) != 0)
    return jnp.logical_and(kj >= qi, has_prev), kj <= qi


def _attn_probs(q_ref, kp_ref, kc_ref, sl, dil, masks):
    mask_prev, mask_cur = masks
    scale = HEAD_DIM ** -0.5
    q = _attn_blocks(q_ref, sl)
    kc = _attn_blocks(kc_ref, sl)
    kp = _attn_prev_blocks(kp_ref, kc, sl, dil)
    sp = jnp.einsum("gqd,gkd->gqk", q, kp, preferred_element_type=F32)
    sc = jnp.einsum("gqd,gkd->gqk", q, kc, preferred_element_type=F32)
    sp = jnp.where(mask_prev, sp, NEG_INF)
    sc = jnp.where(mask_cur, sc, NEG_INF)
    m = jnp.maximum(jnp.max(sp, axis=2, keepdims=True), jnp.max(sc, axis=2, keepdims=True))
    pp = jnp.exp2((sp - m) * (scale * LOG2E))
    pc = jnp.exp2((sc - m) * (scale * LOG2E))
    l = jnp.sum(pp, axis=2, keepdims=True) + jnp.sum(pc, axis=2, keepdims=True)
    return pp.astype(BF16), pc.astype(BF16), l, m * scale + jnp.log(l)


def _attn_output(pp, pc, l, vp_ref, vc_ref, sl, dil, o_scr, slot):
    vc = _attn_blocks(vc_ref, sl)
    vp = _attn_prev_blocks(vp_ref, vc, sl, dil)
    o = (jnp.einsum("gqk,gkd->gqd", pp, vp, preferred_element_type=F32)
         + jnp.einsum("gqk,gkd->gqd", pc, vc, preferred_element_type=F32)) / l
    for g in range(MAX_DIL):
        o_scr[slot, _attn_token_rows(g, dil), :] = o[g]


def _attn_scatter_lse(lse_tile, dil, lse_scr):
    for g in range(MAX_DIL):
        lse_scr[_attn_token_rows(g, dil), :] = lse_tile[g]


def _attn_accumulate_lse(lse_scr, lse_ref, first):
    @pl.when(first)
    def _():
        lse_ref[...] = lse_scr[...]

    @pl.when(jnp.logical_not(first))
    def _():
        lse_ref[...] += lse_scr[...]


def _mix_qkv_kernel(u_ref, ssq_ref, wq_ref, wk_ref, wv_ref, cos_ref, sin_ref, *rest, tn, dil,
                    attn_dil, tiles_per_batch, tiles_per_attn_tile, attn_heads):
    if attn_dil is None:
        q_ref, k_ref, v_ref, scr_ref, tab_ref = rest
    else:
        (aq_ref, akp_ref, akc_ref, avp_ref, avc_ref, q_ref, k_ref, v_ref, ao_ref, alse_ref,
         scr_ref, tab_ref, ao_scr, alse_scr) = rest
        tile_has_prev, first_head = _attn_host_unit(tiles_per_batch, tiles_per_attn_tile, attn_heads)
    tm, d_in = u_ref.shape
    rows = tm // dil

    def regroup(slab, val):
        if dil == 1:
            return [val]
        scr_ref[slab] = val
        return [scr_ref[slab, pl.ds(r, rows, stride=dil), :] for r in range(dil)]

    @pl.when(pl.program_id(1) == 0)
    def _():
        rs = _row_scale(ssq_ref, d_in)
        tables = (cos_ref[...] * rs, sin_ref[...] * rs, jnp.broadcast_to(rs, (tm, HEAD_DIM)))
        for t, table in enumerate(tables):
            for r, piece in enumerate(regroup(0, table)):
                tab_ref[t, r] = piece

    heads = tn // HEAD_DIM
    for part, (w_ref, o_ref) in enumerate(((wq_ref, q_ref), (wk_ref, k_ref), (wv_ref, v_ref))):
        acc = _dot(u_ref[...], w_ref[...].astype(BF16))
        if part == 0 and attn_dil is not None:
            probs = _attn_host_probs(aq_ref, akp_ref, akc_ref, attn_dil, tile_has_prev, attn_heads)
        for h in range(heads):
            pieces = regroup(part * heads + h, acc[:, h * HEAD_DIM:(h + 1) * HEAD_DIM])
            for r, piece in enumerate(pieces):
                if o_ref is v_ref:
                    piece = piece * tab_ref[2, r]
                else:
                    piece = piece * tab_ref[0, r] + pltpu.roll(piece, HEAD_DIM // 2, axis=1) * tab_ref[1, r]
                o_ref[r, :, h * HEAD_DIM:(h + 1) * HEAD_DIM] = piece.astype(o_ref.dtype)

    if attn_dil is not None:
        _attn_host_outputs(probs, first_head, avp_ref, avc_ref, attn_dil, ao_ref, ao_scr, alse_scr)
        _attn_accumulate_lse(alse_scr, alse_ref, first_head == 0)


def _attn_unit_specs(dil, unit, heads):
    rows = ATTN_TILE // dil

    def cur_map(*idx):
        b, t, h = unit(*idx)
        return b, 0, t, h

    def prev_map(*idx):
        b, t, h = unit(*idx)
        return b, 0, jnp.maximum(t * (rows // BLOCK) - 1, 0), h

    cur = pl.BlockSpec((None, dil, rows, heads * HEAD_DIM), cur_map)
    prev = pl.BlockSpec((None, dil, BLOCK, heads * HEAD_DIM), prev_map)
    return [cur, prev, cur, prev, cur]


def _mix_qkv(u, ssq, w, cos, sin, group, dil, batch, attn_qkv=None, attn_dil=None, tm=1024, tn=256):
    m, k = u.shape
    seq = m // batch
    tiles_per_seq = seq // tm
    part_blocks = ATTN_WIDTH // tn

    def w_spec(part):
        col_blk0 = (group * 3 + part) * part_blocks
        return pl.BlockSpec((k, tn), lambda i, j: (0, j + col_blk0))

    out_spec = pl.BlockSpec((None, dil, tm // dil, tn),
                            lambda i, j: (i // tiles_per_seq, 0, i % tiles_per_seq, j))
    out_shape = jax.ShapeDtypeStruct((batch, dil, seq // dil, ATTN_WIDTH), BF16)
    vmem = (2 * _nbytes((tm, k), BF16) + 3 * _weight_vmem((k, tn), w.dtype)
            + 3 * 6 * _nbytes((tm, tn), F32) + 10 * _nbytes((tm, HEAD_DIM), F32))
    in_specs = [pl.BlockSpec((tm, k), lambda i, j: (i, 0)),
                pl.BlockSpec((tm, V7X_LANES), lambda i, j: (i, 0)),
                w_spec(0), w_spec(1), w_spec(2),
                pl.BlockSpec((tm, HEAD_DIM), lambda i, j: (i, 0)),
                pl.BlockSpec((tm, HEAD_DIM), lambda i, j: (i, 0))]
    out_specs, out_shapes = [out_spec] * 3, [out_shape] * 3
    scratch = [pltpu.VMEM((3 * tn // HEAD_DIM, tm, HEAD_DIM), F32),
               pltpu.VMEM((3, dil, tm // dil, HEAD_DIM), F32)]
    args = [u, ssq, w, w, w, cos, sin]
    host = _attn_host(attn_qkv, attn_dil, m, seq, tm, part_blocks)
    if attn_qkv is not None:
        in_specs += host["in_specs"]
        args += host["args"]
        out_specs += host["out_specs"]
        out_shapes += host["out_shapes"]
        scratch += host["scratch"]
        vmem += host["vmem"]
    return pl.pallas_call(
        functools.partial(_mix_qkv_kernel, tn=tn, dil=dil, attn_dil=attn_dil, **host["kwargs"]),
        grid=(m // tm, part_blocks),
        in_specs=in_specs,
        out_specs=out_specs,
        out_shape=out_shapes,
        scratch_shapes=scratch,
        compiler_params=_params(2, vmem),
        name=f"mix_qkv_{dil}",
    )(*args)


def _attn_host(attn_qkv, attn_dil, m, seq, tm, col_steps):
    tiles_per_seq = seq // tm
    tiles_per_attn_tile = ATTN_TILE // tm
    heads = N_ATTN_HEADS // (tiles_per_attn_tile * col_steps)
    kwargs = dict(tiles_per_batch=tiles_per_seq, tiles_per_attn_tile=tiles_per_attn_tile, attn_heads=heads)
    if attn_qkv is None:
        return dict(kwargs=kwargs)
    assert heads * tiles_per_attn_tile * col_steps == N_ATTN_HEADS and seq % ATTN_TILE == 0
    attn_tiles_per_seq = seq // ATTN_TILE

    def unit(i, j):
        in_seq = i % tiles_per_seq
        return (i // tiles_per_seq, in_seq // tiles_per_attn_tile,
                (in_seq % tiles_per_attn_tile) * col_steps + j)

    def row_tile(i, j):
        b, t, _ = unit(i, j)
        return b * attn_tiles_per_seq + t

    aq, ak, av = attn_qkv
    return dict(
        kwargs=kwargs,
        in_specs=_attn_unit_specs(attn_dil, unit, heads),
        args=[aq, ak, ak, av, av],
        out_specs=[pl.BlockSpec((ATTN_TILE, heads * HEAD_DIM), lambda i, j: (row_tile(i, j), unit(i, j)[2])),
                   pl.BlockSpec((ATTN_TILE, V7X_LANES), lambda i, j: (row_tile(i, j), 0))],
        out_shapes=[jax.ShapeDtypeStruct((m, ATTN_WIDTH), BF16),
                    jax.ShapeDtypeStruct((m, V7X_LANES), F32)],
        scratch=[pltpu.VMEM((heads, ATTN_TILE, HEAD_DIM), F32), pltpu.VMEM((ATTN_TILE, V7X_LANES), F32)],
        vmem=heads * (8 * _nbytes((attn_dil, ATTN_TILE // attn_dil, HEAD_DIM), BF16)
                      + 4 * _nbytes((ATTN_TILE, HEAD_DIM), BF16) + 8 * _nbytes((ATTN_TILE, BLOCK), F32))
        + 4 * _nbytes((ATTN_TILE, V7X_LANES), F32))


def _attn_host_unit(tiles_per_batch, tiles_per_attn_tile, attn_heads):
    tile_has_prev = (pl.program_id(0) % tiles_per_batch) >= tiles_per_attn_tile
    group = (pl.program_id(0) % tiles_per_attn_tile) * pl.num_programs(1) + pl.program_id(1)
    return tile_has_prev, group * attn_heads


def _attn_host_probs(aq_ref, akp_ref, akc_ref, attn_dil, tile_has_prev, attn_heads):
    masks = _attn_masks(attn_dil, tile_has_prev)
    return [_attn_probs(aq_ref, akp_ref, akc_ref, slice(h * HEAD_DIM, (h + 1) * HEAD_DIM), attn_dil, masks)
            for h in range(attn_heads)]


def _attn_host_outputs(probs, first_head, avp_ref, avc_ref, attn_dil, ao_ref, ao_scr, alse_scr):
    lane = lax.broadcasted_iota(jnp.int32, (MAX_DIL, BLOCK, V7X_LANES), 2)
    lse_tile = jnp.zeros((MAX_DIL, BLOCK, V7X_LANES), F32)
    for h, (pp, pc, l, lse) in enumerate(probs):
        sl = slice(h * HEAD_DIM, (h + 1) * HEAD_DIM)
        _attn_output(pp, pc, l, avp_ref, avc_ref, sl, attn_dil, ao_scr, h)
        ao_ref[:, sl] = ao_scr[h].astype(ao_ref.dtype)
        lse_tile = jnp.where(lane == first_head + h, lse, lse_tile)
    _attn_scatter_lse(lse_tile, attn_dil, alse_scr)


def _pool_proj_kernel(u_ref, ssq_ref, w_ref, aq_ref, akp_ref, akc_ref, avp_ref, avc_ref,
                      z_ref, ao_ref, alse_ref, ao_scr, alse_scr, *, attn_dil, tiles_per_batch,
                      tiles_per_attn_tile, attn_heads):
    tile_has_prev, first_head = _attn_host_unit(tiles_per_batch, tiles_per_attn_tile, attn_heads)
    rs = _row_scale(ssq_ref, u_ref.shape[1])
    probs = None
    for sl in _column_halves(z_ref):
        z_ref[:, sl] = _dot(u_ref[...], w_ref[:, sl].astype(BF16)) * rs
        if probs is None:
            probs = _attn_host_probs(aq_ref, akp_ref, akc_ref, attn_dil, tile_has_prev, attn_heads)
    _attn_host_outputs(probs, first_head, avp_ref, avc_ref, attn_dil, ao_ref, ao_scr, alse_scr)
    _attn_accumulate_lse(alse_scr, alse_ref, first_head == 0)


def _pool_proj_attn(u, ssq, w, col0, attn_qkv, attn_dil, batch, tm=1024, tn=512):
    m, k = u.shape
    seq = m // batch
    col_steps = POOL_WIDTH // tn
    col_blk0 = col0 // tn
    host = _attn_host(attn_qkv, attn_dil, m, seq, tm, col_steps)
    vmem = (2 * _nbytes((tm, k), BF16) + _weight_vmem((k, tn), w.dtype) + 6 * _nbytes((tm, tn), F32)
            + host["vmem"])
    return pl.pallas_call(
        functools.partial(_pool_proj_kernel, attn_dil=attn_dil, **host["kwargs"]),
        grid=(m // tm, col_steps),
        in_specs=[pl.BlockSpec((tm, k), lambda i, j: (i, 0)),
                  pl.BlockSpec((tm, V7X_LANES), lambda i, j: (i, 0)),
                  pl.BlockSpec((k, tn), lambda i, j: (0, j + col_blk0))] + host["in_specs"],
        out_specs=[pl.BlockSpec((tm, tn), lambda i, j: (i, j))] + host["out_specs"],
        out_shape=[jax.ShapeDtypeStruct((m, POOL_WIDTH), F32)] + host["out_shapes"],
        scratch_shapes=host["scratch"],
        compiler_params=_params(2, vmem),
        name="pool_proj_attn",
    )(u, ssq, w, *host["args"])


def _combine_kernel(o0_ref, o1_ref, o2_ref, l0_ref, l1_ref, l2_ref, out_ref):
    l0, l1, l2 = l0_ref[...], l1_ref[...], l2_ref[...]
    m = jnp.maximum(jnp.maximum(l0, l1), l2)
    e0, e1, e2 = jnp.exp(l0 - m), jnp.exp(l1 - m), jnp.exp(l2 - m)
    den = e0 + e1 + e2
    w0, w1, w2 = e0 / den, e1 / den, e2 / den
    for h in range(N_ATTN_HEADS):
        sl = slice(h * HEAD_DIM, (h + 1) * HEAD_DIM)
        acc = (w0[:, h:h + 1] * o0_ref[:, sl].astype(F32)
               + w1[:, h:h + 1] * o1_ref[:, sl].astype(F32)
               + w2[:, h:h + 1] * o2_ref[:, sl].astype(F32))
        out_ref[:, sl] = acc.astype(out_ref.dtype)


def _combine(outs, lses, tq=512):
    m = outs[0].shape[0]
    o_spec = pl.BlockSpec((tq, ATTN_WIDTH), lambda i: (i, 0))
    l_spec = pl.BlockSpec((tq, V7X_LANES), lambda i: (i, 0))
    vmem = 8 * _nbytes((tq, ATTN_WIDTH), BF16) + 16 * _nbytes((tq, V7X_LANES), F32) + VMEM_ELEMENTWISE_TEMPORARIES
    return pl.pallas_call(
        _combine_kernel,
        grid=(m // tq,),
        in_specs=[o_spec] * 3 + [l_spec] * 3,
        out_specs=o_spec,
        out_shape=jax.ShapeDtypeStruct((m, ATTN_WIDTH), BF16),
        compiler_params=_params(1, vmem),
        name="attn_combine",
    )(*outs, *lses)


def _pool_kernel(zm_ref, zh_ref, wp_ref, scale_ref, o_ref, *, tp):
    t = pl.program_id(1)
    pos = t * tp + lax.broadcasted_iota(jnp.int32, (tp, 1), 0)
    for g, win in enumerate(POOL_WINDOWS):
        sl = slice(g * POOL_GROUP, (g + 1) * POOL_GROUP)
        main = zm_ref[:, sl]
        halo = jnp.where(t > 0, zh_ref[:, sl], 0.0)
        s = jnp.concatenate([halo, main], axis=0)
        shift = 1
        while shift < win:
            s = s + pltpu.roll(s, shift, axis=0)
            shift *= 2
        cnt = jnp.minimum(pos + 1, win).astype(F32)
        y = s[POOL_HALO:] / cnt - main
        yo = _dot(y.astype(BF16), wp_ref[g].astype(BF16))
        o_ref[:, sl] = (yo * scale_ref[:, sl]).astype(o_ref.dtype)


def _pool_mixer(zp, w_pool, pool_scale, batch, tp=512):
    m = zp.shape[0]
    seq = m // batch
    tiles_per_seq = seq // tp
    halo_blocks = tp // POOL_HALO
    vmem = (6 * _nbytes((tp, POOL_WIDTH), F32) + 2 * _nbytes(w_pool.shape, F32)
            + 12 * _nbytes((tp, POOL_GROUP), F32) + VMEM_ELEMENTWISE_TEMPORARIES)
    return pl.pallas_call(
        functools.partial(_pool_kernel, tp=tp),
        grid=(batch, tiles_per_seq),
        in_specs=[pl.BlockSpec((tp, POOL_WIDTH), lambda bi, t: (bi * tiles_per_seq + t, 0)),
                  pl.BlockSpec((POOL_HALO, POOL_WIDTH),
                               lambda bi, t: (jnp.maximum((bi * tiles_per_seq + t) * halo_blocks - 1, 0), 0)),
                  pl.BlockSpec(w_pool.shape, lambda bi, t: (0, 0, 0)),
                  pl.BlockSpec((1, POOL_WIDTH), lambda bi, t: (0, 0))],
        out_specs=pl.BlockSpec((tp, POOL_WIDTH), lambda bi, t: (bi * tiles_per_seq + t, 0)),
        out_shape=jax.ShapeDtypeStruct((m, POOL_WIDTH), BF16),
        compiler_params=_params(2, vmem),
        name="pool_mixer",
    )(zp, zp, w_pool, pool_scale.reshape(1, POOL_WIDTH))


def _mix_out_kernel(a_ref, p_ref, wa_ref, wp_ref, res_ref, gain_ref, o_ref, hg_ref, ssq_ref):
    _init_prenorm(ssq_ref)
    for sl in _column_halves(o_ref):
        acc = (_dot(a_ref[...], wa_ref[:, sl].astype(BF16))
               + _dot(p_ref[...], wp_ref[:, sl].astype(BF16)))
        out = res_ref[:, sl] + acc
        o_ref[:, sl] = out
        _emit_prenorm(out, gain_ref.at[:, sl], hg_ref.at[:, sl], ssq_ref)


def _mix_out(o_attn, o_pool, w, res, next_gain, tm=1024, tn=512):
    m = o_attn.shape[0]
    n = w.shape[1]
    ka, kp = o_attn.shape[1], o_pool.shape[1]
    assert ka == kp
    vmem = (2 * _nbytes((tm, ka + kp), BF16) + _weight_vmem((ka + kp, tn), w.dtype)
            + 8 * _nbytes((tm, tn), F32))
    norm_specs, norm_shapes = _prenorm_specs(m, n, tm, tn)
    return pl.pallas_call(
        _mix_out_kernel,
        grid=(m // tm, n // tn),
        in_specs=[pl.BlockSpec((tm, ka), lambda i, j: (i, 0)),
                  pl.BlockSpec((tm, kp), lambda i, j: (i, 0)),
                  pl.BlockSpec((ka, tn), lambda i, j: (0, j)),
                  pl.BlockSpec((kp, tn), lambda i, j: (1, j)),
                  pl.BlockSpec((tm, tn), lambda i, j: (i, j)),
                  pl.BlockSpec((1, tn), lambda i, j: (0, j))],
        out_specs=[pl.BlockSpec((tm, tn), lambda i, j: (i, j))] + norm_specs,
        out_shape=[jax.ShapeDtypeStruct((m, n), F32)] + norm_shapes,
        compiler_params=_params(2, vmem),
        name="mix_out",
    )(o_attn, o_pool, w, w, res, next_gain.reshape(1, n))


def _cast_kernel(x_ref, o_ref):
    o_ref[...] = x_ref[...].astype(o_ref.dtype)


def _cast_bf16(w, rows=512):
    k, n = w.shape
    rows = min(rows, k)
    return pl.pallas_call(
        _cast_kernel,
        grid=(k // rows,),
        in_specs=[pl.BlockSpec((rows, n), lambda i: (i, 0))],
        out_specs=pl.BlockSpec((rows, n), lambda i: (i, 0)),
        out_shape=jax.ShapeDtypeStruct((k, n), BF16),
        compiler_params=_params(1, 8 * _nbytes((rows, n), F32)),
        name="cast_bf16",
    )(w)


def _cross_kernel(hg_ref, ssq_ref, kv_ref, wq_ref, wo_ref, res_ref, gain_ref, o_ref, u_ref):
    scale = CROSS_DIM ** -0.5
    q = (_dot(hg_ref[...], wq_ref[...]) * _row_scale(ssq_ref, hg_ref.shape[1])).astype(BF16)
    heads = []
    for h in range(CROSS_HEADS):
        sl = slice(h * CROSS_DIM, (h + 1) * CROSS_DIM)
        k = kv_ref[:, sl]
        v = kv_ref[:, CROSS_WIDTH + h * CROSS_DIM:CROSS_WIDTH + (h + 1) * CROSS_DIM]
        s = _dot_nt(q[:, sl], k)
        m = jnp.max(s, axis=1, keepdims=True)
        p = jnp.exp2((s - m) * (scale * LOG2E))
        l = jnp.sum(p, axis=1, keepdims=True)
        heads.append((_dot(p.astype(BF16), v) / l).astype(BF16))
    out = res_ref[...] + _dot(jnp.concatenate(heads, axis=1), wo_ref[...])
    o_ref[...] = out
    ms = jnp.mean(out * out, axis=-1, keepdims=True)
    u_ref[...] = (out * lax.rsqrt(ms + EPS) * gain_ref[...]).astype(u_ref.dtype)


def _cross_attention(hg, ssq, kv, w_q, w_o, res, next_gain, batch, ts=256):
    m, d = res.shape
    mem_len = kv.shape[0] // batch
    steps_per_batch = (m // batch) // ts
    vmem = (2 * _nbytes((ts, d), BF16) * 2 + 4 * _nbytes((ts, d), F32) + _nbytes(w_q.shape, BF16)
            + _nbytes(w_o.shape, BF16) + 4 * _nbytes((ts, d), F32) + VMEM_ELEMENTWISE_TEMPORARIES)
    row_f32 = pl.BlockSpec((ts, d), lambda i: (i, 0))
    return pl.pallas_call(
        _cross_kernel,
        grid=(m // ts,),
        in_specs=[pl.BlockSpec((ts, d), lambda i: (i, 0)),
                  pl.BlockSpec((ts, V7X_LANES), lambda i: (i, 0)),
                  pl.BlockSpec((mem_len, 2 * CROSS_WIDTH), lambda i: (i // steps_per_batch, 0)),
                  _resident(w_q.shape, lambda i: (0, 0)),
                  _resident(w_o.shape, lambda i: (0, 0)),
                  row_f32,
                  pl.BlockSpec((1, d), lambda i: (0, 0))],
        out_specs=[row_f32, pl.BlockSpec((ts, d), lambda i: (i, 0))],
        out_shape=[jax.ShapeDtypeStruct((m, d), F32), jax.ShapeDtypeStruct((m, d), BF16)],
        compiler_params=_params(1, vmem),
        name="cross_attention",
    )(hg, ssq, kv, w_q, w_o, res, next_gain.reshape(1, d))


def _swiglu_block(h, u, w_in, w_out, next_gain=None, later_weights=()):
    gate, (w_out_bf16, *later_bf16) = _ffn_in(u, w_in, cast_jobs=(w_out,) + tuple(later_weights))
    return _proj_res(gate, w_out_bf16, h, 0.5, tm=512, tn=512, next_gain=next_gain), later_bf16


def kernel(x, mem, positions, g_ffn1, w_ffn1_in, w_ffn1_out, g_mix, w_mix_in, w_pool, pool_scale, w_mix_out, g_cross, g_mem, w_cross_q, w_cross_kv, w_cross_o, g_ffn2, w_ffn2_in, w_ffn2_out, g_final):
    batch, seq, d = x.shape
    m = batch * seq
    depth = g_ffn1.shape[0]
    cos, sin = _rope_tables(positions)
    h = x.reshape(m, d)
    mem2 = mem.reshape(batch * mem.shape[1], d)
    for l in range(depth):
        u = _rmsnorm(h, g_ffn1[l], BF16)
        (h, hg, ssq), (w_mix_in_bf16, w_mix_out_bf16, w_ffn2_in_bf16) = _swiglu_block(
            h, u, w_ffn1_in[l], w_ffn1_out[l], next_gain=g_mix[l],
            later_weights=(w_mix_in[l], w_mix_out[l], w_ffn2_in[l]))

        outs, lses = [], []
        qkv, qkv_dil = None, None
        for group, (_, dil) in enumerate(DILATED_CONFIGS):
            res = _mix_qkv(hg, ssq, w_mix_in_bf16, cos, sin, group, dil, batch,
                           attn_qkv=qkv, attn_dil=qkv_dil)
            if qkv is not None:
                outs.append(res[3])
                lses.append(res[4])
            qkv, qkv_dil = res[:3], dil
        zp, o, lse = _pool_proj_attn(hg, ssq, w_mix_in_bf16, ATTN_IN, qkv, qkv_dil, batch)
        outs.append(o)
        lses.append(lse)
        o_attn = _combine(outs, lses)
        o_pool = _pool_mixer(zp, w_pool[l], pool_scale[l], batch)
        h, hg, ssq = _mix_out(o_attn, o_pool, w_mix_out_bf16, h, g_cross[l])

        um = _rmsnorm(mem2, g_mem[l], BF16)
        kv = _proj(um, w_cross_kv[l], tm=um.shape[0], tn=CROSS_WIDTH)
        h, u = _cross_attention(hg, ssq, kv, _cast_bf16(w_cross_q[l]), _cast_bf16(w_cross_o[l]),
                                h, g_ffn2[l], batch)

        h, _ = _swiglu_block(h, u, w_ffn2_in_bf16, w_ffn2_out[l])
    return _rmsnorm(h, g_final, F32).reshape(batch, seq, d)
```

```python
import functools
import math

import jax
import jax.numpy as jnp
from jax import lax
from jax.experimental import pallas as pl
from jax.experimental.pallas import tpu as pltpu

F32 = jnp.float32
BF16 = jnp.bfloat16

D_MODEL = 4096
HEAD_DIM = 128
ATTN_WIDTH = D_MODEL // 2
N_ATTN_HEADS = ATTN_WIDTH // HEAD_DIM
DILATED_CONFIGS = ((128, 1), (512, 4), (2048, 16))
N_DIL = len(DILATED_CONFIGS)
BLOCK = 128
POOL_WIDTH = D_MODEL - ATTN_WIDTH
POOL_WINDOWS = (2, 4, 8, 16)
POOL_GROUP = POOL_WIDTH // len(POOL_WINDOWS)
QKV_WIDTH = 3 * ATTN_WIDTH
ATTN_IN = N_DIL * QKV_WIDTH
D_FF = ((8 * D_MODEL // 3 + 255) // 256) * 256
CROSS_HEADS = 4
CROSS_DIM = 128
CROSS_WIDTH = CROSS_HEADS * CROSS_DIM
ROPE_THETA = 10000.0
EPS = 1e-6
NEG_INF = -1e30
LOG2E = math.log2(math.e)

V7X_LANES = 128
V7X_BF16_SUBLANES = 16
V7X_VMEM_BYTES = 64 * 1024 * 1024
V7X_VMEM_REQUEST_CAP = V7X_VMEM_BYTES - 6 * 1024 * 1024
VMEM_ELEMENTWISE_TEMPORARIES = 8 * 1024 * 1024

POOL_HALO = max(POOL_WINDOWS)
MAX_DIL = max(d for _, d in DILATED_CONFIGS)
ATTN_TILE = BLOCK * MAX_DIL


def _params(n_grid, vmem_bytes):
    return pltpu.CompilerParams(
        dimension_semantics=("arbitrary",) * n_grid,
        vmem_limit_bytes=int(min(vmem_bytes, V7X_VMEM_REQUEST_CAP)),
    )


def _nbytes(shape, dtype):
    return math.prod(shape) * jnp.dtype(dtype).itemsize


def _weight_vmem(block_shape, dtype):
    cast_copy = 0 if dtype == BF16 else _nbytes(block_shape, BF16)
    return 2 * _nbytes(block_shape, dtype) + cast_copy


def _resident(block_shape, index_map):
    return pl.BlockSpec(block_shape, index_map, pipeline_mode=pl.Buffered(1))


def _dot(a, b):
    return jnp.dot(a, b, preferred_element_type=F32)


def _dot_nt(a, b):
    return lax.dot_general(a, b, (((1,), (1,)), ((), ())), preferred_element_type=F32)


def _rmsnorm_kernel(x_ref, g_ref, *rest):
    x = x_ref[...]
    ms = jnp.mean(x * x, axis=-1, keepdims=True)
    if len(rest) == 1:
        o_ref, = rest
    else:
        pos_ref, inv_ref, sign_ref, o_ref, cos_ref, sin_ref = rest
        ang = pos_ref[...].astype(F32) * inv_ref[...]
        cos_ref[...] = jnp.cos(ang)
        sin_ref[...] = jnp.sin(ang) * sign_ref[...]
    o_ref[...] = (x * lax.rsqrt(ms + EPS) * g_ref[...]).astype(o_ref.dtype)


def _rmsnorm(x, g, out_dtype, rope_positions=None, tm=256):
    m, d = x.shape
    vmem = 2 * tm * d * (4 + jnp.dtype(out_dtype).itemsize) + 4 * tm * d * 4
    in_specs = [pl.BlockSpec((tm, d), lambda i: (i, 0)), pl.BlockSpec((1, d), lambda i: (0, 0))]
    out_specs = [pl.BlockSpec((tm, d), lambda i: (i, 0))]
    out_shape = [jax.ShapeDtypeStruct((m, d), out_dtype)]
    args = [x, g.reshape(1, d)]
    if rope_positions is not None:
        half = HEAD_DIM // 2
        inv = 1.0 / (ROPE_THETA ** (jnp.arange(0, HEAD_DIM, 2, dtype=F32) / HEAD_DIM))
        inv = jnp.concatenate([inv, inv]).reshape(1, HEAD_DIM)
        sign = jnp.concatenate([-jnp.ones((half,), F32), jnp.ones((half,), F32)]).reshape(1, HEAD_DIM)
        row = pl.BlockSpec((1, HEAD_DIM), lambda i: (0, 0))
        tab = pl.BlockSpec((tm, HEAD_DIM), lambda i: (i, 0))
        in_specs += [pl.BlockSpec((tm, 1), lambda i: (i, 0)), row, row]
        out_specs += [tab, tab]
        out_shape += [jax.ShapeDtypeStruct((m, HEAD_DIM), F32)] * 2
        args += [rope_positions.reshape(m, 1), inv, sign]
        vmem += 16 * tm * HEAD_DIM * 4
    outs = pl.pallas_call(
        _rmsnorm_kernel,
        grid=(m // tm,),
        in_specs=in_specs,
        out_specs=out_specs,
        out_shape=out_shape,
        compiler_params=_params(1, vmem),
        name="rmsnorm",
    )(*args)
    return outs[0] if rope_positions is None else outs


def _ffn_in_kernel(u_ref, wa_ref, wb_ref, *rest):
    n_jobs = (len(rest) - 1) // 2
    srcs, o_ref, dsts = rest[:n_jobs], rest[n_jobs], rest[n_jobs + 1:]
    half = o_ref.shape[1] // 2
    for c in range(2):
        sl = slice(c * half, (c + 1) * half)
        w = jnp.concatenate([wa_ref[:, sl].astype(BF16), wb_ref[:, sl].astype(BF16)], axis=1)
        ab = _dot(u_ref[...], w)
        a, b = ab[:, :half], ab[:, half:]
        o_ref[:, sl] = (a * jax.nn.sigmoid(a) * b).astype(o_ref.dtype)
    for src, dst in zip(srcs, dsts):
        dst[...] = src[...].astype(dst.dtype)


def _ffn_in(u, w_in, cast_jobs=(), tm=2048, tf=256):
    m, d = u.shape
    nf = D_FF // tf
    n_steps = (m // tm) * nf
    lhs_spec, lhs_buffers = (pl.BlockSpec, 2) if w_in.dtype == BF16 else (_resident, 1)
    vmem = (lhs_buffers * _nbytes((tm, d), BF16) + 2 * _weight_vmem((d, tf), w_in.dtype)
            + 2 * _nbytes((tm, tf), BF16) + 6 * _nbytes((tm, tf), F32))
    job_specs, job_shapes = [], []
    for w in cast_jobs:
        rows = next(r for r in range(V7X_BF16_SUBLANES, w.shape[0] + 1, V7X_BF16_SUBLANES)
                    if w.shape[0] % r == 0 and w.shape[0] // r <= n_steps)
        n_blocks = w.shape[0] // rows
        job_specs.append(pl.BlockSpec(
            (rows, w.shape[1]), lambda i, j, n_blocks=n_blocks: (jnp.minimum(i * nf + j, n_blocks - 1), 0)))
        job_shapes.append(jax.ShapeDtypeStruct(w.shape, BF16))
        vmem += 2 * _nbytes((rows, w.shape[1]), F32) + 2 * _nbytes((rows, w.shape[1]), BF16)
    outs = pl.pallas_call(
        _ffn_in_kernel,
        grid=(m // tm, nf),
        in_specs=[lhs_spec((tm, d), lambda i, j: (i, 0)),
                  pl.BlockSpec((d, tf), lambda i, j: (0, j)),
                  pl.BlockSpec((d, tf), lambda i, j: (0, j + nf))] + job_specs,
        out_specs=[pl.BlockSpec((tm, tf), lambda i, j: (i, j))] + job_specs,
        out_shape=[jax.ShapeDtypeStruct((m, D_FF), BF16)] + job_shapes,
        compiler_params=_params(2, vmem),
        name="ffn_in",
    )(u, w_in, w_in, *cast_jobs)
    return outs[0], outs[1:]


def _emit_prenorm(out, gain_ref, hg_ref, ssq_ref):
    hg_ref[...] = (out * gain_ref[...]).astype(hg_ref.dtype)
    ssq_ref[...] += jnp.broadcast_to(jnp.sum(out * out, axis=1, keepdims=True), ssq_ref.shape)


def _init_prenorm(ssq_ref):
    @pl.when(pl.program_id(1) == 0)
    def _():
        ssq_ref[...] = jnp.zeros_like(ssq_ref)


def _row_scale(ssq_ref, d):
    return lax.rsqrt(ssq_ref[:, :1] * (1.0 / d) + EPS)


def _prenorm_specs(m, n, tm, tn):
    specs = [pl.BlockSpec((tm, tn), lambda i, j: (i, j)), pl.BlockSpec((tm, V7X_LANES), lambda i, j: (i, 0))]
    shapes = [jax.ShapeDtypeStruct((m, n), BF16), jax.ShapeDtypeStruct((m, V7X_LANES), F32)]
    return specs, shapes


def _column_halves(ref):
    half = ref.shape[1] // 2
    return [slice(0, half), slice(half, 2 * half)]


def _proj_res_kernel(lhs_ref, w_ref, res_ref, *rest, scale):
    prenorm = len(rest) > 1
    if prenorm:
        gain_ref, o_ref, hg_ref, ssq_ref = rest
        _init_prenorm(ssq_ref)
    else:
        o_ref, = rest
    for sl in _column_halves(o_ref):
        acc = _dot(lhs_ref[...], w_ref[:, sl].astype(BF16))
        out = res_ref[:, sl] + scale * acc
        o_ref[:, sl] = out
        if prenorm:
            _emit_prenorm(out, gain_ref.at[:, sl], hg_ref.at[:, sl], ssq_ref)


def _proj_res(lhs, w, res, scale, tm, tn, next_gain=None):
    m, k = lhs.shape
    n = w.shape[1]
    vmem = 2 * _nbytes((tm, k), BF16) + _weight_vmem((k, tn), w.dtype) + 8 * _nbytes((tm, tn), F32)
    in_specs = [pl.BlockSpec((tm, k), lambda i, j: (i, 0)),
                pl.BlockSpec((k, tn), lambda i, j: (0, j)),
                pl.BlockSpec((tm, tn), lambda i, j: (i, j))]
    out_specs = [pl.BlockSpec((tm, tn), lambda i, j: (i, j))]
    out_shape = [jax.ShapeDtypeStruct((m, n), F32)]
    args = [lhs, w, res]
    if next_gain is not None:
        in_specs.append(pl.BlockSpec((1, tn), lambda i, j: (0, j)))
        args.append(next_gain.reshape(1, n))
        specs, shapes = _prenorm_specs(m, n, tm, tn)
        out_specs += specs
        out_shape += shapes
    outs = pl.pallas_call(
        functools.partial(_proj_res_kernel, scale=scale),
        grid=(m // tm, n // tn),
        in_specs=in_specs,
        out_specs=out_specs,
        out_shape=out_shape,
        compiler_params=_params(2, vmem),
        name="proj_residual",
    )(*args)
    return outs[0] if next_gain is None else outs


def _proj_kernel(lhs_ref, w_ref, *rest):
    acc = _dot(lhs_ref[...], w_ref[...].astype(BF16))
    if len(rest) == 2:
        acc = acc * _row_scale(rest[0], lhs_ref.shape[1])
    rest[-1][...] = acc.astype(rest[-1].dtype)


def _proj(lhs, w, tm, tn, out_dtype=BF16, col0=0, n=None, ssq=None):
    m, k = lhs.shape
    n = w.shape[1] if n is None else n
    col_blk0 = col0 // tn
    vmem = 2 * _nbytes((tm, k), BF16) + _weight_vmem((k, tn), w.dtype) + 6 * _nbytes((tm, tn), F32)
    in_specs = [pl.BlockSpec((tm, k), lambda i, j: (i, 0)),
                pl.BlockSpec((k, tn), lambda i, j: (0, j + col_blk0))]
    args = [lhs, w]
    if ssq is not None:
        in_specs.append(pl.BlockSpec((tm, V7X_LANES), lambda i, j: (i, 0)))
        args.append(ssq)
    return pl.pallas_call(
        _proj_kernel,
        grid=(m // tm, n // tn),
        in_specs=in_specs,
        out_specs=pl.BlockSpec((tm, tn), lambda i, j: (i, j)),
        out_shape=jax.ShapeDtypeStruct((m, n), out_dtype),
        compiler_params=_params(2, vmem),
        name="proj",
    )(*args)


def _attn_blocks(ref, sl):
    return ref[:, :, sl].reshape(MAX_DIL, BLOCK, HEAD_DIM)


def _attn_prev_blocks(first_ref, cur, sl, dil):
    nq = MAX_DIL // dil
    first = first_ref[:, :, sl]
    if nq == 1:
        return first
    cur4 = cur.reshape(dil, nq, BLOCK, HEAD_DIM)
    return jnp.concatenate([first[:, None], cur4[:, :-1]], axis=1).reshape(MAX_DIL, BLOCK, HEAD_DIM)


def _attn_token_rows(g, dil):
    r, i = divmod(g, MAX_DIL // dil)
    return pl.ds(i * BLOCK * dil + r, BLOCK, stride=dil)


def _attn_masks(dil, tile_has_prev):
    shape = (MAX_DIL, BLOCK, BLOCK)
    blk = lax.broadcasted_iota(jnp.int32, shape, 0)
    qi = lax.broadcasted_iota(jnp.int32, shape, 1)
    kj = lax.broadcasted_iota(jnp.int32, shape, 2)
    has_prev = jnp.logical_or(tile_has_prev, blk % (MAX_DIL // dil) != 0)
    return jnp.logical_and(kj >= qi, has_prev), kj <= qi


def _attn_probs(q_ref, kp_ref, kc_ref, sl, dil, masks):
    mask_prev, mask_cur = masks
    scale = HEAD_DIM ** -0.5
    q = _attn_blocks(q_ref, sl)
    kc = _attn_blocks(kc_ref, sl)
    kp = _attn_prev_blocks(kp_ref, kc, sl, dil)
    sp = jnp.einsum("gqd,gkd->gqk", q, kp, preferred_element_type=F32)
    sc = jnp.einsum("gqd,gkd->gqk", q, kc, preferred_element_type=F32)
    sp = jnp.where(mask_prev, sp, NEG_INF)
    sc = jnp.where(mask_cur, sc, NEG_INF)
    m = jnp.maximum(jnp.max(sp, axis=2, keepdims=True), jnp.max(sc, axis=2, keepdims=True))
    pp = jnp.exp2((sp - m) * (scale * LOG2E))
    pc = jnp.exp2((sc - m) * (scale * LOG2E))
    l = jnp.sum(pp, axis=2, keepdims=True) + jnp.sum(pc, axis=2, keepdims=True)
    return pp.astype(BF16), pc.astype(BF16), l, m * scale + jnp.log(l)


def _attn_output(pp, pc, l, vp_ref, vc_ref, sl, dil, o_scr, slot):
    vc = _attn_blocks(vc_ref, sl)
    vp = _attn_prev_blocks(vp_ref, vc, sl, dil)
    o = (jnp.einsum("gqk,gkd->gqd", pp, vp, preferred_element_type=F32)
         + jnp.einsum("gqk,gkd->gqd", pc, vc, preferred_element_type=F32)) / l
    for g in range(MAX_DIL):
        o_scr[slot, _attn_token_rows(g, dil), :] = o[g]


def _attn_scatter_lse(lse_tile, dil, lse_scr):
    for g in range(MAX_DIL):
        lse_scr[_attn_token_rows(g, dil), :] = lse_tile[g]


def _attn_accumulate_lse(lse_scr, lse_ref, first):
    @pl.when(first)
    def _():
        lse_ref[...] = lse_scr[...]

    @pl.when(jnp.logical_not(first))
    def _():
        lse_ref[...] += lse_scr[...]


def _mix_qkv_kernel(u_ref, ssq_ref, wq_ref, wk_ref, wv_ref, cos_ref, sin_ref, *rest, tn, dil,
                    attn_dil, tiles_per_batch, tiles_per_attn_tile, attn_heads):
    if attn_dil is None:
        q_ref, k_ref, v_ref, scr_ref, tab_ref = rest
    else:
        (aq_ref, akp_ref, akc_ref, avp_ref, avc_ref, q_ref, k_ref, v_ref, ao_ref, alse_ref,
         scr_ref, tab_ref, ao_scr, alse_scr) = rest
        tile_has_prev, first_head = _attn_host_unit(tiles_per_batch, tiles_per_attn_tile, attn_heads)
    tm, d_in = u_ref.shape
    rows = tm // dil

    def regroup(slab, val):
        if dil == 1:
            return [val]
        scr_ref[slab] = val
        return [scr_ref[slab, pl.ds(r, rows, stride=dil), :] for r in range(dil)]

    @pl.when(pl.program_id(1) == 0)
    def _():
        rs = _row_scale(ssq_ref, d_in)
        tables = (cos_ref[...] * rs, sin_ref[...] * rs, jnp.broadcast_to(rs, (tm, HEAD_DIM)))
        for t, table in enumerate(tables):
            for r, piece in enumerate(regroup(0, table)):
                tab_ref[t, r] = piece

    heads = tn // HEAD_DIM
    for part, (w_ref, o_ref) in enumerate(((wq_ref, q_ref), (wk_ref, k_ref), (wv_ref, v_ref))):
        acc = _dot(u_ref[...], w_ref[...].astype(BF16))
        if part == 0 and attn_dil is not None:
            probs = _attn_host_probs(aq_ref, akp_ref, akc_ref, attn_dil, tile_has_prev, attn_heads)
        for h in range(heads):
            pieces = regroup(part * heads + h, acc[:, h * HEAD_DIM:(h + 1) * HEAD_DIM])
            for r, piece in enumerate(pieces):
                if o_ref is v_ref:
                    piece = piece * tab_ref[2, r]
                else:
                    piece = piece * tab_ref[0, r] + pltpu.roll(piece, HEAD_DIM // 2, axis=1) * tab_ref[1, r]
                o_ref[r, :, h * HEAD_DIM:(h + 1) * HEAD_DIM] = piece.astype(o_ref.dtype)

    if attn_dil is not None:
        _attn_host_outputs(probs, first_head, avp_ref, avc_ref, attn_dil, ao_ref, ao_scr, alse_scr)
        _attn_accumulate_lse(alse_scr, alse_ref, first_head == 0)


def _attn_unit_specs(dil, unit, heads):
    rows = ATTN_TILE // dil

    def cur_map(*idx):
        b, t, h = unit(*idx)
        return b, 0, t, h

    def prev_map(*idx):
        b, t, h = unit(*idx)
        return b, 0, jnp.maximum(t * (rows // BLOCK) - 1, 0), h

    cur = pl.BlockSpec((None, dil, rows, heads * HEAD_DIM), cur_map)
    prev = pl.BlockSpec((None, dil, BLOCK, heads * HEAD_DIM), prev_map)
    return [cur, prev, cur, prev, cur]


def _mix_qkv(u, ssq, w, cos, sin, group, dil, batch, attn_qkv=None, attn_dil=None, tm=1024, tn=256):
    m, k = u.shape
    seq = m // batch
    tiles_per_seq = seq // tm
    part_blocks = ATTN_WIDTH // tn

    def w_spec(part):
        col_blk0 = (group * 3 + part) * part_blocks
        return pl.BlockSpec((k, tn), lambda i, j: (0, j + col_blk0))

    out_spec = pl.BlockSpec((None, dil, tm // dil, tn),
                            lambda i, j: (i // tiles_per_seq, 0, i % tiles_per_seq, j))
    out_shape = jax.ShapeDtypeStruct((batch, dil, seq // dil, ATTN_WIDTH), BF16)
    vmem = (2 * _nbytes((tm, k), BF16) + 3 * _weight_vmem((k, tn), w.dtype)
            + 3 * 6 * _nbytes((tm, tn), F32) + 10 * _nbytes((tm, HEAD_DIM), F32))
    in_specs = [pl.BlockSpec((tm, k), lambda i, j: (i, 0)),
                pl.BlockSpec((tm, V7X_LANES), lambda i, j: (i, 0)),
                w_spec(0), w_spec(1), w_spec(2),
                pl.BlockSpec((tm, HEAD_DIM), lambda i, j: (i, 0)),
                pl.BlockSpec((tm, HEAD_DIM), lambda i, j: (i, 0))]
    out_specs, out_shapes = [out_spec] * 3, [out_shape] * 3
    scratch = [pltpu.VMEM((3 * tn // HEAD_DIM, tm, HEAD_DIM), F32),
               pltpu.VMEM((3, dil, tm // dil, HEAD_DIM), F32)]
    args = [u, ssq, w, w, w, cos, sin]
    host = _attn_host(attn_qkv, attn_dil, m, seq, tm, part_blocks)
    if attn_qkv is not None:
        in_specs += host["in_specs"]
        args += host["args"]
        out_specs += host["out_specs"]
        out_shapes += host["out_shapes"]
        scratch += host["scratch"]
        vmem += host["vmem"]
    return pl.pallas_call(
        functools.partial(_mix_qkv_kernel, tn=tn, dil=dil, attn_dil=attn_dil, **host["kwargs"]),
        grid=(m // tm, part_blocks),
        in_specs=in_specs,
        out_specs=out_specs,
        out_shape=out_shapes,
        scratch_shapes=scratch,
        compiler_params=_params(2, vmem),
        name=f"mix_qkv_{dil}",
    )(*args)


def _attn_host(attn_qkv, attn_dil, m, seq, tm, col_steps):
    tiles_per_seq = seq // tm
    tiles_per_attn_tile = ATTN_TILE // tm
    heads = N_ATTN_HEADS // (tiles_per_attn_tile * col_steps)
    kwargs = dict(tiles_per_batch=tiles_per_seq, tiles_per_attn_tile=tiles_per_attn_tile, attn_heads=heads)
    if attn_qkv is None:
        return dict(kwargs=kwargs)
    assert heads * tiles_per_attn_tile * col_steps == N_ATTN_HEADS and seq % ATTN_TILE == 0
    attn_tiles_per_seq = seq // ATTN_TILE

    def unit(i, j):
        in_seq = i % tiles_per_seq
        return (i // tiles_per_seq, in_seq // tiles_per_attn_tile,
                (in_seq % tiles_per_attn_tile) * col_steps + j)

    def row_tile(i, j):
        b, t, _ = unit(i, j)
        return b * attn_tiles_per_seq + t

    aq, ak, av = attn_qkv
    return dict(
        kwargs=kwargs,
        in_specs=_attn_unit_specs(attn_dil, unit, heads),
        args=[aq, ak, ak, av, av],
        out_specs=[pl.BlockSpec((ATTN_TILE, heads * HEAD_DIM), lambda i, j: (row_tile(i, j), unit(i, j)[2])),
                   pl.BlockSpec((ATTN_TILE, V7X_LANES), lambda i, j: (row_tile(i, j), 0))],
        out_shapes=[jax.ShapeDtypeStruct((m, ATTN_WIDTH), BF16),
                    jax.ShapeDtypeStruct((m, V7X_LANES), F32)],
        scratch=[pltpu.VMEM((heads, ATTN_TILE, HEAD_DIM), F32), pltpu.VMEM((ATTN_TILE, V7X_LANES), F32)],
        vmem=heads * (8 * _nbytes((attn_dil, ATTN_TILE // attn_dil, HEAD_DIM), BF16)
                      + 4 * _nbytes((ATTN_TILE, HEAD_DIM), BF16) + 8 * _nbytes((ATTN_TILE, BLOCK), F32))
        + 4 * _nbytes((ATTN_TILE, V7X_LANES), F32))


def _attn_host_unit(tiles_per_batch, tiles_per_attn_tile, attn_heads):
    tile_has_prev = (pl.program_id(0) % tiles_per_batch) >= tiles_per_attn_tile
    group = (pl.program_id(0) % tiles_per_attn_tile) * pl.num_programs(1) + pl.program_id(1)
    return tile_has_prev, group * attn_heads


def _attn_host_probs(aq_ref, akp_ref, akc_ref, attn_dil, tile_has_prev, attn_heads):
    masks = _attn_masks(attn_dil, tile_has_prev)
    return [_attn_probs(aq_ref, akp_ref, akc_ref, slice(h * HEAD_DIM, (h + 1) * HEAD_DIM), attn_dil, masks)
            for h in range(attn_heads)]


def _attn_host_outputs(probs, first_head, avp_ref, avc_ref, attn_dil, ao_ref, ao_scr, alse_scr):
    lane = lax.broadcasted_iota(jnp.int32, (MAX_DIL, BLOCK, V7X_LANES), 2)
    lse_tile = jnp.zeros((MAX_DIL, BLOCK, V7X_LANES), F32)
    for h, (pp, pc, l, lse) in enumerate(probs):
        sl = slice(h * HEAD_DIM, (h + 1) * HEAD_DIM)
        _attn_output(pp, pc, l, avp_ref, avc_ref, sl, attn_dil, ao_scr, h)
        ao_ref[:, sl] = ao_scr[h].astype(ao_ref.dtype)
        lse_tile = jnp.where(lane == first_head + h, lse, lse_tile)
    _attn_scatter_lse(lse_tile, attn_dil, alse_scr)


def _pool_proj_kernel(u_ref, ssq_ref, w_ref, aq_ref, akp_ref, akc_ref, avp_ref, avc_ref,
                      z_ref, ao_ref, alse_ref, ao_scr, alse_scr, *, attn_dil, tiles_per_batch,
                      tiles_per_attn_tile, attn_heads):
    tile_has_prev, first_head = _attn_host_unit(tiles_per_batch, tiles_per_attn_tile, attn_heads)
    rs = _row_scale(ssq_ref, u_ref.shape[1])
    probs = None
    for sl in _column_halves(z_ref):
        z_ref[:, sl] = _dot(u_ref[...], w_ref[:, sl].astype(BF16)) * rs
        if probs is None:
            probs = _attn_host_probs(aq_ref, akp_ref, akc_ref, attn_dil, tile_has_prev, attn_heads)
    _attn_host_outputs(probs, first_head, avp_ref, avc_ref, attn_dil, ao_ref, ao_scr, alse_scr)
    _attn_accumulate_lse(alse_scr, alse_ref, first_head == 0)


def _pool_proj_attn(u, ssq, w, col0, attn_qkv, attn_dil, batch, tm=1024, tn=512):
    m, k = u.shape
    seq = m // batch
    col_steps = POOL_WIDTH // tn
    col_blk0 = col0 // tn
    host = _attn_host(attn_qkv, attn_dil, m, seq, tm, col_steps)
    vmem = (2 * _nbytes((tm, k), BF16) + _weight_vmem((k, tn), w.dtype) + 6 * _nbytes((tm, tn), F32)
            + host["vmem"])
    return pl.pallas_call(
        functools.partial(_pool_proj_kernel, attn_dil=attn_dil, **host["kwargs"]),
        grid=(m // tm, col_steps),
        in_specs=[pl.BlockSpec((tm, k), lambda i, j: (i, 0)),
                  pl.BlockSpec((tm, V7X_LANES), lambda i, j: (i, 0)),
                  pl.BlockSpec((k, tn), lambda i, j: (0, j + col_blk0))] + host["in_specs"],
        out_specs=[pl.BlockSpec((tm, tn), lambda i, j: (i, j))] + host["out_specs"],
        out_shape=[jax.ShapeDtypeStruct((m, POOL_WIDTH), F32)] + host["out_shapes"],
        scratch_shapes=host["scratch"],
        compiler_params=_params(2, vmem),
        name="pool_proj_attn",
    )(u, ssq, w, *host["args"])


def _combine_kernel(o0_ref, o1_ref, o2_ref, l0_ref, l1_ref, l2_ref, out_ref):
    l0, l1, l2 = l0_ref[...], l1_ref[...], l2_ref[...]
    m = jnp.maximum(jnp.maximum(l0, l1), l2)
    e0, e1, e2 = jnp.exp(l0 - m), jnp.exp(l1 - m), jnp.exp(l2 - m)
    den = e0 + e1 + e2
    w0, w1, w2 = e0 / den, e1 / den, e2 / den
    for h in range(N_ATTN_HEADS):
        sl = slice(h * HEAD_DIM, (h + 1) * HEAD_DIM)
        acc = (w0[:, h:h + 1] * o0_ref[:, sl].astype(F32)
               + w1[:, h:h + 1] * o1_ref[:, sl].astype(F32)
               + w2[:, h:h + 1] * o2_ref[:, sl].astype(F32))
        out_ref[:, sl] = acc.astype(out_ref.dtype)


def _combine(outs, lses, tq=512):
    m = outs[0].shape[0]
    o_spec = pl.BlockSpec((tq, ATTN_WIDTH), lambda i: (i, 0))
    l_spec = pl.BlockSpec((tq, V7X_LANES), lambda i: (i, 0))
    vmem = 8 * _nbytes((tq, ATTN_WIDTH), BF16) + 16 * _nbytes((tq, V7X_LANES), F32) + VMEM_ELEMENTWISE_TEMPORARIES
    return pl.pallas_call(
        _combine_kernel,
        grid=(m // tq,),
        in_specs=[o_spec] * 3 + [l_spec] * 3,
        out_specs=o_spec,
        out_shape=jax.ShapeDtypeStruct((m, ATTN_WIDTH), BF16),
        compiler_params=_params(1, vmem),
        name="attn_combine",
    )(*outs, *lses)


def _pool_kernel(zm_ref, zh_ref, wp_ref, scale_ref, o_ref, *, tp):
    t = pl.program_id(1)
    pos = t * tp + lax.broadcasted_iota(jnp.int32, (tp, 1), 0)
    for g, win in enumerate(POOL_WINDOWS):
        sl = slice(g * POOL_GROUP, (g + 1) * POOL_GROUP)
        main = zm_ref[:, sl]
        halo = jnp.where(t > 0, zh_ref[:, sl], 0.0)
        s = jnp.concatenate([halo, main], axis=0)
        shift = 1
        while shift < win:
            s = s + pltpu.roll(s, shift, axis=0)
            shift *= 2
        cnt = jnp.minimum(pos + 1, win).astype(F32)
        y = s[POOL_HALO:] / cnt - main
        yo = _dot(y.astype(BF16), wp_ref[g].astype(BF16))
        o_ref[:, sl] = (yo * scale_ref[:, sl]).astype(o_ref.dtype)


def _pool_mixer(zp, w_pool, pool_scale, batch, tp=512):
    m = zp.shape[0]
    seq = m // batch
    tiles_per_seq = seq // tp
    halo_blocks = tp // POOL_HALO
    vmem = (6 * _nbytes((tp, POOL_WIDTH), F32) + 2 * _nbytes(w_pool.shape, F32)
            + 12 * _nbytes((tp, POOL_GROUP), F32) + VMEM_ELEMENTWISE_TEMPORARIES)
    return pl.pallas_call(
        functools.partial(_pool_kernel, tp=tp),
        grid=(batch, tiles_per_seq),
        in_specs=[pl.BlockSpec((tp, POOL_WIDTH), lambda bi, t: (bi * tiles_per_seq + t, 0)),
                  pl.BlockSpec((POOL_HALO, POOL_WIDTH),
                               lambda bi, t: (jnp.maximum((bi * tiles_per_seq + t) * halo_blocks - 1, 0), 0)),
                  pl.BlockSpec(w_pool.shape, lambda bi, t: (0, 0, 0)),
                  pl.BlockSpec((1, POOL_WIDTH), lambda bi, t: (0, 0))],
        out_specs=pl.BlockSpec((tp, POOL_WIDTH), lambda bi, t: (bi * tiles_per_seq + t, 0)),
        out_shape=jax.ShapeDtypeStruct((m, POOL_WIDTH), BF16),
        compiler_params=_params(2, vmem),
        name="pool_mixer",
    )(zp, zp, w_pool, pool_scale.reshape(1, POOL_WIDTH))


def _mix_out_kernel(a_ref, p_ref, wa_ref, wp_ref, res_ref, o_ref):
    for sl in _column_halves(o_ref):
        acc = (_dot(a_ref[...], wa_ref[:, sl].astype(BF16))
               + _dot(p_ref[...], wp_ref[:, sl].astype(BF16)))
        o_ref[:, sl] = res_ref[:, sl] + acc


def _mix_out(o_attn, o_pool, w, res, tm=1024, tn=512):
    m = o_attn.shape[0]
    n = w.shape[1]
    ka, kp = o_attn.shape[1], o_pool.shape[1]
    assert ka == kp
    vmem = (2 * _nbytes((tm, ka + kp), BF16) + _weight_vmem((ka + kp, tn), w.dtype)
            + 8 * _nbytes((tm, tn), F32))
    return pl.pallas_call(
        _mix_out_kernel,
        grid=(m // tm, n // tn),
        in_specs=[pl.BlockSpec((tm, ka), lambda i, j: (i, 0)),
                  pl.BlockSpec((tm, kp), lambda i, j: (i, 0)),
                  pl.BlockSpec((ka, tn), lambda i, j: (0, j)),
                  pl.BlockSpec((kp, tn), lambda i, j: (1, j)),
                  pl.BlockSpec((tm, tn), lambda i, j: (i, j))],
        out_specs=pl.BlockSpec((tm, tn), lambda i, j: (i, j)),
        out_shape=jax.ShapeDtypeStruct((m, n), F32),
        compiler_params=_params(2, vmem),
        name="mix_out",
    )(o_attn, o_pool, w, w, res)


def _cast_kernel(x_ref, o_ref):
    o_ref[...] = x_ref[...].astype(o_ref.dtype)


def _cast_bf16(w, rows=512):
    k, n = w.shape
    rows = min(rows, k)
    return pl.pallas_call(
        _cast_kernel,
        grid=(k // rows,),
        in_specs=[pl.BlockSpec((rows, n), lambda i: (i, 0))],
        out_specs=pl.BlockSpec((rows, n), lambda i: (i, 0)),
        out_shape=jax.ShapeDtypeStruct((k, n), BF16),
        compiler_params=_params(1, 8 * _nbytes((rows, n), F32)),
        name="cast_bf16",
    )(w)


def _rms_rows(x, gain_ref):
    ms = jnp.mean(x * x, axis=-1, keepdims=True)
    return (x * lax.rsqrt(ms + EPS) * gain_ref[...]).astype(BF16)


def _cross_kernel(kv_ref, wq_ref, wo_ref, res_ref, gain_in_ref, gain_out_ref, o_ref, u_ref):
    scale = CROSS_DIM ** -0.5
    q = _dot(_rms_rows(res_ref[...], gain_in_ref), wq_ref[...]).astype(BF16)
    heads = []
    for h in range(CROSS_HEADS):
        sl = slice(h * CROSS_DIM, (h + 1) * CROSS_DIM)
        k = kv_ref[:, sl]
        v = kv_ref[:, CROSS_WIDTH + h * CROSS_DIM:CROSS_WIDTH + (h + 1) * CROSS_DIM]
        s = _dot_nt(q[:, sl], k)
        m = jnp.max(s, axis=1, keepdims=True)
        p = jnp.exp2((s - m) * (scale * LOG2E))
        l = jnp.sum(p, axis=1, keepdims=True)
        heads.append((_dot(p.astype(BF16), v) / l).astype(BF16))
    out = res_ref[...] + _dot(jnp.concatenate(heads, axis=1), wo_ref[...])
    o_ref[...] = out
    u_ref[...] = _rms_rows(out, gain_out_ref)


def _cross_attention(kv, w_q, w_o, res, gain, next_gain, batch, ts=256):
    m, d = res.shape
    mem_len = kv.shape[0] // batch
    steps_per_batch = (m // batch) // ts
    vmem = (2 * _nbytes((ts, d), BF16) + 4 * _nbytes((ts, d), F32) + _nbytes(w_q.shape, BF16)
            + _nbytes(w_o.shape, BF16) + 6 * _nbytes((ts, d), F32) + VMEM_ELEMENTWISE_TEMPORARIES)
    row_f32 = pl.BlockSpec((ts, d), lambda i: (i, 0))
    gain_spec = pl.BlockSpec((1, d), lambda i: (0, 0))
    return pl.pallas_call(
        _cross_kernel,
        grid=(m // ts,),
        in_specs=[pl.BlockSpec((mem_len, 2 * CROSS_WIDTH), lambda i: (i // steps_per_batch, 0)),
                  _resident(w_q.shape, lambda i: (0, 0)),
                  _resident(w_o.shape, lambda i: (0, 0)),
                  row_f32, gain_spec, gain_spec],
        out_specs=[row_f32, pl.BlockSpec((ts, d), lambda i: (i, 0))],
        out_shape=[jax.ShapeDtypeStruct((m, d), F32), jax.ShapeDtypeStruct((m, d), BF16)],
        compiler_params=_params(1, vmem),
        name="cross_attention",
    )(kv, w_q, w_o, res, gain.reshape(1, d), next_gain.reshape(1, d))


def _swiglu_block(h, u, w_in, w_out, next_gain=None, later_weights=()):
    gate, (w_out_bf16, *later_bf16) = _ffn_in(u, w_in, cast_jobs=(w_out,) + tuple(later_weights))
    return _proj_res(gate, w_out_bf16, h, 0.5, tm=512, tn=512, next_gain=next_gain), later_bf16


def kernel(x, mem, positions, g_ffn1, w_ffn1_in, w_ffn1_out, g_mix, w_mix_in, w_pool, pool_scale, w_mix_out, g_cross, g_mem, w_cross_q, w_cross_kv, w_cross_o, g_ffn2, w_ffn2_in, w_ffn2_out, g_final):
    batch, seq, d = x.shape
    m = batch * seq
    depth = g_ffn1.shape[0]
    h = x.reshape(m, d)
    mem2 = mem.reshape(batch * mem.shape[1], d)
    for l in range(depth):
        if l == 0:
            u, cos, sin = _rmsnorm(h, g_ffn1[l], BF16, rope_positions=positions)
        else:
            u = _rmsnorm(h, g_ffn1[l], BF16)
        (h, hg, ssq), (w_mix_in_bf16, w_mix_out_bf16, w_ffn2_in_bf16) = _swiglu_block(
            h, u, w_ffn1_in[l], w_ffn1_out[l], next_gain=g_mix[l],
            later_weights=(w_mix_in[l], w_mix_out[l], w_ffn2_in[l]))

        outs, lses = [], []
        qkv, qkv_dil = None, None
        for group, (_, dil) in enumerate(DILATED_CONFIGS):
            res = _mix_qkv(hg, ssq, w_mix_in_bf16, cos, sin, group, dil, batch,
                           attn_qkv=qkv, attn_dil=qkv_dil)
            if qkv is not None:
                outs.append(res[3])
                lses.append(res[4])
            qkv, qkv_dil = res[:3], dil
        zp, o, lse = _pool_proj_attn(hg, ssq, w_mix_in_bf16, ATTN_IN, qkv, qkv_dil, batch)
        outs.append(o)
        lses.append(lse)
        o_attn = _combine(outs, lses)
        o_pool = _pool_mixer(zp, w_pool[l], pool_scale[l], batch)
        h = _mix_out(o_attn, o_pool, w_mix_out_bf16, h)

        um = _rmsnorm(mem2, g_mem[l], BF16)
        kv = _proj(um, w_cross_kv[l], tm=um.shape[0], tn=CROSS_WIDTH)
        h, u = _cross_attention(kv, _cast_bf16(w_cross_q[l]), _cast_bf16(w_cross_o[l]),
                                h, g_cross[l], g_ffn2[l], batch)

        h, _ = _swiglu_block(h, u, w_ffn2_in_bf16, w_ffn2_out[l])
    return _rmsnorm(h, g_final, F32).reshape(batch, seq, d)
```

```python
import functools
import math

import jax
import jax.numpy as jnp
from jax import lax
from jax.experimental import pallas as pl
from jax.experimental.pallas import tpu as pltpu

F32 = jnp.float32
BF16 = jnp.bfloat16

D_MODEL = 4096
HEAD_DIM = 128
ATTN_WIDTH = D_MODEL // 2
N_ATTN_HEADS = ATTN_WIDTH // HEAD_DIM
DILATED_CONFIGS = ((128, 1), (512, 4), (2048, 16))
N_DIL = len(DILATED_CONFIGS)
BLOCK = 128
POOL_WIDTH = D_MODEL - ATTN_WIDTH
POOL_WINDOWS = (2, 4, 8, 16)
POOL_GROUP = POOL_WIDTH // len(POOL_WINDOWS)
QKV_WIDTH = 3 * ATTN_WIDTH
ATTN_IN = N_DIL * QKV_WIDTH
D_FF = ((8 * D_MODEL // 3 + 255) // 256) * 256
CROSS_HEADS = 4
CROSS_DIM = 128
CROSS_WIDTH = CROSS_HEADS * CROSS_DIM
ROPE_THETA = 10000.0
EPS = 1e-6
NEG_INF = -1e30
LOG2E = math.log2(math.e)

V7X_LANES = 128
V7X_BF16_SUBLANES = 16
V7X_VMEM_BYTES = 64 * 1024 * 1024
V7X_VMEM_REQUEST_CAP = V7X_VMEM_BYTES - 6 * 1024 * 1024
VMEM_ELEMENTWISE_TEMPORARIES = 8 * 1024 * 1024

POOL_HALO = max(POOL_WINDOWS)
MAX_DIL = max(d for _, d in DILATED_CONFIGS)
ATTN_TILE = BLOCK * MAX_DIL


def _params(n_grid, vmem_bytes):
    return pltpu.CompilerParams(
        dimension_semantics=("arbitrary",) * n_grid,
        vmem_limit_bytes=int(min(vmem_bytes, V7X_VMEM_REQUEST_CAP)),
    )


def _nbytes(shape, dtype):
    return math.prod(shape) * jnp.dtype(dtype).itemsize


def _weight_vmem(block_shape, dtype):
    cast_copy = 0 if dtype == BF16 else _nbytes(block_shape, BF16)
    return 2 * _nbytes(block_shape, dtype) + cast_copy


def _resident(block_shape, index_map):
    return pl.BlockSpec(block_shape, index_map, pipeline_mode=pl.Buffered(1))


def _dot(a, b):
    return jnp.dot(a, b, preferred_element_type=F32)


def _dot_nt(a, b):
    return lax.dot_general(a, b, (((1,), (1,)), ((), ())), preferred_element_type=F32)


def _rmsnorm_kernel(x_ref, g_ref, *rest):
    x = x_ref[...]
    ms = jnp.mean(x * x, axis=-1, keepdims=True)
    if len(rest) == 1:
        o_ref, = rest
    else:
        pos_ref, inv_ref, sign_ref, o_ref, cos_ref, sin_ref = rest
        ang = pos_ref[...].astype(F32) * inv_ref[...]
        cos_ref[...] = jnp.cos(ang)
        sin_ref[...] = jnp.sin(ang) * sign_ref[...]
    o_ref[...] = (x * lax.rsqrt(ms + EPS) * g_ref[...]).astype(o_ref.dtype)


def _rmsnorm(x, g, out_dtype, rope_positions=None, tm=256):
    m, d = x.shape
    vmem = 2 * tm * d * (4 + jnp.dtype(out_dtype).itemsize) + 4 * tm * d * 4
    in_specs = [pl.BlockSpec((tm, d), lambda i: (i, 0)), pl.BlockSpec((1, d), lambda i: (0, 0))]
    out_specs = [pl.BlockSpec((tm, d), lambda i: (i, 0))]
    out_shape = [jax.ShapeDtypeStruct((m, d), out_dtype)]
    args = [x, g.reshape(1, d)]
    if rope_positions is not None:
        half = HEAD_DIM // 2
        inv = 1.0 / (ROPE_THETA ** (jnp.arange(0, HEAD_DIM, 2, dtype=F32) / HEAD_DIM))
        inv = jnp.concatenate([inv, inv]).reshape(1, HEAD_DIM)
        sign = jnp.concatenate([-jnp.ones((half,), F32), jnp.ones((half,), F32)]).reshape(1, HEAD_DIM)
        row = pl.BlockSpec((1, HEAD_DIM), lambda i: (0, 0))
        tab = pl.BlockSpec((tm, HEAD_DIM), lambda i: (i, 0))
        in_specs += [pl.BlockSpec((tm, 1), lambda i: (i, 0)), row, row]
        out_specs += [tab, tab]
        out_shape += [jax.ShapeDtypeStruct((m, HEAD_DIM), F32)] * 2
        args += [rope_positions.reshape(m, 1), inv, sign]
        vmem += 16 * tm * HEAD_DIM * 4
    outs = pl.pallas_call(
        _rmsnorm_kernel,
        grid=(m // tm,),
        in_specs=in_specs,
        out_specs=out_specs,
        out_shape=out_shape,
        compiler_params=_params(1, vmem),
        name="rmsnorm",
    )(*args)
    return outs[0] if rope_positions is None else outs


def _ffn_in_kernel(u_ref, wa_ref, wb_ref, *rest):
    n_jobs = (len(rest) - 1) // 2
    srcs, o_ref, dsts = rest[:n_jobs], rest[n_jobs], rest[n_jobs + 1:]
    half = o_ref.shape[1] // 2
    for c in range(2):
        sl = slice(c * half, (c + 1) * half)
        w = jnp.concatenate([wa_ref[:, sl].astype(BF16), wb_ref[:, sl].astype(BF16)], axis=1)
        ab = _dot(u_ref[...], w)
        a, b = ab[:, :half], ab[:, half:]
        o_ref[:, sl] = (a * jax.nn.sigmoid(a) * b).astype(o_ref.dtype)
    for src, dst in zip(srcs, dsts):
        dst[...] = src[...].astype(dst.dtype)


def _ffn_in(u, w_in, cast_jobs=(), tm=2048, tf=256):
    m, d = u.shape
    nf = D_FF // tf
    n_steps = (m // tm) * nf
    lhs_spec, lhs_buffers = (pl.BlockSpec, 2) if w_in.dtype == BF16 else (_resident, 1)
    vmem = (lhs_buffers * _nbytes((tm, d), BF16) + 2 * _weight_vmem((d, tf), w_in.dtype)
            + 2 * _nbytes((tm, tf), BF16) + 6 * _nbytes((tm, tf), F32))
    job_specs, job_shapes = [], []
    for w in cast_jobs:
        rows = next(r for r in range(V7X_BF16_SUBLANES, w.shape[0] + 1, V7X_BF16_SUBLANES)
                    if w.shape[0] % r == 0 and w.shape[0] // r <= n_steps)
        n_blocks = w.shape[0] // rows
        job_specs.append(pl.BlockSpec(
            (rows, w.shape[1]), lambda i, j, n_blocks=n_blocks: (jnp.minimum(i * nf + j, n_blocks - 1), 0)))
        job_shapes.append(jax.ShapeDtypeStruct(w.shape, BF16))
        vmem += 2 * _nbytes((rows, w.shape[1]), F32) + 2 * _nbytes((rows, w.shape[1]), BF16)
    outs = pl.pallas_call(
        _ffn_in_kernel,
        grid=(m // tm, nf),
        in_specs=[lhs_spec((tm, d), lambda i, j: (i, 0)),
                  pl.BlockSpec((d, tf), lambda i, j: (0, j)),
                  pl.BlockSpec((d, tf), lambda i, j: (0, j + nf))] + job_specs,
        out_specs=[pl.BlockSpec((tm, tf), lambda i, j: (i, j))] + job_specs,
        out_shape=[jax.ShapeDtypeStruct((m, D_FF), BF16)] + job_shapes,
        compiler_params=_params(2, vmem),
        name="ffn_in",
    )(u, w_in, w_in, *cast_jobs)
    return outs[0], outs[1:]


def _emit_prenorm(out, gain_ref, hg_ref, ssq_ref):
    hg_ref[...] = (out * gain_ref[...]).astype(hg_ref.dtype)
    ssq_ref[...] += jnp.broadcast_to(jnp.sum(out * out, axis=1, keepdims=True), ssq_ref.shape)


def _init_prenorm(ssq_ref):
    @pl.when(pl.program_id(1) == 0)
    def _():
        ssq_ref[...] = jnp.zeros_like(ssq_ref)


def _row_scale(ssq_ref, d):
    return lax.rsqrt(ssq_ref[:, :1] * (1.0 / d) + EPS)


def _prenorm_specs(m, n, tm, tn):
    specs = [pl.BlockSpec((tm, tn), lambda i, j: (i, j)), pl.BlockSpec((tm, V7X_LANES), lambda i, j: (i, 0))]
    shapes = [jax.ShapeDtypeStruct((m, n), BF16), jax.ShapeDtypeStruct((m, V7X_LANES), F32)]
    return specs, shapes


def _column_halves(ref):
    half = ref.shape[1] // 2
    return [slice(0, half), slice(half, 2 * half)]


def _proj_res_kernel(lhs_ref, w_ref, res_ref, *rest, scale):
    prenorm = len(rest) > 1
    if prenorm:
        gain_ref, o_ref, hg_ref, ssq_ref = rest
        _init_prenorm(ssq_ref)
    else:
        o_ref, = rest
    for sl in _column_halves(o_ref):
        acc = _dot(lhs_ref[...], w_ref[:, sl].astype(BF16))
        out = res_ref[:, sl] + scale * acc
        o_ref[:, sl] = out
        if prenorm:
            _emit_prenorm(out, gain_ref.at[:, sl], hg_ref.at[:, sl], ssq_ref)


def _proj_res(lhs, w, res, scale, tm, tn, next_gain=None):
    m, k = lhs.shape
    n = w.shape[1]
    vmem = 2 * _nbytes((tm, k), BF16) + _weight_vmem((k, tn), w.dtype) + 8 * _nbytes((tm, tn), F32)
    in_specs = [pl.BlockSpec((tm, k), lambda i, j: (i, 0)),
                pl.BlockSpec((k, tn), lambda i, j: (0, j)),
                pl.BlockSpec((tm, tn), lambda i, j: (i, j))]
    out_specs = [pl.BlockSpec((tm, tn), lambda i, j: (i, j))]
    out_shape = [jax.ShapeDtypeStruct((m, n), F32)]
    args = [lhs, w, res]
    if next_gain is not None:
        in_specs.append(pl.BlockSpec((1, tn), lambda i, j: (0, j)))
        args.append(next_gain.reshape(1, n))
        specs, shapes = _prenorm_specs(m, n, tm, tn)
        out_specs += specs
        out_shape += shapes
    outs = pl.pallas_call(
        functools.partial(_proj_res_kernel, scale=scale),
        grid=(m // tm, n // tn),
        in_specs=in_specs,
        out_specs=out_specs,
        out_shape=out_shape,
        compiler_params=_params(2, vmem),
        name="proj_residual",
    )(*args)
    return outs[0] if next_gain is None else outs


def _proj_kernel(lhs_ref, w_ref, *rest):
    acc = _dot(lhs_ref[...], w_ref[...].astype(BF16))
    if len(rest) == 2:
        acc = acc * _row_scale(rest[0], lhs_ref.shape[1])
    rest[-1][...] = acc.astype(rest[-1].dtype)


def _proj(lhs, w, tm, tn, out_dtype=BF16, col0=0, n=None, ssq=None):
    m, k = lhs.shape
    n = w.shape[1] if n is None else n
    col_blk0 = col0 // tn
    vmem = 2 * _nbytes((tm, k), BF16) + _weight_vmem((k, tn), w.dtype) + 6 * _nbytes((tm, tn), F32)
    in_specs = [pl.BlockSpec((tm, k), lambda i, j: (i, 0)),
                pl.BlockSpec((k, tn), lambda i, j: (0, j + col_blk0))]
    args = [lhs, w]
    if ssq is not None:
        in_specs.append(pl.BlockSpec((tm, V7X_LANES), lambda i, j: (i, 0)))
        args.append(ssq)
    return pl.pallas_call(
        _proj_kernel,
        grid=(m // tm, n // tn),
        in_specs=in_specs,
        out_specs=pl.BlockSpec((tm, tn), lambda i, j: (i, j)),
        out_shape=jax.ShapeDtypeStruct((m, n), out_dtype),
        compiler_params=_params(2, vmem),
        name="proj",
    )(*args)


def _attn_blocks(ref, sl):
    return ref[:, :, sl].reshape(MAX_DIL, BLOCK, HEAD_DIM)


def _attn_prev_blocks(first_ref, cur, sl, dil):
    nq = MAX_DIL // dil
    first = first_ref[:, :, sl]
    if nq == 1:
        return first
    cur4 = cur.reshape(dil, nq, BLOCK, HEAD_DIM)
    return jnp.concatenate([first[:, None], cur4[:, :-1]], axis=1).reshape(MAX_DIL, BLOCK, HEAD_DIM)


def _attn_token_rows(g, dil):
    r, i = divmod(g, MAX_DIL // dil)
    return pl.ds(i * BLOCK * dil + r, BLOCK, stride=dil)


def _attn_masks(dil, tile_has_prev):
    shape = (MAX_DIL, BLOCK, BLOCK)
    blk = lax.broadcasted_iota(jnp.int32, shape, 0)
    qi = lax.broadcasted_iota(jnp.int32, shape, 1)
    kj = lax.broadcasted_iota(jnp.int32, shape, 2)
    has_prev = jnp.logical_or(tile_has_prev, blk % (MAX_DIL // dil) != 0)
    return jnp.logical_and(kj >= qi, has_prev), kj <= qi


def _attn_probs(q_ref, kp_ref, kc_ref, sl, dil, masks):
    mask_prev, mask_cur = masks
    scale = HEAD_DIM ** -0.5
    q = _attn_blocks(q_ref, sl)
    kc = _attn_blocks(kc_ref, sl)
    kp = _attn_prev_blocks(kp_ref, kc, sl, dil)
    sp = jnp.einsum("gqd,gkd->gqk", q, kp, preferred_element_type=F32)
    sc = jnp.einsum("gqd,gkd->gqk", q, kc, preferred_element_type=F32)
    sp = jnp.where(mask_prev, sp, NEG_INF)
    sc = jnp.where(mask_cur, sc, NEG_INF)
    m = jnp.maximum(jnp.max(sp, axis=2, keepdims=True), jnp.max(sc, axis=2, keepdims=True))
    pp = jnp.exp2((sp - m) * (scale * LOG2E))
    pc = jnp.exp2((sc - m) * (scale * LOG2E))
    l = jnp.sum(pp, axis=2, keepdims=True) + jnp.sum(pc, axis=2, keepdims=True)
    return pp.astype(BF16), pc.astype(BF16), l, m * scale + jnp.log(l)


def _attn_output(pp, pc, l, vp_ref, vc_ref, sl, dil, o_scr, slot):
    vc = _attn_blocks(vc_ref, sl)
    vp = _attn_prev_blocks(vp_ref, vc, sl, dil)
    o = (jnp.einsum("gqk,gkd->gqd", pp, vp, preferred_element_type=F32)
         + jnp.einsum("gqk,gkd->gqd", pc, vc, preferred_element_type=F32)) / l
    for g in range(MAX_DIL):
        o_scr[slot, _attn_token_rows(g, dil), :] = o[g]


def _attn_scatter_lse(lse_tile, dil, lse_scr):
    for g in range(MAX_DIL):
        lse_scr[_attn_token_rows(g, dil), :] = lse_tile[g]


def _attn_accumulate_lse(lse_scr, lse_ref, first):
    @pl.when(first)
    def _():
        lse_ref[...] = lse_scr[...]

    @pl.when(jnp.logical_not(first))
    def _():
        lse_ref[...] += lse_scr[...]


def _mix_qkv_kernel(u_ref, ssq_ref, wq_ref, wk_ref, wv_ref, cos_ref, sin_ref, *rest, tn, dil,
                    attn_dil, tiles_per_batch, tiles_per_attn_tile, attn_heads):
    if attn_dil is None:
        q_ref, k_ref, v_ref, scr_ref, tab_ref = rest
    else:
        (aq_ref, akp_ref, akc_ref, avp_ref, avc_ref, q_ref, k_ref, v_ref, ao_ref, alse_ref,
         scr_ref, tab_ref, ao_scr, alse_scr) = rest
        tile_has_prev, first_head = _attn_host_unit(tiles_per_batch, tiles_per_attn_tile, attn_heads)
    tm, d_in = u_ref.shape
    rows = tm // dil

    def regroup(slab, val):
        if dil == 1:
            return [val]
        scr_ref[slab] = val
        return [scr_ref[slab, pl.ds(r, rows, stride=dil), :] for r in range(dil)]

    @pl.when(pl.program_id(1) == 0)
    def _():
        rs = _row_scale(ssq_ref, d_in)
        tables = (cos_ref[...] * rs, sin_ref[...] * rs, jnp.broadcast_to(rs, (tm, HEAD_DIM)))
        for t, table in enumerate(tables):
            for r, piece in enumerate(regroup(0, table)):
                tab_ref[t, r] = piece

    heads = tn // HEAD_DIM
    for part, (w_ref, o_ref) in enumerate(((wq_ref, q_ref), (wk_ref, k_ref), (wv_ref, v_ref))):
        acc = _dot(u_ref[...], w_ref[...].astype(BF16))
        if part == 0 and attn_dil is not None:
            probs = _attn_host_probs(aq_ref, akp_ref, akc_ref, attn_dil, tile_has_prev, attn_heads)
        for h in range(heads):
            pieces = regroup(part * heads + h, acc[:, h * HEAD_DIM:(h + 1) * HEAD_DIM])
            for r, piece in enumerate(pieces):
                if o_ref is v_ref:
                    piece = piece * tab_ref[2, r]
                else:
                    piece = piece * tab_ref[0, r] + pltpu.roll(piece, HEAD_DIM // 2, axis=1) * tab_ref[1, r]
                o_ref[r, :, h * HEAD_DIM:(h + 1) * HEAD_DIM] = piece.astype(o_ref.dtype)

    if attn_dil is not None:
        _attn_host_outputs(probs, first_head, avp_ref, avc_ref, attn_dil, ao_ref, ao_scr, alse_scr)
        _attn_accumulate_lse(alse_scr, alse_ref, first_head == 0)


def _attn_unit_specs(dil, unit, heads):
    rows = ATTN_TILE // dil

    def cur_map(*idx):
        b, t, h = unit(*idx)
        return b, 0, t, h

    def prev_map(*idx):
        b, t, h = unit(*idx)
        return b, 0, jnp.maximum(t * (rows // BLOCK) - 1, 0), h

    cur = pl.BlockSpec((None, dil, rows, heads * HEAD_DIM), cur_map)
    prev = pl.BlockSpec((None, dil, BLOCK, heads * HEAD_DIM), prev_map)
    return [cur, prev, cur, prev, cur]


def _mix_qkv(u, ssq, w, cos, sin, group, dil, batch, attn_qkv=None, attn_dil=None, tm=1024, tn=256):
    m, k = u.shape
    seq = m // batch
    tiles_per_seq = seq // tm
    part_blocks = ATTN_WIDTH // tn

    def w_spec(part):
        col_blk0 = (group * 3 + part) * part_blocks
        return pl.BlockSpec((k, tn), lambda i, j: (0, j + col_blk0))

    out_spec = pl.BlockSpec((None, dil, tm // dil, tn),
                            lambda i, j: (i // tiles_per_seq, 0, i % tiles_per_seq, j))
    out_shape = jax.ShapeDtypeStruct((batch, dil, seq // dil, ATTN_WIDTH), BF16)
    vmem = (2 * _nbytes((tm, k), BF16) + 3 * _weight_vmem((k, tn), w.dtype)
            + 3 * 6 * _nbytes((tm, tn), F32) + 10 * _nbytes((tm, HEAD_DIM), F32))
    in_specs = [pl.BlockSpec((tm, k), lambda i, j: (i, 0)),
                pl.BlockSpec((tm, V7X_LANES), lambda i, j: (i, 0)),
                w_spec(0), w_spec(1), w_spec(2),
                pl.BlockSpec((tm, HEAD_DIM), lambda i, j: (i, 0)),
                pl.BlockSpec((tm, HEAD_DIM), lambda i, j: (i, 0))]
    out_specs, out_shapes = [out_spec] * 3, [out_shape] * 3
    scratch = [pltpu.VMEM((3 * tn // HEAD_DIM, tm, HEAD_DIM), F32),
               pltpu.VMEM((3, dil, tm // dil, HEAD_DIM), F32)]
    args = [u, ssq, w, w, w, cos, sin]
    host = _attn_host(attn_qkv, attn_dil, m, seq, tm, part_blocks)
    if attn_qkv is not None:
        in_specs += host["in_specs"]
        args += host["args"]
        out_specs += host["out_specs"]
        out_shapes += host["out_shapes"]
        scratch += host["scratch"]
        vmem += host["vmem"]
    return pl.pallas_call(
        functools.partial(_mix_qkv_kernel, tn=tn, dil=dil, attn_dil=attn_dil, **host["kwargs"]),
        grid=(m // tm, part_blocks),
        in_specs=in_specs,
        out_specs=out_specs,
        out_shape=out_shapes,
        scratch_shapes=scratch,
        compiler_params=_params(2, vmem),
        name=f"mix_qkv_{dil}",
    )(*args)


def _attn_host(attn_qkv, attn_dil, m, seq, tm, col_steps):
    tiles_per_seq = seq // tm
    tiles_per_attn_tile = ATTN_TILE // tm
    heads = N_ATTN_HEADS // (tiles_per_attn_tile * col_steps)
    kwargs = dict(tiles_per_batch=tiles_per_seq, tiles_per_attn_tile=tiles_per_attn_tile, attn_heads=heads)
    if attn_qkv is None:
        return dict(kwargs=kwargs)
    assert heads * tiles_per_attn_tile * col_steps == N_ATTN_HEADS and seq % ATTN_TILE == 0
    attn_tiles_per_seq = seq // ATTN_TILE

    def unit(i, j):
        in_seq = i % tiles_per_seq
        return (i // tiles_per_seq, in_seq // tiles_per_attn_tile,
                (in_seq % tiles_per_attn_tile) * col_steps + j)

    def row_tile(i, j):
        b, t, _ = unit(i, j)
        return b * attn_tiles_per_seq + t

    aq, ak, av = attn_qkv
    return dict(
        kwargs=kwargs,
        in_specs=_attn_unit_specs(attn_dil, unit, heads),
        args=[aq, ak, ak, av, av],
        out_specs=[pl.BlockSpec((ATTN_TILE, heads * HEAD_DIM), lambda i, j: (row_tile(i, j), unit(i, j)[2])),
                   pl.BlockSpec((ATTN_TILE, V7X_LANES), lambda i, j: (row_tile(i, j), 0))],
        out_shapes=[jax.ShapeDtypeStruct((m, ATTN_WIDTH), BF16),
                    jax.ShapeDtypeStruct((m, V7X_LANES), F32)],
        scratch=[pltpu.VMEM((heads, ATTN_TILE, HEAD_DIM), F32), pltpu.VMEM((ATTN_TILE, V7X_LANES), F32)],
        vmem=heads * (8 * _nbytes((attn_dil, ATTN_TILE // attn_dil, HEAD_DIM), BF16)
                      + 4 * _nbytes((ATTN_TILE, HEAD_DIM), BF16) + 8 * _nbytes((ATTN_TILE, BLOCK), F32))
        + 4 * _nbytes((ATTN_TILE, V7X_LANES), F32))


def _attn_host_unit(tiles_per_batch, tiles_per_attn_tile, attn_heads):
    tile_has_prev = (pl.program_id(0) % tiles_per_batch) >= tiles_per_attn_tile
    group = (pl.program_id(0) % tiles_per_attn_tile) * pl.num_programs(1) + pl.program_id(1)
    return tile_has_prev, group * attn_heads


def _attn_host_probs(aq_ref, akp_ref, akc_ref, attn_dil, tile_has_prev, attn_heads):
    masks = _attn_masks(attn_dil, tile_has_prev)
    return [_attn_probs(aq_ref, akp_ref, akc_ref, slice(h * HEAD_DIM, (h + 1) * HEAD_DIM), attn_dil, masks)
            for h in range(attn_heads)]


def _attn_host_outputs(probs, first_head, avp_ref, avc_ref, attn_dil, ao_ref, ao_scr, alse_scr):
    lane = lax.broadcasted_iota(jnp.int32, (MAX_DIL, BLOCK, V7X_LANES), 2)
    lse_tile = jnp.zeros((MAX_DIL, BLOCK, V7X_LANES), F32)
    for h, (pp, pc, l, lse) in enumerate(probs):
        sl = slice(h * HEAD_DIM, (h + 1) * HEAD_DIM)
        _attn_output(pp, pc, l, avp_ref, avc_ref, sl, attn_dil, ao_scr, h)
        ao_ref[:, sl] = ao_scr[h].astype(ao_ref.dtype)
        lse_tile = jnp.where(lane == first_head + h, lse, lse_tile)
    _attn_scatter_lse(lse_tile, attn_dil, alse_scr)


def _pool_proj_kernel(u_ref, ssq_ref, w_ref, aq_ref, akp_ref, akc_ref, avp_ref, avc_ref,
                      z_ref, ao_ref, alse_ref, ao_scr, alse_scr, *, attn_dil, tiles_per_batch,
                      tiles_per_attn_tile, attn_heads):
    tile_has_prev, first_head = _attn_host_unit(tiles_per_batch, tiles_per_attn_tile, attn_heads)
    rs = _row_scale(ssq_ref, u_ref.shape[1])
    probs = None
    for sl in _column_halves(z_ref):
        z_ref[:, sl] = _dot(u_ref[...], w_ref[:, sl].astype(BF16)) * rs
        if probs is None:
            probs = _attn_host_probs(aq_ref, akp_ref, akc_ref, attn_dil, tile_has_prev, attn_heads)
    _attn_host_outputs(probs, first_head, avp_ref, avc_ref, attn_dil, ao_ref, ao_scr, alse_scr)
    _attn_accumulate_lse(alse_scr, alse_ref, first_head == 0)


def _pool_proj_attn(u, ssq, w, col0, attn_qkv, attn_dil, batch, tm=1024, tn=512):
    m, k = u.shape
    seq = m // batch
    col_steps = POOL_WIDTH // tn
    col_blk0 = col0 // tn
    host = _attn_host(attn_qkv, attn_dil, m, seq, tm, col_steps)
    vmem = (2 * _nbytes((tm, k), BF16) + _weight_vmem((k, tn), w.dtype) + 6 * _nbytes((tm, tn), F32)
            + host["vmem"])
    return pl.pallas_call(
        functools.partial(_pool_proj_kernel, attn_dil=attn_dil, **host["kwargs"]),
        grid=(m // tm, col_steps),
        in_specs=[pl.BlockSpec((tm, k), lambda i, j: (i, 0)),
                  pl.BlockSpec((tm, V7X_LANES), lambda i, j: (i, 0)),
                  pl.BlockSpec((k, tn), lambda i, j: (0, j + col_blk0))] + host["in_specs"],
        out_specs=[pl.BlockSpec((tm, tn), lambda i, j: (i, j))] + host["out_specs"],
        out_shape=[jax.ShapeDtypeStruct((m, POOL_WIDTH), F32)] + host["out_shapes"],
        scratch_shapes=host["scratch"],
        compiler_params=_params(2, vmem),
        name="pool_proj_attn",
    )(u, ssq, w, *host["args"])


def _combine_kernel(o0_ref, o1_ref, o2_ref, l0_ref, l1_ref, l2_ref, out_ref):
    l0, l1, l2 = l0_ref[...], l1_ref[...], l2_ref[...]
    m = jnp.maximum(jnp.maximum(l0, l1), l2)
    e0, e1, e2 = jnp.exp(l0 - m), jnp.exp(l1 - m), jnp.exp(l2 - m)
    den = e0 + e1 + e2
    w0, w1, w2 = e0 / den, e1 / den, e2 / den
    for h in range(N_ATTN_HEADS):
        sl = slice(h * HEAD_DIM, (h + 1) * HEAD_DIM)
        acc = (w0[:, h:h + 1] * o0_ref[:, sl].astype(F32)
               + w1[:, h:h + 1] * o1_ref[:, sl].astype(F32)
               + w2[:, h:h + 1] * o2_ref[:, sl].astype(F32))
        out_ref[:, sl] = acc.astype(out_ref.dtype)


def _pool_kernel(zm_ref, zh_ref, wp_ref, scale_ref, o_ref, *, tp):
    t = pl.program_id(1)
    pos = t * tp + lax.broadcasted_iota(jnp.int32, (tp, 1), 0)
    for g, win in enumerate(POOL_WINDOWS):
        sl = slice(g * POOL_GROUP, (g + 1) * POOL_GROUP)
        main = zm_ref[:, sl]
        halo = jnp.where(t > 0, zh_ref[:, sl], 0.0)
        s = jnp.concatenate([halo, main], axis=0)
        shift = 1
        while shift < win:
            s = s + pltpu.roll(s, shift, axis=0)
            shift *= 2
        cnt = jnp.minimum(pos + 1, win).astype(F32)
        y = s[POOL_HALO:] / cnt - main
        yo = _dot(y.astype(BF16), wp_ref[g].astype(BF16))
        o_ref[:, sl] = (yo * scale_ref[:, sl]).astype(o_ref.dtype)


def _combine_pool_kernel(o0_ref, o1_ref, o2_ref, l0_ref, l1_ref, l2_ref, zm_ref, zh_ref, wp_ref,
                         scale_ref, attn_ref, pool_ref, *, tp):
    _combine_kernel(o0_ref, o1_ref, o2_ref, l0_ref, l1_ref, l2_ref, attn_ref)
    _pool_kernel(zm_ref, zh_ref, wp_ref, scale_ref, pool_ref, tp=tp)


def _combine_and_pool(outs, lses, zp, w_pool, pool_scale, batch, tp=512):
    m = zp.shape[0]
    seq = m // batch
    tiles_per_seq = seq // tp
    halo_blocks = tp // POOL_HALO

    def rows(bi, t):
        return bi * tiles_per_seq + t

    o_spec = pl.BlockSpec((tp, ATTN_WIDTH), lambda bi, t: (rows(bi, t), 0))
    l_spec = pl.BlockSpec((tp, V7X_LANES), lambda bi, t: (rows(bi, t), 0))
    vmem = (8 * _nbytes((tp, ATTN_WIDTH), BF16) + 16 * _nbytes((tp, V7X_LANES), F32)
            + 6 * _nbytes((tp, POOL_WIDTH), F32) + 2 * _nbytes(w_pool.shape, F32)
            + 12 * _nbytes((tp, POOL_GROUP), F32) + 2 * VMEM_ELEMENTWISE_TEMPORARIES)
    return pl.pallas_call(
        functools.partial(_combine_pool_kernel, tp=tp),
        grid=(batch, tiles_per_seq),
        in_specs=[o_spec] * 3 + [l_spec] * 3 + [
            pl.BlockSpec((tp, POOL_WIDTH), lambda bi, t: (rows(bi, t), 0)),
            pl.BlockSpec((POOL_HALO, POOL_WIDTH),
                         lambda bi, t: (jnp.maximum(rows(bi, t) * halo_blocks - 1, 0), 0)),
            pl.BlockSpec(w_pool.shape, lambda bi, t: (0, 0, 0)),
            pl.BlockSpec((1, POOL_WIDTH), lambda bi, t: (0, 0))],
        out_specs=[o_spec, pl.BlockSpec((tp, POOL_WIDTH), lambda bi, t: (rows(bi, t), 0))],
        out_shape=[jax.ShapeDtypeStruct((m, ATTN_WIDTH), BF16),
                   jax.ShapeDtypeStruct((m, POOL_WIDTH), BF16)],
        compiler_params=_params(2, vmem),
        name="combine_pool",
    )(*outs, *lses, zp, zp, w_pool, pool_scale.reshape(1, POOL_WIDTH))


def _mix_out_kernel(a_ref, p_ref, wa_ref, wp_ref, res_ref, o_ref):
    for sl in _column_halves(o_ref):
        acc = (_dot(a_ref[...], wa_ref[:, sl].astype(BF16))
               + _dot(p_ref[...], wp_ref[:, sl].astype(BF16)))
        o_ref[:, sl] = res_ref[:, sl] + acc


def _mix_out(o_attn, o_pool, w, res, tm=1024, tn=512):
    m = o_attn.shape[0]
    n = w.shape[1]
    ka, kp = o_attn.shape[1], o_pool.shape[1]
    assert ka == kp
    vmem = (2 * _nbytes((tm, ka + kp), BF16) + _weight_vmem((ka + kp, tn), w.dtype)
            + 8 * _nbytes((tm, tn), F32))
    return pl.pallas_call(
        _mix_out_kernel,
        grid=(m // tm, n // tn),
        in_specs=[pl.BlockSpec((tm, ka), lambda i, j: (i, 0)),
                  pl.BlockSpec((tm, kp), lambda i, j: (i, 0)),
                  pl.BlockSpec((ka, tn), lambda i, j: (0, j)),
                  pl.BlockSpec((kp, tn), lambda i, j: (1, j)),
                  pl.BlockSpec((tm, tn), lambda i, j: (i, j))],
        out_specs=pl.BlockSpec((tm, tn), lambda i, j: (i, j)),
        out_shape=jax.ShapeDtypeStruct((m, n), F32),
        compiler_params=_params(2, vmem),
        name="mix_out",
    )(o_attn, o_pool, w, w, res)


def _cast_kernel(x_ref, o_ref):
    o_ref[...] = x_ref[...].astype(o_ref.dtype)


def _cast_bf16(w, rows=512):
    k, n = w.shape
    rows = min(rows, k)
    return pl.pallas_call(
        _cast_kernel,
        grid=(k // rows,),
        in_specs=[pl.BlockSpec((rows, n), lambda i: (i, 0))],
        out_specs=pl.BlockSpec((rows, n), lambda i: (i, 0)),
        out_shape=jax.ShapeDtypeStruct((k, n), BF16),
        compiler_params=_params(1, 8 * _nbytes((rows, n), F32)),
        name="cast_bf16",
    )(w)


def _rms_rows(x, gain_ref):
    ms = jnp.mean(x * x, axis=-1, keepdims=True)
    return (x * lax.rsqrt(ms + EPS) * gain_ref[...]).astype(BF16)


def _cross_kernel(kv_ref, wq_ref, wo_ref, res_ref, gain_in_ref, gain_out_ref, o_ref, u_ref):
    scale = CROSS_DIM ** -0.5
    q = _dot(_rms_rows(res_ref[...], gain_in_ref), wq_ref[...]).astype(BF16)
    heads = []
    for h in range(CROSS_HEADS):
        sl = slice(h * CROSS_DIM, (h + 1) * CROSS_DIM)
        k = kv_ref[:, sl]
        v = kv_ref[:, CROSS_WIDTH + h * CROSS_DIM:CROSS_WIDTH + (h + 1) * CROSS_DIM]
        s = _dot_nt(q[:, sl], k)
        m = jnp.max(s, axis=1, keepdims=True)
        p = jnp.exp2((s - m) * (scale * LOG2E))
        l = jnp.sum(p, axis=1, keepdims=True)
        heads.append((_dot(p.astype(BF16), v) / l).astype(BF16))
    out = res_ref[...] + _dot(jnp.concatenate(heads, axis=1), wo_ref[...])
    o_ref[...] = out
    u_ref[...] = _rms_rows(out, gain_out_ref)


def _cross_attention(kv, w_q, w_o, res, gain, next_gain, batch, ts=256):
    m, d = res.shape
    mem_len = kv.shape[0] // batch
    steps_per_batch = (m // batch) // ts
    vmem = (2 * _nbytes((ts, d), BF16) + 4 * _nbytes((ts, d), F32) + _nbytes(w_q.shape, BF16)
            + _nbytes(w_o.shape, BF16) + 6 * _nbytes((ts, d), F32) + VMEM_ELEMENTWISE_TEMPORARIES)
    row_f32 = pl.BlockSpec((ts, d), lambda i: (i, 0))
    gain_spec = pl.BlockSpec((1, d), lambda i: (0, 0))
    return pl.pallas_call(
        _cross_kernel,
        grid=(m // ts,),
        in_specs=[pl.BlockSpec((mem_len, 2 * CROSS_WIDTH), lambda i: (i // steps_per_batch, 0)),
                  _resident(w_q.shape, lambda i: (0, 0)),
                  _resident(w_o.shape, lambda i: (0, 0)),
                  row_f32, gain_spec, gain_spec],
        out_specs=[row_f32, pl.BlockSpec((ts, d), lambda i: (i, 0))],
        out_shape=[jax.ShapeDtypeStruct((m, d), F32), jax.ShapeDtypeStruct((m, d), BF16)],
        compiler_params=_params(1, vmem),
        name="cross_attention",
    )(kv, w_q, w_o, res, gain.reshape(1, d), next_gain.reshape(1, d))


def _swiglu_block(h, u, w_in, w_out, next_gain=None, later_weights=()):
    gate, (w_out_bf16, *later_bf16) = _ffn_in(u, w_in, cast_jobs=(w_out,) + tuple(later_weights))
    return _proj_res(gate, w_out_bf16, h, 0.5, tm=512, tn=512, next_gain=next_gain), later_bf16


def kernel(x, mem, positions, g_ffn1, w_ffn1_in, w_ffn1_out, g_mix, w_mix_in, w_pool, pool_scale, w_mix_out, g_cross, g_mem, w_cross_q, w_cross_kv, w_cross_o, g_ffn2, w_ffn2_in, w_ffn2_out, g_final):
    batch, seq, d = x.shape
    m = batch * seq
    depth = g_ffn1.shape[0]
    h = x.reshape(m, d)
    mem2 = mem.reshape(batch * mem.shape[1], d)
    for l in range(depth):
        if l == 0:
            u, cos, sin = _rmsnorm(h, g_ffn1[l], BF16, rope_positions=positions)
        else:
            u = _rmsnorm(h, g_ffn1[l], BF16)
        (h, hg, ssq), (w_mix_in_bf16, w_mix_out_bf16, w_ffn2_in_bf16) = _swiglu_block(
            h, u, w_ffn1_in[l], w_ffn1_out[l], next_gain=g_mix[l],
            later_weights=(w_mix_in[l], w_mix_out[l], w_ffn2_in[l]))

        outs, lses = [], []
        qkv, qkv_dil = None, None
        for group, (_, dil) in enumerate(DILATED_CONFIGS):
            res = _mix_qkv(hg, ssq, w_mix_in_bf16, cos, sin, group, dil, batch,
                           attn_qkv=qkv, attn_dil=qkv_dil)
            if qkv is not None:
                outs.append(res[3])
                lses.append(res[4])
            qkv, qkv_dil = res[:3], dil
        zp, o, lse = _pool_proj_attn(hg, ssq, w_mix_in_bf16, ATTN_IN, qkv, qkv_dil, batch)
        outs.append(o)
        lses.append(lse)
        o_attn, o_pool = _combine_and_pool(outs, lses, zp, w_pool[l], pool_scale[l], batch)
        h = _mix_out(o_attn, o_pool, w_mix_out_bf16, h)

        um = _rmsnorm(mem2, g_mem[l], BF16)
        kv = _proj(um, w_cross_kv[l], tm=um.shape[0], tn=CROSS_WIDTH)
        h, u = _cross_attention(kv, _cast_bf16(w_cross_q[l]), _cast_bf16(w_cross_o[l]),
                                h, g_cross[l], g_ffn2[l], batch)

        h, _ = _swiglu_block(h, u, w_ffn2_in_bf16, w_ffn2_out[l])
    return _rmsnorm(h, g_final, F32).reshape(batch, seq, d)
```

```python
import functools
import math

import jax
import jax.numpy as jnp
from jax import lax
from jax.experimental import pallas as pl
from jax.experimental.pallas import tpu as pltpu

F32 = jnp.float32
BF16 = jnp.bfloat16

D_MODEL = 4096
HEAD_DIM = 128
ATTN_WIDTH = D_MODEL // 2
N_ATTN_HEADS = ATTN_WIDTH // HEAD_DIM
DILATED_CONFIGS = ((128, 1), (512, 4), (2048, 16))
N_DIL = len(DILATED_CONFIGS)
BLOCK = 128
POOL_WIDTH = D_MODEL - ATTN_WIDTH
POOL_WINDOWS = (2, 4, 8, 16)
POOL_GROUP = POOL_WIDTH // len(POOL_WINDOWS)
QKV_WIDTH = 3 * ATTN_WIDTH
ATTN_IN = N_DIL * QKV_WIDTH
D_FF = ((8 * D_MODEL // 3 + 255) // 256) * 256
CROSS_HEADS = 4
CROSS_DIM = 128
CROSS_WIDTH = CROSS_HEADS * CROSS_DIM
ROPE_THETA = 10000.0
EPS = 1e-6
NEG_INF = -1e30
LOG2E = math.log2(math.e)

V7X_LANES = 128
V7X_BF16_SUBLANES = 16
V7X_VMEM_BYTES = 64 * 1024 * 1024
V7X_VMEM_REQUEST_CAP = V7X_VMEM_BYTES - 6 * 1024 * 1024
VMEM_ELEMENTWISE_TEMPORARIES = 8 * 1024 * 1024

POOL_HALO = max(POOL_WINDOWS)
MAX_DIL = max(d for _, d in DILATED_CONFIGS)
ATTN_TILE = BLOCK * MAX_DIL


def _params(n_grid, vmem_bytes):
    return pltpu.CompilerParams(
        dimension_semantics=("arbitrary",) * n_grid,
        vmem_limit_bytes=int(min(vmem_bytes, V7X_VMEM_REQUEST_CAP)),
    )


def _nbytes(shape, dtype):
    return math.prod(shape) * jnp.dtype(dtype).itemsize


def _weight_vmem(block_shape, dtype):
    cast_copy = 0 if dtype == BF16 else _nbytes(block_shape, BF16)
    return 2 * _nbytes(block_shape, dtype) + cast_copy


def _resident(block_shape, index_map):
    return pl.BlockSpec(block_shape, index_map, pipeline_mode=pl.Buffered(1))


def _dot(a, b):
    return jnp.dot(a, b, preferred_element_type=F32)


def _dot_nt(a, b):
    return lax.dot_general(a, b, (((1,), (1,)), ((), ())), preferred_element_type=F32)


def _rmsnorm_kernel(x_ref, g_ref, *rest):
    x = x_ref[...]
    ms = jnp.mean(x * x, axis=-1, keepdims=True)
    if len(rest) == 1:
        o_ref, = rest
    else:
        pos_ref, inv_ref, sign_ref, o_ref, cos_ref, sin_ref = rest
        ang = pos_ref[...].astype(F32) * inv_ref[...]
        cos_ref[...] = jnp.cos(ang)
        sin_ref[...] = jnp.sin(ang) * sign_ref[...]
    o_ref[...] = (x * lax.rsqrt(ms + EPS) * g_ref[...]).astype(o_ref.dtype)


def _rmsnorm(x, g, out_dtype, rope_positions=None, tm=256):
    m, d = x.shape
    vmem = 2 * tm * d * (4 + jnp.dtype(out_dtype).itemsize) + 4 * tm * d * 4
    in_specs = [pl.BlockSpec((tm, d), lambda i: (i, 0)), pl.BlockSpec((1, d), lambda i: (0, 0))]
    out_specs = [pl.BlockSpec((tm, d), lambda i: (i, 0))]
    out_shape = [jax.ShapeDtypeStruct((m, d), out_dtype)]
    args = [x, g.reshape(1, d)]
    if rope_positions is not None:
        half = HEAD_DIM // 2
        inv = 1.0 / (ROPE_THETA ** (jnp.arange(0, HEAD_DIM, 2, dtype=F32) / HEAD_DIM))
        inv = jnp.concatenate([inv, inv]).reshape(1, HEAD_DIM)
        sign = jnp.concatenate([-jnp.ones((half,), F32), jnp.ones((half,), F32)]).reshape(1, HEAD_DIM)
        row = pl.BlockSpec((1, HEAD_DIM), lambda i: (0, 0))
        tab = pl.BlockSpec((tm, HEAD_DIM), lambda i: (i, 0))
        in_specs += [pl.BlockSpec((tm, 1), lambda i: (i, 0)), row, row]
        out_specs += [tab, tab]
        out_shape += [jax.ShapeDtypeStruct((m, HEAD_DIM), F32)] * 2
        args += [rope_positions.reshape(m, 1), inv, sign]
        vmem += 16 * tm * HEAD_DIM * 4
    outs = pl.pallas_call(
        _rmsnorm_kernel,
        grid=(m // tm,),
        in_specs=in_specs,
        out_specs=out_specs,
        out_shape=out_shape,
        compiler_params=_params(1, vmem),
        name="rmsnorm",
    )(*args)
    return outs[0] if rope_positions is None else outs


def _ffn_in_kernel(u_ref, wa_ref, wb_ref, *rest):
    n_jobs = (len(rest) - 1) // 2
    srcs, o_ref, dsts = rest[:n_jobs], rest[n_jobs], rest[n_jobs + 1:]
    half = o_ref.shape[1] // 2
    for c in range(2):
        sl = slice(c * half, (c + 1) * half)
        w = jnp.concatenate([wa_ref[:, sl].astype(BF16), wb_ref[:, sl].astype(BF16)], axis=1)
        ab = _dot(u_ref[...], w)
        a, b = ab[:, :half], ab[:, half:]
        o_ref[:, sl] = (a * jax.nn.sigmoid(a) * b).astype(o_ref.dtype)
    for src, dst in zip(srcs, dsts):
        dst[...] = src[...].astype(dst.dtype)


def _ffn_in(u, w_in, cast_jobs=(), tm=2048, tf=256):
    m, d = u.shape
    nf = D_FF // tf
    n_steps = (m // tm) * nf
    lhs_spec, lhs_buffers = (pl.BlockSpec, 2) if w_in.dtype == BF16 else (_resident, 1)
    vmem = (lhs_buffers * _nbytes((tm, d), BF16) + 2 * _weight_vmem((d, tf), w_in.dtype)
            + 2 * _nbytes((tm, tf), BF16) + 6 * _nbytes((tm, tf), F32))
    job_specs, job_shapes = [], []
    for w in cast_jobs:
        rows = next(r for r in range(V7X_BF16_SUBLANES, w.shape[0] + 1, V7X_BF16_SUBLANES)
                    if w.shape[0] % r == 0 and w.shape[0] // r <= n_steps)
        n_blocks = w.shape[0] // rows
        job_specs.append(pl.BlockSpec(
            (rows, w.shape[1]), lambda i, j, n_blocks=n_blocks: (jnp.minimum(i * nf + j, n_blocks - 1), 0)))
        job_shapes.append(jax.ShapeDtypeStruct(w.shape, BF16))
        vmem += 2 * _nbytes((rows, w.shape[1]), F32) + 2 * _nbytes((rows, w.shape[1]), BF16)
    outs = pl.pallas_call(
        _ffn_in_kernel,
        grid=(m // tm, nf),
        in_specs=[lhs_spec((tm, d), lambda i, j: (i, 0)),
                  pl.BlockSpec((d, tf), lambda i, j: (0, j)),
                  pl.BlockSpec((d, tf), lambda i, j: (0, j + nf))] + job_specs,
        out_specs=[pl.BlockSpec((tm, tf), lambda i, j: (i, j))] + job_specs,
        out_shape=[jax.ShapeDtypeStruct((m, D_FF), BF16)] + job_shapes,
        compiler_params=_params(2, vmem),
        name="ffn_in",
    )(u, w_in, w_in, *cast_jobs)
    return outs[0], outs[1:]


def _emit_prenorm(out, gain_ref, hg_ref, ssq_ref):
    hg_ref[...] = (out * gain_ref[...]).astype(hg_ref.dtype)
    ssq_ref[...] += jnp.broadcast_to(jnp.sum(out * out, axis=1, keepdims=True), ssq_ref.shape)


def _init_prenorm(ssq_ref):
    @pl.when(pl.program_id(1) == 0)
    def _():
        ssq_ref[...] = jnp.zeros_like(ssq_ref)


def _row_scale(ssq_ref, d):
    return lax.rsqrt(ssq_ref[:, :1] * (1.0 / d) + EPS)


def _prenorm_specs(m, n, tm, tn):
    specs = [pl.BlockSpec((tm, tn), lambda i, j: (i, j)), pl.BlockSpec((tm, V7X_LANES), lambda i, j: (i, 0))]
    shapes = [jax.ShapeDtypeStruct((m, n), BF16), jax.ShapeDtypeStruct((m, V7X_LANES), F32)]
    return specs, shapes


def _column_halves(ref):
    half = ref.shape[1] // 2
    return [slice(0, half), slice(half, 2 * half)]


def _proj_res_kernel(lhs_ref, w_ref, res_ref, *rest, scale):
    prenorm = len(rest) > 1
    if prenorm:
        gain_ref, o_ref, hg_ref, ssq_ref = rest
        _init_prenorm(ssq_ref)
    else:
        o_ref, = rest
    for sl in _column_halves(o_ref):
        acc = _dot(lhs_ref[...], w_ref[:, sl].astype(BF16))
        out = res_ref[:, sl] + scale * acc
        o_ref[:, sl] = out
        if prenorm:
            _emit_prenorm(out, gain_ref.at[:, sl], hg_ref.at[:, sl], ssq_ref)


def _proj_res(lhs, w, res, scale, tm, tn, next_gain=None):
    m, k = lhs.shape
    n = w.shape[1]
    vmem = 2 * _nbytes((tm, k), BF16) + _weight_vmem((k, tn), w.dtype) + 8 * _nbytes((tm, tn), F32)
    in_specs = [pl.BlockSpec((tm, k), lambda i, j: (i, 0)),
                pl.BlockSpec((k, tn), lambda i, j: (0, j)),
                pl.BlockSpec((tm, tn), lambda i, j: (i, j))]
    out_specs = [pl.BlockSpec((tm, tn), lambda i, j: (i, j))]
    out_shape = [jax.ShapeDtypeStruct((m, n), F32)]
    args = [lhs, w, res]
    if next_gain is not None:
        in_specs.append(pl.BlockSpec((1, tn), lambda i, j: (0, j)))
        args.append(next_gain.reshape(1, n))
        specs, shapes = _prenorm_specs(m, n, tm, tn)
        out_specs += specs
        out_shape += shapes
    outs = pl.pallas_call(
        functools.partial(_proj_res_kernel, scale=scale),
        grid=(m // tm, n // tn),
        in_specs=in_specs,
        out_specs=out_specs,
        out_shape=out_shape,
        compiler_params=_params(2, vmem),
        name="proj_residual",
    )(*args)
    return outs[0] if next_gain is None else outs


def _proj_kernel(lhs_ref, w_ref, *rest):
    acc = _dot(lhs_ref[...], w_ref[...].astype(BF16))
    if len(rest) == 2:
        acc = acc * _row_scale(rest[0], lhs_ref.shape[1])
    rest[-1][...] = acc.astype(rest[-1].dtype)


def _proj(lhs, w, tm, tn, out_dtype=BF16, col0=0, n=None, ssq=None):
    m, k = lhs.shape
    n = w.shape[1] if n is None else n
    col_blk0 = col0 // tn
    vmem = 2 * _nbytes((tm, k), BF16) + _weight_vmem((k, tn), w.dtype) + 6 * _nbytes((tm, tn), F32)
    in_specs = [pl.BlockSpec((tm, k), lambda i, j: (i, 0)),
                pl.BlockSpec((k, tn), lambda i, j: (0, j + col_blk0))]
    args = [lhs, w]
    if ssq is not None:
        in_specs.append(pl.BlockSpec((tm, V7X_LANES), lambda i, j: (i, 0)))
        args.append(ssq)
    return pl.pallas_call(
        _proj_kernel,
        grid=(m // tm, n // tn),
        in_specs=in_specs,
        out_specs=pl.BlockSpec((tm, tn), lambda i, j: (i, j)),
        out_shape=jax.ShapeDtypeStruct((m, n), out_dtype),
        compiler_params=_params(2, vmem),
        name="proj",
    )(*args)


def _attn_blocks(ref, sl):
    return ref[:, :, sl].reshape(MAX_DIL, BLOCK, HEAD_DIM)


def _attn_prev_blocks(first_ref, cur, sl, dil):
    nq = MAX_DIL // dil
    first = first_ref[:, :, sl]
    if nq == 1:
        return first
    cur4 = cur.reshape(dil, nq, BLOCK, HEAD_DIM)
    return jnp.concatenate([first[:, None], cur4[:, :-1]], axis=1).reshape(MAX_DIL, BLOCK, HEAD_DIM)


def _attn_token_rows(g, dil):
    r, i = divmod(g, MAX_DIL // dil)
    return pl.ds(i * BLOCK * dil + r, BLOCK, stride=dil)


def _attn_masks(dil, tile_has_prev):
    shape = (MAX_DIL, BLOCK, BLOCK)
    blk = lax.broadcasted_iota(jnp.int32, shape, 0)
    qi = lax.broadcasted_iota(jnp.int32, shape, 1)
    kj = lax.broadcasted_iota(jnp.int32, shape, 2)
    has_prev = jnp.logical_or(tile_has_prev, blk % (MAX_DIL // dil) != 0)
    return jnp.logical_and(kj >= qi, has_prev), kj <= qi


def _attn_probs(q_ref, kp_ref, kc_ref, sl, dil, masks):
    mask_prev, mask_cur = masks
    scale = HEAD_DIM ** -0.5
    q = _attn_blocks(q_ref, sl)
    kc = _attn_blocks(kc_ref, sl)
    kp = _attn_prev_blocks(kp_ref, kc, sl, dil)
    sp = jnp.einsum("gqd,gkd->gqk", q, kp, preferred_element_type=F32)
    sc = jnp.einsum("gqd,gkd->gqk", q, kc, preferred_element_type=F32)
    sp = jnp.where(mask_prev, sp, NEG_INF)
    sc = jnp.where(mask_cur, sc, NEG_INF)
    m = jnp.maximum(jnp.max(sp, axis=2, keepdims=True), jnp.max(sc, axis=2, keepdims=True))
    pp = jnp.exp2((sp - m) * (scale * LOG2E))
    pc = jnp.exp2((sc - m) * (scale * LOG2E))
    l = jnp.sum(pp, axis=2, keepdims=True) + jnp.sum(pc, axis=2, keepdims=True)
    return pp.astype(BF16), pc.astype(BF16), l, m * scale + jnp.log(l)


def _attn_output(pp, pc, l, vp_ref, vc_ref, sl, dil, o_scr, slot):
    vc = _attn_blocks(vc_ref, sl)
    vp = _attn_prev_blocks(vp_ref, vc, sl, dil)
    o = (jnp.einsum("gqk,gkd->gqd", pp, vp, preferred_element_type=F32)
         + jnp.einsum("gqk,gkd->gqd", pc, vc, preferred_element_type=F32)) / l
    for g in range(MAX_DIL):
        o_scr[slot, _attn_token_rows(g, dil), :] = o[g]


def _attn_scatter_lse(lse_tile, dil, lse_scr):
    for g in range(MAX_DIL):
        lse_scr[_attn_token_rows(g, dil), :] = lse_tile[g]


def _attn_accumulate_lse(lse_scr, lse_ref, first):
    @pl.when(first)
    def _():
        lse_ref[...] = lse_scr[...]

    @pl.when(jnp.logical_not(first))
    def _():
        lse_ref[...] += lse_scr[...]


def _mix_qkv_kernel(u_ref, ssq_ref, wq_ref, wk_ref, wv_ref, cos_ref, sin_ref, *rest, tn, dil,
                    attn_dil, tiles_per_batch, tiles_per_attn_tile, attn_heads):
    if attn_dil is None:
        q_ref, k_ref, v_ref, scr_ref, tab_ref = rest
    else:
        (aq_ref, akp_ref, akc_ref, avp_ref, avc_ref, q_ref, k_ref, v_ref, ao_ref, alse_ref,
         scr_ref, tab_ref, ao_scr, alse_scr) = rest
        tile_has_prev, first_head = _attn_host_unit(tiles_per_batch, tiles_per_attn_tile, attn_heads)
    tm, d_in = u_ref.shape
    rows = tm // dil

    def regroup(slab, val):
        if dil == 1:
            return [val]
        scr_ref[slab] = val
        return [scr_ref[slab, pl.ds(r, rows, stride=dil), :] for r in range(dil)]

    @pl.when(pl.program_id(1) == 0)
    def _():
        rs = _row_scale(ssq_ref, d_in)
        tables = (cos_ref[...] * rs, sin_ref[...] * rs, jnp.broadcast_to(rs, (tm, HEAD_DIM)))
        for t, table in enumerate(tables):
            for r, piece in enumerate(regroup(0, table)):
                tab_ref[t, r] = piece

    heads = tn // HEAD_DIM
    for part, (w_ref, o_ref) in enumerate(((wq_ref, q_ref), (wk_ref, k_ref), (wv_ref, v_ref))):
        acc = _dot(u_ref[...], w_ref[...].astype(BF16))
        if part == 0 and attn_dil is not None:
            probs = _attn_host_probs(aq_ref, akp_ref, akc_ref, attn_dil, tile_has_prev, attn_heads)
        for h in range(heads):
            pieces = regroup(part * heads + h, acc[:, h * HEAD_DIM:(h + 1) * HEAD_DIM])
            for r, piece in enumerate(pieces):
                if o_ref is v_ref:
                    piece = piece * tab_ref[2, r]
                else:
                    piece = piece * tab_ref[0, r] + pltpu.roll(piece, HEAD_DIM // 2, axis=1) * tab_ref[1, r]
                o_ref[r, :, h * HEAD_DIM:(h + 1) * HEAD_DIM] = piece.astype(o_ref.dtype)

    if attn_dil is not None:
        _attn_host_outputs(probs, first_head, avp_ref, avc_ref, attn_dil, ao_ref, ao_scr, alse_scr)
        _attn_accumulate_lse(alse_scr, alse_ref, first_head == 0)


def _attn_unit_specs(dil, unit, heads):
    rows = ATTN_TILE // dil

    def cur_map(*idx):
        b, t, h = unit(*idx)
        return b, 0, t, h

    def prev_map(*idx):
        b, t, h = unit(*idx)
        return b, 0, jnp.maximum(t * (rows // BLOCK) - 1, 0), h

    cur = pl.BlockSpec((None, dil, rows, heads * HEAD_DIM), cur_map)
    prev = pl.BlockSpec((None, dil, BLOCK, heads * HEAD_DIM), prev_map)
    return [cur, prev, cur, prev, cur]


def _mix_qkv(u, ssq, w, cos, sin, group, dil, batch, attn_qkv=None, attn_dil=None, tm=1024, tn=256):
    m, k = u.shape
    seq = m // batch
    tiles_per_seq = seq // tm
    part_blocks = ATTN_WIDTH // tn

    def w_spec(part):
        col_blk0 = (group * 3 + part) * part_blocks
        return pl.BlockSpec((k, tn), lambda i, j: (0, j + col_blk0))

    out_spec = pl.BlockSpec((None, dil, tm // dil, tn),
                            lambda i, j: (i // tiles_per_seq, 0, i % tiles_per_seq, j))
    out_shape = jax.ShapeDtypeStruct((batch, dil, seq // dil, ATTN_WIDTH), BF16)
    vmem = (2 * _nbytes((tm, k), BF16) + 3 * _weight_vmem((k, tn), w.dtype)
            + 3 * 6 * _nbytes((tm, tn), F32) + 10 * _nbytes((tm, HEAD_DIM), F32))
    in_specs = [pl.BlockSpec((tm, k), lambda i, j: (i, 0)),
                pl.BlockSpec((tm, V7X_LANES), lambda i, j: (i, 0)),
                w_spec(0), w_spec(1), w_spec(2),
                pl.BlockSpec((tm, HEAD_DIM), lambda i, j: (i, 0)),
                pl.BlockSpec((tm, HEAD_DIM), lambda i, j: (i, 0))]
    out_specs, out_shapes = [out_spec] * 3, [out_shape] * 3
    scratch = [pltpu.VMEM((3 * tn // HEAD_DIM, tm, HEAD_DIM), F32),
               pltpu.VMEM((3, dil, tm // dil, HEAD_DIM), F32)]
    args = [u, ssq, w, w, w, cos, sin]
    host = _attn_host(attn_qkv, attn_dil, m, seq, tm, part_blocks)
    if attn_qkv is not None:
        in_specs += host["in_specs"]
        args += host["args"]
        out_specs += host["out_specs"]
        out_shapes += host["out_shapes"]
        scratch += host["scratch"]
        vmem += host["vmem"]
    return pl.pallas_call(
        functools.partial(_mix_qkv_kernel, tn=tn, dil=dil, attn_dil=attn_dil, **host["kwargs"]),
        grid=(m // tm, part_blocks),
        in_specs=in_specs,
        out_specs=out_specs,
        out_shape=out_shapes,
        scratch_shapes=scratch,
        compiler_params=_params(2, vmem),
        name=f"mix_qkv_{dil}",
    )(*args)


def _attn_host(attn_qkv, attn_dil, m, seq, tm, col_steps):
    tiles_per_seq = seq // tm
    tiles_per_attn_tile = ATTN_TILE // tm
    heads = N_ATTN_HEADS // (tiles_per_attn_tile * col_steps)
    kwargs = dict(tiles_per_batch=tiles_per_seq, tiles_per_attn_tile=tiles_per_attn_tile, attn_heads=heads)
    if attn_qkv is None:
        return dict(kwargs=kwargs)
    assert heads * tiles_per_attn_tile * col_steps == N_ATTN_HEADS and seq % ATTN_TILE == 0
    attn_tiles_per_seq = seq // ATTN_TILE

    def unit(i, j):
        in_seq = i % tiles_per_seq
        return (i // tiles_per_seq, in_seq // tiles_per_attn_tile,
                (in_seq % tiles_per_attn_tile) * col_steps + j)

    def row_tile(i, j):
        b, t, _ = unit(i, j)
        return b * attn_tiles_per_seq + t

    aq, ak, av = attn_qkv
    return dict(
        kwargs=kwargs,
        in_specs=_attn_unit_specs(attn_dil, unit, heads),
        args=[aq, ak, ak, av, av],
        out_specs=[pl.BlockSpec((ATTN_TILE, heads * HEAD_DIM), lambda i, j: (row_tile(i, j), unit(i, j)[2])),
                   pl.BlockSpec((ATTN_TILE, V7X_LANES), lambda i, j: (row_tile(i, j), 0))],
        out_shapes=[jax.ShapeDtypeStruct((m, ATTN_WIDTH), BF16),
                    jax.ShapeDtypeStruct((m, V7X_LANES), F32)],
        scratch=[pltpu.VMEM((heads, ATTN_TILE, HEAD_DIM), F32), pltpu.VMEM((ATTN_TILE, V7X_LANES), F32)],
        vmem=heads * (8 * _nbytes((attn_dil, ATTN_TILE // attn_dil, HEAD_DIM), BF16)
                      + 4 * _nbytes((ATTN_TILE, HEAD_DIM), BF16) + 8 * _nbytes((ATTN_TILE, BLOCK), F32))
        + 4 * _nbytes((ATTN_TILE, V7X_LANES), F32))


def _attn_host_unit(tiles_per_batch, tiles_per_attn_tile, attn_heads):
    tile_has_prev = (pl.program_id(0) % tiles_per_batch) >= tiles_per_attn_tile
    group = (pl.program_id(0) % tiles_per_attn_tile) * pl.num_programs(1) + pl.program_id(1)
    return tile_has_prev, group * attn_heads


def _attn_host_probs(aq_ref, akp_ref, akc_ref, attn_dil, tile_has_prev, attn_heads):
    masks = _attn_masks(attn_dil, tile_has_prev)
    return [_attn_probs(aq_ref, akp_ref, akc_ref, slice(h * HEAD_DIM, (h + 1) * HEAD_DIM), attn_dil, masks)
            for h in range(attn_heads)]


def _attn_host_outputs(probs, first_head, avp_ref, avc_ref, attn_dil, ao_ref, ao_scr, alse_scr):
    lane = lax.broadcasted_iota(jnp.int32, (MAX_DIL, BLOCK, V7X_LANES), 2)
    lse_tile = jnp.zeros((MAX_DIL, BLOCK, V7X_LANES), F32)
    for h, (pp, pc, l, lse) in enumerate(probs):
        sl = slice(h * HEAD_DIM, (h + 1) * HEAD_DIM)
        _attn_output(pp, pc, l, avp_ref, avc_ref, sl, attn_dil, ao_scr, h)
        ao_ref[:, sl] = ao_scr[h].astype(ao_ref.dtype)
        lse_tile = jnp.where(lane == first_head + h, lse, lse_tile)
    _attn_scatter_lse(lse_tile, attn_dil, alse_scr)


def _pool_proj_kernel(u_ref, ssq_ref, w_ref, aq_ref, akp_ref, akc_ref, avp_ref, avc_ref,
                      z_ref, ao_ref, alse_ref, ao_scr, alse_scr, *, attn_dil, tiles_per_batch,
                      tiles_per_attn_tile, attn_heads):
    tile_has_prev, first_head = _attn_host_unit(tiles_per_batch, tiles_per_attn_tile, attn_heads)
    rs = _row_scale(ssq_ref, u_ref.shape[1])
    probs = None
    for sl in _column_halves(z_ref):
        z_ref[:, sl] = _dot(u_ref[...], w_ref[:, sl].astype(BF16)) * rs
        if probs is None:
            probs = _attn_host_probs(aq_ref, akp_ref, akc_ref, attn_dil, tile_has_prev, attn_heads)
    _attn_host_outputs(probs, first_head, avp_ref, avc_ref, attn_dil, ao_ref, ao_scr, alse_scr)
    _attn_accumulate_lse(alse_scr, alse_ref, first_head == 0)


def _pool_proj_attn(u, ssq, w, col0, attn_qkv, attn_dil, batch, tm=1024, tn=512):
    m, k = u.shape
    seq = m // batch
    col_steps = POOL_WIDTH // tn
    col_blk0 = col0 // tn
    host = _attn_host(attn_qkv, attn_dil, m, seq, tm, col_steps)
    vmem = (2 * _nbytes((tm, k), BF16) + _weight_vmem((k, tn), w.dtype) + 6 * _nbytes((tm, tn), F32)
            + host["vmem"])
    return pl.pallas_call(
        functools.partial(_pool_proj_kernel, attn_dil=attn_dil, **host["kwargs"]),
        grid=(m // tm, col_steps),
        in_specs=[pl.BlockSpec((tm, k), lambda i, j: (i, 0)),
                  pl.BlockSpec((tm, V7X_LANES), lambda i, j: (i, 0)),
                  pl.BlockSpec((k, tn), lambda i, j: (0, j + col_blk0))] + host["in_specs"],
        out_specs=[pl.BlockSpec((tm, tn), lambda i, j: (i, j))] + host["out_specs"],
        out_shape=[jax.ShapeDtypeStruct((m, POOL_WIDTH), F32)] + host["out_shapes"],
        scratch_shapes=host["scratch"],
        compiler_params=_params(2, vmem),
        name="pool_proj_attn",
    )(u, ssq, w, *host["args"])


def _combine_kernel(o0_ref, o1_ref, o2_ref, l0_ref, l1_ref, l2_ref, out_ref):
    l0, l1, l2 = l0_ref[...], l1_ref[...], l2_ref[...]
    m = jnp.maximum(jnp.maximum(l0, l1), l2)
    e0, e1, e2 = jnp.exp(l0 - m), jnp.exp(l1 - m), jnp.exp(l2 - m)
    den = e0 + e1 + e2
    w0, w1, w2 = e0 / den, e1 / den, e2 / den
    for h in range(N_ATTN_HEADS):
        sl = slice(h * HEAD_DIM, (h + 1) * HEAD_DIM)
        acc = (w0[:, h:h + 1] * o0_ref[:, sl].astype(F32)
               + w1[:, h:h + 1] * o1_ref[:, sl].astype(F32)
               + w2[:, h:h + 1] * o2_ref[:, sl].astype(F32))
        out_ref[:, sl] = acc.astype(out_ref.dtype)


def _pool_kernel(zm_ref, zh_ref, wp_ref, scale_ref, o_ref, *, tp):
    t = pl.program_id(1)
    pos = t * tp + lax.broadcasted_iota(jnp.int32, (tp, 1), 0)
    for g, win in enumerate(POOL_WINDOWS):
        sl = slice(g * POOL_GROUP, (g + 1) * POOL_GROUP)
        main = zm_ref[:, sl]
        halo = jnp.where(t > 0, zh_ref[:, sl], 0.0)
        s = jnp.concatenate([halo, main], axis=0)
        shift = 1
        while shift < win:
            s = s + pltpu.roll(s, shift, axis=0)
            shift *= 2
        cnt = jnp.minimum(pos + 1, win).astype(F32)
        y = s[POOL_HALO:] / cnt - main
        yo = _dot(y.astype(BF16), wp_ref[g].astype(BF16))
        o_ref[:, sl] = (yo * scale_ref[:, sl]).astype(o_ref.dtype)


def _combine_pool_kernel(o0_ref, o1_ref, o2_ref, l0_ref, l1_ref, l2_ref, zm_ref, zh_ref, wp_ref,
                         scale_ref, attn_ref, pool_ref, *, tp):
    _combine_kernel(o0_ref, o1_ref, o2_ref, l0_ref, l1_ref, l2_ref, attn_ref)
    _pool_kernel(zm_ref, zh_ref, wp_ref, scale_ref, pool_ref, tp=tp)


def _combine_and_pool(outs, lses, zp, w_pool, pool_scale, batch, tp=512):
    m = zp.shape[0]
    seq = m // batch
    tiles_per_seq = seq // tp
    halo_blocks = tp // POOL_HALO

    def rows(bi, t):
        return bi * tiles_per_seq + t

    o_spec = pl.BlockSpec((tp, ATTN_WIDTH), lambda bi, t: (rows(bi, t), 0))
    l_spec = pl.BlockSpec((tp, V7X_LANES), lambda bi, t: (rows(bi, t), 0))
    vmem = (8 * _nbytes((tp, ATTN_WIDTH), BF16) + 16 * _nbytes((tp, V7X_LANES), F32)
            + 6 * _nbytes((tp, POOL_WIDTH), F32) + 2 * _nbytes(w_pool.shape, F32)
            + 12 * _nbytes((tp, POOL_GROUP), F32) + 2 * VMEM_ELEMENTWISE_TEMPORARIES)
    return pl.pallas_call(
        functools.partial(_combine_pool_kernel, tp=tp),
        grid=(batch, tiles_per_seq),
        in_specs=[o_spec] * 3 + [l_spec] * 3 + [
            pl.BlockSpec((tp, POOL_WIDTH), lambda bi, t: (rows(bi, t), 0)),
            pl.BlockSpec((POOL_HALO, POOL_WIDTH),
                         lambda bi, t: (jnp.maximum(rows(bi, t) * halo_blocks - 1, 0), 0)),
            pl.BlockSpec(w_pool.shape, lambda bi, t: (0, 0, 0)),
            pl.BlockSpec((1, POOL_WIDTH), lambda bi, t: (0, 0))],
        out_specs=[o_spec, pl.BlockSpec((tp, POOL_WIDTH), lambda bi, t: (rows(bi, t), 0))],
        out_shape=[jax.ShapeDtypeStruct((m, ATTN_WIDTH), BF16),
                   jax.ShapeDtypeStruct((m, POOL_WIDTH), BF16)],
        compiler_params=_params(2, vmem),
        name="combine_pool",
    )(*outs, *lses, zp, zp, w_pool, pool_scale.reshape(1, POOL_WIDTH))


def _mix_out_kernel(a_ref, p_ref, wa_ref, wp_ref, res_ref, o_ref):
    for sl in _column_halves(o_ref):
        acc = (_dot(a_ref[...], wa_ref[:, sl].astype(BF16))
               + _dot(p_ref[...], wp_ref[:, sl].astype(BF16)))
        o_ref[:, sl] = res_ref[:, sl] + acc


def _mix_out(o_attn, o_pool, w, res, tm=1024, tn=1024):
    m = o_attn.shape[0]
    n = w.shape[1]
    ka, kp = o_attn.shape[1], o_pool.shape[1]
    assert ka == kp
    vmem = (2 * _nbytes((tm, ka + kp), BF16) + _weight_vmem((ka + kp, tn), w.dtype)
            + 8 * _nbytes((tm, tn), F32))
    return pl.pallas_call(
        _mix_out_kernel,
        grid=(m // tm, n // tn),
        in_specs=[pl.BlockSpec((tm, ka), lambda i, j: (i, 0)),
                  pl.BlockSpec((tm, kp), lambda i, j: (i, 0)),
                  pl.BlockSpec((ka, tn), lambda i, j: (0, j)),
                  pl.BlockSpec((kp, tn), lambda i, j: (1, j)),
                  pl.BlockSpec((tm, tn), lambda i, j: (i, j))],
        out_specs=pl.BlockSpec((tm, tn), lambda i, j: (i, j)),
        out_shape=jax.ShapeDtypeStruct((m, n), F32),
        compiler_params=_params(2, vmem),
        name="mix_out",
    )(o_attn, o_pool, w, w, res)


def _cast_kernel(x_ref, o_ref):
    o_ref[...] = x_ref[...].astype(o_ref.dtype)


def _cast_bf16(w, rows=512):
    k, n = w.shape
    rows = min(rows, k)
    return pl.pallas_call(
        _cast_kernel,
        grid=(k // rows,),
        in_specs=[pl.BlockSpec((rows, n), lambda i: (i, 0))],
        out_specs=pl.BlockSpec((rows, n), lambda i: (i, 0)),
        out_shape=jax.ShapeDtypeStruct((k, n), BF16),
        compiler_params=_params(1, 8 * _nbytes((rows, n), F32)),
        name="cast_bf16",
    )(w)


def _rms_rows(x, gain_ref):
    ms = jnp.mean(x * x, axis=-1, keepdims=True)
    return (x * lax.rsqrt(ms + EPS) * gain_ref[...]).astype(BF16)


def _cross_kernel(kv_ref, wq_ref, wo_ref, res_ref, gain_in_ref, gain_out_ref, o_ref, u_ref):
    scale = CROSS_DIM ** -0.5
    q = _dot(_rms_rows(res_ref[...], gain_in_ref), wq_ref[...]).astype(BF16)
    heads = []
    for h in range(CROSS_HEADS):
        sl = slice(h * CROSS_DIM, (h + 1) * CROSS_DIM)
        k = kv_ref[:, sl]
        v = kv_ref[:, CROSS_WIDTH + h * CROSS_DIM:CROSS_WIDTH + (h + 1) * CROSS_DIM]
        s = _dot_nt(q[:, sl], k)
        m = jnp.max(s, axis=1, keepdims=True)
        p = jnp.exp2((s - m) * (scale * LOG2E))
        l = jnp.sum(p, axis=1, keepdims=True)
        heads.append((_dot(p.astype(BF16), v) / l).astype(BF16))
    out = res_ref[...] + _dot(jnp.concatenate(heads, axis=1), wo_ref[...])
    o_ref[...] = out
    u_ref[...] = _rms_rows(out, gain_out_ref)


def _cross_attention(kv, w_q, w_o, res, gain, next_gain, batch, ts=256):
    m, d = res.shape
    mem_len = kv.shape[0] // batch
    steps_per_batch = (m // batch) // ts
    vmem = (2 * _nbytes((ts, d), BF16) + 4 * _nbytes((ts, d), F32) + _nbytes(w_q.shape, BF16)
            + _nbytes(w_o.shape, BF16) + 6 * _nbytes((ts, d), F32) + VMEM_ELEMENTWISE_TEMPORARIES)
    row_f32 = pl.BlockSpec((ts, d), lambda i: (i, 0))
    gain_spec = pl.BlockSpec((1, d), lambda i: (0, 0))
    return pl.pallas_call(
        _cross_kernel,
        grid=(m // ts,),
        in_specs=[pl.BlockSpec((mem_len, 2 * CROSS_WIDTH), lambda i: (i // steps_per_batch, 0)),
                  _resident(w_q.shape, lambda i: (0, 0)),
                  _resident(w_o.shape, lambda i: (0, 0)),
                  row_f32, gain_spec, gain_spec],
        out_specs=[row_f32, pl.BlockSpec((ts, d), lambda i: (i, 0))],
        out_shape=[jax.ShapeDtypeStruct((m, d), F32), jax.ShapeDtypeStruct((m, d), BF16)],
        compiler_params=_params(1, vmem),
        name="cross_attention",
    )(kv, w_q, w_o, res, gain.reshape(1, d), next_gain.reshape(1, d))


def _swiglu_block(h, u, w_in, w_out, next_gain=None, later_weights=()):
    gate, (w_out_bf16, *later_bf16) = _ffn_in(u, w_in, cast_jobs=(w_out,) + tuple(later_weights))
    return _proj_res(gate, w_out_bf16, h, 0.5, tm=512, tn=512, next_gain=next_gain), later_bf16


def kernel(x, mem, positions, g_ffn1, w_ffn1_in, w_ffn1_out, g_mix, w_mix_in, w_pool, pool_scale, w_mix_out, g_cross, g_mem, w_cross_q, w_cross_kv, w_cross_o, g_ffn2, w_ffn2_in, w_ffn2_out, g_final):
    batch, seq, d = x.shape
    m = batch * seq
    depth = g_ffn1.shape[0]
    h = x.reshape(m, d)
    mem2 = mem.reshape(batch * mem.shape[1], d)
    for l in range(depth):
        if l == 0:
            u, cos, sin = _rmsnorm(h, g_ffn1[l], BF16, rope_positions=positions)
        else:
            u = _rmsnorm(h, g_ffn1[l], BF16)
        (h, hg, ssq), (w_mix_in_bf16, w_mix_out_bf16, w_ffn2_in_bf16) = _swiglu_block(
            h, u, w_ffn1_in[l], w_ffn1_out[l], next_gain=g_mix[l],
            later_weights=(w_mix_in[l], w_mix_out[l], w_ffn2_in[l]))

        outs, lses = [], []
        qkv, qkv_dil = None, None
        for group, (_, dil) in enumerate(DILATED_CONFIGS):
            res = _mix_qkv(hg, ssq, w_mix_in_bf16, cos, sin, group, dil, batch,
                           attn_qkv=qkv, attn_dil=qkv_dil)
            if qkv is not None:
                outs.append(res[3])
                lses.append(res[4])
            qkv, qkv_dil = res[:3], dil
        zp, o, lse = _pool_proj_attn(hg, ssq, w_mix_in_bf16, ATTN_IN, qkv, qkv_dil, batch)
        outs.append(o)
        lses.append(lse)
        o_attn, o_pool = _combine_and_pool(outs, lses, zp, w_pool[l], pool_scale[l], batch)
        h = _mix_out(o_attn, o_pool, w_mix_out_bf16, h)

        um = _rmsnorm(mem2, g_mem[l], BF16)
        kv = _proj(um, w_cross_kv[l], tm=um.shape[0], tn=CROSS_WIDTH)
        h, u = _cross_attention(kv, _cast_bf16(w_cross_q[l]), _cast_bf16(w_cross_o[l]),
                                h, g_cross[l], g_ffn2[l], batch)

        h, _ = _swiglu_block(h, u, w_ffn2_in_bf16, w_ffn2_out[l])
    return _rmsnorm(h, g_final, F32).reshape(batch, seq, d)
```

```python
import functools
import math

import jax
import jax.numpy as jnp
from jax import lax
from jax.experimental import pallas as pl
from jax.experimental.pallas import tpu as pltpu

F32 = jnp.float32
BF16 = jnp.bfloat16

D_MODEL = 4096
HEAD_DIM = 128
ATTN_WIDTH = D_MODEL // 2
N_ATTN_HEADS = ATTN_WIDTH // HEAD_DIM
DILATED_CONFIGS = ((128, 1), (512, 4), (2048, 16))
N_DIL = len(DILATED_CONFIGS)
BLOCK = 128
POOL_WIDTH = D_MODEL - ATTN_WIDTH
POOL_WINDOWS = (2, 4, 8, 16)
POOL_GROUP = POOL_WIDTH // len(POOL_WINDOWS)
QKV_WIDTH = 3 * ATTN_WIDTH
ATTN_IN = N_DIL * QKV_WIDTH
D_FF = ((8 * D_MODEL // 3 + 255) // 256) * 256
CROSS_HEADS = 4
CROSS_DIM = 128
CROSS_WIDTH = CROSS_HEADS * CROSS_DIM
ROPE_THETA = 10000.0
EPS = 1e-6
NEG_INF = -1e30
LOG2E = math.log2(math.e)

V7X_LANES = 128
V7X_BF16_SUBLANES = 16
V7X_VMEM_BYTES = 64 * 1024 * 1024
V7X_VMEM_REQUEST_CAP = V7X_VMEM_BYTES - 6 * 1024 * 1024
VMEM_ELEMENTWISE_TEMPORARIES = 8 * 1024 * 1024

POOL_HALO = max(POOL_WINDOWS)
MAX_DIL = max(d for _, d in DILATED_CONFIGS)
ATTN_TILE = BLOCK * MAX_DIL


def _params(n_grid, vmem_bytes):
    return pltpu.CompilerParams(
        dimension_semantics=("arbitrary",) * n_grid,
        vmem_limit_bytes=int(min(vmem_bytes, V7X_VMEM_REQUEST_CAP)),
    )


def _nbytes(shape, dtype):
    return math.prod(shape) * jnp.dtype(dtype).itemsize


def _weight_vmem(block_shape, dtype):
    cast_copy = 0 if dtype == BF16 else _nbytes(block_shape, BF16)
    return 2 * _nbytes(block_shape, dtype) + cast_copy


def _resident(block_shape, index_map):
    return pl.BlockSpec(block_shape, index_map, pipeline_mode=pl.Buffered(1))


def _dot(a, b):
    return jnp.dot(a, b, preferred_element_type=F32)


def _dot_nt(a, b):
    return lax.dot_general(a, b, (((1,), (1,)), ((), ())), preferred_element_type=F32)


def _rmsnorm_kernel(x_ref, g_ref, *rest):
    x = x_ref[...]
    ms = jnp.mean(x * x, axis=-1, keepdims=True)
    if len(rest) == 1:
        o_ref, = rest
    else:
        pos_ref, inv_ref, sign_ref, o_ref, cos_ref, sin_ref = rest
        ang = pos_ref[...].astype(F32) * inv_ref[...]
        cos_ref[...] = jnp.cos(ang)
        sin_ref[...] = jnp.sin(ang) * sign_ref[...]
    o_ref[...] = (x * lax.rsqrt(ms + EPS) * g_ref[...]).astype(o_ref.dtype)


def _rmsnorm(x, g, out_dtype, rope_positions=None, tm=512):
    m, d = x.shape
    vmem = 2 * tm * d * (4 + jnp.dtype(out_dtype).itemsize) + 4 * tm * d * 4
    in_specs = [pl.BlockSpec((tm, d), lambda i: (i, 0)), pl.BlockSpec((1, d), lambda i: (0, 0))]
    out_specs = [pl.BlockSpec((tm, d), lambda i: (i, 0))]
    out_shape = [jax.ShapeDtypeStruct((m, d), out_dtype)]
    args = [x, g.reshape(1, d)]
    if rope_positions is not None:
        half = HEAD_DIM // 2
        inv = 1.0 / (ROPE_THETA ** (jnp.arange(0, HEAD_DIM, 2, dtype=F32) / HEAD_DIM))
        inv = jnp.concatenate([inv, inv]).reshape(1, HEAD_DIM)
        sign = jnp.concatenate([-jnp.ones((half,), F32), jnp.ones((half,), F32)]).reshape(1, HEAD_DIM)
        row = pl.BlockSpec((1, HEAD_DIM), lambda i: (0, 0))
        tab = pl.BlockSpec((tm, HEAD_DIM), lambda i: (i, 0))
        in_specs += [pl.BlockSpec((tm, 1), lambda i: (i, 0)), row, row]
        out_specs += [tab, tab]
        out_shape += [jax.ShapeDtypeStruct((m, HEAD_DIM), F32)] * 2
        args += [rope_positions.reshape(m, 1), inv, sign]
        vmem += 16 * tm * HEAD_DIM * 4
    outs = pl.pallas_call(
        _rmsnorm_kernel,
        grid=(m // tm,),
        in_specs=in_specs,
        out_specs=out_specs,
        out_shape=out_shape,
        compiler_params=_params(1, vmem),
        name="rmsnorm",
    )(*args)
    return outs[0] if rope_positions is None else outs


def _ffn_in_kernel(u_ref, wa_ref, wb_ref, *rest):
    n_jobs = (len(rest) - 1) // 2
    srcs, o_ref, dsts = rest[:n_jobs], rest[n_jobs], rest[n_jobs + 1:]
    half = o_ref.shape[1] // 2
    for c in range(2):
        sl = slice(c * half, (c + 1) * half)
        w = jnp.concatenate([wa_ref[:, sl].astype(BF16), wb_ref[:, sl].astype(BF16)], axis=1)
        ab = _dot(u_ref[...], w)
        a, b = ab[:, :half], ab[:, half:]
        o_ref[:, sl] = (a * jax.nn.sigmoid(a) * b).astype(o_ref.dtype)
    for src, dst in zip(srcs, dsts):
        dst[...] = src[...].astype(dst.dtype)


def _ffn_in(u, w_in, cast_jobs=(), tm=2048, tf=256):
    m, d = u.shape
    nf = D_FF // tf
    n_steps = (m // tm) * nf
    lhs_spec, lhs_buffers = (pl.BlockSpec, 2) if w_in.dtype == BF16 else (_resident, 1)
    vmem = (lhs_buffers * _nbytes((tm, d), BF16) + 2 * _weight_vmem((d, tf), w_in.dtype)
            + 2 * _nbytes((tm, tf), BF16) + 6 * _nbytes((tm, tf), F32))
    job_specs, job_shapes = [], []
    for w in cast_jobs:
        rows = next(r for r in range(V7X_BF16_SUBLANES, w.shape[0] + 1, V7X_BF16_SUBLANES)
                    if w.shape[0] % r == 0 and w.shape[0] // r <= n_steps)
        n_blocks = w.shape[0] // rows
        job_specs.append(pl.BlockSpec(
            (rows, w.shape[1]), lambda i, j, n_blocks=n_blocks: (jnp.minimum(i * nf + j, n_blocks - 1), 0)))
        job_shapes.append(jax.ShapeDtypeStruct(w.shape, BF16))
        vmem += 2 * _nbytes((rows, w.shape[1]), F32) + 2 * _nbytes((rows, w.shape[1]), BF16)
    outs = pl.pallas_call(
        _ffn_in_kernel,
        grid=(m // tm, nf),
        in_specs=[lhs_spec((tm, d), lambda i, j: (i, 0)),
                  pl.BlockSpec((d, tf), lambda i, j: (0, j)),
                  pl.BlockSpec((d, tf), lambda i, j: (0, j + nf))] + job_specs,
        out_specs=[pl.BlockSpec((tm, tf), lambda i, j: (i, j))] + job_specs,
        out_shape=[jax.ShapeDtypeStruct((m, D_FF), BF16)] + job_shapes,
        compiler_params=_params(2, vmem),
        name="ffn_in",
    )(u, w_in, w_in, *cast_jobs)
    return outs[0], outs[1:]


def _emit_prenorm(out, gain_ref, hg_ref, ssq_ref):
    hg_ref[...] = (out * gain_ref[...]).astype(hg_ref.dtype)
    ssq_ref[...] += jnp.broadcast_to(jnp.sum(out * out, axis=1, keepdims=True), ssq_ref.shape)


def _init_prenorm(ssq_ref):
    @pl.when(pl.program_id(1) == 0)
    def _():
        ssq_ref[...] = jnp.zeros_like(ssq_ref)


def _row_scale(ssq_ref, d):
    return lax.rsqrt(ssq_ref[:, :1] * (1.0 / d) + EPS)


def _prenorm_specs(m, n, tm, tn):
    specs = [pl.BlockSpec((tm, tn), lambda i, j: (i, j)), pl.BlockSpec((tm, V7X_LANES), lambda i, j: (i, 0))]
    shapes = [jax.ShapeDtypeStruct((m, n), BF16), jax.ShapeDtypeStruct((m, V7X_LANES), F32)]
    return specs, shapes


def _column_halves(ref):
    half = ref.shape[1] // 2
    return [slice(0, half), slice(half, 2 * half)]


def _proj_res_kernel(lhs_ref, w_ref, res_ref, *rest, scale):
    prenorm = len(rest) > 1
    if prenorm:
        gain_ref, o_ref, hg_ref, ssq_ref = rest
        _init_prenorm(ssq_ref)
    else:
        o_ref, = rest
    for sl in _column_halves(o_ref):
        acc = _dot(lhs_ref[...], w_ref[:, sl].astype(BF16))
        out = res_ref[:, sl] + scale * acc
        o_ref[:, sl] = out
        if prenorm:
            _emit_prenorm(out, gain_ref.at[:, sl], hg_ref.at[:, sl], ssq_ref)


def _proj_res(lhs, w, res, scale, tm, tn, next_gain=None):
    m, k = lhs.shape
    n = w.shape[1]
    vmem = 2 * _nbytes((tm, k), BF16) + _weight_vmem((k, tn), w.dtype) + 8 * _nbytes((tm, tn), F32)
    in_specs = [pl.BlockSpec((tm, k), lambda i, j: (i, 0)),
                pl.BlockSpec((k, tn), lambda i, j: (0, j)),
                pl.BlockSpec((tm, tn), lambda i, j: (i, j))]
    out_specs = [pl.BlockSpec((tm, tn), lambda i, j: (i, j))]
    out_shape = [jax.ShapeDtypeStruct((m, n), F32)]
    args = [lhs, w, res]
    if next_gain is not None:
        in_specs.append(pl.BlockSpec((1, tn), lambda i, j: (0, j)))
        args.append(next_gain.reshape(1, n))
        specs, shapes = _prenorm_specs(m, n, tm, tn)
        out_specs += specs
        out_shape += shapes
    outs = pl.pallas_call(
        functools.partial(_proj_res_kernel, scale=scale),
        grid=(m // tm, n // tn),
        in_specs=in_specs,
        out_specs=out_specs,
        out_shape=out_shape,
        compiler_params=_params(2, vmem),
        name="proj_residual",
    )(*args)
    return outs[0] if next_gain is None else outs


def _proj_kernel(lhs_ref, w_ref, *rest):
    acc = _dot(lhs_ref[...], w_ref[...].astype(BF16))
    if len(rest) == 2:
        acc = acc * _row_scale(rest[0], lhs_ref.shape[1])
    rest[-1][...] = acc.astype(rest[-1].dtype)


def _proj(lhs, w, tm, tn, out_dtype=BF16, col0=0, n=None, ssq=None):
    m, k = lhs.shape
    n = w.shape[1] if n is None else n
    col_blk0 = col0 // tn
    vmem = 2 * _nbytes((tm, k), BF16) + _weight_vmem((k, tn), w.dtype) + 6 * _nbytes((tm, tn), F32)
    in_specs = [pl.BlockSpec((tm, k), lambda i, j: (i, 0)),
                pl.BlockSpec((k, tn), lambda i, j: (0, j + col_blk0))]
    args = [lhs, w]
    if ssq is not None:
        in_specs.append(pl.BlockSpec((tm, V7X_LANES), lambda i, j: (i, 0)))
        args.append(ssq)
    return pl.pallas_call(
        _proj_kernel,
        grid=(m // tm, n // tn),
        in_specs=in_specs,
        out_specs=pl.BlockSpec((tm, tn), lambda i, j: (i, j)),
        out_shape=jax.ShapeDtypeStruct((m, n), out_dtype),
        compiler_params=_params(2, vmem),
        name="proj",
    )(*args)


def _attn_blocks(ref, sl):
    return ref[:, :, sl].reshape(MAX_DIL, BLOCK, HEAD_DIM)


def _attn_prev_blocks(first_ref, cur, sl, dil):
    nq = MAX_DIL // dil
    first = first_ref[:, :, sl]
    if nq == 1:
        return first
    cur4 = cur.reshape(dil, nq, BLOCK, HEAD_DIM)
    return jnp.concatenate([first[:, None], cur4[:, :-1]], axis=1).reshape(MAX_DIL, BLOCK, HEAD_DIM)


def _attn_token_rows(g, dil):
    r, i = divmod(g, MAX_DIL // dil)
    return pl.ds(i * BLOCK * dil + r, BLOCK, stride=dil)


def _attn_masks(dil, tile_has_prev):
    shape = (MAX_DIL, BLOCK, BLOCK)
    blk = lax.broadcasted_iota(jnp.int32, shape, 0)
    qi = lax.broadcasted_iota(jnp.int32, shape, 1)
    kj = lax.broadcasted_iota(jnp.int32, shape, 2)
    has_prev = jnp.logical_or(tile_has_prev, blk % (MAX_DIL // dil) != 0)
    return jnp.logical_and(kj >= qi, has_prev), kj <= qi


def _attn_probs(q_ref, kp_ref, kc_ref, sl, dil, masks):
    mask_prev, mask_cur = masks
    scale = HEAD_DIM ** -0.5
    q = _attn_blocks(q_ref, sl)
    kc = _attn_blocks(kc_ref, sl)
    kp = _attn_prev_blocks(kp_ref, kc, sl, dil)
    sp = jnp.einsum("gqd,gkd->gqk", q, kp, preferred_element_type=F32)
    sc = jnp.einsum("gqd,gkd->gqk", q, kc, preferred_element_type=F32)
    sp = jnp.where(mask_prev, sp, NEG_INF)
    sc = jnp.where(mask_cur, sc, NEG_INF)
    m = jnp.maximum(jnp.max(sp, axis=2, keepdims=True), jnp.max(sc, axis=2, keepdims=True))
    pp = jnp.exp2((sp - m) * (scale * LOG2E))
    pc = jnp.exp2((sc - m) * (scale * LOG2E))
    l = jnp.sum(pp, axis=2, keepdims=True) + jnp.sum(pc, axis=2, keepdims=True)
    return pp.astype(BF16), pc.astype(BF16), l, m * scale + jnp.log(l)


def _attn_output(pp, pc, l, vp_ref, vc_ref, sl, dil, o_scr, slot):
    vc = _attn_blocks(vc_ref, sl)
    vp = _attn_prev_blocks(vp_ref, vc, sl, dil)
    o = (jnp.einsum("gqk,gkd->gqd", pp, vp, preferred_element_type=F32)
         + jnp.einsum("gqk,gkd->gqd", pc, vc, preferred_element_type=F32)) / l
    for g in range(MAX_DIL):
        o_scr[slot, _attn_token_rows(g, dil), :] = o[g]


def _attn_scatter_lse(lse_tile, dil, lse_scr):
    for g in range(MAX_DIL):
        lse_scr[_attn_token_rows(g, dil), :] = lse_tile[g]


def _attn_accumulate_lse(lse_scr, lse_ref, first):
    @pl.when(first)
    def _():
        lse_ref[...] = lse_scr[...]

    @pl.when(jnp.logical_not(first))
    def _():
        lse_ref[...] += lse_scr[...]


def _mix_qkv_kernel(u_ref, ssq_ref, wq_ref, wk_ref, wv_ref, cos_ref, sin_ref, *rest, tn, dil,
                    attn_dil, tiles_per_batch, tiles_per_attn_tile, attn_heads):
    if attn_dil is None:
        q_ref, k_ref, v_ref, scr_ref, tab_ref = rest
    else:
        (aq_ref, akp_ref, akc_ref, avp_ref, avc_ref, q_ref, k_ref, v_ref, ao_ref, alse_ref,
         scr_ref, tab_ref, ao_scr, alse_scr) = rest
        tile_has_prev, first_head = _attn_host_unit(tiles_per_batch, tiles_per_attn_tile, attn_heads)
    tm, d_in = u_ref.shape
    rows = tm // dil

    def regroup(slab, val):
        if dil == 1:
            return [val]
        scr_ref[slab] = val
        return [scr_ref[slab, pl.ds(r, rows, stride=dil), :] for r in range(dil)]

    @pl.when(pl.program_id(1) == 0)
    def _():
        rs = _row_scale(ssq_ref, d_in)
        tables = (cos_ref[...] * rs, sin_ref[...] * rs, jnp.broadcast_to(rs, (tm, HEAD_DIM)))
        for t, table in enumerate(tables):
            for r, piece in enumerate(regroup(0, table)):
                tab_ref[t, r] = piece

    heads = tn // HEAD_DIM
    for part, (w_ref, o_ref) in enumerate(((wq_ref, q_ref), (wk_ref, k_ref), (wv_ref, v_ref))):
        acc = _dot(u_ref[...], w_ref[...].astype(BF16))
        if part == 0 and attn_dil is not None:
            probs = _attn_host_probs(aq_ref, akp_ref, akc_ref, attn_dil, tile_has_prev, attn_heads)
        for h in range(heads):
            pieces = regroup(part * heads + h, acc[:, h * HEAD_DIM:(h + 1) * HEAD_DIM])
            for r, piece in enumerate(pieces):
                if o_ref is v_ref:
                    piece = piece * tab_ref[2, r]
                else:
                    piece = piece * tab_ref[0, r] + pltpu.roll(piece, HEAD_DIM // 2, axis=1) * tab_ref[1, r]
                o_ref[r, :, h * HEAD_DIM:(h + 1) * HEAD_DIM] = piece.astype(o_ref.dtype)

    if attn_dil is not None:
        _attn_host_outputs(probs, first_head, avp_ref, avc_ref, attn_dil, ao_ref, ao_scr, alse_scr)
        _attn_accumulate_lse(alse_scr, alse_ref, first_head == 0)


def _attn_unit_specs(dil, unit, heads):
    rows = ATTN_TILE // dil

    def cur_map(*idx):
        b, t, h = unit(*idx)
        return b, 0, t, h

    def prev_map(*idx):
        b, t, h = unit(*idx)
        return b, 0, jnp.maximum(t * (rows // BLOCK) - 1, 0), h

    cur = pl.BlockSpec((None, dil, rows, heads * HEAD_DIM), cur_map)
    prev = pl.BlockSpec((None, dil, BLOCK, heads * HEAD_DIM), prev_map)
    return [cur, prev, cur, prev, cur]


def _mix_qkv(u, ssq, w, cos, sin, group, dil, batch, attn_qkv=None, attn_dil=None, tm=1024, tn=256):
    m, k = u.shape
    seq = m // batch
    tiles_per_seq = seq // tm
    part_blocks = ATTN_WIDTH // tn

    def w_spec(part):
        col_blk0 = (group * 3 + part) * part_blocks
        return pl.BlockSpec((k, tn), lambda i, j: (0, j + col_blk0))

    out_spec = pl.BlockSpec((None, dil, tm // dil, tn),
                            lambda i, j: (i // tiles_per_seq, 0, i % tiles_per_seq, j))
    out_shape = jax.ShapeDtypeStruct((batch, dil, seq // dil, ATTN_WIDTH), BF16)
    vmem = (2 * _nbytes((tm, k), BF16) + 3 * _weight_vmem((k, tn), w.dtype)
            + 3 * 6 * _nbytes((tm, tn), F32) + 10 * _nbytes((tm, HEAD_DIM), F32))
    in_specs = [pl.BlockSpec((tm, k), lambda i, j: (i, 0)),
                pl.BlockSpec((tm, V7X_LANES), lambda i, j: (i, 0)),
                w_spec(0), w_spec(1), w_spec(2),
                pl.BlockSpec((tm, HEAD_DIM), lambda i, j: (i, 0)),
                pl.BlockSpec((tm, HEAD_DIM), lambda i, j: (i, 0))]
    out_specs, out_shapes = [out_spec] * 3, [out_shape] * 3
    scratch = [pltpu.VMEM((3 * tn // HEAD_DIM, tm, HEAD_DIM), F32),
               pltpu.VMEM((3, dil, tm // dil, HEAD_DIM), F32)]
    args = [u, ssq, w, w, w, cos, sin]
    host = _attn_host(attn_qkv, attn_dil, m, seq, tm, part_blocks)
    if attn_qkv is not None:
        in_specs += host["in_specs"]
        args += host["args"]
        out_specs += host["out_specs"]
        out_shapes += host["out_shapes"]
        scratch += host["scratch"]
        vmem += host["vmem"]
    return pl.pallas_call(
        functools.partial(_mix_qkv_kernel, tn=tn, dil=dil, attn_dil=attn_dil, **host["kwargs"]),
        grid=(m // tm, part_blocks),
        in_specs=in_specs,
        out_specs=out_specs,
        out_shape=out_shapes,
        scratch_shapes=scratch,
        compiler_params=_params(2, vmem),
        name=f"mix_qkv_{dil}",
    )(*args)


def _attn_host(attn_qkv, attn_dil, m, seq, tm, col_steps):
    tiles_per_seq = seq // tm
    tiles_per_attn_tile = ATTN_TILE // tm
    heads = N_ATTN_HEADS // (tiles_per_attn_tile * col_steps)
    kwargs = dict(tiles_per_batch=tiles_per_seq, tiles_per_attn_tile=tiles_per_attn_tile, attn_heads=heads)
    if attn_qkv is None:
        return dict(kwargs=kwargs)
    assert heads * tiles_per_attn_tile * col_steps == N_ATTN_HEADS and seq % ATTN_TILE == 0
    attn_tiles_per_seq = seq // ATTN_TILE

    def unit(i, j):
        in_seq = i % tiles_per_seq
        return (i // tiles_per_seq, in_seq // tiles_per_attn_tile,
                (in_seq % tiles_per_attn_tile) * col_steps + j)

    def row_tile(i, j):
        b, t, _ = unit(i, j)
        return b * attn_tiles_per_seq + t

    aq, ak, av = attn_qkv
    return dict(
        kwargs=kwargs,
        in_specs=_attn_unit_specs(attn_dil, unit, heads),
        args=[aq, ak, ak, av, av],
        out_specs=[pl.BlockSpec((ATTN_TILE, heads * HEAD_DIM), lambda i, j: (row_tile(i, j), unit(i, j)[2])),
                   pl.BlockSpec((ATTN_TILE, V7X_LANES), lambda i, j: (row_tile(i, j), 0))],
        out_shapes=[jax.ShapeDtypeStruct((m, ATTN_WIDTH), BF16),
                    jax.ShapeDtypeStruct((m, V7X_LANES), F32)],
        scratch=[pltpu.VMEM((heads, ATTN_TILE, HEAD_DIM), F32), pltpu.VMEM((ATTN_TILE, V7X_LANES), F32)],
        vmem=heads * (8 * _nbytes((attn_dil, ATTN_TILE // attn_dil, HEAD_DIM), BF16)
                      + 4 * _nbytes((ATTN_TILE, HEAD_DIM), BF16) + 8 * _nbytes((ATTN_TILE, BLOCK), F32))
        + 4 * _nbytes((ATTN_TILE, V7X_LANES), F32))


def _attn_host_unit(tiles_per_batch, tiles_per_attn_tile, attn_heads):
    tile_has_prev = (pl.program_id(0) % tiles_per_batch) >= tiles_per_attn_tile
    group = (pl.program_id(0) % tiles_per_attn_tile) * pl.num_programs(1) + pl.program_id(1)
    return tile_has_prev, group * attn_heads


def _attn_host_probs(aq_ref, akp_ref, akc_ref, attn_dil, tile_has_prev, attn_heads):
    masks = _attn_masks(attn_dil, tile_has_prev)
    return [_attn_probs(aq_ref, akp_ref, akc_ref, slice(h * HEAD_DIM, (h + 1) * HEAD_DIM), attn_dil, masks)
            for h in range(attn_heads)]


def _attn_host_outputs(probs, first_head, avp_ref, avc_ref, attn_dil, ao_ref, ao_scr, alse_scr):
    lane = lax.broadcasted_iota(jnp.int32, (MAX_DIL, BLOCK, V7X_LANES), 2)
    lse_tile = jnp.zeros((MAX_DIL, BLOCK, V7X_LANES), F32)
    for h, (pp, pc, l, lse) in enumerate(probs):
        sl = slice(h * HEAD_DIM, (h + 1) * HEAD_DIM)
        _attn_output(pp, pc, l, avp_ref, avc_ref, sl, attn_dil, ao_scr, h)
        ao_ref[:, sl] = ao_scr[h].astype(ao_ref.dtype)
        lse_tile = jnp.where(lane == first_head + h, lse, lse_tile)
    _attn_scatter_lse(lse_tile, attn_dil, alse_scr)


def _pool_proj_kernel(u_ref, ssq_ref, w_ref, aq_ref, akp_ref, akc_ref, avp_ref, avc_ref,
                      z_ref, ao_ref, alse_ref, ao_scr, alse_scr, *, attn_dil, tiles_per_batch,
                      tiles_per_attn_tile, attn_heads):
    tile_has_prev, first_head = _attn_host_unit(tiles_per_batch, tiles_per_attn_tile, attn_heads)
    rs = _row_scale(ssq_ref, u_ref.shape[1])
    probs = None
    for sl in _column_halves(z_ref):
        z_ref[:, sl] = _dot(u_ref[...], w_ref[:, sl].astype(BF16)) * rs
        if probs is None:
            probs = _attn_host_probs(aq_ref, akp_ref, akc_ref, attn_dil, tile_has_prev, attn_heads)
    _attn_host_outputs(probs, first_head, avp_ref, avc_ref, attn_dil, ao_ref, ao_scr, alse_scr)
    _attn_accumulate_lse(alse_scr, alse_ref, first_head == 0)


def _pool_proj_attn(u, ssq, w, col0, attn_qkv, attn_dil, batch, tm=1024, tn=512):
    m, k = u.shape
    seq = m // batch
    col_steps = POOL_WIDTH // tn
    col_blk0 = col0 // tn
    host = _attn_host(attn_qkv, attn_dil, m, seq, tm, col_steps)
    vmem = (2 * _nbytes((tm, k), BF16) + _weight_vmem((k, tn), w.dtype) + 6 * _nbytes((tm, tn), F32)
            + host["vmem"])
    return pl.pallas_call(
        functools.partial(_pool_proj_kernel, attn_dil=attn_dil, **host["kwargs"]),
        grid=(m // tm, col_steps),
        in_specs=[pl.BlockSpec((tm, k), lambda i, j: (i, 0)),
                  pl.BlockSpec((tm, V7X_LANES), lambda i, j: (i, 0)),
                  pl.BlockSpec((k, tn), lambda i, j: (0, j + col_blk0))] + host["in_specs"],
        out_specs=[pl.BlockSpec((tm, tn), lambda i, j: (i, j))] + host["out_specs"],
        out_shape=[jax.ShapeDtypeStruct((m, POOL_WIDTH), F32)] + host["out_shapes"],
        scratch_shapes=host["scratch"],
        compiler_params=_params(2, vmem),
        name="pool_proj_attn",
    )(u, ssq, w, *host["args"])


def _combine_kernel(o0_ref, o1_ref, o2_ref, l0_ref, l1_ref, l2_ref, out_ref):
    l0, l1, l2 = l0_ref[...], l1_ref[...], l2_ref[...]
    m = jnp.maximum(jnp.maximum(l0, l1), l2)
    e0, e1, e2 = jnp.exp(l0 - m), jnp.exp(l1 - m), jnp.exp(l2 - m)
    den = e0 + e1 + e2
    w0, w1, w2 = e0 / den, e1 / den, e2 / den
    for h in range(N_ATTN_HEADS):
        sl = slice(h * HEAD_DIM, (h + 1) * HEAD_DIM)
        acc = (w0[:, h:h + 1] * o0_ref[:, sl].astype(F32)
               + w1[:, h:h + 1] * o1_ref[:, sl].astype(F32)
               + w2[:, h:h + 1] * o2_ref[:, sl].astype(F32))
        out_ref[:, sl] = acc.astype(out_ref.dtype)


def _pool_kernel(zm_ref, zh_ref, wp_ref, scale_ref, o_ref, *, tp):
    t = pl.program_id(1)
    pos = t * tp + lax.broadcasted_iota(jnp.int32, (tp, 1), 0)
    for g, win in enumerate(POOL_WINDOWS):
        sl = slice(g * POOL_GROUP, (g + 1) * POOL_GROUP)
        main = zm_ref[:, sl]
        halo = jnp.where(t > 0, zh_ref[:, sl], 0.0)
        s = jnp.concatenate([halo, main], axis=0)
        shift = 1
        while shift < win:
            s = s + pltpu.roll(s, shift, axis=0)
            shift *= 2
        cnt = jnp.minimum(pos + 1, win).astype(F32)
        y = s[POOL_HALO:] / cnt - main
        yo = _dot(y.astype(BF16), wp_ref[g].astype(BF16))
        o_ref[:, sl] = (yo * scale_ref[:, sl]).astype(o_ref.dtype)


def _combine_pool_kernel(o0_ref, o1_ref, o2_ref, l0_ref, l1_ref, l2_ref, zm_ref, zh_ref, wp_ref,
                         scale_ref, attn_ref, pool_ref, *, tp):
    _combine_kernel(o0_ref, o1_ref, o2_ref, l0_ref, l1_ref, l2_ref, attn_ref)
    _pool_kernel(zm_ref, zh_ref, wp_ref, scale_ref, pool_ref, tp=tp)


def _combine_and_pool(outs, lses, zp, w_pool, pool_scale, batch, tp=512):
    m = zp.shape[0]
    seq = m // batch
    tiles_per_seq = seq // tp
    halo_blocks = tp // POOL_HALO

    def rows(bi, t):
        return bi * tiles_per_seq + t

    o_spec = pl.BlockSpec((tp, ATTN_WIDTH), lambda bi, t: (rows(bi, t), 0))
    l_spec = pl.BlockSpec((tp, V7X_LANES), lambda bi, t: (rows(bi, t), 0))
    vmem = (8 * _nbytes((tp, ATTN_WIDTH), BF16) + 16 * _nbytes((tp, V7X_LANES), F32)
            + 6 * _nbytes((tp, POOL_WIDTH), F32) + 2 * _nbytes(w_pool.shape, F32)
            + 12 * _nbytes((tp, POOL_GROUP), F32) + 2 * VMEM_ELEMENTWISE_TEMPORARIES)
    return pl.pallas_call(
        functools.partial(_combine_pool_kernel, tp=tp),
        grid=(batch, tiles_per_seq),
        in_specs=[o_spec] * 3 + [l_spec] * 3 + [
            pl.BlockSpec((tp, POOL_WIDTH), lambda bi, t: (rows(bi, t), 0)),
            pl.BlockSpec((POOL_HALO, POOL_WIDTH),
                         lambda bi, t: (jnp.maximum(rows(bi, t) * halo_blocks - 1, 0), 0)),
            pl.BlockSpec(w_pool.shape, lambda bi, t: (0, 0, 0)),
            pl.BlockSpec((1, POOL_WIDTH), lambda bi, t: (0, 0))],
        out_specs=[o_spec, pl.BlockSpec((tp, POOL_WIDTH), lambda bi, t: (rows(bi, t), 0))],
        out_shape=[jax.ShapeDtypeStruct((m, ATTN_WIDTH), BF16),
                   jax.ShapeDtypeStruct((m, POOL_WIDTH), BF16)],
        compiler_params=_params(2, vmem),
        name="combine_pool",
    )(*outs, *lses, zp, zp, w_pool, pool_scale.reshape(1, POOL_WIDTH))


def _mix_out_kernel(a_ref, p_ref, wa_ref, wp_ref, res_ref, o_ref):
    for sl in _column_halves(o_ref):
        acc = (_dot(a_ref[...], wa_ref[:, sl].astype(BF16))
               + _dot(p_ref[...], wp_ref[:, sl].astype(BF16)))
        o_ref[:, sl] = res_ref[:, sl] + acc


def _mix_out(o_attn, o_pool, w, res, tm=1024, tn=1024):
    m = o_attn.shape[0]
    n = w.shape[1]
    ka, kp = o_attn.shape[1], o_pool.shape[1]
    assert ka == kp
    vmem = (2 * _nbytes((tm, ka + kp), BF16) + _weight_vmem((ka + kp, tn), w.dtype)
            + 8 * _nbytes((tm, tn), F32))
    return pl.pallas_call(
        _mix_out_kernel,
        grid=(m // tm, n // tn),
        in_specs=[pl.BlockSpec((tm, ka), lambda i, j: (i, 0)),
                  pl.BlockSpec((tm, kp), lambda i, j: (i, 0)),
                  pl.BlockSpec((ka, tn), lambda i, j: (0, j)),
                  pl.BlockSpec((kp, tn), lambda i, j: (1, j)),
                  pl.BlockSpec((tm, tn), lambda i, j: (i, j))],
        out_specs=pl.BlockSpec((tm, tn), lambda i, j: (i, j)),
        out_shape=jax.ShapeDtypeStruct((m, n), F32),
        compiler_params=_params(2, vmem),
        name="mix_out",
    )(o_attn, o_pool, w, w, res)


def _cast_kernel(x_ref, o_ref):
    o_ref[...] = x_ref[...].astype(o_ref.dtype)


def _cast_bf16(w, rows=512):
    k, n = w.shape
    rows = min(rows, k)
    return pl.pallas_call(
        _cast_kernel,
        grid=(k // rows,),
        in_specs=[pl.BlockSpec((rows, n), lambda i: (i, 0))],
        out_specs=pl.BlockSpec((rows, n), lambda i: (i, 0)),
        out_shape=jax.ShapeDtypeStruct((k, n), BF16),
        compiler_params=_params(1, 8 * _nbytes((rows, n), F32)),
        name="cast_bf16",
    )(w)


def _rms_rows(x, gain_ref):
    ms = jnp.mean(x * x, axis=-1, keepdims=True)
    return (x * lax.rsqrt(ms + EPS) * gain_ref[...]).astype(BF16)


def _cross_kernel(kv_ref, wq_ref, wo_ref, res_ref, gain_in_ref, gain_out_ref, o_ref, u_ref):
    scale = CROSS_DIM ** -0.5
    q = _dot(_rms_rows(res_ref[...], gain_in_ref), wq_ref[...]).astype(BF16)
    heads = []
    for h in range(CROSS_HEADS):
        sl = slice(h * CROSS_DIM, (h + 1) * CROSS_DIM)
        k = kv_ref[:, sl]
        v = kv_ref[:, CROSS_WIDTH + h * CROSS_DIM:CROSS_WIDTH + (h + 1) * CROSS_DIM]
        s = _dot_nt(q[:, sl], k)
        m = jnp.max(s, axis=1, keepdims=True)
        p = jnp.exp2((s - m) * (scale * LOG2E))
        l = jnp.sum(p, axis=1, keepdims=True)
        heads.append((_dot(p.astype(BF16), v) / l).astype(BF16))
    out = res_ref[...] + _dot(jnp.concatenate(heads, axis=1), wo_ref[...])
    o_ref[...] = out
    u_ref[...] = _rms_rows(out, gain_out_ref)


def _cross_attention(kv, w_q, w_o, res, gain, next_gain, batch, ts=256):
    m, d = res.shape
    mem_len = kv.shape[0] // batch
    steps_per_batch = (m // batch) // ts
    vmem = (2 * _nbytes((ts, d), BF16) + 4 * _nbytes((ts, d), F32) + _nbytes(w_q.shape, BF16)
            + _nbytes(w_o.shape, BF16) + 6 * _nbytes((ts, d), F32) + VMEM_ELEMENTWISE_TEMPORARIES)
    row_f32 = pl.BlockSpec((ts, d), lambda i: (i, 0))
    gain_spec = pl.BlockSpec((1, d), lambda i: (0, 0))
    return pl.pallas_call(
        _cross_kernel,
        grid=(m // ts,),
        in_specs=[pl.BlockSpec((mem_len, 2 * CROSS_WIDTH), lambda i: (i // steps_per_batch, 0)),
                  _resident(w_q.shape, lambda i: (0, 0)),
                  _resident(w_o.shape, lambda i: (0, 0)),
                  row_f32, gain_spec, gain_spec],
        out_specs=[row_f32, pl.BlockSpec((ts, d), lambda i: (i, 0))],
        out_shape=[jax.ShapeDtypeStruct((m, d), F32), jax.ShapeDtypeStruct((m, d), BF16)],
        compiler_params=_params(1, vmem),
        name="cross_attention",
    )(kv, w_q, w_o, res, gain.reshape(1, d), next_gain.reshape(1, d))


def _swiglu_block(h, u, w_in, w_out, next_gain=None, later_weights=()):
    gate, (w_out_bf16, *later_bf16) = _ffn_in(u, w_in, cast_jobs=(w_out,) + tuple(later_weights))
    return _proj_res(gate, w_out_bf16, h, 0.5, tm=512, tn=512, next_gain=next_gain), later_bf16


def kernel(x, mem, positions, g_ffn1, w_ffn1_in, w_ffn1_out, g_mix, w_mix_in, w_pool, pool_scale, w_mix_out, g_cross, g_mem, w_cross_q, w_cross_kv, w_cross_o, g_ffn2, w_ffn2_in, w_ffn2_out, g_final):
    batch, seq, d = x.shape
    m = batch * seq
    depth = g_ffn1.shape[0]
    h = x.reshape(m, d)
    mem2 = mem.reshape(batch * mem.shape[1], d)
    for l in range(depth):
        if l == 0:
            u, cos, sin = _rmsnorm(h, g_ffn1[l], BF16, rope_positions=positions)
        else:
            u = _rmsnorm(h, g_ffn1[l], BF16)
        (h, hg, ssq), (w_mix_in_bf16, w_mix_out_bf16, w_ffn2_in_bf16) = _swiglu_block(
            h, u, w_ffn1_in[l], w_ffn1_out[l], next_gain=g_mix[l],
            later_weights=(w_mix_in[l], w_mix_out[l], w_ffn2_in[l]))

        outs, lses = [], []
        qkv, qkv_dil = None, None
        for group, (_, dil) in enumerate(DILATED_CONFIGS):
            res = _mix_qkv(hg, ssq, w_mix_in_bf16, cos, sin, group, dil, batch,
                           attn_qkv=qkv, attn_dil=qkv_dil)
            if qkv is not None:
                outs.append(res[3])
                lses.append(res[4])
            qkv, qkv_dil = res[:3], dil
        zp, o, lse = _pool_proj_attn(hg, ssq, w_mix_in_bf16, ATTN_IN, qkv, qkv_dil, batch)
        outs.append(o)
        lses.append(lse)
        o_attn, o_pool = _combine_and_pool(outs, lses, zp, w_pool[l], pool_scale[l], batch)
        h = _mix_out(o_attn, o_pool, w_mix_out_bf16, h)

        um = _rmsnorm(mem2, g_mem[l], BF16)
        kv = _proj(um, w_cross_kv[l], tm=um.shape[0], tn=CROSS_WIDTH)
        h, u = _cross_attention(kv, _cast_bf16(w_cross_q[l]), _cast_bf16(w_cross_o[l]),
                                h, g_cross[l], g_ffn2[l], batch)

        h, _ = _swiglu_block(h, u, w_ffn2_in_bf16, w_ffn2_out[l])
    return _rmsnorm(h, g_final, F32).reshape(batch, seq, d)
```

```python
import functools
import math

import jax
import jax.numpy as jnp
from jax import lax
from jax.experimental import pallas as pl
from jax.experimental.pallas import tpu as pltpu

F32 = jnp.float32
BF16 = jnp.bfloat16

D_MODEL = 4096
HEAD_DIM = 128
ATTN_WIDTH = D_MODEL // 2
N_ATTN_HEADS = ATTN_WIDTH // HEAD_DIM
DILATED_CONFIGS = ((128, 1), (512, 4), (2048, 16))
N_DIL = len(DILATED_CONFIGS)
BLOCK = 128
POOL_WIDTH = D_MODEL - ATTN_WIDTH
POOL_WINDOWS = (2, 4, 8, 16)
POOL_GROUP = POOL_WIDTH // len(POOL_WINDOWS)
QKV_WIDTH = 3 * ATTN_WIDTH
ATTN_IN = N_DIL * QKV_WIDTH
D_FF = ((8 * D_MODEL // 3 + 255) // 256) * 256
CROSS_HEADS = 4
CROSS_DIM = 128
CROSS_WIDTH = CROSS_HEADS * CROSS_DIM
ROPE_THETA = 10000.0
EPS = 1e-6
NEG_INF = -1e30
LOG2E = math.log2(math.e)

V7X_LANES = 128
V7X_BF16_SUBLANES = 16
V7X_VMEM_BYTES = 64 * 1024 * 1024
V7X_VMEM_REQUEST_CAP = V7X_VMEM_BYTES - 6 * 1024 * 1024
VMEM_ELEMENTWISE_TEMPORARIES = 8 * 1024 * 1024

POOL_HALO = max(POOL_WINDOWS)
MAX_DIL = max(d for _, d in DILATED_CONFIGS)
ATTN_TILE = BLOCK * MAX_DIL


def _params(n_grid, vmem_bytes):
    return pltpu.CompilerParams(
        dimension_semantics=("arbitrary",) * n_grid,
        vmem_limit_bytes=int(min(vmem_bytes, V7X_VMEM_REQUEST_CAP)),
    )


def _nbytes(shape, dtype):
    return math.prod(shape) * jnp.dtype(dtype).itemsize


def _weight_vmem(block_shape, dtype):
    cast_copy = 0 if dtype == BF16 else _nbytes(block_shape, BF16)
    return 2 * _nbytes(block_shape, dtype) + cast_copy


def _resident(block_shape, index_map):
    return pl.BlockSpec(block_shape, index_map, pipeline_mode=pl.Buffered(1))


def _dot(a, b):
    return jnp.dot(a, b, preferred_element_type=F32)


def _dot_nt(a, b):
    return lax.dot_general(a, b, (((1,), (1,)), ((), ())), preferred_element_type=F32)


def _rmsnorm_kernel(x_ref, g_ref, *rest):
    x = x_ref[...]
    ms = jnp.mean(x * x, axis=-1, keepdims=True)
    if len(rest) == 1:
        o_ref, = rest
    else:
        pos_ref, inv_ref, sign_ref, o_ref, cos_ref, sin_ref = rest
        ang = pos_ref[...].astype(F32) * inv_ref[...]
        cos_ref[...] = jnp.cos(ang)
        sin_ref[...] = jnp.sin(ang) * sign_ref[...]
    o_ref[...] = (x * lax.rsqrt(ms + EPS) * g_ref[...]).astype(o_ref.dtype)


def _rmsnorm(x, g, out_dtype, rope_positions=None, tm=512):
    m, d = x.shape
    vmem = 2 * tm * d * (4 + jnp.dtype(out_dtype).itemsize) + 4 * tm * d * 4
    in_specs = [pl.BlockSpec((tm, d), lambda i: (i, 0)), pl.BlockSpec((1, d), lambda i: (0, 0))]
    out_specs = [pl.BlockSpec((tm, d), lambda i: (i, 0))]
    out_shape = [jax.ShapeDtypeStruct((m, d), out_dtype)]
    args = [x, g.reshape(1, d)]
    if rope_positions is not None:
        half = HEAD_DIM // 2
        inv = 1.0 / (ROPE_THETA ** (jnp.arange(0, HEAD_DIM, 2, dtype=F32) / HEAD_DIM))
        inv = jnp.concatenate([inv, inv]).reshape(1, HEAD_DIM)
        sign = jnp.concatenate([-jnp.ones((half,), F32), jnp.ones((half,), F32)]).reshape(1, HEAD_DIM)
        row = pl.BlockSpec((1, HEAD_DIM), lambda i: (0, 0))
        tab = pl.BlockSpec((tm, HEAD_DIM), lambda i: (i, 0))
        in_specs += [pl.BlockSpec((tm, 1), lambda i: (i, 0)), row, row]
        out_specs += [tab, tab]
        out_shape += [jax.ShapeDtypeStruct((m, HEAD_DIM), F32)] * 2
        args += [rope_positions.reshape(m, 1), inv, sign]
        vmem += 16 * tm * HEAD_DIM * 4
    outs = pl.pallas_call(
        _rmsnorm_kernel,
        grid=(m // tm,),
        in_specs=in_specs,
        out_specs=out_specs,
        out_shape=out_shape,
        compiler_params=_params(1, vmem),
        name="rmsnorm",
    )(*args)
    return outs[0] if rope_positions is None else outs


def _ffn_in_kernel(u_ref, wa_ref, wb_ref, *rest):
    n_jobs = (len(rest) - 1) // 2
    srcs, o_ref, dsts = rest[:n_jobs], rest[n_jobs], rest[n_jobs + 1:]
    half = o_ref.shape[1] // 2
    for c in range(2):
        sl = slice(c * half, (c + 1) * half)
        w = jnp.concatenate([wa_ref[:, sl].astype(BF16), wb_ref[:, sl].astype(BF16)], axis=1)
        ab = _dot(u_ref[...], w)
        a, b = ab[:, :half], ab[:, half:]
        o_ref[:, sl] = (a * jax.nn.sigmoid(a) * b).astype(o_ref.dtype)
    for src, dst in zip(srcs, dsts):
        dst[...] = src[...].astype(dst.dtype)


def _ffn_in(u, w_in, cast_jobs=(), tm=2048, tf=256):
    m, d = u.shape
    nf = D_FF // tf
    n_steps = (m // tm) * nf
    lhs_spec, lhs_buffers = (pl.BlockSpec, 2) if w_in.dtype == BF16 else (_resident, 1)
    vmem = (lhs_buffers * _nbytes((tm, d), BF16) + 2 * _weight_vmem((d, tf), w_in.dtype)
            + 2 * _nbytes((tm, tf), BF16) + 6 * _nbytes((tm, tf), F32))
    job_specs, job_shapes = [], []
    for w in cast_jobs:
        rows = next(r for r in range(V7X_BF16_SUBLANES, w.shape[0] + 1, V7X_BF16_SUBLANES)
                    if w.shape[0] % r == 0 and w.shape[0] // r <= n_steps)
        n_blocks = w.shape[0] // rows
        job_specs.append(pl.BlockSpec(
            (rows, w.shape[1]), lambda i, j, n_blocks=n_blocks: (jnp.minimum(i * nf + j, n_blocks - 1), 0)))
        job_shapes.append(jax.ShapeDtypeStruct(w.shape, BF16))
        vmem += 2 * _nbytes((rows, w.shape[1]), F32) + 2 * _nbytes((rows, w.shape[1]), BF16)
    outs = pl.pallas_call(
        _ffn_in_kernel,
        grid=(m // tm, nf),
        in_specs=[lhs_spec((tm, d), lambda i, j: (i, 0)),
                  pl.BlockSpec((d, tf), lambda i, j: (0, j)),
                  pl.BlockSpec((d, tf), lambda i, j: (0, j + nf))] + job_specs,
        out_specs=[pl.BlockSpec((tm, tf), lambda i, j: (i, j))] + job_specs,
        out_shape=[jax.ShapeDtypeStruct((m, D_FF), BF16)] + job_shapes,
        compiler_params=_params(2, vmem),
        name="ffn_in",
    )(u, w_in, w_in, *cast_jobs)
    return outs[0], outs[1:]


def _emit_prenorm(out, gain_ref, hg_ref, ssq_ref):
    hg_ref[...] = (out * gain_ref[...]).astype(hg_ref.dtype)
    ssq_ref[...] += jnp.broadcast_to(jnp.sum(out * out, axis=1, keepdims=True), ssq_ref.shape)


def _init_prenorm(ssq_ref):
    @pl.when(pl.program_id(1) == 0)
    def _():
        ssq_ref[...] = jnp.zeros_like(ssq_ref)


def _row_scale(ssq_ref, d):
    return lax.rsqrt(ssq_ref[:, :1] * (1.0 / d) + EPS)


def _prenorm_specs(m, n, tm, tn):
    specs = [pl.BlockSpec((tm, tn), lambda i, j: (i, j)), pl.BlockSpec((tm, V7X_LANES), lambda i, j: (i, 0))]
    shapes = [jax.ShapeDtypeStruct((m, n), BF16), jax.ShapeDtypeStruct((m, V7X_LANES), F32)]
    return specs, shapes


def _column_halves(ref):
    half = ref.shape[1] // 2
    return [slice(0, half), slice(half, 2 * half)]


def _proj_res_kernel(lhs_ref, w_ref, res_ref, *rest, scale):
    prenorm = len(rest) > 1
    if prenorm:
        gain_ref, o_ref, hg_ref, ssq_ref = rest
        _init_prenorm(ssq_ref)
    else:
        o_ref, = rest
    for sl in _column_halves(o_ref):
        acc = _dot(lhs_ref[...], w_ref[:, sl].astype(BF16))
        out = res_ref[:, sl] + scale * acc
        o_ref[:, sl] = out
        if prenorm:
            _emit_prenorm(out, gain_ref.at[:, sl], hg_ref.at[:, sl], ssq_ref)


def _proj_res(lhs, w, res, scale, tm, tn, next_gain=None):
    m, k = lhs.shape
    n = w.shape[1]
    vmem = 2 * _nbytes((tm, k), BF16) + _weight_vmem((k, tn), w.dtype) + 8 * _nbytes((tm, tn), F32)
    in_specs = [pl.BlockSpec((tm, k), lambda i, j: (i, 0)),
                pl.BlockSpec((k, tn), lambda i, j: (0, j)),
                pl.BlockSpec((tm, tn), lambda i, j: (i, j))]
    out_specs = [pl.BlockSpec((tm, tn), lambda i, j: (i, j))]
    out_shape = [jax.ShapeDtypeStruct((m, n), F32)]
    args = [lhs, w, res]
    if next_gain is not None:
        in_specs.append(pl.BlockSpec((1, tn), lambda i, j: (0, j)))
        args.append(next_gain.reshape(1, n))
        specs, shapes = _prenorm_specs(m, n, tm, tn)
        out_specs += specs
        out_shape += shapes
    outs = pl.pallas_call(
        functools.partial(_proj_res_kernel, scale=scale),
        grid=(m // tm, n // tn),
        in_specs=in_specs,
        out_specs=out_specs,
        out_shape=out_shape,
        compiler_params=_params(2, vmem),
        name="proj_residual",
    )(*args)
    return outs[0] if next_gain is None else outs


def _proj_kernel(lhs_ref, w_ref, *rest):
    acc = _dot(lhs_ref[...], w_ref[...].astype(BF16))
    if len(rest) == 2:
        acc = acc * _row_scale(rest[0], lhs_ref.shape[1])
    rest[-1][...] = acc.astype(rest[-1].dtype)


def _proj(lhs, w, tm, tn, out_dtype=BF16, col0=0, n=None, ssq=None):
    m, k = lhs.shape
    n = w.shape[1] if n is None else n
    col_blk0 = col0 // tn
    vmem = 2 * _nbytes((tm, k), BF16) + _weight_vmem((k, tn), w.dtype) + 6 * _nbytes((tm, tn), F32)
    in_specs = [pl.BlockSpec((tm, k), lambda i, j: (i, 0)),
                pl.BlockSpec((k, tn), lambda i, j: (0, j + col_blk0))]
    args = [lhs, w]
    if ssq is not None:
        in_specs.append(pl.BlockSpec((tm, V7X_LANES), lambda i, j: (i, 0)))
        args.append(ssq)
    return pl.pallas_call(
        _proj_kernel,
        grid=(m // tm, n // tn),
        in_specs=in_specs,
        out_specs=pl.BlockSpec((tm, tn), lambda i, j: (i, j)),
        out_shape=jax.ShapeDtypeStruct((m, n), out_dtype),
        compiler_params=_params(2, vmem),
        name="proj",
    )(*args)


def _attn_blocks(ref, sl):
    return ref[:, :, sl].reshape(MAX_DIL, BLOCK, HEAD_DIM)


def _attn_prev_blocks(first_ref, cur, sl, dil):
    nq = MAX_DIL // dil
    first = first_ref[:, :, sl]
    if nq == 1:
        return first
    cur4 = cur.reshape(dil, nq, BLOCK, HEAD_DIM)
    return jnp.concatenate([first[:, None], cur4[:, :-1]], axis=1).reshape(MAX_DIL, BLOCK, HEAD_DIM)


def _attn_token_rows(g, dil):
    r, i = divmod(g, MAX_DIL // dil)
    return pl.ds(i * BLOCK * dil + r, BLOCK, stride=dil)


def _attn_masks(dil, tile_has_prev):
    shape = (MAX_DIL, BLOCK, BLOCK)
    blk = lax.broadcasted_iota(jnp.int32, shape, 0)
    qi = lax.broadcasted_iota(jnp.int32, shape, 1)
    kj = lax.broadcasted_iota(jnp.int32, shape, 2)
    has_prev = jnp.logical_or(tile_has_prev, blk % (MAX_DIL // dil) != 0)
    return jnp.logical_and(kj >= qi, has_prev), kj <= qi


def _attn_probs(q_ref, kp_ref, kc_ref, sl, dil, masks):
    mask_prev, mask_cur = masks
    scale = HEAD_DIM ** -0.5
    q = _attn_blocks(q_ref, sl)
    kc = _attn_blocks(kc_ref, sl)
    kp = _attn_prev_blocks(kp_ref, kc, sl, dil)
    sp = jnp.einsum("gqd,gkd->gqk", q, kp, preferred_element_type=F32)
    sc = jnp.einsum("gqd,gkd->gqk", q, kc, preferred_element_type=F32)
    sp = jnp.where(mask_prev, sp, NEG_INF)
    sc = jnp.where(mask_cur, sc, NEG_INF)
    m = jnp.maximum(jnp.max(sp, axis=2, keepdims=True), jnp.max(sc, axis=2, keepdims=True))
    pp = jnp.exp2((sp - m) * (scale * LOG2E))
    pc = jnp.exp2((sc - m) * (scale * LOG2E))
    l = jnp.sum(pp, axis=2, keepdims=True) + jnp.sum(pc, axis=2, keepdims=True)
    return pp.astype(BF16), pc.astype(BF16), l, m * scale + jnp.log(l)


def _attn_output(pp, pc, l, vp_ref, vc_ref, sl, dil, o_scr, slot):
    vc = _attn_blocks(vc_ref, sl)
    vp = _attn_prev_blocks(vp_ref, vc, sl, dil)
    o = (jnp.einsum("gqk,gkd->gqd", pp, vp, preferred_element_type=F32)
         + jnp.einsum("gqk,gkd->gqd", pc, vc, preferred_element_type=F32)) / l
    for g in range(MAX_DIL):
        o_scr[slot, _attn_token_rows(g, dil), :] = o[g]


def _attn_scatter_lse(lse_tile, dil, lse_scr):
    for g in range(MAX_DIL):
        lse_scr[_attn_token_rows(g, dil), :] = lse_tile[g]


def _attn_accumulate_lse(lse_scr, lse_ref, first):
    @pl.when(first)
    def _():
        lse_ref[...] = lse_scr[...]

    @pl.when(jnp.logical_not(first))
    def _():
        lse_ref[...] += lse_scr[...]


def _mix_qkv_kernel(u_ref, ssq_ref, wq_ref, wk_ref, wv_ref, cos_ref, sin_ref, *rest, tn, dil,
                    attn_dil, tiles_per_batch, tiles_per_attn_tile, attn_heads):
    if attn_dil is None:
        q_ref, k_ref, v_ref, scr_ref, tab_ref = rest
    else:
        (aq_ref, akp_ref, akc_ref, avp_ref, avc_ref, q_ref, k_ref, v_ref, ao_ref, alse_ref,
         scr_ref, tab_ref, ao_scr, alse_scr) = rest
        tile_has_prev, first_head = _attn_host_unit(tiles_per_batch, tiles_per_attn_tile, attn_heads)
    tm, d_in = u_ref.shape
    rows = tm // dil

    def regroup(slab, val):
        if dil == 1:
            return [val]
        scr_ref[slab] = val
        return [scr_ref[slab, pl.ds(r, rows, stride=dil), :] for r in range(dil)]

    @pl.when(pl.program_id(1) == 0)
    def _():
        rs = _row_scale(ssq_ref, d_in)
        tables = (cos_ref[...] * rs, sin_ref[...] * rs, jnp.broadcast_to(rs, (tm, HEAD_DIM)))
        for t, table in enumerate(tables):
            for r, piece in enumerate(regroup(0, table)):
                tab_ref[t, r] = piece

    heads = tn // HEAD_DIM
    for part, (w_ref, o_ref) in enumerate(((wq_ref, q_ref), (wk_ref, k_ref), (wv_ref, v_ref))):
        acc = _dot(u_ref[...], w_ref[...].astype(BF16))
        if part == 0 and attn_dil is not None:
            probs = _attn_host_probs(aq_ref, akp_ref, akc_ref, attn_dil, tile_has_prev, attn_heads)
        for h in range(heads):
            pieces = regroup(part * heads + h, acc[:, h * HEAD_DIM:(h + 1) * HEAD_DIM])
            for r, piece in enumerate(pieces):
                if o_ref is v_ref:
                    piece = piece * tab_ref[2, r]
                else:
                    piece = piece * tab_ref[0, r] + pltpu.roll(piece, HEAD_DIM // 2, axis=1) * tab_ref[1, r]
                o_ref[r, :, h * HEAD_DIM:(h + 1) * HEAD_DIM] = piece.astype(o_ref.dtype)

    if attn_dil is not None:
        _attn_host_outputs(probs, first_head, avp_ref, avc_ref, attn_dil, ao_ref, ao_scr, alse_scr)
        _attn_accumulate_lse(alse_scr, alse_ref, first_head == 0)


def _attn_unit_specs(dil, unit, heads):
    rows = ATTN_TILE // dil

    def cur_map(*idx):
        b, t, h = unit(*idx)
        return b, 0, t, h

    def prev_map(*idx):
        b, t, h = unit(*idx)
        return b, 0, jnp.maximum(t * (rows // BLOCK) - 1, 0), h

    cur = pl.BlockSpec((None, dil, rows, heads * HEAD_DIM), cur_map)
    prev = pl.BlockSpec((None, dil, BLOCK, heads * HEAD_DIM), prev_map)
    return [cur, prev, cur, prev, cur]


def _mix_qkv(u, ssq, w, cos, sin, group, dil, batch, attn_qkv=None, attn_dil=None, tm=1024, tn=256):
    m, k = u.shape
    seq = m // batch
    tiles_per_seq = seq // tm
    part_blocks = ATTN_WIDTH // tn

    def w_spec(part):
        col_blk0 = (group * 3 + part) * part_blocks
        return pl.BlockSpec((k, tn), lambda i, j: (0, j + col_blk0))

    out_spec = pl.BlockSpec((None, dil, tm // dil, tn),
                            lambda i, j: (i // tiles_per_seq, 0, i % tiles_per_seq, j))
    out_shape = jax.ShapeDtypeStruct((batch, dil, seq // dil, ATTN_WIDTH), BF16)
    vmem = (2 * _nbytes((tm, k), BF16) + 3 * _weight_vmem((k, tn), w.dtype)
            + 3 * 6 * _nbytes((tm, tn), F32) + 10 * _nbytes((tm, HEAD_DIM), F32))
    in_specs = [pl.BlockSpec((tm, k), lambda i, j: (i, 0)),
                pl.BlockSpec((tm, V7X_LANES), lambda i, j: (i, 0)),
                w_spec(0), w_spec(1), w_spec(2),
                pl.BlockSpec((tm, HEAD_DIM), lambda i, j: (i, 0)),
                pl.BlockSpec((tm, HEAD_DIM), lambda i, j: (i, 0))]
    out_specs, out_shapes = [out_spec] * 3, [out_shape] * 3
    scratch = [pltpu.VMEM((3 * tn // HEAD_DIM, tm, HEAD_DIM), F32),
               pltpu.VMEM((3, dil, tm // dil, HEAD_DIM), F32)]
    args = [u, ssq, w, w, w, cos, sin]
    host = _attn_host(attn_qkv, attn_dil, m, seq, tm, part_blocks)
    if attn_qkv is not None:
        in_specs += host["in_specs"]
        args += host["args"]
        out_specs += host["out_specs"]
        out_shapes += host["out_shapes"]
        scratch += host["scratch"]
        vmem += host["vmem"]
    return pl.pallas_call(
        functools.partial(_mix_qkv_kernel, tn=tn, dil=dil, attn_dil=attn_dil, **host["kwargs"]),
        grid=(m // tm, part_blocks),
        in_specs=in_specs,
        out_specs=out_specs,
        out_shape=out_shapes,
        scratch_shapes=scratch,
        compiler_params=_params(2, vmem),
        name=f"mix_qkv_{dil}",
    )(*args)


def _attn_host(attn_qkv, attn_dil, m, seq, tm, col_steps):
    tiles_per_seq = seq // tm
    tiles_per_attn_tile = ATTN_TILE // tm
    heads = N_ATTN_HEADS // (tiles_per_attn_tile * col_steps)
    kwargs = dict(tiles_per_batch=tiles_per_seq, tiles_per_attn_tile=tiles_per_attn_tile, attn_heads=heads)
    if attn_qkv is None:
        return dict(kwargs=kwargs)
    assert heads * tiles_per_attn_tile * col_steps == N_ATTN_HEADS and seq % ATTN_TILE == 0
    attn_tiles_per_seq = seq // ATTN_TILE

    def unit(i, j):
        in_seq = i % tiles_per_seq
        return (i // tiles_per_seq, in_seq // tiles_per_attn_tile,
                (in_seq % tiles_per_attn_tile) * col_steps + j)

    def row_tile(i, j):
        b, t, _ = unit(i, j)
        return b * attn_tiles_per_seq + t

    aq, ak, av = attn_qkv
    return dict(
        kwargs=kwargs,
        in_specs=_attn_unit_specs(attn_dil, unit, heads),
        args=[aq, ak, ak, av, av],
        out_specs=[pl.BlockSpec((ATTN_TILE, heads * HEAD_DIM), lambda i, j: (row_tile(i, j), unit(i, j)[2])),
                   pl.BlockSpec((ATTN_TILE, V7X_LANES), lambda i, j: (row_tile(i, j), 0))],
        out_shapes=[jax.ShapeDtypeStruct((m, ATTN_WIDTH), BF16),
                    jax.ShapeDtypeStruct((m, V7X_LANES), F32)],
        scratch=[pltpu.VMEM((heads, ATTN_TILE, HEAD_DIM), F32), pltpu.VMEM((ATTN_TILE, V7X_LANES), F32)],
        vmem=heads * (8 * _nbytes((attn_dil, ATTN_TILE // attn_dil, HEAD_DIM), BF16)
                      + 4 * _nbytes((ATTN_TILE, HEAD_DIM), BF16) + 8 * _nbytes((ATTN_TILE, BLOCK), F32))
        + 4 * _nbytes((ATTN_TILE, V7X_LANES), F32))


def _attn_host_unit(tiles_per_batch, tiles_per_attn_tile, attn_heads):
    tile_has_prev = (pl.program_id(0) % tiles_per_batch) >= tiles_per_attn_tile
    group = (pl.program_id(0) % tiles_per_attn_tile) * pl.num_programs(1) + pl.program_id(1)
    return tile_has_prev, group * attn_heads


def _attn_host_probs(aq_ref, akp_ref, akc_ref, attn_dil, tile_has_prev, attn_heads):
    masks = _attn_masks(attn_dil, tile_has_prev)
    return [_attn_probs(aq_ref, akp_ref, akc_ref, slice(h * HEAD_DIM, (h + 1) * HEAD_DIM), attn_dil, masks)
            for h in range(attn_heads)]


def _attn_host_outputs(probs, first_head, avp_ref, avc_ref, attn_dil, ao_ref, ao_scr, alse_scr):
    lane = lax.broadcasted_iota(jnp.int32, (MAX_DIL, BLOCK, V7X_LANES), 2)
    lse_tile = jnp.zeros((MAX_DIL, BLOCK, V7X_LANES), F32)
    for h, (pp, pc, l, lse) in enumerate(probs):
        sl = slice(h * HEAD_DIM, (h + 1) * HEAD_DIM)
        _attn_output(pp, pc, l, avp_ref, avc_ref, sl, attn_dil, ao_scr, h)
        ao_ref[:, sl] = ao_scr[h].astype(ao_ref.dtype)
        lse_tile = jnp.where(lane == first_head + h, lse, lse_tile)
    _attn_scatter_lse(lse_tile, attn_dil, alse_scr)


def _pool_proj_kernel(u_ref, ssq_ref, w_ref, aq_ref, akp_ref, akc_ref, avp_ref, avc_ref,
                      z_ref, ao_ref, alse_ref, ao_scr, alse_scr, *, attn_dil, tiles_per_batch,
                      tiles_per_attn_tile, attn_heads):
    tile_has_prev, first_head = _attn_host_unit(tiles_per_batch, tiles_per_attn_tile, attn_heads)
    rs = _row_scale(ssq_ref, u_ref.shape[1])
    probs = None
    for sl in _column_halves(z_ref):
        z_ref[:, sl] = _dot(u_ref[...], w_ref[:, sl].astype(BF16)) * rs
        if probs is None:
            probs = _attn_host_probs(aq_ref, akp_ref, akc_ref, attn_dil, tile_has_prev, attn_heads)
    _attn_host_outputs(probs, first_head, avp_ref, avc_ref, attn_dil, ao_ref, ao_scr, alse_scr)
    _attn_accumulate_lse(alse_scr, alse_ref, first_head == 0)


def _pool_proj_attn(u, ssq, w, col0, attn_qkv, attn_dil, batch, tm=1024, tn=512):
    m, k = u.shape
    seq = m // batch
    col_steps = POOL_WIDTH // tn
    col_blk0 = col0 // tn
    host = _attn_host(attn_qkv, attn_dil, m, seq, tm, col_steps)
    vmem = (2 * _nbytes((tm, k), BF16) + _weight_vmem((k, tn), w.dtype) + 6 * _nbytes((tm, tn), F32)
            + host["vmem"])
    return pl.pallas_call(
        functools.partial(_pool_proj_kernel, attn_dil=attn_dil, **host["kwargs"]),
        grid=(m // tm, col_steps),
        in_specs=[pl.BlockSpec((tm, k), lambda i, j: (i, 0)),
                  pl.BlockSpec((tm, V7X_LANES), lambda i, j: (i, 0)),
                  pl.BlockSpec((k, tn), lambda i, j: (0, j + col_blk0))] + host["in_specs"],
        out_specs=[pl.BlockSpec((tm, tn), lambda i, j: (i, j))] + host["out_specs"],
        out_shape=[jax.ShapeDtypeStruct((m, POOL_WIDTH), F32)] + host["out_shapes"],
        scratch_shapes=host["scratch"],
        compiler_params=_params(2, vmem),
        name="pool_proj_attn",
    )(u, ssq, w, *host["args"])


def _combine_kernel(o0_ref, o1_ref, o2_ref, l0_ref, l1_ref, l2_ref, out_ref):
    l0, l1, l2 = l0_ref[...], l1_ref[...], l2_ref[...]
    m = jnp.maximum(jnp.maximum(l0, l1), l2)
    e0, e1, e2 = jnp.exp(l0 - m), jnp.exp(l1 - m), jnp.exp(l2 - m)
    den = e0 + e1 + e2
    w0, w1, w2 = e0 / den, e1 / den, e2 / den
    for h in range(N_ATTN_HEADS):
        sl = slice(h * HEAD_DIM, (h + 1) * HEAD_DIM)
        acc = (w0[:, h:h + 1] * o0_ref[:, sl].astype(F32)
               + w1[:, h:h + 1] * o1_ref[:, sl].astype(F32)
               + w2[:, h:h + 1] * o2_ref[:, sl].astype(F32))
        out_ref[:, sl] = acc.astype(out_ref.dtype)


def _pool_kernel(zm_ref, zh_ref, wp_ref, scale_ref, o_ref, *, tp):
    t = pl.program_id(1)
    pos = t * tp + lax.broadcasted_iota(jnp.int32, (tp, 1), 0)
    for g, win in enumerate(POOL_WINDOWS):
        sl = slice(g * POOL_GROUP, (g + 1) * POOL_GROUP)
        main = zm_ref[:, sl]
        halo = jnp.where(t > 0, zh_ref[:, sl], 0.0)
        s = jnp.concatenate([halo, main], axis=0)
        shift = 1
        while shift < win:
            s = s + pltpu.roll(s, shift, axis=0)
            shift *= 2
        cnt = jnp.minimum(pos + 1, win).astype(F32)
        y = s[POOL_HALO:] / cnt - main
        yo = _dot(y.astype(BF16), wp_ref[g].astype(BF16))
        o_ref[:, sl] = (yo * scale_ref[:, sl]).astype(o_ref.dtype)


def _combine_pool_kernel(o0_ref, o1_ref, o2_ref, l0_ref, l1_ref, l2_ref, zm_ref, zh_ref, wp_ref,
                         scale_ref, attn_ref, pool_ref, *, tp):
    _combine_kernel(o0_ref, o1_ref, o2_ref, l0_ref, l1_ref, l2_ref, attn_ref)
    _pool_kernel(zm_ref, zh_ref, wp_ref, scale_ref, pool_ref, tp=tp)


def _combine_and_pool(outs, lses, zp, w_pool, pool_scale, batch, tp=512):
    m = zp.shape[0]
    seq = m // batch
    tiles_per_seq = seq // tp
    halo_blocks = tp // POOL_HALO

    def rows(bi, t):
        return bi * tiles_per_seq + t

    o_spec = pl.BlockSpec((tp, ATTN_WIDTH), lambda bi, t: (rows(bi, t), 0))
    l_spec = pl.BlockSpec((tp, V7X_LANES), lambda bi, t: (rows(bi, t), 0))
    vmem = (8 * _nbytes((tp, ATTN_WIDTH), BF16) + 16 * _nbytes((tp, V7X_LANES), F32)
            + 6 * _nbytes((tp, POOL_WIDTH), F32) + 2 * _nbytes(w_pool.shape, F32)
            + 12 * _nbytes((tp, POOL_GROUP), F32) + 2 * VMEM_ELEMENTWISE_TEMPORARIES)
    return pl.pallas_call(
        functools.partial(_combine_pool_kernel, tp=tp),
        grid=(batch, tiles_per_seq),
        in_specs=[o_spec] * 3 + [l_spec] * 3 + [
            pl.BlockSpec((tp, POOL_WIDTH), lambda bi, t: (rows(bi, t), 0)),
            pl.BlockSpec((POOL_HALO, POOL_WIDTH),
                         lambda bi, t: (jnp.maximum(rows(bi, t) * halo_blocks - 1, 0), 0)),
            pl.BlockSpec(w_pool.shape, lambda bi, t: (0, 0, 0)),
            pl.BlockSpec((1, POOL_WIDTH), lambda bi, t: (0, 0))],
        out_specs=[o_spec, pl.BlockSpec((tp, POOL_WIDTH), lambda bi, t: (rows(bi, t), 0))],
        out_shape=[jax.ShapeDtypeStruct((m, ATTN_WIDTH), BF16),
                   jax.ShapeDtypeStruct((m, POOL_WIDTH), BF16)],
        compiler_params=_params(2, vmem),
        name="combine_pool",
    )(*outs, *lses, zp, zp, w_pool, pool_scale.reshape(1, POOL_WIDTH))


def _mix_out_kernel(a_ref, p_ref, wa_ref, wp_ref, res_ref, o_ref):
    for sl in _column_halves(o_ref):
        acc = (_dot(a_ref[...], wa_ref[:, sl].astype(BF16))
               + _dot(p_ref[...], wp_ref[:, sl].astype(BF16)))
        o_ref[:, sl] = res_ref[:, sl] + acc


def _mix_out(o_attn, o_pool, w, res, tm=1024, tn=1024):
    m = o_attn.shape[0]
    n = w.shape[1]
    ka, kp = o_attn.shape[1], o_pool.shape[1]
    assert ka == kp
    vmem = (2 * _nbytes((tm, ka + kp), BF16) + _weight_vmem((ka + kp, tn), w.dtype)
            + 8 * _nbytes((tm, tn), F32))
    return pl.pallas_call(
        _mix_out_kernel,
        grid=(m // tm, n // tn),
        in_specs=[pl.BlockSpec((tm, ka), lambda i, j: (i, 0)),
                  pl.BlockSpec((tm, kp), lambda i, j: (i, 0)),
                  pl.BlockSpec((ka, tn), lambda i, j: (0, j)),
                  pl.BlockSpec((kp, tn), lambda i, j: (1, j)),
                  pl.BlockSpec((tm, tn), lambda i, j: (i, j))],
        out_specs=pl.BlockSpec((tm, tn), lambda i, j: (i, j)),
        out_shape=jax.ShapeDtypeStruct((m, n), F32),
        compiler_params=_params(2, vmem),
        name="mix_out",
    )(o_attn, o_pool, w, w, res)


def _cast_kernel(x_ref, o_ref):
    o_ref[...] = x_ref[...].astype(o_ref.dtype)


def _cast_bf16(w, rows=512):
    k, n = w.shape
    rows = min(rows, k)
    return pl.pallas_call(
        _cast_kernel,
        grid=(k // rows,),
        in_specs=[pl.BlockSpec((rows, n), lambda i: (i, 0))],
        out_specs=pl.BlockSpec((rows, n), lambda i: (i, 0)),
        out_shape=jax.ShapeDtypeStruct((k, n), BF16),
        compiler_params=_params(1, 8 * _nbytes((rows, n), F32)),
        name="cast_bf16",
    )(w)


def _rms_rows(x, gain_ref):
    ms = jnp.mean(x * x, axis=-1, keepdims=True)
    return (x * lax.rsqrt(ms + EPS) * gain_ref[...]).astype(BF16)


def _cross_kernel(kv_ref, wq_ref, wo_ref, res_ref, gain_in_ref, gain_out_ref, o_ref, u_ref):
    scale = CROSS_DIM ** -0.5
    q = _dot(_rms_rows(res_ref[...], gain_in_ref), wq_ref[...]).astype(BF16)
    heads = []
    for h in range(CROSS_HEADS):
        sl = slice(h * CROSS_DIM, (h + 1) * CROSS_DIM)
        k = kv_ref[:, sl]
        v = kv_ref[:, CROSS_WIDTH + h * CROSS_DIM:CROSS_WIDTH + (h + 1) * CROSS_DIM]
        s = _dot_nt(q[:, sl], k)
        m = jnp.max(s, axis=1, keepdims=True)
        p = jnp.exp2((s - m) * (scale * LOG2E))
        l = jnp.sum(p, axis=1, keepdims=True)
        heads.append((_dot(p.astype(BF16), v) / l).astype(BF16))
    out = res_ref[...] + _dot(jnp.concatenate(heads, axis=1), wo_ref[...])
    o_ref[...] = out
    u_ref[...] = _rms_rows(out, gain_out_ref)


def _cross_attention(kv, w_q, w_o, res, gain, next_gain, batch, ts=256):
    m, d = res.shape
    mem_len = kv.shape[0] // batch
    steps_per_batch = (m // batch) // ts
    vmem = (2 * _nbytes((ts, d), BF16) + 4 * _nbytes((ts, d), F32) + _nbytes(w_q.shape, BF16)
            + _nbytes(w_o.shape, BF16) + 6 * _nbytes((ts, d), F32) + VMEM_ELEMENTWISE_TEMPORARIES)
    row_f32 = pl.BlockSpec((ts, d), lambda i: (i, 0))
    gain_spec = pl.BlockSpec((1, d), lambda i: (0, 0))
    return pl.pallas_call(
        _cross_kernel,
        grid=(m // ts,),
        in_specs=[pl.BlockSpec((mem_len, 2 * CROSS_WIDTH), lambda i: (i // steps_per_batch, 0)),
                  _resident(w_q.shape, lambda i: (0, 0)),
                  _resident(w_o.shape, lambda i: (0, 0)),
                  row_f32, gain_spec, gain_spec],
        out_specs=[row_f32, pl.BlockSpec((ts, d), lambda i: (i, 0))],
        out_shape=[jax.ShapeDtypeStruct((m, d), F32), jax.ShapeDtypeStruct((m, d), BF16)],
        compiler_params=_params(1, vmem),
        name="cross_attention",
    )(kv, w_q, w_o, res, gain.reshape(1, d), next_gain.reshape(1, d))


def _swiglu_block(h, u, w_in, w_out, next_gain=None, later_weights=()):
    gate, (w_out_bf16, *later_bf16) = _ffn_in(u, w_in, cast_jobs=(w_out,) + tuple(later_weights))
    return _proj_res(gate, w_out_bf16, h, 0.5, tm=512, tn=512, next_gain=next_gain), later_bf16


def kernel(x, mem, positions, g_ffn1, w_ffn1_in, w_ffn1_out, g_mix, w_mix_in, w_pool, pool_scale, w_mix_out, g_cross, g_mem, w_cross_q, w_cross_kv, w_cross_o, g_ffn2, w_ffn2_in, w_ffn2_out, g_final):
    batch, seq, d = x.shape
    m = batch * seq
    depth = g_ffn1.shape[0]
    h = x.reshape(m, d)
    mem2 = mem.reshape(batch * mem.shape[1], d)
    for l in range(depth):
        if l == 0:
            u, cos, sin = _rmsnorm(h, g_ffn1[l], BF16, rope_positions=positions)
        else:
            u = _rmsnorm(h, g_ffn1[l], BF16)
        (h, hg, ssq), (w_mix_in_bf16, w_mix_out_bf16, w_ffn2_in_bf16) = _swiglu_block(
            h, u, w_ffn1_in[l], w_ffn1_out[l], next_gain=g_mix[l],
            later_weights=(w_mix_in[l], w_mix_out[l], w_ffn2_in[l]))

        outs, lses = [], []
        qkv, qkv_dil = None, None
        for group, (_, dil) in enumerate(DILATED_CONFIGS):
            res = _mix_qkv(hg, ssq, w_mix_in_bf16, cos, sin, group, dil, batch,
                           attn_qkv=qkv, attn_dil=qkv_dil, tn=512 if qkv is None else 256)
            if qkv is not None:
                outs.append(res[3])
                lses.append(res[4])
            qkv, qkv_dil = res[:3], dil
        zp, o, lse = _pool_proj_attn(hg, ssq, w_mix_in_bf16, ATTN_IN, qkv, qkv_dil, batch)
        outs.append(o)
        lses.append(lse)
        o_attn, o_pool = _combine_and_pool(outs, lses, zp, w_pool[l], pool_scale[l], batch)
        h = _mix_out(o_attn, o_pool, w_mix_out_bf16, h)

        um = _rmsnorm(mem2, g_mem[l], BF16)
        kv = _proj(um, w_cross_kv[l], tm=um.shape[0], tn=CROSS_WIDTH)
        h, u = _cross_attention(kv, _cast_bf16(w_cross_q[l]), _cast_bf16(w_cross_o[l]),
                                h, g_cross[l], g_ffn2[l], batch)

        h, _ = _swiglu_block(h, u, w_ffn2_in_bf16, w_ffn2_out[l])
    return _rmsnorm(h, g_final, F32).reshape(batch, seq, d)
```
